```python
import math
import jax
import jax.numpy as jnp
from jax import lax
import numpy as np

D_MODEL = 1024
BATCH = 8
SEQ = 4096
DEPTH = 4
DEC_BATCH = 16
DEC_SEQ = 4096
PAST_LEN = 128

HEAD_DIM = 64
GROUP_W = 256
N_GROUP_HEADS = GROUP_W // HEAD_DIM
N_GROUPS = 5
MIX_W = N_GROUPS * GROUP_W
NORM_EPS = 1e-6
NEG_INF = -1e30
ROPE_THETA = 500000.0
ROPE_FRACTION = 4
DIL_PATTERNS = ((128, 1), (512, 4), (2048, 16))
DIFF_QK_DIM = HEAD_DIM // 2
DIFF_Q_BLOCK = 128
DIFF_SUBLN_EPS = 1e-5
HYENA_ORDER = 2
HYENA_BANDS = 16
HYENA_EMB = 1 + 2 * HYENA_BANDS
HYENA_HIDDEN = 64
HYENA_FAST_DECAY = 0.3
HYENA_SLOW_DECAY = 1.5
HYENA_TARGET = 1e-2
HYENA_FILTER_STD = 0.01
MLSTM_CHUNK = 64
N_MLSTM_GATES = 4 * N_GROUP_HEADS
N_MEM = 256

OFF_A = 0
OFF_B = OFF_A + 4 * GROUP_W
OFF_C = OFF_B + 4 * GROUP_W
OFF_D = OFF_C + 4 * GROUP_W
OFF_X = OFF_D + 5 * GROUP_W + N_MLSTM_GATES
IN_COLS = OFF_X + 2 * GROUP_W

kernel_name = 'hybrid_parallel_head_encoder'


def _rms_norm(x, g, eps=NORM_EPS):
    xf = x.astype(jnp.float32)
    y = xf * lax.rsqrt(jnp.mean(xf * xf, axis=-1, keepdims=True) + eps)
    return (y * g.astype(jnp.float32)).astype(x.dtype)


def _rope_tables(L, rot_dim):
    inv = 1.0 / (ROPE_THETA ** (jnp.arange(0, rot_dim, 2, dtype=jnp.float32) / rot_dim))
    ang = jnp.arange(L, dtype=jnp.float32)[:, None] * inv[None, :]
    return jnp.cos(ang), jnp.sin(ang)


def _partial_rope(x, cos, sin):
    half = cos.shape[-1]
    bshape = (x.shape[1],) + (1,) * (x.ndim - 3) + (half,)
    c = cos.reshape(bshape)
    s = sin.reshape(bshape)
    x1 = x[..., :half]
    x2 = x[..., half:2 * half]
    return jnp.concatenate([x1 * c - x2 * s, x2 * c + x1 * s, x[..., 2 * half:]], axis=-1)


def _band_attend(q, k, v, dil, half):
    B_, L, H, hd = q.shape
    n = L // dil
    nb = -(-n // half)
    npad = nb * half

    def sub(t):
        return t.reshape(B_, n, dil, H, hd).transpose(0, 2, 3, 1, 4)

    qs = jnp.pad(sub(q), ((0, 0), (0, 0), (0, 0), (0, npad - n), (0, 0)))
    qs = qs.reshape(B_, dil, H, nb, half, hd)

    def windows(t):
        tp = jnp.pad(sub(t), ((0, 0), (0, 0), (0, 0), (half, npad - n + half), (0, 0)))
        tb = tp.reshape(B_, dil, H, nb + 2, half, hd)
        return jnp.concatenate([tb[:, :, :, :-2], tb[:, :, :, 1:-1], tb[:, :, :, 2:]], axis=4)

    kw = windows(k)
    vw = windows(v)
    s = jnp.einsum('bghnqd,bghnkd->bghnqk', qs, kw) / math.sqrt(hd)
    qi = jnp.arange(nb)[:, None, None] * half + jnp.arange(half)[None, :, None]
    kj = jnp.arange(nb)[:, None, None] * half - half + jnp.arange(3 * half)[None, None, :]
    valid = (kj >= 0) & (kj < n) & (jnp.abs(kj - qi) <= half)
    s = jnp.where(valid, s, NEG_INF)
    m = jnp.max(s, axis=-1)
    p = jnp.exp(s - m[..., None])
    den = jnp.sum(p, axis=-1)
    num = jnp.einsum('bghnqk,bghnkd->bghnqd', p, vw)

    def back(t):
        t = t.reshape((B_, dil, H, npad) + t.shape[5:])[:, :, :, :n]
        t = jnp.moveaxis(t, 3, 1)
        return t.reshape((B_, L, H) + t.shape[4:])

    return back(num), back(den), back(m)


def _dilated_mixture(q, k, v):
    outs = [_band_attend(q, k, v, d, w // (2 * d)) for (w, d) in DIL_PATTERNS]
    m_all = jnp.max(jnp.stack([o[2] for o in outs], axis=0), axis=0)
    num = jnp.zeros_like(q)
    den = jnp.zeros_like(m_all)
    for o_num, o_den, o_m in outs:
        w = jnp.exp(o_m - m_all)
        num = num + o_num * w[..., None]
        den = den + o_den * w
    return num / den[..., None]


def _diff_attention(q, k, v, lam):
    B_, L, H, _, dq = q.shape
    nblk = L // DIFF_Q_BLOCK
    qb = q.reshape(B_, nblk, DIFF_Q_BLOCK, H, 2, dq).transpose(1, 0, 3, 4, 2, 5)
    kt = k.transpose(0, 2, 3, 1, 4)
    vt = v.transpose(0, 2, 1, 3)
    scale = 1.0 / math.sqrt(dq)

    def block(qblk):
        p = jax.nn.softmax(jnp.einsum('bhcqd,bhckd->bhcqk', qblk, kt) * scale, axis=-1)
        return jnp.einsum('bhqk,bhkd->bhqd', p[:, :, 0] - lam * p[:, :, 1], vt)

    o = lax.map(block, qb)
    return o.transpose(1, 0, 3, 2, 4).reshape(B_, L, H, -1)


def _hyena_pos_features(L):
    t = jnp.linspace(0.0, 1.0, L, dtype=jnp.float32)[:, None]
    bands = jnp.linspace(1e-4, HYENA_BANDS - 1, HYENA_BANDS, dtype=jnp.float32)
    ang = (2.0 * math.pi / L) * jnp.arange(L, dtype=jnp.float32)[:, None] * bands[None, :]
    feats = jnp.concatenate([t, jnp.cos(ang), -jnp.sin(ang)], axis=-1)
    return feats, t


def _hyena_filters(feats, t, w1, b1, w2, b2, w3, freq):
    f32 = jnp.float32
    L = feats.shape[0]
    freq = freq.astype(f32)
    z = jnp.sin(freq * (feats @ w1.astype(f32) + b1.astype(f32)))
    z = jnp.sin(freq * (z @ w2.astype(f32) + b2.astype(f32)))
    h = (z @ w3.astype(f32)).reshape(L, HYENA_ORDER, 2, GROUP_W)
    deltas = jnp.linspace(math.log(HYENA_TARGET) / HYENA_SLOW_DECAY,
                          math.log(HYENA_TARGET) / HYENA_FAST_DECAY, GROUP_W, dtype=f32)
    h = h * jnp.exp(-t * jnp.abs(deltas))[:, None, None, :]
    h_fw = h[:, :, 0]
    h_bw = h[:, :, 1]
    g = jnp.concatenate([h_fw, jnp.zeros_like(h_fw[:1]), h_bw[:0:-1]], axis=0)
    return jnp.fft.rfft(g, axis=0)


def _fft_conv(z, g_freq, bias):
    L = z.shape[1]
    zf = jnp.fft.rfft(z, n=2 * L, axis=1)
    y = jnp.fft.irfft(zf * g_freq[None], n=2 * L, axis=1)[:, :L]
    return y + z * bias


def _conv3_centered(u, w, b):
    up = jnp.pad(u, ((0, 0), (1, 1), (0, 0)))
    return up[:, :-2] * w[0] + up[:, 1:-1] * w[1] + up[:, 2:] * w[2] + b


def _mlstm_chunkwise(q, k, v, i_pre, logf):
    B_, H, L, d = q.shape
    T = MLSTM_CHUNK
    nc = L // T
    qc = q.reshape(B_, H, nc, T, d)
    kc = (k / math.sqrt(d)).reshape(B_, H, nc, T, d)
    vc = v.reshape(B_, H, nc, T, d)
    ic = i_pre.reshape(B_, H, nc, T)
    b = jnp.cumsum(logf.reshape(B_, H, nc, T), axis=-1)
    b_end = b[..., -1]
    a = b_end[..., None] - b + ic
    m_loc = jnp.max(a, axis=-1)
    wa = jnp.exp(a - m_loc[..., None])
    kv_c = jnp.einsum('bhcsk,bhcsv->bhckv', wa[..., None] * kc, vc)
    n_c = jnp.einsum('bhcs,bhcsk->bhck', wa, kc)

    def step(carry, inp):
        C, n, m = carry
        kv_i, n_i, ml_i, be_i = inp
        m_new = jnp.maximum(be_i + m, ml_i)
        sp = jnp.exp(be_i + m - m_new)
        sc = jnp.exp(ml_i - m_new)
        C_new = sp[..., None, None] * C + sc[..., None, None] * kv_i
        n_new = sp[..., None] * n + sc[..., None] * n_i
        return (C_new, n_new, m_new), (C, n, m)

    init = (jnp.zeros((B_, H, d, d), jnp.float32), jnp.zeros((B_, H, d), jnp.float32),
            jnp.zeros((B_, H), jnp.float32))
    xs = (jnp.moveaxis(kv_c, 2, 0), jnp.moveaxis(n_c, 2, 0), jnp.moveaxis(m_loc, 2, 0),
          jnp.moveaxis(b_end, 2, 0))
    _, (C_prev, n_prev, m_prev) = lax.scan(step, init, xs)
    C_prev = jnp.moveaxis(C_prev, 0, 2)
    n_prev = jnp.moveaxis(n_prev, 0, 2)
    m_prev = jnp.moveaxis(m_prev, 0, 2)

    lower = jnp.tril(jnp.ones((T, T), dtype=bool))
    dmat = jnp.where(lower, b[..., :, None] - b[..., None, :] + ic[..., None, :], NEG_INF)
    inter = b + m_prev[..., None]
    m_t = jnp.maximum(inter, jnp.max(dmat, axis=-1))
    qk = jnp.einsum('bhctd,bhcsd->bhcts', qc, kc) * jnp.exp(dmat - m_t[..., None])
    w_inter = jnp.exp(inter - m_t)
    num = (jnp.einsum('bhcts,bhcsd->bhctd', qk, vc)
           + w_inter[..., None] * jnp.einsum('bhctk,bhckv->bhctv', qc, C_prev))
    den = jnp.sum(qk, axis=-1) + w_inter * jnp.einsum('bhctk,bhck->bhct', qc, n_prev)
    h = num / jnp.maximum(jnp.abs(den), jnp.exp(-m_t))[..., None]
    return h.reshape(B_, H, L, d)


def _memory_attention(q, mk, mv):
    p = jax.nn.softmax(jnp.einsum('blhd,bmhd->bhlm', q, mk) / math.sqrt(q.shape[-1]), axis=-1)
    return jnp.einsum('bhlm,bmhd->blhd', p, mv)


def _trunk(x, mem, pre_norm_g, post_norm_g, w_in, w_out, diff_lambda, diff_subln_g,
           hy_conv_w, hy_conv_b, hy_ffn_w1, hy_ffn_b1, hy_ffn_w2, hy_ffn_b2, hy_ffn_w3,
           hy_freq, hy_bias, ml_conv_w, ml_conv_b, ml_gate_b, mem_norm_g, w_mem_kv):
    f32 = jnp.float32
    B_, L, _ = x.shape
    M = mem.shape[1]
    G, H, hd = GROUP_W, N_GROUP_HEADS, HEAD_DIM
    cos_a, sin_a = _rope_tables(L, HEAD_DIM // ROPE_FRACTION)
    cos_b, sin_b = _rope_tables(L, DIFF_QK_DIM // ROPE_FRACTION)
    feats, t_pos = _hyena_pos_features(L)

    def heads(u):
        return u.reshape(B_, L, H, hd)

    def diff_heads(u):
        return u.reshape(B_, L, H, 2, DIFF_QK_DIM)

    def bhld(u):
        return heads(u).transpose(0, 2, 1, 3)

    def rev(u):
        return jnp.flip(u, axis=2)

    for li in range(DEPTH):
        lam_init = 0.8 - 0.6 * math.exp(-0.3 * li)
        h = (_rms_norm(x, pre_norm_g[li]) @ w_in[li]).astype(f32)

        qa = _partial_rope(heads(h[..., OFF_A:OFF_A + G]), cos_a, sin_a)
        ka = _partial_rope(heads(h[..., OFF_A + G:OFF_A + 2 * G]), cos_a, sin_a)
        va = heads(h[..., OFF_A + 2 * G:OFF_A + 3 * G])
        ya = (_dilated_mixture(qa, ka, va).reshape(B_, L, G)
              * jax.nn.silu(h[..., OFF_A + 3 * G:OFF_A + 4 * G]))

        qb = _partial_rope(diff_heads(h[..., OFF_B:OFF_B + G]), cos_b, sin_b)
        kb = _partial_rope(diff_heads(h[..., OFF_B + G:OFF_B + 2 * G]), cos_b, sin_b)
        vb = heads(h[..., OFF_B + 2 * G:OFF_B + 3 * G])
        lq1, lk1, lq2, lk2 = diff_lambda[li].astype(f32)
        lam = jnp.exp(jnp.sum(lq1 * lk1)) - jnp.exp(jnp.sum(lq2 * lk2)) + lam_init
        ob = _diff_attention(qb, kb, vb, lam)
        ob = _rms_norm(ob, diff_subln_g[li], DIFF_SUBLN_EPS) * (1.0 - lam_init)
        yb = ob.reshape(B_, L, G) * jax.nn.silu(h[..., OFF_B + 3 * G:OFF_B + 4 * G])

        u = _conv3_centered(h[..., OFF_C:OFF_C + 3 * G], hy_conv_w[li], hy_conv_b[li])
        filt = _hyena_filters(feats, t_pos, hy_ffn_w1[li], hy_ffn_b1[li], hy_ffn_w2[li],
                              hy_ffn_b2[li], hy_ffn_w3[li], hy_freq[li])
        hy_d = hy_bias[li].astype(f32)
        z = u[..., :G]
        for o in range(HYENA_ORDER):
            z = u[..., (o + 1) * G:(o + 2) * G] * _fft_conv(z, filt[:, o], hy_d[o])
        yc = z * jax.nn.silu(h[..., OFF_C + 3 * G:OFF_C + 4 * G])

        qk = jax.nn.silu(_conv3_centered(h[..., OFF_D:OFF_D + 2 * G], ml_conv_w[li], ml_conv_b[li]))
        qh, kh = bhld(qk[..., :G]), bhld(qk[..., G:])
        vh = bhld(h[..., OFF_D + 2 * G:OFF_D + 3 * G])
        gts = h[..., OFF_D + 5 * G:OFF_D + 5 * G + N_MLSTM_GATES] + ml_gate_b[li]
        gts = gts.reshape(B_, L, 4, H).transpose(2, 0, 3, 1)
        i_fw, f_fw, i_bw, f_bw = gts[0], gts[1], gts[2], gts[3]
        h_fw = _mlstm_chunkwise(qh, kh, vh, i_fw, jax.nn.log_sigmoid(f_fw))
        h_bw = rev(_mlstm_chunkwise(rev(qh), rev(kh), rev(vh), rev(i_bw),
                                    jax.nn.log_sigmoid(rev(f_bw))))
        yd = ((h_fw + h_bw).transpose(0, 2, 1, 3).reshape(B_, L, G)
              * jax.nn.sigmoid(h[..., OFF_D + 3 * G:OFF_D + 4 * G])
              * jax.nn.silu(h[..., OFF_D + 4 * G:OFF_D + 5 * G]))

        mkv = (_rms_norm(mem, mem_norm_g[li]) @ w_mem_kv[li]).astype(f32)
        mk = mkv[..., :G].reshape(B_, M, H, hd)
        mv = mkv[..., G:].reshape(B_, M, H, hd)
        yx = (_memory_attention(heads(h[..., OFF_X:OFF_X + G]), mk, mv).reshape(B_, L, G)
              * jax.nn.silu(h[..., OFF_X + G:OFF_X + 2 * G]))

        y = jnp.concatenate([ya, yb, yc, yd, yx], axis=-1).astype(x.dtype) @ w_out[li]
        x = x + _rms_norm(y, post_norm_g[li])
    return x


def setup_inputs(seed: int = 0) -> dict:
    key = jax.random.key(seed)
    ks = jax.random.split(key, 24)

    def nrm(k, shape, s):
        return s * jax.random.normal(k, shape, jnp.float32)

    f_bias = jnp.linspace(3.0, 6.0, N_GROUP_HEADS, dtype=jnp.float32)
    kind_is_forget = jnp.array([0.0, 1.0, 0.0, 1.0], jnp.float32)
    gate_b = (nrm(ks[21], (DEPTH, 4, N_GROUP_HEADS), 0.1)
              + kind_is_forget[None, :, None] * f_bias[None, None, :]).reshape(DEPTH, N_MLSTM_GATES)
    return {
        'x_prompt': nrm(ks[0], (BATCH, SEQ, D_MODEL), 1.0),
        'x_sample': nrm(ks[1], (DEC_BATCH, DEC_SEQ, D_MODEL), 1.0),
        'mem_prompt': nrm(ks[2], (BATCH, N_MEM, D_MODEL), 1.0),
        'mem_sample': nrm(ks[3], (DEC_BATCH, N_MEM, D_MODEL), 1.0),
        'pre_norm_g': 1.0 + nrm(ks[4], (DEPTH, D_MODEL), 0.05),
        'post_norm_g': 1.0 + nrm(ks[5], (DEPTH, D_MODEL), 0.05),
        'w_in': nrm(ks[6], (DEPTH, D_MODEL, IN_COLS), D_MODEL ** -0.5),
        'w_out': nrm(ks[7], (DEPTH, MIX_W, D_MODEL), MIX_W ** -0.5),
        'diff_lambda': nrm(ks[8], (DEPTH, 4, DIFF_QK_DIM), 0.1),
        'diff_subln_g': 1.0 + nrm(ks[9], (DEPTH, HEAD_DIM), 0.05),
        'hy_conv_w': nrm(ks[10], (DEPTH, 3, 3 * GROUP_W), 3 ** -0.5),
        'hy_conv_b': nrm(ks[11], (DEPTH, 3 * GROUP_W), 0.02),
        'hy_ffn_w1': nrm(ks[12], (DEPTH, HYENA_EMB, HYENA_HIDDEN), HYENA_EMB ** -0.5),
        'hy_ffn_b1': nrm(ks[13], (DEPTH, HYENA_HIDDEN), 0.02),
        'hy_ffn_w2': nrm(ks[14], (DEPTH, HYENA_HIDDEN, HYENA_HIDDEN), HYENA_HIDDEN ** -0.5),
        'hy_ffn_b2': nrm(ks[15], (DEPTH, HYENA_HIDDEN), 0.02),
        'hy_ffn_w3': nrm(ks[16], (DEPTH, HYENA_HIDDEN, HYENA_ORDER * 2 * GROUP_W), HYENA_FILTER_STD),
        'hy_freq': 1.0 + nrm(ks[17], (DEPTH, HYENA_HIDDEN), 0.05),
        'hy_bias': nrm(ks[18], (DEPTH, HYENA_ORDER, GROUP_W), 1.0),
        'ml_conv_w': nrm(ks[19], (DEPTH, 3, 2 * GROUP_W), 3 ** -0.5),
        'ml_conv_b': nrm(ks[20], (DEPTH, 2 * GROUP_W), 0.02),
        'ml_gate_b': gate_b,
        'mem_norm_g': 1.0 + nrm(ks[22], (DEPTH, D_MODEL), 0.05),
        'w_mem_kv': nrm(ks[23], (DEPTH, D_MODEL, 2 * GROUP_W), D_MODEL ** -0.5),
    }


def reference(x_prompt, x_sample, mem_prompt, mem_sample, pre_norm_g, post_norm_g, w_in, w_out,
              diff_lambda, diff_subln_g, hy_conv_w, hy_conv_b, hy_ffn_w1, hy_ffn_b1, hy_ffn_w2,
              hy_ffn_b2, hy_ffn_w3, hy_freq, hy_bias, ml_conv_w, ml_conv_b, ml_gate_b,
              mem_norm_g, w_mem_kv):
    y_prompt = _trunk(x_prompt, mem_prompt, pre_norm_g, post_norm_g, w_in, w_out, diff_lambda,
                      diff_subln_g, hy_conv_w, hy_conv_b, hy_ffn_w1, hy_ffn_b1, hy_ffn_w2,
                      hy_ffn_b2, hy_ffn_w3, hy_freq, hy_bias, ml_conv_w, ml_conv_b, ml_gate_b,
                      mem_norm_g, w_mem_kv)
    y_sample = _trunk(x_sample, mem_sample, pre_norm_g, post_norm_g, w_in, w_out, diff_lambda,
                      diff_subln_g, hy_conv_w, hy_conv_b, hy_ffn_w1, hy_ffn_b1, hy_ffn_w2,
                      hy_ffn_b2, hy_ffn_w3, hy_freq, hy_bias, ml_conv_w, ml_conv_b, ml_gate_b,
                      mem_norm_g, w_mem_kv)
    return (y_prompt, y_sample)
```

```python
import functools
import math

import numpy as np
import jax
import jax.numpy as jnp
from jax import lax
from jax.experimental import pallas as pl
from jax.experimental.pallas import tpu as pltpu

F32 = jnp.float32
BF16 = jnp.bfloat16

D_MODEL = 1024
HEAD_DIM = 64
GROUP_W = 256
N_HEADS = GROUP_W // HEAD_DIM
NORM_EPS = 1e-6
NEG_INF = -1e30
ROPE_THETA = 500000.0
ROPE_FRACTION = 4
DIL_PATTERNS = ((128, 1), (512, 4), (2048, 16))
DIFF_QK_DIM = HEAD_DIM // 2
DIFF_SUBLN_EPS = 1e-5
HYENA_ORDER = 2
HYENA_BANDS = 16
HYENA_EMB = 1 + 2 * HYENA_BANDS
HYENA_HIDDEN = 64
HYENA_FAST_DECAY = 0.3
HYENA_SLOW_DECAY = 1.5
HYENA_TARGET = 1e-2
N_MLSTM_GATES = 4 * N_HEADS

(BLK_AQ, BLK_AK, BLK_AV, BLK_AG, BLK_BQ, BLK_BK, BLK_BV, BLK_BG, BLK_CG,
 BLK_DQ, BLK_DK, BLK_DV, BLK_DO, BLK_DG, BLK_XQ, BLK_XG) = range(16)
N_MAIN_BLK = 16
MAIN_W = N_MAIN_BLK * GROUP_W
W_ALL = MAIN_W + 3 * GROUP_W + 128

VMEM_LIMIT_BYTES = 56 * 1024 * 1024
LANE = 128
BF16_SUBLANES = 16

ROW_TILE = 512
DIL_Q_TILE = 256
DIL_PAD = 1024
DIFF_Q_TILE = 128
MEM_Q_TILE = 512
MLSTM_T = 128
CONV_HALO = 16
HY_N2 = 128


def _cparams(sem, vmem=VMEM_LIMIT_BYTES):
    return pltpu.CompilerParams(dimension_semantics=sem, vmem_limit_bytes=vmem)


def _sigmoid(v):
    return 1.0 / (1.0 + jnp.exp(-v))


def _silu(v):
    return v * _sigmoid(v)


def _dot(a, b):
    return jnp.dot(a, b, preferred_element_type=F32)


def _dot_nt(a, b):
    return lax.dot_general(a, b, (((1,), (1,)), ((), ())), preferred_element_type=F32)


def _dot_tn(a, b):
    return lax.dot_general(a, b, (((0,), (0,)), ((), ())), preferred_element_type=F32)


def _head_mask(h, width, dtype):
    lane = lax.broadcasted_iota(jnp.int32, (1, GROUP_W), 1)
    return ((lane // width) == h).astype(dtype)


def _inproj_kernel(x_ref, g_ref, w_ref, ca_ref, sa_ref, cb_ref, sb_ref, pa_ref, pb_ref,
                   h_ref, cv_ref, cx1_ref, cx2_ref, gate_ref):
    x = x_ref[...]
    ms = jnp.mean(x * x, axis=-1, keepdims=True)
    xn = (x * lax.rsqrt(ms + NORM_EPS) * g_ref[...]).astype(BF16)

    def proj(j, width=GROUP_W):
        return _dot(xn, w_ref[:, j * GROUP_W:j * GROUP_W + width])

    def rope(acc, c_ref, s_ref, p_ref):
        partner = _dot(acc.astype(BF16), p_ref[...])
        return acc * c_ref[...] + partner * s_ref[...]

    for j in range(N_MAIN_BLK):
        acc = proj(j)
        if j in (BLK_AQ, BLK_AK):
            acc = rope(acc, ca_ref, sa_ref, pa_ref)
        if j in (BLK_BQ, BLK_BK):
            acc = rope(acc, cb_ref, sb_ref, pb_ref)
        if j in (BLK_AQ, BLK_XQ):
            acc = acc * (1.0 / math.sqrt(HEAD_DIM))
        if j == BLK_BQ:
            acc = acc * (1.0 / math.sqrt(DIFF_QK_DIM))
        h_ref[:, j * GROUP_W:(j + 1) * GROUP_W] = acc.astype(BF16)
    cv_ref[...] = proj(N_MAIN_BLK).astype(BF16)
    cx1_ref[...] = proj(N_MAIN_BLK + 1).astype(BF16)
    cx2_ref[...] = proj(N_MAIN_BLK + 2).astype(BF16)
    gate_ref[...] = proj(N_MAIN_BLK + 3, LANE)


def _inproj(x2d, g, w_all, rope_tabs, B, L):
    ca, sa, cb, sb, pa, pb = rope_tabs
    nt = L // ROW_TILE
    n_tok = B * L
    row = lambda i, b: (b * nt + i, 0)
    tab = lambda i, b: (i, 0)
    const = lambda i, b: (0, 0)
    out_shapes = (
        jax.ShapeDtypeStruct((n_tok, MAIN_W), BF16),
        jax.ShapeDtypeStruct((n_tok, GROUP_W), BF16),
        jax.ShapeDtypeStruct((n_tok, GROUP_W), BF16),
        jax.ShapeDtypeStruct((n_tok, GROUP_W), BF16),
        jax.ShapeDtypeStruct((n_tok, LANE), F32),
    )
    return pl.pallas_call(
        _inproj_kernel,
        grid=(nt, B),
        in_specs=[
            pl.BlockSpec((ROW_TILE, D_MODEL), row),
            pl.BlockSpec((1, D_MODEL), const),
            pl.BlockSpec((D_MODEL, W_ALL), const),
            pl.BlockSpec((ROW_TILE, GROUP_W), tab),
            pl.BlockSpec((ROW_TILE, GROUP_W), tab),
            pl.BlockSpec((ROW_TILE, GROUP_W), tab),
            pl.BlockSpec((ROW_TILE, GROUP_W), tab),
            pl.BlockSpec((GROUP_W, GROUP_W), const),
            pl.BlockSpec((GROUP_W, GROUP_W), const),
        ],
        out_specs=(
            pl.BlockSpec((ROW_TILE, MAIN_W), row),
            pl.BlockSpec((ROW_TILE, GROUP_W), row),
            pl.BlockSpec((ROW_TILE, GROUP_W), row),
            pl.BlockSpec((ROW_TILE, GROUP_W), row),
            pl.BlockSpec((ROW_TILE, LANE), row),
        ),
        out_shape=out_shapes,
        compiler_params=_cparams(("parallel", "parallel")),
        name="inproj",
    )(x2d, g, w_all, ca, sa, cb, sb, pa, pb)


def _memkv_kernel(m_ref, g_ref, w_ref, o_ref):
    x = m_ref[...]
    ms = jnp.mean(x * x, axis=-1, keepdims=True)
    xn = (x * lax.rsqrt(ms + NORM_EPS) * g_ref[...]).astype(BF16)
    o_ref[...] = _dot(xn, w_ref[...]).astype(BF16)


def _memkv(mem2d, g, w):
    rows = mem2d.shape[0]
    tile = math.gcd(rows, ROW_TILE)
    return pl.pallas_call(
        _memkv_kernel,
        grid=(rows // tile,),
        in_specs=[
            pl.BlockSpec((tile, D_MODEL), lambda i: (i, 0)),
            pl.BlockSpec((1, D_MODEL), lambda i: (0, 0)),
            pl.BlockSpec((D_MODEL, 2 * GROUP_W), lambda i: (0, 0)),
        ],
        out_specs=pl.BlockSpec((tile, 2 * GROUP_W), lambda i: (i, 0)),
        out_shape=jax.ShapeDtypeStruct((rows, 2 * GROUP_W), BF16),
        compiler_params=_cparams(("parallel",)),
        name="memkv",
    )(mem2d, g, w)


def _memattn_kernel(q_ref, mk_ref, mv_ref, o_ref):
    q = q_ref[...]
    mk = mk_ref[...]
    mv = mv_ref[...]
    acc = jnp.zeros(q.shape, F32)
    for h in range(N_HEADS):
        hm = _head_mask(h, HEAD_DIM, BF16)
        s = _dot_nt(q * hm, mk)
        m = jnp.max(s, axis=-1, keepdims=True)
        p = jnp.exp(s - m)
        l = jnp.sum(p, axis=-1, keepdims=True)
        acc = acc + _dot(p.astype(BF16), mv * hm) * (1.0 / l)
    o_ref[...] = acc.astype(BF16)


def _memattn(h_main, mkv, B, L, M):
    nq = L // MEM_Q_TILE
    return pl.pallas_call(
        _memattn_kernel,
        grid=(B, nq),
        in_specs=[
            pl.BlockSpec((MEM_Q_TILE, GROUP_W), lambda b, i: (b * nq + i, BLK_XQ)),
            pl.BlockSpec((M, GROUP_W), lambda b, i: (b, 0)),
            pl.BlockSpec((M, GROUP_W), lambda b, i: (b, 1)),
        ],
        out_specs=pl.BlockSpec((MEM_Q_TILE, GROUP_W), lambda b, i: (b * nq + i, 0)),
        out_shape=jax.ShapeDtypeStruct((B * L, GROUP_W), BF16),
        compiler_params=_cparams(("parallel", "parallel")),
        name="memattn",
    )(h_main, mkv, mkv)


def _dil_bias_table():
    w = DIL_Q_TILE + 2 * DIL_PAD
    d = np.arange(DIL_Q_TILE)[:, None] - np.arange(w)[None, :] + DIL_PAD
    count = np.zeros(d.shape, np.float64)
    for win, dil in DIL_PATTERNS:
        reach = (win // (2 * dil)) * dil
        count += (d % dil == 0) & (np.abs(d) <= reach)
    return np.where(count > 0, np.log(np.maximum(count, 1.0)), NEG_INF).astype(np.float32)


def _dilattn_kernel(q_ref, k_ref, v_ref, bias_ref, o_ref, kpad, vpad, *, L):
    i = pl.program_id(1)
    w = DIL_Q_TILE + 2 * DIL_PAD

    @pl.when(i == 0)
    def _():
        zeros = jnp.zeros((DIL_PAD, GROUP_W), BF16)
        for pad, src in ((kpad, k_ref), (vpad, v_ref)):
            pad[0:DIL_PAD, :] = zeros
            pad[DIL_PAD + L:DIL_PAD + L + DIL_PAD, :] = zeros
            pad[DIL_PAD:DIL_PAD + L, :] = src[...]

    q0 = pl.multiple_of(i * DIL_Q_TILE, DIL_Q_TILE)
    q = q_ref[...]
    kw = kpad[pl.ds(q0, w), :]
    vw = vpad[pl.ds(q0, w), :]
    jpos = q0 - DIL_PAD + lax.broadcasted_iota(jnp.int32, (1, w), 1)
    in_seq = jnp.where(jpos >= 0, jnp.where(jpos < L, 0.0, NEG_INF), NEG_INF)
    bias = bias_ref[...] + in_seq
    acc = jnp.zeros(q.shape, F32)
    for h in range(N_HEADS):
        hm = _head_mask(h, HEAD_DIM, BF16)
        s = _dot_nt(q * hm, kw) + bias
        m = jnp.max(s, axis=-1, keepdims=True)
        p = jnp.exp(s - m)
        l = jnp.sum(p, axis=-1, keepdims=True)
        acc = acc + _dot(p.astype(BF16), vw * hm) * (1.0 / l)
    o_ref[...] = acc.astype(BF16)


def _dilattn(h_main, bias, B, L):
    nq = L // DIL_Q_TILE
    w = DIL_Q_TILE + 2 * DIL_PAD
    return pl.pallas_call(
        functools.partial(_dilattn_kernel, L=L),
        grid=(B, nq),
        in_specs=[
            pl.BlockSpec((DIL_Q_TILE, GROUP_W), lambda b, i: (b * nq + i, BLK_AQ)),
            pl.BlockSpec((L, GROUP_W), lambda b, i: (b, BLK_AK)),
            pl.BlockSpec((L, GROUP_W), lambda b, i: (b, BLK_AV)),
            pl.BlockSpec((DIL_Q_TILE, w), lambda b, i: (0, 0)),
        ],
        out_specs=pl.BlockSpec((DIL_Q_TILE, GROUP_W), lambda b, i: (b * nq + i, 0)),
        out_shape=jax.ShapeDtypeStruct((B * L, GROUP_W), BF16),
        scratch_shapes=[pltpu.VMEM((L + 2 * DIL_PAD, GROUP_W), BF16),
                        pltpu.VMEM((L + 2 * DIL_PAD, GROUP_W), BF16)],
        compiler_params=_cparams(("parallel", "arbitrary")),
        name="dilattn",
    )(h_main, h_main, h_main, bias)


def _split3(x):
    hi = x.astype(BF16)
    r1 = x - hi.astype(F32)
    mid = r1.astype(BF16)
    lo = (r1 - mid.astype(F32)).astype(BF16)
    return hi, mid, lo


def _split_dot(x, mat):
    return sum(_dot(t, mat) for t in _split3(x))


def _tri_dot(mat, x):
    return sum(_dot(mat, t) for t in _split3(x))


def _dot_f32(a, b):
    ah, am, al = _split3(a)
    bh, bm, bl = _split3(b)
    return (_dot(ah, bh) + (_dot(ah, bm) + _dot(am, bh))
            + (_dot(ah, bl) + _dot(am, bm) + _dot(al, bh)))


def _diffattn_kernel(q_ref, k_ref, v_ref, lam_ref, g_ref, hmean_ref, o_ref, *, lam_init):
    q = q_ref[...]
    k = k_ref[...]
    v = v_ref[...]
    lp = lam_ref[...]
    lam = (jnp.exp(jnp.sum(lp[0:1] * lp[1:2], axis=-1, keepdims=True))
           - jnp.exp(jnp.sum(lp[2:3] * lp[3:4], axis=-1, keepdims=True)) + lam_init)
    acc = jnp.zeros(q.shape, F32)
    for h in range(N_HEADS):
        maps = []
        for c in range(2):
            cm = _head_mask(2 * h + c, DIFF_QK_DIM, BF16)
            s = _dot_nt(q * cm, k)
            m = jnp.max(s, axis=-1, keepdims=True)
            p = jnp.exp(s - m)
            l = jnp.sum(p, axis=-1, keepdims=True)
            maps.append((p, l))
        pd = maps[0][0] * (1.0 / maps[0][1]) - maps[1][0] * (lam / maps[1][1])
        acc = acc + _dot(pd.astype(BF16), v * _head_mask(h, HEAD_DIM, BF16))
    ms = _split_dot(acc * acc, hmean_ref[...])
    y = acc * lax.rsqrt(ms + DIFF_SUBLN_EPS) * g_ref[...] * (1.0 - lam_init)
    o_ref[...] = y.astype(BF16)


def _diffattn(h_main, lam_p, subln_g, hmean, lam_init, B, L):
    nq = L // DIFF_Q_TILE
    return pl.pallas_call(
        functools.partial(_diffattn_kernel, lam_init=lam_init),
        grid=(B, nq),
        in_specs=[
            pl.BlockSpec((DIFF_Q_TILE, GROUP_W), lambda b, i: (b * nq + i, BLK_BQ)),
            pl.BlockSpec((L, GROUP_W), lambda b, i: (b, BLK_BK)),
            pl.BlockSpec((L, GROUP_W), lambda b, i: (b, BLK_BV)),
            pl.BlockSpec((4, DIFF_QK_DIM), lambda b, i: (0, 0)),
            pl.BlockSpec((1, GROUP_W), lambda b, i: (0, 0)),
            pl.BlockSpec((GROUP_W, GROUP_W), lambda b, i: (0, 0)),
        ],
        out_specs=pl.BlockSpec((DIFF_Q_TILE, GROUP_W), lambda b, i: (b * nq + i, 0)),
        out_shape=jax.ShapeDtypeStruct((B * L, GROUP_W), BF16),
        compiler_params=_cparams(("parallel", "parallel")),
        name="diffattn",
    )(h_main, h_main, h_main, lam_p, subln_g, hmean)


def _log_sigmoid(v):
    return jnp.minimum(v, 0.0) - jnp.log(1.0 + jnp.exp(-jnp.abs(v)))


def _mlstm_kernel(q_ref, k_ref, v_ref, gate_ref, cw_ref, cb_ref, gb_ref, ltri_ref, utri_ref,
                  bd_ref, hsum_ref, o_ref,
                  qpad, kpad, qs, ks, hfw, hbw, cst, nst, mst, *, L):
    T = MLSTM_T
    nc = L // T
    halo = CONV_HALO

    zpad = jnp.zeros((halo, GROUP_W), BF16)
    for pad, src in ((qpad, q_ref), (kpad, k_ref)):
        pad[0:halo, :] = zpad
        pad[halo + L:halo + L + halo, :] = zpad
        pad[halo:halo + L, :] = src[...]

    def conv_body(c, carry):
        r0 = pl.multiple_of(c * T, T)
        for idx, (pad, dst, scale) in enumerate(((qpad, qs, 1.0),
                                                 (kpad, ks, 1.0 / math.sqrt(HEAD_DIM)))):
            xw = pad[pl.ds(r0, T + 2 * halo), :].astype(F32)
            xm = pltpu.roll(xw, 1, axis=0)[halo:halo + T]
            xp = pltpu.roll(xw, T + 2 * halo - 1, axis=0)[halo:halo + T]
            xc = xw[halo:halo + T]
            lo, hi = idx * GROUP_W, (idx + 1) * GROUP_W
            y = (xm * cw_ref[0:1, lo:hi] + xc * cw_ref[1:2, lo:hi] + xp * cw_ref[2:3, lo:hi]
                 + cb_ref[0:1, lo:hi])
            dst[pl.ds(r0, T), :] = (_silu(y) * scale).astype(BF16)
        return carry

    lax.fori_loop(0, nc, conv_body, 0)

    cst[...] = jnp.zeros(cst.shape, F32)
    nst[...] = jnp.zeros(nst.shape, F32)
    mst[...] = jnp.zeros(mst.shape, F32)

    row = lax.broadcasted_iota(jnp.int32, (T, T), 0)
    col = lax.broadcasted_iota(jnp.int32, (T, T), 1)
    causal = (col <= row, col >= row)
    bd = bd_ref[...]
    hsum = hsum_ref[...]

    def chunk(d, r0):
        qc = qs[pl.ds(r0, T), :]
        kc = ks[pl.ds(r0, T), :]
        vc = v_ref[pl.ds(r0, T), :]
        g = gate_ref[pl.ds(r0, T), :] + gb_ref[...]
        tri = ltri_ref[...] if d == 0 else utri_ref[...]
        cum = _tri_dot(tri, _log_sigmoid(g))
        g_t = g.T
        cum_t = cum.T
        i_off = 0 if d == 0 else 2 * N_HEADS
        f_off = i_off + N_HEADS
        end_row = T - 1 if d == 0 else 0

        num = jnp.zeros((T, GROUP_W), F32)
        w_inter = jnp.zeros((T, GROUP_W), F32)
        den_intra = jnp.zeros((T, GROUP_W), F32)
        floor = jnp.zeros((T, GROUP_W), F32)
        w_key = jnp.zeros((T, GROUP_W), F32)
        sp_row = jnp.zeros((1, GROUP_W), F32)
        for h in range(N_HEADS):
            hm = _head_mask(h, HEAD_DIM, F32)
            hmb = hm.astype(BF16)
            i_col = g[:, i_off + h:i_off + h + 1]
            i_row = g_t[i_off + h:i_off + h + 1, :]
            b_col = cum[:, f_off + h:f_off + h + 1]
            b_row = cum_t[f_off + h:f_off + h + 1, :]
            b_end = b_col[end_row:end_row + 1, :]
            sr = d * N_HEADS + h
            m_prev = mst[sr:sr + 1, 0:1]

            dmat = jnp.where(causal[d], b_col - b_row + i_row, NEG_INF)
            inter = b_col + m_prev
            m_t = jnp.maximum(inter, jnp.max(dmat, axis=-1, keepdims=True))
            decay = jnp.exp(dmat - m_t)
            qk = _dot_nt(qc * hmb, kc) * decay
            num = num + _dot(qk.astype(BF16), vc * hmb)
            w_inter = w_inter + jnp.exp(inter - m_t) * hm
            den_intra = den_intra + jnp.sum(qk, axis=-1, keepdims=True) * hm
            floor = floor + jnp.exp(-m_t) * hm

            a_col = b_end - b_col + i_col
            m_loc = jnp.max(a_col, axis=0, keepdims=True)
            m_new = jnp.maximum(b_end + m_prev, m_loc)
            w_key = w_key + jnp.exp(a_col - m_new) * hm
            sp_row = sp_row + jnp.exp(b_end + m_prev - m_new) * hm
            mst[sr:sr + 1, :] = jnp.broadcast_to(m_new, (1, LANE))

        c_prev = cst[d]
        n_prev = nst[d:d + 1, :]
        q_c = _dot(qc, c_prev.astype(BF16))
        q_n = _dot((qc.astype(F32) * n_prev).astype(BF16), hsum)
        num = num + w_inter * q_c
        den = den_intra + w_inter * q_n
        h_out = num / jnp.maximum(jnp.abs(den), floor)

        kw = kc.astype(F32) * w_key
        cst[d] = c_prev * sp_row + _dot_tn(kw.astype(BF16), vc) * bd
        nst[d:d + 1, :] = n_prev * sp_row + jnp.sum(kw, axis=0, keepdims=True)
        return h_out

    def body(c, carry):
        rf = pl.multiple_of(c * T, T)
        rb = pl.multiple_of((nc - 1 - c) * T, T)
        hfw[pl.ds(rf, T), :] = chunk(0, rf)
        hbw[pl.ds(rb, T), :] = chunk(1, rb)
        return carry

    lax.fori_loop(0, nc, body, 0)
    o_ref[...] = (hfw[...] + hbw[...]).astype(BF16)


def _mlstm(h_main, gates, conv_w, conv_b, gate_b, consts, B, L):
    ltri, utri, bd, hsum = consts
    T = MLSTM_T
    const = lambda b: (0, 0)
    return pl.pallas_call(
        functools.partial(_mlstm_kernel, L=L),
        grid=(B,),
        in_specs=[
            pl.BlockSpec((L, GROUP_W), lambda b: (b, BLK_DQ)),
            pl.BlockSpec((L, GROUP_W), lambda b: (b, BLK_DK)),
            pl.BlockSpec((L, GROUP_W), lambda b: (b, BLK_DV)),
            pl.BlockSpec((L, LANE), lambda b: (b, 0)),
            pl.BlockSpec((3, 2 * GROUP_W), const),
            pl.BlockSpec((1, 2 * GROUP_W), const),
            pl.BlockSpec((1, LANE), const),
            pl.BlockSpec((T, T), const),
            pl.BlockSpec((T, T), const),
            pl.BlockSpec((GROUP_W, GROUP_W), const),
            pl.BlockSpec((GROUP_W, GROUP_W), const),
        ],
        out_specs=pl.BlockSpec((L, GROUP_W), lambda b: (b, 0)),
        out_shape=jax.ShapeDtypeStruct((B * L, GROUP_W), BF16),
        scratch_shapes=[
            pltpu.VMEM((L + 2 * CONV_HALO, GROUP_W), BF16),
            pltpu.VMEM((L + 2 * CONV_HALO, GROUP_W), BF16),
            pltpu.VMEM((L, GROUP_W), BF16),
            pltpu.VMEM((L, GROUP_W), BF16),
            pltpu.VMEM((L, GROUP_W), F32),
            pltpu.VMEM((L, GROUP_W), F32),
            pltpu.VMEM((2, GROUP_W, GROUP_W), F32),
            pltpu.VMEM((8, GROUP_W), F32),
            pltpu.VMEM((8, LANE), F32),
        ],
        compiler_params=_cparams(("parallel",)),
        name="mlstm",
    )(h_main, h_main, h_main, gates, conv_w, conv_b, gate_b, ltri, utri, bd, hsum)


def _hy_dims(L):
    h1 = L // HY_N2
    k1h = h1 + 1
    k1p = -(-k1h // BF16_SUBLANES) * BF16_SUBLANES
    kg = 12 if k1p % 12 == 0 else BF16_SUBLANES
    return h1, k1h, k1p, kg


def _hy_tables(L):
    h1, k1h, k1p, _ = _hy_dims(L)
    n1_len = 2 * h1
    n = 2 * L
    k1 = np.arange(k1p)[:, None].astype(np.float64)
    live = (np.arange(k1p) < k1h)[:, None]
    n1 = np.arange(h1)[None, :].astype(np.float64)
    ang = 2.0 * np.pi * k1 * n1 / n1_len
    m1 = np.concatenate([np.where(live, np.cos(ang), 0.0), np.where(live, -np.sin(ang), 0.0)], 0)

    n2 = np.arange(HY_N2)[None, None, :].astype(np.float64)
    k2 = np.arange(HY_N2)[None, :, None].astype(np.float64)
    kk = np.arange(k1p)[:, None, None] + n1_len * k2
    th = 2.0 * np.pi * kk * n2 / n
    fr, fi = np.cos(th), -np.sin(th)
    f3 = np.concatenate([np.concatenate([fr, -fi], 2), np.concatenate([fi, fr], 2)], 1)
    er, ei = np.transpose(np.cos(th), (0, 2, 1)), np.transpose(np.sin(th), (0, 2, 1))
    f3i = np.concatenate([np.concatenate([er, -ei], 2), np.concatenate([ei, er], 2)], 1)
    live3 = (np.arange(k1p) < k1h)[:, None, None]
    f3 = np.where(live3, f3, 0.0)
    f3i = np.where(live3, f3i, 0.0)

    nn1 = np.arange(h1)[:, None].astype(np.float64)
    kc = np.arange(k1p)[None, :].astype(np.float64)
    ph = 2.0 * np.pi * nn1 * kc / n1_len
    edge = (np.arange(k1p) == 0) | (np.arange(k1p) == h1)
    livec = (np.arange(k1p) < k1h)[None, :]
    m4r = np.where(livec, np.where(edge[None, :], np.cos(ph), 2.0 * np.cos(ph)), 0.0) / n
    m4i = np.where(livec & ~edge[None, :], -2.0 * np.sin(ph), 0.0) / n
    return (jnp.asarray(m1, BF16), jnp.asarray(f3, BF16), jnp.asarray(f3i, BF16),
            jnp.asarray(m4r, BF16), jnp.asarray(m4i, BF16))


def _hy_stage1(z_bf16, m1_ref, a_ref, k1p):
    a = _dot(m1_ref[...], z_bf16)
    a_ref[0, 0] = a[0:k1p].astype(BF16)
    a_ref[0, 1] = a[k1p:2 * k1p].astype(BF16)


def _hy_conv3_wide(x, w_ref, b_ref, j, h1):
    c = GROUP_W
    wl = HY_N2 * c
    rowi = lax.broadcasted_iota(jnp.int32, (h1, c), 0)
    tail = x[:, wl - c:wl]
    head = x[:, 0:c]
    prev_tail = jnp.where(rowi == 0, 0.0, pltpu.roll(tail, 1, axis=0))
    next_head = jnp.where(rowi == h1 - 1, 0.0, pltpu.roll(head, h1 - 1, axis=0))
    xm = jnp.concatenate([prev_tail, x[:, 0:wl - c]], axis=1)
    xp = jnp.concatenate([x[:, c:wl], next_head], axis=1)
    return (xm * w_ref[3 * j:3 * j + 1, :] + x * w_ref[3 * j + 1:3 * j + 2, :]
            + xp * w_ref[3 * j + 2:3 * j + 3, :] + b_ref[j:j + 1, :])


def _hy_front_kernel(v_ref, x1_ref, x2_ref, w_ref, b_ref, m1_ref,
                     z_ref, x1c_ref, x2c_ref, a_ref, *, h1, k1p):
    z = _hy_conv3_wide(v_ref[0].astype(F32), w_ref, b_ref, 0, h1).astype(BF16)
    z_ref[0] = z
    x1c_ref[0] = _hy_conv3_wide(x1_ref[0].astype(F32), w_ref, b_ref, 1, h1).astype(BF16)
    x2c_ref[0] = _hy_conv3_wide(x2_ref[0].astype(F32), w_ref, b_ref, 2, h1).astype(BF16)
    _hy_stage1(z, m1_ref, a_ref, k1p)


def _hy_front(cv, cx1, cx2, w_wide, b_wide, m1, B, L):
    h1, _, k1p, _ = _hy_dims(L)
    wl = HY_N2 * GROUP_W
    seq = pl.BlockSpec((1, h1, wl), lambda b: (b, 0, 0))
    wide = jax.ShapeDtypeStruct((B, h1, wl), BF16)
    return pl.pallas_call(
        functools.partial(_hy_front_kernel, h1=h1, k1p=k1p),
        grid=(B,),
        in_specs=[seq, seq, seq,
                  pl.BlockSpec((9, wl), lambda b: (0, 0)),
                  pl.BlockSpec((3, wl), lambda b: (0, 0)),
                  pl.BlockSpec((2 * k1p, h1), lambda b: (0, 0))],
        out_specs=(seq, seq, seq, pl.BlockSpec((1, 2, k1p, wl), lambda b: (b, 0, 0, 0))),
        out_shape=(wide, wide, wide, jax.ShapeDtypeStruct((B, 2, k1p, wl), BF16)),
        compiler_params=_cparams(("parallel",)),
        name="hy_front",
    )(cv, cx1, cx2, w_wide, b_wide, m1)


def _hy_stage1_kernel(z_ref, m1_ref, a_ref, *, k1p):
    _hy_stage1(z_ref[0], m1_ref, a_ref, k1p)


def _hy_stage1_call(z_wide, m1, L):
    h1, _, k1p, _ = _hy_dims(L)
    wl = HY_N2 * GROUP_W
    nb = z_wide.shape[0]
    return pl.pallas_call(
        functools.partial(_hy_stage1_kernel, k1p=k1p),
        grid=(nb,),
        in_specs=[pl.BlockSpec((1, h1, wl), lambda b: (b, 0, 0)),
                  pl.BlockSpec((2 * k1p, h1), lambda b: (0, 0))],
        out_specs=pl.BlockSpec((1, 2, k1p, wl), lambda b: (b, 0, 0, 0)),
        out_shape=jax.ShapeDtypeStruct((nb, 2, k1p, wl), BF16),
        compiler_params=_cparams(("parallel",)),
        name="hy_stage1",
    )(z_wide, m1)


def _hy_spectrum_kernel(af_ref, ab_ref, f3_ref, g_ref, *, kg, k1h):
    grp = pl.program_id(0)

    @pl.when(grp * kg < k1h)
    def _():
        for kk in range(kg):
            xs = []
            for a_ref in (af_ref, ab_ref):
                a2 = jnp.concatenate([a_ref[0, 0, kk], a_ref[0, 1, kk]], axis=0)
                xs.append(_dot(f3_ref[kk], a2))
            g_ref[0, 0, kk] = xs[0][0:HY_N2] + xs[1][0:HY_N2]
            g_ref[0, 1, kk] = xs[0][HY_N2:2 * HY_N2] - xs[1][HY_N2:2 * HY_N2]

    @pl.when(grp * kg >= k1h)
    def _():
        g_ref[...] = jnp.zeros(g_ref.shape, F32)


def _hy_spectrum(a_filt, f3, L):
    _, k1h, k1p, kg = _hy_dims(L)
    a5 = a_filt.reshape(2 * HYENA_ORDER, 2, k1p, HY_N2, GROUP_W)
    blk = (1, 2, kg, HY_N2, GROUP_W)
    return pl.pallas_call(
        functools.partial(_hy_spectrum_kernel, kg=kg, k1h=k1h),
        grid=(k1p // kg, HYENA_ORDER),
        in_specs=[pl.BlockSpec(blk, lambda g, o: (2 * o, 0, g, 0, 0)),
                  pl.BlockSpec(blk, lambda g, o: (2 * o + 1, 0, g, 0, 0)),
                  pl.BlockSpec((kg, 2 * HY_N2, 2 * HY_N2), lambda g, o: (g, 0, 0))],
        out_specs=pl.BlockSpec(blk, lambda g, o: (o, 0, g, 0, 0)),
        out_shape=jax.ShapeDtypeStruct((HYENA_ORDER, 2, k1p, HY_N2, GROUP_W), F32),
        compiler_params=_cparams(("parallel", "parallel")),
        name="hy_spectrum",
    )(a5, a5, f3)


def _hy_mid_kernel(a_ref, g_ref, f3_ref, f3i_ref, b_ref, *, kg, k1h):
    grp = pl.program_id(0)

    @pl.when(grp * kg < k1h)
    def _():
        for kk in range(kg):
            a2 = jnp.concatenate([a_ref[0, 0, kk], a_ref[0, 1, kk]], axis=0)
            x = _dot(f3_ref[kk], a2)
            xr, xi = x[0:HY_N2], x[HY_N2:2 * HY_N2]
            gr, gi = g_ref[0, 0, kk], g_ref[0, 1, kk]
            y2 = jnp.concatenate([xr * gr - xi * gi, xr * gi + xi * gr], axis=0).astype(BF16)
            bm = _dot(f3i_ref[kk], y2)
            b_ref[0, 0, kk] = bm[0:HY_N2].astype(BF16)
            b_ref[0, 1, kk] = bm[HY_N2:2 * HY_N2].astype(BF16)

    @pl.when(grp * kg >= k1h)
    def _():
        b_ref[...] = jnp.zeros(b_ref.shape, BF16)


def _hy_mid(a, g_spec, order, f3, f3i, B, L):
    _, k1h, k1p, kg = _hy_dims(L)
    a5 = a.reshape(B, 2, k1p, HY_N2, GROUP_W)
    blk = (1, 2, kg, HY_N2, GROUP_W)
    tab = pl.BlockSpec((kg, 2 * HY_N2, 2 * HY_N2), lambda g, b: (g, 0, 0))
    out = pl.pallas_call(
        functools.partial(_hy_mid_kernel, kg=kg, k1h=k1h),
        grid=(k1p // kg, B),
        in_specs=[pl.BlockSpec(blk, lambda g, b: (b, 0, g, 0, 0)),
                  pl.BlockSpec(blk, lambda g, b: (order, 0, g, 0, 0)),
                  tab, tab],
        out_specs=pl.BlockSpec(blk, lambda g, b: (b, 0, g, 0, 0)),
        out_shape=jax.ShapeDtypeStruct((B, 2, k1p, HY_N2, GROUP_W), BF16),
        compiler_params=_cparams(("parallel", "parallel")),
        name="hy_mid",
    )(a5, g_spec, f3, f3i)
    return out.reshape(B, 2, k1p, HY_N2 * GROUP_W)


def _hy_back_kernel(b_ref, z_ref, x_ref, bias_ref, m4r_ref, m4i_ref, *rest, k1p, last):
    y = _dot(m4r_ref[...], b_ref[0, 0]) + _dot(m4i_ref[...], b_ref[0, 1])
    z_new = x_ref[0].astype(F32) * (y + z_ref[0].astype(F32) * bias_ref[...])
    zb = z_new.astype(BF16)
    if last:
        (o_ref,) = rest
        o_ref[0] = zb
    else:
        m1_ref, o_ref, a_ref = rest
        o_ref[0] = zb
        _hy_stage1(zb, m1_ref, a_ref, k1p)


def _hy_back(b, z, xg, bias_wide, m4r, m4i, m1, B, L, last):
    h1, _, k1p, _ = _hy_dims(L)
    wl = HY_N2 * GROUP_W
    seq = pl.BlockSpec((1, h1, wl), lambda i: (i, 0, 0))
    spec4 = pl.BlockSpec((1, 2, k1p, wl), lambda i: (i, 0, 0, 0))
    const = lambda i: (0, 0)
    in_specs = [spec4, seq, seq, pl.BlockSpec((1, wl), const),
                pl.BlockSpec((h1, k1p), const), pl.BlockSpec((h1, k1p), const)]
    args = [b, z, xg, bias_wide, m4r, m4i]
    wide = jax.ShapeDtypeStruct((B, h1, wl), BF16)
    if last:
        out_specs, out_shape = seq, wide
    else:
        in_specs.append(pl.BlockSpec((2 * k1p, h1), const))
        args.append(m1)
        out_specs = (seq, spec4)
        out_shape = (wide, jax.ShapeDtypeStruct((B, 2, k1p, wl), BF16))
    return pl.pallas_call(
        functools.partial(_hy_back_kernel, k1p=k1p, last=last),
        grid=(B,),
        in_specs=in_specs,
        out_specs=out_specs,
        out_shape=out_shape,
        compiler_params=_cparams(("parallel",)),
        name="hy_back_last" if last else "hy_back",
    )(*args)


def _hy_filter_kernel(f_ref, t_ref, w1_ref, b1_ref, w2_ref, b2_ref, w3_ref, fr_ref, ad_ref,
                      o_ref, *, tile):
    i = pl.program_id(0)
    freq = fr_ref[...]
    z = jnp.sin(freq * (_dot_f32(f_ref[...], w1_ref[...]) + b1_ref[...]))
    z = jnp.sin(freq * (_dot_f32(z, w2_ref[...]) + b2_ref[...]))
    decay = jnp.exp(-t_ref[...] * ad_ref[...])
    rowi = i * tile + lax.broadcasted_iota(jnp.int32, (tile, GROUP_W), 0)
    for j in range(2 * HYENA_ORDER):
        hj = _dot_f32(z, w3_ref[:, j * GROUP_W:(j + 1) * GROUP_W]) * decay
        if j % 2 == 1:
            hj = jnp.where(rowi == 0, 0.0, hj)
        o_ref[j] = hj.astype(BF16)


def _hy_filter(feats, tcol, w1, b1, w2, b2, w3, freq, absdelta, L):
    tile = math.gcd(L, ROW_TILE)
    const = lambda i: (0, 0)
    return pl.pallas_call(
        functools.partial(_hy_filter_kernel, tile=tile),
        grid=(L // tile,),
        in_specs=[pl.BlockSpec((tile, LANE), lambda i: (i, 0)),
                  pl.BlockSpec((tile, 1), lambda i: (i, 0)),
                  pl.BlockSpec((LANE, HYENA_HIDDEN), const),
                  pl.BlockSpec((1, HYENA_HIDDEN), const),
                  pl.BlockSpec((HYENA_HIDDEN, HYENA_HIDDEN), const),
                  pl.BlockSpec((1, HYENA_HIDDEN), const),
                  pl.BlockSpec((HYENA_HIDDEN, 2 * HYENA_ORDER * GROUP_W), const),
                  pl.BlockSpec((1, HYENA_HIDDEN), const),
                  pl.BlockSpec((1, GROUP_W), const)],
        out_specs=pl.BlockSpec((2 * HYENA_ORDER, tile, GROUP_W), lambda i: (0, i, 0)),
        out_shape=jax.ShapeDtypeStruct((2 * HYENA_ORDER, L, GROUP_W), BF16),
        compiler_params=_cparams(("parallel",)),
        name="hy_filter",
    )(feats, tcol, w1, b1, w2, b2, w3, freq, absdelta)


def _outproj_kernel(x_ref, oa_ref, ob_ref, oc_ref, od_ref, ox_ref,
                    ga_ref, gb_ref, gc_ref, gdo_ref, gdg_ref, gx_ref, w_ref, png_ref, out_ref):
    f = lambda r: r[...].astype(F32)
    branches = (
        f(oa_ref) * _silu(f(ga_ref)),
        f(ob_ref) * _silu(f(gb_ref)),
        f(oc_ref) * _silu(f(gc_ref)),
        f(od_ref) * _sigmoid(f(gdo_ref)) * _silu(f(gdg_ref)),
        f(ox_ref) * _silu(f(gx_ref)),
    )
    y = jnp.zeros(x_ref.shape, F32)
    for j, br in enumerate(branches):
        y = y + _dot(br.astype(BF16), w_ref[j * GROUP_W:(j + 1) * GROUP_W, :])
    ms = jnp.mean(y * y, axis=-1, keepdims=True)
    out_ref[...] = x_ref[...] + y * lax.rsqrt(ms + NORM_EPS) * png_ref[...]


def _outproj(x2d, outs, h_main, w_out, png):
    n_tok = x2d.shape[0]
    nt = n_tok // ROW_TILE
    row = lambda i: (i, 0)
    blk = lambda j: pl.BlockSpec((ROW_TILE, GROUP_W), lambda i: (i, j))
    return pl.pallas_call(
        _outproj_kernel,
        grid=(nt,),
        in_specs=[pl.BlockSpec((ROW_TILE, D_MODEL), row)]
        + [pl.BlockSpec((ROW_TILE, GROUP_W), row)] * 5
        + [blk(BLK_AG), blk(BLK_BG), blk(BLK_CG), blk(BLK_DO), blk(BLK_DG), blk(BLK_XG)]
        + [pl.BlockSpec((5 * GROUP_W, D_MODEL), lambda i: (0, 0)),
           pl.BlockSpec((1, D_MODEL), lambda i: (0, 0))],
        out_specs=pl.BlockSpec((ROW_TILE, D_MODEL), row),
        out_shape=jax.ShapeDtypeStruct((n_tok, D_MODEL), F32),
        compiler_params=_cparams(("parallel",)),
        name="outproj",
    )(x2d, *outs, h_main, h_main, h_main, h_main, h_main, h_main, w_out, png)


def _rope_tables(L, group, rot_dim):
    half = rot_dim // 2
    inv = 1.0 / (ROPE_THETA ** (jnp.arange(0, rot_dim, 2, dtype=F32) / rot_dim))
    ang = jnp.arange(L, dtype=F32)[:, None] * inv[None, :]
    cos, sin = jnp.cos(ang), jnp.sin(ang)
    lane = np.arange(GROUP_W) % group
    in_rot = lane < rot_dim
    idx = lane % half
    c = jnp.where(in_rot[None, :], cos[:, idx], 1.0)
    s = jnp.where(in_rot[None, :], sin[:, idx], 0.0)
    p = np.zeros((GROUP_W, GROUP_W), np.float32)
    for j in range(GROUP_W):
        if lane[j] < half:
            p[j + half, j] = -1.0
        elif lane[j] < rot_dim:
            p[j - half, j] = 1.0
    return c, s, jnp.asarray(p, BF16)


def _hyena_features(L):
    t = jnp.linspace(0.0, 1.0, L, dtype=F32)[:, None]
    bands = jnp.linspace(1e-4, HYENA_BANDS - 1, HYENA_BANDS, dtype=F32)
    ang = (2.0 * math.pi / L) * jnp.arange(L, dtype=F32)[:, None] * bands[None, :]
    feats = jnp.concatenate([t, jnp.cos(ang), -jnp.sin(ang)], axis=-1)
    return jnp.pad(feats, ((0, 0), (0, LANE - HYENA_EMB))), t


def _relayout_w_in(w):
    g = GROUP_W
    off_c, off_d = 8 * g, 12 * g
    off_gate = off_d + 5 * g
    off_x = off_gate + N_MLSTM_GATES
    main = jnp.concatenate([w[:, 0:off_c], w[:, off_c + 3 * g:off_c + 4 * g],
                            w[:, off_d:off_gate], w[:, off_x:off_x + 2 * g]], axis=1)
    hy = w[:, off_c:off_c + 3 * g]
    gate = jnp.pad(w[:, off_gate:off_x], ((0, 0), (0, LANE - N_MLSTM_GATES)))
    return jnp.concatenate([main, hy, gate], axis=1).astype(BF16)


def _trunk(x, mem, pre_norm_g, post_norm_g, w_in, w_out, diff_lambda, diff_subln_g,
           hy_conv_w, hy_conv_b, hy_ffn_w1, hy_ffn_b1, hy_ffn_w2, hy_ffn_b2, hy_ffn_w3,
           hy_freq, hy_bias, ml_conv_w, ml_conv_b, ml_gate_b, mem_norm_g, w_mem_kv):
    B, L, _ = x.shape
    M = mem.shape[1]
    depth = w_in.shape[0]
    g = GROUP_W
    h1, _, k1p, _ = _hy_dims(L)
    wl = HY_N2 * g

    ca, sa, pa = _rope_tables(L, HEAD_DIM, HEAD_DIM // ROPE_FRACTION)
    cb, sb, pb = _rope_tables(L, DIFF_QK_DIM, DIFF_QK_DIM // ROPE_FRACTION)
    rope_tabs = (ca, sa, cb, sb, pa, pb)
    dil_bias = jnp.asarray(_dil_bias_table())
    feats, tcol = _hyena_features(L)
    absdelta = jnp.abs(jnp.linspace(math.log(HYENA_TARGET) / HYENA_SLOW_DECAY,
                                    math.log(HYENA_TARGET) / HYENA_FAST_DECAY, g, dtype=F32))[None]
    m1, f3, f3i, m4r, m4i = _hy_tables(L)
    head_of = np.arange(g) // HEAD_DIM
    same_head = (head_of[:, None] == head_of[None, :]).astype(np.float32)
    hmean = jnp.asarray(same_head / HEAD_DIM, BF16)
    hsum = jnp.asarray(same_head, BF16)
    bd = jnp.asarray(same_head, F32)
    tri = np.tril(np.ones((MLSTM_T, MLSTM_T), np.float32))
    ltri, utri = jnp.asarray(tri, BF16), jnp.asarray(tri.T, BF16)

    x2d = x.reshape(B * L, D_MODEL)
    mem2d = mem.reshape(B * M, D_MODEL)
    for li in range(depth):
        lam_init = 0.8 - 0.6 * math.exp(-0.3 * li)
        w_all = _relayout_w_in(w_in[li])
        h_main, cv, cx1, cx2, gates = _inproj(x2d, pre_norm_g[li][None], w_all, rope_tabs, B, L)

        oa = _dilattn(h_main, dil_bias, B, L)
        ob = _diffattn(h_main, diff_lambda[li], jnp.tile(diff_subln_g[li], N_HEADS)[None],
                       hmean, lam_init, B, L)

        w1 = jnp.pad(hy_ffn_w1[li], ((0, LANE - HYENA_EMB), (0, 0)))
        hfilt = _hy_filter(feats, tcol, w1, hy_ffn_b1[li][None], hy_ffn_w2[li],
                           hy_ffn_b2[li][None], hy_ffn_w3[li], hy_freq[li][None], absdelta, L)
        a_filt = _hy_stage1_call(hfilt.reshape(2 * HYENA_ORDER, h1, wl), m1, L)
        g_spec = _hy_spectrum(a_filt, f3, L)
        cw = jnp.tile(hy_conv_w[li].reshape(3, 3, g).transpose(1, 0, 2).reshape(9, g), (1, HY_N2))
        cbw = jnp.tile(hy_conv_b[li].reshape(3, g), (1, HY_N2))
        z, x1c, x2c, a = _hy_front(cv.reshape(B, h1, wl), cx1.reshape(B, h1, wl),
                                   cx2.reshape(B, h1, wl), cw, cbw, m1, B, L)
        bias_w = jnp.tile(hy_bias[li], (1, HY_N2))
        bsp = _hy_mid(a, g_spec, 0, f3, f3i, B, L)
        z, a = _hy_back(bsp, z, x1c, bias_w[0:1], m4r, m4i, m1, B, L, last=False)
        bsp = _hy_mid(a, g_spec, 1, f3, f3i, B, L)
        oc = _hy_back(bsp, z, x2c, bias_w[1:2], m4r, m4i, m1, B, L, last=True).reshape(B * L, g)

        gate_b = jnp.pad(ml_gate_b[li], (0, LANE - N_MLSTM_GATES))[None]
        od = _mlstm(h_main, gates, ml_conv_w[li], ml_conv_b[li][None], gate_b,
                    (ltri, utri, bd, hsum), B, L)

        mkv = _memkv(mem2d, mem_norm_g[li][None], w_mem_kv[li].astype(BF16))
        ox = _memattn(h_main, mkv, B, L, M)

        x2d = _outproj(x2d, (oa, ob, oc, od, ox), h_main, w_out[li].astype(BF16),
                       post_norm_g[li][None])
    return x2d.reshape(B, L, D_MODEL)


def kernel(x_prompt, x_sample, mem_prompt, mem_sample, pre_norm_g, post_norm_g, w_in, w_out,
           diff_lambda, diff_subln_g, hy_conv_w, hy_conv_b, hy_ffn_w1, hy_ffn_b1, hy_ffn_w2,
           hy_ffn_b2, hy_ffn_w3, hy_freq, hy_bias, ml_conv_w, ml_conv_b, ml_gate_b,
           mem_norm_g, w_mem_kv):
    nb = x_prompt.shape[0]
    x = jnp.concatenate([x_prompt, x_sample], axis=0)
    mem = jnp.concatenate([mem_prompt, mem_sample], axis=0)
    y = _trunk(x, mem, pre_norm_g, post_norm_g, w_in, w_out, diff_lambda, diff_subln_g,
               hy_conv_w, hy_conv_b, hy_ffn_w1, hy_ffn_b1, hy_ffn_w2, hy_ffn_b2, hy_ffn_w3,
               hy_freq, hy_bias, ml_conv_w, ml_conv_b, ml_gate_b, mem_norm_g, w_mem_kv)
    return y[:nb], y[nb:]
```

```python
import functools
import math

import numpy as np
import jax
import jax.numpy as jnp
from jax import lax
from jax.experimental import pallas as pl
from jax.experimental.pallas import tpu as pltpu

F32 = jnp.float32
BF16 = jnp.bfloat16

D_MODEL = 1024
HEAD_DIM = 64
GROUP_W = 256
N_HEADS = GROUP_W // HEAD_DIM
NORM_EPS = 1e-6
NEG_INF = -1e30
ROPE_THETA = 500000.0
ROPE_FRACTION = 4
DIL_PATTERNS = ((128, 1), (512, 4), (2048, 16))
DIFF_QK_DIM = HEAD_DIM // 2
DIFF_SUBLN_EPS = 1e-5
HYENA_ORDER = 2
HYENA_BANDS = 16
HYENA_EMB = 1 + 2 * HYENA_BANDS
HYENA_HIDDEN = 64
HYENA_FAST_DECAY = 0.3
HYENA_SLOW_DECAY = 1.5
HYENA_TARGET = 1e-2
N_MLSTM_GATES = 4 * N_HEADS

(BLK_AQ, BLK_AK, BLK_AV, BLK_AG, BLK_BQ, BLK_BK, BLK_BV, BLK_BG, BLK_CG,
 BLK_DQ, BLK_DK, BLK_DV, BLK_DO, BLK_DG, BLK_XQ, BLK_XG) = range(16)
N_MAIN_BLK = 16
MAIN_W = N_MAIN_BLK * GROUP_W
W_ALL = MAIN_W + 4 * GROUP_W

VMEM_LIMIT_BYTES = 56 * 1024 * 1024
LANE = 128
BF16_SUBLANES = 16

ROW_TILE = 512
DIL_Q_TILE = 256
DIL_PAD = 1024
DIFF_Q_TILE = 256
DIFF_K_CHUNK = 512
DIFF_VT_ROWS = HEAD_DIM + BF16_SUBLANES
LOG2E = 1.4426950408889634
MEM_Q_TILE = 512
MLSTM_T = 128
CONV_HALO = 16
HY_N2 = 128


def _cparams(sem, vmem=VMEM_LIMIT_BYTES):
    return pltpu.CompilerParams(dimension_semantics=sem, vmem_limit_bytes=vmem)


def _sigmoid(v):
    return 1.0 / (1.0 + jnp.exp(-v))


def _silu(v):
    return v * _sigmoid(v)


def _dot(a, b):
    return jnp.dot(a, b, preferred_element_type=F32)


def _dot_nt(a, b):
    return lax.dot_general(a, b, (((1,), (1,)), ((), ())), preferred_element_type=F32)


def _dot_tn(a, b):
    return lax.dot_general(a, b, (((0,), (0,)), ((), ())), preferred_element_type=F32)


def _head_mask(h, width, dtype):
    lane = lax.broadcasted_iota(jnp.int32, (1, GROUP_W), 1)
    return ((lane // width) == h).astype(dtype)


def _inproj_kernel(x_ref, g_ref, w_ref, ca_ref, sa_ref, cb_ref, sb_ref, pa_ref, pb_ref,
                   h_ref, cv_ref, cx1_ref, cx2_ref, gi_ref, gf_ref):
    x = x_ref[...]
    ms = jnp.mean(x * x, axis=-1, keepdims=True)
    xn = (x * lax.rsqrt(ms + NORM_EPS) * g_ref[...]).astype(BF16)

    def proj(j, width=GROUP_W):
        return _dot(xn, w_ref[:, j * GROUP_W:j * GROUP_W + width])

    def rope(acc, c_ref, s_ref, p_ref):
        partner = _dot(acc.astype(BF16), p_ref[...])
        return acc * c_ref[...] + partner * s_ref[...]

    for j in range(N_MAIN_BLK):
        acc = proj(j)
        if j in (BLK_AQ, BLK_AK):
            acc = rope(acc, ca_ref, sa_ref, pa_ref)
        if j in (BLK_BQ, BLK_BK):
            acc = rope(acc, cb_ref, sb_ref, pb_ref)
        if j in (BLK_AQ, BLK_XQ):
            acc = acc * (LOG2E / math.sqrt(HEAD_DIM))
        if j == BLK_BQ:
            acc = acc * (LOG2E / math.sqrt(DIFF_QK_DIM))
        h_ref[:, j * GROUP_W:(j + 1) * GROUP_W] = acc.astype(BF16)
    cv_ref[...] = proj(N_MAIN_BLK).astype(BF16)
    cx1_ref[...] = proj(N_MAIN_BLK + 1).astype(BF16)
    cx2_ref[...] = proj(N_MAIN_BLK + 2).astype(BF16)
    gates = proj(N_MAIN_BLK + 3)
    gi_ref[...] = gates[:, 0:LANE]
    gf_ref[...] = gates[:, LANE:2 * LANE]


def _inproj(x2d, g, w_all, rope_tabs, B, L):
    ca, sa, cb, sb, pa, pb = rope_tabs
    nt = L // ROW_TILE
    n_tok = B * L
    row = lambda i, b: (b * nt + i, 0)
    tab = lambda i, b: (i, 0)
    const = lambda i, b: (0, 0)
    out_shapes = (
        jax.ShapeDtypeStruct((n_tok, MAIN_W), BF16),
        jax.ShapeDtypeStruct((n_tok, GROUP_W), BF16),
        jax.ShapeDtypeStruct((n_tok, GROUP_W), BF16),
        jax.ShapeDtypeStruct((n_tok, GROUP_W), BF16),
        jax.ShapeDtypeStruct((n_tok, LANE), F32),
        jax.ShapeDtypeStruct((n_tok, LANE), F32),
    )
    return pl.pallas_call(
        _inproj_kernel,
        grid=(nt, B),
        in_specs=[
            pl.BlockSpec((ROW_TILE, D_MODEL), row),
            pl.BlockSpec((1, D_MODEL), const),
            pl.BlockSpec((D_MODEL, W_ALL), const),
            pl.BlockSpec((ROW_TILE, GROUP_W), tab),
            pl.BlockSpec((ROW_TILE, GROUP_W), tab),
            pl.BlockSpec((ROW_TILE, GROUP_W), tab),
            pl.BlockSpec((ROW_TILE, GROUP_W), tab),
            pl.BlockSpec((GROUP_W, GROUP_W), const),
            pl.BlockSpec((GROUP_W, GROUP_W), const),
        ],
        out_specs=(
            pl.BlockSpec((ROW_TILE, MAIN_W), row),
            pl.BlockSpec((ROW_TILE, GROUP_W), row),
            pl.BlockSpec((ROW_TILE, GROUP_W), row),
            pl.BlockSpec((ROW_TILE, GROUP_W), row),
            pl.BlockSpec((ROW_TILE, LANE), row),
            pl.BlockSpec((ROW_TILE, LANE), row),
        ),
        out_shape=out_shapes,
        compiler_params=_cparams(("parallel", "parallel")),
        name="inproj",
    )(x2d, g, w_all, ca, sa, cb, sb, pa, pb)


def _memkv_kernel(m_ref, g_ref, w_ref, o_ref):
    x = m_ref[...]
    ms = jnp.mean(x * x, axis=-1, keepdims=True)
    xn = (x * lax.rsqrt(ms + NORM_EPS) * g_ref[...]).astype(BF16)
    o_ref[...] = _dot(xn, w_ref[...]).astype(BF16)


def _memkv(mem2d, g, w):
    rows = mem2d.shape[0]
    tile = math.gcd(rows, ROW_TILE)
    return pl.pallas_call(
        _memkv_kernel,
        grid=(rows // tile,),
        in_specs=[
            pl.BlockSpec((tile, D_MODEL), lambda i: (i, 0)),
            pl.BlockSpec((1, D_MODEL), lambda i: (0, 0)),
            pl.BlockSpec((D_MODEL, 2 * GROUP_W), lambda i: (0, 0)),
        ],
        out_specs=pl.BlockSpec((tile, 2 * GROUP_W), lambda i: (i, 0)),
        out_shape=jax.ShapeDtypeStruct((rows, 2 * GROUP_W), BF16),
        compiler_params=_cparams(("parallel",)),
        name="memkv",
    )(mem2d, g, w)


def _memattn_kernel(q_ref, mk_ref, mv_ref, o_ref):
    q = q_ref[...]
    mk = mk_ref[...]
    mv = mv_ref[...]
    acc = jnp.zeros(q.shape, F32)
    for h in range(N_HEADS):
        hm = _head_mask(h, HEAD_DIM, BF16)
        s = _dot_nt(q * hm, mk)
        m = jnp.max(s, axis=-1, keepdims=True)
        p = jnp.exp2(s - m)
        l = jnp.sum(p, axis=-1, keepdims=True)
        acc = acc + _dot(p.astype(BF16), mv * hm) * (1.0 / l)
    o_ref[...] = acc.astype(BF16)


def _memattn(h_main, mkv, B, L, M):
    nq = L // MEM_Q_TILE
    return pl.pallas_call(
        _memattn_kernel,
        grid=(B, nq),
        in_specs=[
            pl.BlockSpec((MEM_Q_TILE, GROUP_W), lambda b, i: (b * nq + i, BLK_XQ)),
            pl.BlockSpec((M, GROUP_W), lambda b, i: (b, 0)),
            pl.BlockSpec((M, GROUP_W), lambda b, i: (b, 1)),
        ],
        out_specs=pl.BlockSpec((MEM_Q_TILE, GROUP_W), lambda b, i: (b * nq + i, 0)),
        out_shape=jax.ShapeDtypeStruct((B * L, GROUP_W), BF16),
        compiler_params=_cparams(("parallel", "parallel")),
        name="memattn",
    )(h_main, mkv, mkv)


def _dil_bias_table():
    w = DIL_Q_TILE + 2 * DIL_PAD
    d = np.arange(DIL_Q_TILE)[:, None] - np.arange(w)[None, :] + DIL_PAD
    count = np.zeros(d.shape, np.float64)
    for win, dil in DIL_PATTERNS:
        reach = (win // (2 * dil)) * dil
        count += (d % dil == 0) & (np.abs(d) <= reach)
    return np.where(count > 0, np.log2(np.maximum(count, 1.0)), NEG_INF).astype(np.float32)


def _dilattn_kernel(q_ref, k_ref, v_ref, bias_ref, o_ref, kpad, vpad, *, L):
    i = pl.program_id(1)
    w = DIL_Q_TILE + 2 * DIL_PAD

    @pl.when(i == 0)
    def _():
        zeros = jnp.zeros((DIL_PAD, GROUP_W), BF16)
        for pad, src in ((kpad, k_ref), (vpad, v_ref)):
            pad[0:DIL_PAD, :] = zeros
            pad[DIL_PAD + L:DIL_PAD + L + DIL_PAD, :] = zeros
            pad[DIL_PAD:DIL_PAD + L, :] = src[...]

    q0 = pl.multiple_of(i * DIL_Q_TILE, DIL_Q_TILE)
    q = q_ref[...]
    kw = kpad[pl.ds(q0, w), :]
    vw = vpad[pl.ds(q0, w), :]
    jpos = q0 - DIL_PAD + lax.broadcasted_iota(jnp.int32, (1, w), 1)
    in_seq = jnp.where(jpos >= 0, jnp.where(jpos < L, 0.0, NEG_INF), NEG_INF)
    bias = bias_ref[...] + in_seq
    acc = jnp.zeros(q.shape, F32)
    for h in range(N_HEADS):
        hm = _head_mask(h, HEAD_DIM, BF16)
        s = _dot_nt(q * hm, kw) + bias
        m = jnp.max(s, axis=-1, keepdims=True)
        p = jnp.exp2(s - m)
        l = jnp.sum(p, axis=-1, keepdims=True)
        acc = acc + _dot(p.astype(BF16), vw * hm) * (1.0 / l)
    o_ref[...] = acc.astype(BF16)


def _dilattn(h_main, bias, B, L):
    nq = L // DIL_Q_TILE
    w = DIL_Q_TILE + 2 * DIL_PAD
    return pl.pallas_call(
        functools.partial(_dilattn_kernel, L=L),
        grid=(B, nq),
        in_specs=[
            pl.BlockSpec((DIL_Q_TILE, GROUP_W), lambda b, i: (b * nq + i, BLK_AQ)),
            pl.BlockSpec((L, GROUP_W), lambda b, i: (b, BLK_AK)),
            pl.BlockSpec((L, GROUP_W), lambda b, i: (b, BLK_AV)),
            pl.BlockSpec((DIL_Q_TILE, w), lambda b, i: (0, 0)),
        ],
        out_specs=pl.BlockSpec((DIL_Q_TILE, GROUP_W), lambda b, i: (b * nq + i, 0)),
        out_shape=jax.ShapeDtypeStruct((B * L, GROUP_W), BF16),
        scratch_shapes=[pltpu.VMEM((L + 2 * DIL_PAD, GROUP_W), BF16),
                        pltpu.VMEM((L + 2 * DIL_PAD, GROUP_W), BF16)],
        compiler_params=_cparams(("parallel", "arbitrary")),
        name="dilattn",
    )(h_main, h_main, h_main, bias)


def _split3(x):
    hi = x.astype(BF16)
    r1 = x - hi.astype(F32)
    mid = r1.astype(BF16)
    lo = (r1 - mid.astype(F32)).astype(BF16)
    return hi, mid, lo


def _split_dot(x, mat):
    return sum(_dot(t, mat) for t in _split3(x))


def _tri_dot(mat, x):
    return sum(_dot(mat, t) for t in _split3(x))


def _dot_f32(a, b):
    ah, am, al = _split3(a)
    bh, bm, bl = _split3(b)
    return (_dot(ah, bh) + (_dot(ah, bm) + _dot(am, bh))
            + (_dot(ah, bl) + _dot(am, bm) + _dot(al, bh)))


def _diffattn_kernel(q_ref, k_ref, v_ref, lam_ref, g_ref, hmean_ref, o_ref, vt_ref, *,
                     lam_init, L):
    hd = HEAD_DIM

    @pl.when(pl.program_id(1) == 0)
    def _():
        tail = (lax.broadcasted_iota(jnp.int32, (DIFF_VT_ROWS - hd, DIFF_K_CHUNK), 0) == 0)
        for c in range(L // DIFF_K_CHUNK):
            lo, hi = c * DIFF_K_CHUNK, (c + 1) * DIFF_K_CHUNK
            vt = v_ref[lo:hi, :].astype(F32).T.astype(BF16)
            for h in range(N_HEADS):
                vt_ref[h, 0:hd, lo:hi] = vt[h * hd:(h + 1) * hd]
                vt_ref[h, hd:DIFF_VT_ROWS, lo:hi] = tail.astype(BF16)

    lp = lam_ref[...]
    lam = (jnp.exp(jnp.sum(lp[0:1] * lp[1:2], axis=-1, keepdims=True))
           - jnp.exp(jnp.sum(lp[2:3] * lp[3:4], axis=-1, keepdims=True)) + lam_init)
    qt = q_ref[...].astype(F32).T
    feat_group = lax.broadcasted_iota(jnp.int32, (GROUP_W, 1), 0) // DIFF_QK_DIM
    k = k_ref[...]
    tq = qt.shape[1]
    heads = []
    for h in range(N_HEADS):
        qt2 = jnp.concatenate([jnp.where(feat_group == 2 * h + c, qt, 0.0) for c in range(2)],
                              axis=1).astype(BF16)
        nck = L // DIFF_K_CHUNK
        chunks = [(c * DIFF_K_CHUNK, (c + 1) * DIFF_K_CHUNK) for c in range(nck)]
        s = [_dot(k_ref[lo:hi, :], qt2) for lo, hi in chunks]
        m = functools.reduce(jnp.maximum, [jnp.max(sc, axis=0, keepdims=True) for sc in s])
        o = jnp.zeros((DIFF_VT_ROWS, 2 * tq), F32)
        for (lo, hi), sc in zip(chunks, s):
            p = jnp.exp2((sc - m).astype(BF16))
            o = o + _dot(vt_ref[h, :, lo:hi], p)
        on = o[0:hd] * (1.0 / o[hd:hd + 1])
        heads.append(on[:, 0:tq] - on[:, tq:2 * tq] * lam)
    acc = jnp.concatenate(heads, axis=0).T
    ms = _split_dot(acc * acc, hmean_ref[...])
    y = acc * lax.rsqrt(ms + DIFF_SUBLN_EPS) * g_ref[...] * (1.0 - lam_init)
    o_ref[...] = y.astype(BF16)


def _diffattn(h_main, lam_p, subln_g, hmean, lam_init, B, L):
    nq = L // DIFF_Q_TILE
    return pl.pallas_call(
        functools.partial(_diffattn_kernel, lam_init=lam_init, L=L),
        grid=(B, nq),
        in_specs=[
            pl.BlockSpec((DIFF_Q_TILE, GROUP_W), lambda b, i: (b * nq + i, BLK_BQ)),
            pl.BlockSpec((L, GROUP_W), lambda b, i: (b, BLK_BK)),
            pl.BlockSpec((L, GROUP_W), lambda b, i: (b, BLK_BV)),
            pl.BlockSpec((4, DIFF_QK_DIM), lambda b, i: (0, 0)),
            pl.BlockSpec((1, GROUP_W), lambda b, i: (0, 0)),
            pl.BlockSpec((GROUP_W, GROUP_W), lambda b, i: (0, 0)),
        ],
        out_specs=pl.BlockSpec((DIFF_Q_TILE, GROUP_W), lambda b, i: (b * nq + i, 0)),
        out_shape=jax.ShapeDtypeStruct((B * L, GROUP_W), BF16),
        scratch_shapes=[pltpu.VMEM((N_HEADS, DIFF_VT_ROWS, L), BF16)],
        compiler_params=_cparams(("parallel", "arbitrary")),
        name="diffattn",
    )(h_main, h_main, h_main, lam_p, subln_g, hmean)


def _log_sigmoid(v):
    return jnp.minimum(v, 0.0) - jnp.log(1.0 + jnp.exp(-jnp.abs(v)))


def _mlstm_kernel(q_ref, k_ref, v_ref, gi_ref, gf_ref, cw_ref, cb_ref, bi_ref, bf_ref,
                  ltri_ref, utri_ref, ecol_ref, elane_ref, kmask_ref, vmask_ref, ones_ref,
                  bd_ref, hsum_ref, o_ref,
                  qpad, kpad, qs, ks, hfw, hbw, cst, nst, mst, *, L):
    T = MLSTM_T
    nc = L // T
    halo = CONV_HALO
    nh = N_HEADS

    zpad = jnp.zeros((halo, GROUP_W), BF16)
    for pad, src in ((qpad, q_ref), (kpad, k_ref)):
        pad[0:halo, :] = zpad
        pad[halo + L:halo + L + halo, :] = zpad
        pad[halo:halo + L, :] = src[...]

    def conv_body(c, carry):
        r0 = pl.multiple_of(c * T, T)
        for idx, (pad, dst, scale) in enumerate(((qpad, qs, 1.0),
                                                 (kpad, ks, 1.0 / math.sqrt(HEAD_DIM)))):
            xw = pad[pl.ds(r0, T + 2 * halo), :].astype(F32)
            xm = pltpu.roll(xw, 1, axis=0)[halo:halo + T]
            xp = pltpu.roll(xw, T + 2 * halo - 1, axis=0)[halo:halo + T]
            xc = xw[halo:halo + T]
            lo, hi = idx * GROUP_W, (idx + 1) * GROUP_W
            y = (xm * cw_ref[0:1, lo:hi] + xc * cw_ref[1:2, lo:hi] + xp * cw_ref[2:3, lo:hi]
                 + cb_ref[0:1, lo:hi])
            dst[pl.ds(r0, T), :] = (_silu(y) * scale).astype(BF16)
        return carry

    lax.fori_loop(0, nc, conv_body, 0)

    cst[...] = jnp.zeros(cst.shape, F32)
    nst[...] = jnp.zeros(nst.shape, F32)
    mst[...] = jnp.zeros(mst.shape, F32)

    is_fw = lax.broadcasted_iota(jnp.int32, (1, LANE), 1) < nh
    rowi = lax.broadcasted_iota(jnp.int32, (T, LANE), 0)
    row4 = lax.broadcasted_iota(jnp.int32, (T, nh * T), 0)
    col4 = lax.broadcasted_iota(jnp.int32, (T, nh * T), 1) % T
    causal = (col4 <= row4, col4 >= row4)

    def body(c, carry):
        rows = (pl.multiple_of(c * T, T), pl.multiple_of((nc - 1 - c) * T, T))
        gate_i = jnp.where(is_fw, gi_ref[pl.ds(rows[0], T), :], gi_ref[pl.ds(rows[1], T), :])
        gate_f = jnp.where(is_fw, gf_ref[pl.ds(rows[0], T), :], gf_ref[pl.ds(rows[1], T), :])
        gate_i = gate_i + bi_ref[...]
        parts = _split3(_log_sigmoid(gate_f + bf_ref[...]))
        pre = sum(_dot(ltri_ref[...], t) for t in parts)
        suf = sum(_dot(utri_ref[...], t) for t in parts)
        cum = jnp.where(is_fw, pre, suf)
        b_end = jnp.where(is_fw, pre[T - 1:T, :], suf[0:1, :])
        key_w = gate_i - cum

        pmax, smax = key_w, key_w
        sh = 1
        while sh < T:
            pmax = jnp.maximum(pmax, jnp.where(rowi >= sh, pltpu.roll(pmax, sh, axis=0), NEG_INF))
            smax = jnp.maximum(smax, jnp.where(rowi < T - sh, pltpu.roll(smax, T - sh, axis=0),
                                               NEG_INF))
            sh *= 2
        m_prev = mst[0:1, :]
        inter = cum + m_prev
        m_t = jnp.maximum(inter, cum + jnp.where(is_fw, pmax, smax))
        u = cum - m_t
        a = b_end + key_w
        m_new = jnp.maximum(b_end + m_prev, jnp.max(a, axis=0, keepdims=True))
        mst[0:1, :] = m_new
        stack = jnp.concatenate(
            [jnp.exp(inter - m_t), jnp.exp(-m_t), jnp.exp(a - m_new),
             jnp.broadcast_to(jnp.exp(b_end + m_prev - m_new), (8, LANE))], axis=0)
        st_hi, st_mid, _ = _split3(stack)
        key_w_t = key_w.T

        for d in range(2):
            r0 = rows[d]
            qc = qs[pl.ds(r0, T), :]
            kc = ks[pl.ds(r0, T), :]
            vc = v_ref[pl.ds(r0, T), :]
            ex = _dot(st_hi, elane_ref[d]) + _dot(st_mid, elane_ref[d])
            w_inter, floor, w_key = ex[0:T], ex[T:2 * T], ex[2 * T:3 * T]
            sp_row = ex[3 * T:3 * T + 1]

            r_row = jnp.concatenate([key_w_t[d * nh + h:d * nh + h + 1, :] for h in range(nh)],
                                    axis=1)
            dlog = _split_dot(u, ecol_ref[d]) + r_row
            decay = jnp.exp(jnp.where(causal[d], dlog, NEG_INF))

            kt = kc.astype(F32).T.astype(BF16)
            qk = _dot(qc, jnp.concatenate([kt] * nh, axis=1) * kmask_ref[...]) * decay
            qk_hi = qk.astype(BF16)
            qk_lo = (qk - qk_hi.astype(F32)).astype(BF16)
            vblk = jnp.concatenate([vc] * nh, axis=0) * vmask_ref[...]
            num = _dot(qk_hi, vblk)
            den = _dot(qk_hi, ones_ref[...]) + _dot(qk_lo, ones_ref[...])

            c_prev = cst[d]
            n_prev = nst[d:d + 1, :]
            num = num + w_inter * _dot(qc, c_prev.astype(BF16))
            den = den + w_inter * _dot((qc.astype(F32) * n_prev).astype(BF16), hsum_ref[...])
            h_out = num / jnp.maximum(jnp.abs(den), floor)
            if d == 0:
                hfw[pl.ds(r0, T), :] = h_out
            else:
                hbw[pl.ds(r0, T), :] = h_out

            kw = kc.astype(F32) * w_key
            cst[d] = c_prev * sp_row + _dot_tn(kw.astype(BF16), vc) * bd_ref[...]
            nst[d:d + 1, :] = n_prev * sp_row + jnp.sum(kw, axis=0, keepdims=True)
        return carry

    lax.fori_loop(0, nc, body, 0)
    o_ref[...] = (hfw[...] + hbw[...]).astype(BF16)


def _mlstm_consts():
    T, nh, g = MLSTM_T, N_HEADS, GROUP_W
    tri = np.tril(np.ones((T, T), np.float32))
    src = np.arange(LANE)[:, None]
    ecol = np.stack([(src == d * nh + np.arange(nh * T)[None, :] // T) for d in range(2)])
    elane = np.stack([(src == d * nh + np.arange(g)[None, :] // HEAD_DIM) for d in range(2)])
    head_of = np.arange(g) // HEAD_DIM
    blk_of = np.arange(nh * T) // T
    kmask = head_of[:, None] == blk_of[None, :]
    vmask = blk_of[:, None] == head_of[None, :]
    same_head = head_of[:, None] == head_of[None, :]
    b16 = lambda a: jnp.asarray(a.astype(np.float32), BF16)
    return (b16(tri), b16(tri.T), b16(ecol), b16(elane), b16(kmask), b16(vmask), b16(vmask),
            jnp.asarray(same_head.astype(np.float32)), b16(same_head))


def _mlstm(h_main, gate_i, gate_f, conv_w, conv_b, bias_i, bias_f, consts, B, L):
    T = MLSTM_T
    const = lambda b: (0, 0)
    const3 = lambda b: (0, 0, 0)
    return pl.pallas_call(
        functools.partial(_mlstm_kernel, L=L),
        grid=(B,),
        in_specs=[
            pl.BlockSpec((L, GROUP_W), lambda b: (b, BLK_DQ)),
            pl.BlockSpec((L, GROUP_W), lambda b: (b, BLK_DK)),
            pl.BlockSpec((L, GROUP_W), lambda b: (b, BLK_DV)),
            pl.BlockSpec((L, LANE), lambda b: (b, 0)),
            pl.BlockSpec((L, LANE), lambda b: (b, 0)),
            pl.BlockSpec((3, 2 * GROUP_W), const),
            pl.BlockSpec((1, 2 * GROUP_W), const),
            pl.BlockSpec((1, LANE), const),
            pl.BlockSpec((1, LANE), const),
            pl.BlockSpec((T, T), const),
            pl.BlockSpec((T, T), const),
            pl.BlockSpec((2, LANE, N_HEADS * T), const3),
            pl.BlockSpec((2, LANE, GROUP_W), const3),
            pl.BlockSpec((GROUP_W, N_HEADS * T), const),
            pl.BlockSpec((N_HEADS * T, GROUP_W), const),
            pl.BlockSpec((N_HEADS * T, GROUP_W), const),
            pl.BlockSpec((GROUP_W, GROUP_W), const),
            pl.BlockSpec((GROUP_W, GROUP_W), const),
        ],
        out_specs=pl.BlockSpec((L, GROUP_W), lambda b: (b, 0)),
        out_shape=jax.ShapeDtypeStruct((B * L, GROUP_W), BF16),
        scratch_shapes=[
            pltpu.VMEM((L + 2 * CONV_HALO, GROUP_W), BF16),
            pltpu.VMEM((L + 2 * CONV_HALO, GROUP_W), BF16),
            pltpu.VMEM((L, GROUP_W), BF16),
            pltpu.VMEM((L, GROUP_W), BF16),
            pltpu.VMEM((L, GROUP_W), F32),
            pltpu.VMEM((L, GROUP_W), F32),
            pltpu.VMEM((2, GROUP_W, GROUP_W), F32),
            pltpu.VMEM((8, GROUP_W), F32),
            pltpu.VMEM((8, LANE), F32),
        ],
        compiler_params=_cparams(("parallel",)),
        name="mlstm",
    )(h_main, h_main, h_main, gate_i, gate_f, conv_w, conv_b, bias_i, bias_f, *consts)


def _hy_dims(L):
    h1 = L // HY_N2
    k1h = h1 + 1
    k1p = -(-k1h // BF16_SUBLANES) * BF16_SUBLANES
    kg = 12 if k1p % 12 == 0 else BF16_SUBLANES
    return h1, k1h, k1p, kg


def _hy_tables(L):
    h1, k1h, k1p, _ = _hy_dims(L)
    n1_len = 2 * h1
    n = 2 * L
    k1 = np.arange(k1p)[:, None].astype(np.float64)
    live = (np.arange(k1p) < k1h)[:, None]
    n1 = np.arange(h1)[None, :].astype(np.float64)
    ang = 2.0 * np.pi * k1 * n1 / n1_len
    m1 = np.concatenate([np.where(live, np.cos(ang), 0.0), np.where(live, -np.sin(ang), 0.0)], 0)

    n2 = np.arange(HY_N2)[None, None, :].astype(np.float64)
    k2 = np.arange(HY_N2)[None, :, None].astype(np.float64)
    kk = np.arange(k1p)[:, None, None] + n1_len * k2
    th = 2.0 * np.pi * kk * n2 / n
    fr, fi = np.cos(th), -np.sin(th)
    f3 = np.concatenate([np.concatenate([fr, -fi], 2), np.concatenate([fi, fr], 2)], 1)
    er, ei = np.transpose(np.cos(th), (0, 2, 1)), np.transpose(np.sin(th), (0, 2, 1))
    f3i = np.concatenate([np.concatenate([er, -ei], 2), np.concatenate([ei, er], 2)], 1)
    live3 = (np.arange(k1p) < k1h)[:, None, None]
    f3 = np.where(live3, f3, 0.0)
    f3i = np.where(live3, f3i, 0.0)

    nn1 = np.arange(h1)[:, None].astype(np.float64)
    kc = np.arange(k1p)[None, :].astype(np.float64)
    ph = 2.0 * np.pi * nn1 * kc / n1_len
    edge = (np.arange(k1p) == 0) | (np.arange(k1p) == h1)
    livec = (np.arange(k1p) < k1h)[None, :]
    m4r = np.where(livec, np.where(edge[None, :], np.cos(ph), 2.0 * np.cos(ph)), 0.0) / n
    m4i = np.where(livec & ~edge[None, :], -2.0 * np.sin(ph), 0.0) / n
    return (jnp.asarray(m1, BF16), jnp.asarray(f3, BF16), jnp.asarray(f3i, BF16),
            jnp.asarray(m4r, BF16), jnp.asarray(m4i, BF16))


def _hy_stage1(z_bf16, m1_ref, a_ref, k1p):
    a = _dot(m1_ref[...], z_bf16)
    a_ref[0, 0] = a[0:k1p].astype(BF16)
    a_ref[0, 1] = a[k1p:2 * k1p].astype(BF16)


def _hy_conv3_wide(x, w_ref, b_ref, j, h1):
    c = GROUP_W
    wl = HY_N2 * c
    rowi = lax.broadcasted_iota(jnp.int32, (h1, c), 0)
    tail = x[:, wl - c:wl]
    head = x[:, 0:c]
    prev_tail = jnp.where(rowi == 0, 0.0, pltpu.roll(tail, 1, axis=0))
    next_head = jnp.where(rowi == h1 - 1, 0.0, pltpu.roll(head, h1 - 1, axis=0))
    xm = jnp.concatenate([prev_tail, x[:, 0:wl - c]], axis=1)
    xp = jnp.concatenate([x[:, c:wl], next_head], axis=1)
    return (xm * w_ref[3 * j:3 * j + 1, :] + x * w_ref[3 * j + 1:3 * j + 2, :]
            + xp * w_ref[3 * j + 2:3 * j + 3, :] + b_ref[j:j + 1, :])


def _hy_front_kernel(v_ref, x1_ref, x2_ref, w_ref, b_ref, m1_ref,
                     z_ref, x1c_ref, x2c_ref, a_ref, *, h1, k1p):
    z = _hy_conv3_wide(v_ref[0].astype(F32), w_ref, b_ref, 0, h1).astype(BF16)
    z_ref[0] = z
    x1c_ref[0] = _hy_conv3_wide(x1_ref[0].astype(F32), w_ref, b_ref, 1, h1).astype(BF16)
    x2c_ref[0] = _hy_conv3_wide(x2_ref[0].astype(F32), w_ref, b_ref, 2, h1).astype(BF16)
    _hy_stage1(z, m1_ref, a_ref, k1p)


def _hy_front(cv, cx1, cx2, w_wide, b_wide, m1, B, L):
    h1, _, k1p, _ = _hy_dims(L)
    wl = HY_N2 * GROUP_W
    seq = pl.BlockSpec((1, h1, wl), lambda b: (b, 0, 0))
    wide = jax.ShapeDtypeStruct((B, h1, wl), BF16)
    return pl.pallas_call(
        functools.partial(_hy_front_kernel, h1=h1, k1p=k1p),
        grid=(B,),
        in_specs=[seq, seq, seq,
                  pl.BlockSpec((9, wl), lambda b: (0, 0)),
                  pl.BlockSpec((3, wl), lambda b: (0, 0)),
                  pl.BlockSpec((2 * k1p, h1), lambda b: (0, 0))],
        out_specs=(seq, seq, seq, pl.BlockSpec((1, 2, k1p, wl), lambda b: (b, 0, 0, 0))),
        out_shape=(wide, wide, wide, jax.ShapeDtypeStruct((B, 2, k1p, wl), BF16)),
        compiler_params=_cparams(("parallel",)),
        name="hy_front",
    )(cv, cx1, cx2, w_wide, b_wide, m1)


def _hy_stage1_kernel(z_ref, m1_ref, a_ref, *, k1p):
    _hy_stage1(z_ref[0], m1_ref, a_ref, k1p)


def _hy_stage1_call(z_wide, m1, L):
    h1, _, k1p, _ = _hy_dims(L)
    wl = HY_N2 * GROUP_W
    nb = z_wide.shape[0]
    return pl.pallas_call(
        functools.partial(_hy_stage1_kernel, k1p=k1p),
        grid=(nb,),
        in_specs=[pl.BlockSpec((1, h1, wl), lambda b: (b, 0, 0)),
                  pl.BlockSpec((2 * k1p, h1), lambda b: (0, 0))],
        out_specs=pl.BlockSpec((1, 2, k1p, wl), lambda b: (b, 0, 0, 0)),
        out_shape=jax.ShapeDtypeStruct((nb, 2, k1p, wl), BF16),
        compiler_params=_cparams(("parallel",)),
        name="hy_stage1",
    )(z_wide, m1)


def _hy_spectrum_kernel(af_ref, ab_ref, f3_ref, g_ref, *, kg, k1h):
    grp = pl.program_id(0)

    @pl.when(grp * kg < k1h)
    def _():
        for kk in range(kg):
            xs = []
            for a_ref in (af_ref, ab_ref):
                a2 = jnp.concatenate([a_ref[0, 0, kk], a_ref[0, 1, kk]], axis=0)
                xs.append(_dot(f3_ref[kk], a2))
            g_ref[0, 0, kk] = xs[0][0:HY_N2] + xs[1][0:HY_N2]
            g_ref[0, 1, kk] = xs[0][HY_N2:2 * HY_N2] - xs[1][HY_N2:2 * HY_N2]

    @pl.when(grp * kg >= k1h)
    def _():
        g_ref[...] = jnp.zeros(g_ref.shape, F32)


def _hy_spectrum(a_filt, f3, L):
    _, k1h, k1p, kg = _hy_dims(L)
    a5 = a_filt.reshape(2 * HYENA_ORDER, 2, k1p, HY_N2, GROUP_W)
    blk = (1, 2, kg, HY_N2, GROUP_W)
    return pl.pallas_call(
        functools.partial(_hy_spectrum_kernel, kg=kg, k1h=k1h),
        grid=(k1p // kg, HYENA_ORDER),
        in_specs=[pl.BlockSpec(blk, lambda g, o: (2 * o, 0, g, 0, 0)),
                  pl.BlockSpec(blk, lambda g, o: (2 * o + 1, 0, g, 0, 0)),
                  pl.BlockSpec((kg, 2 * HY_N2, 2 * HY_N2), lambda g, o: (g, 0, 0))],
        out_specs=pl.BlockSpec(blk, lambda g, o: (o, 0, g, 0, 0)),
        out_shape=jax.ShapeDtypeStruct((HYENA_ORDER, 2, k1p, HY_N2, GROUP_W), F32),
        compiler_params=_cparams(("parallel", "parallel")),
        name="hy_spectrum",
    )(a5, a5, f3)


def _hy_mid_kernel(a_ref, g_ref, f3_ref, f3i_ref, b_ref, *, kg, k1h):
    grp = pl.program_id(0)

    @pl.when(grp * kg < k1h)
    def _():
        for kk in range(kg):
            a2 = jnp.concatenate([a_ref[0, 0, kk], a_ref[0, 1, kk]], axis=0)
            x = _dot(f3_ref[kk], a2)
            xr, xi = x[0:HY_N2], x[HY_N2:2 * HY_N2]
            gr, gi = g_ref[0, 0, kk], g_ref[0, 1, kk]
            y2 = jnp.concatenate([xr * gr - xi * gi, xr * gi + xi * gr], axis=0).astype(BF16)
            bm = _dot(f3i_ref[kk], y2)
            b_ref[0, 0, kk] = bm[0:HY_N2].astype(BF16)
            b_ref[0, 1, kk] = bm[HY_N2:2 * HY_N2].astype(BF16)

    @pl.when(grp * kg >= k1h)
    def _():
        b_ref[...] = jnp.zeros(b_ref.shape, BF16)


def _hy_mid(a, g_spec, order, f3, f3i, B, L):
    _, k1h, k1p, kg = _hy_dims(L)
    a5 = a.reshape(B, 2, k1p, HY_N2, GROUP_W)
    blk = (1, 2, kg, HY_N2, GROUP_W)
    tab = pl.BlockSpec((kg, 2 * HY_N2, 2 * HY_N2), lambda g, b: (g, 0, 0))
    out = pl.pallas_call(
        functools.partial(_hy_mid_kernel, kg=kg, k1h=k1h),
        grid=(k1p // kg, B),
        in_specs=[pl.BlockSpec(blk, lambda g, b: (b, 0, g, 0, 0)),
                  pl.BlockSpec(blk, lambda g, b: (order, 0, g, 0, 0)),
                  tab, tab],
        out_specs=pl.BlockSpec(blk, lambda g, b: (b, 0, g, 0, 0)),
        out_shape=jax.ShapeDtypeStruct((B, 2, k1p, HY_N2, GROUP_W), BF16),
        compiler_params=_cparams(("parallel", "parallel")),
        name="hy_mid",
    )(a5, g_spec, f3, f3i)
    return out.reshape(B, 2, k1p, HY_N2 * GROUP_W)


def _hy_back_kernel(b_ref, z_ref, x_ref, bias_ref, m4r_ref, m4i_ref, *rest, k1p, last):
    y = _dot(m4r_ref[...], b_ref[0, 0]) + _dot(m4i_ref[...], b_ref[0, 1])
    z_new = x_ref[0].astype(F32) * (y + z_ref[0].astype(F32) * bias_ref[...])
    zb = z_new.astype(BF16)
    if last:
        (o_ref,) = rest
        o_ref[0] = zb
    else:
        m1_ref, o_ref, a_ref = rest
        o_ref[0] = zb
        _hy_stage1(zb, m1_ref, a_ref, k1p)


def _hy_back(b, z, xg, bias_wide, m4r, m4i, m1, B, L, last):
    h1, _, k1p, _ = _hy_dims(L)
    wl = HY_N2 * GROUP_W
    seq = pl.BlockSpec((1, h1, wl), lambda i: (i, 0, 0))
    spec4 = pl.BlockSpec((1, 2, k1p, wl), lambda i: (i, 0, 0, 0))
    const = lambda i: (0, 0)
    in_specs = [spec4, seq, seq, pl.BlockSpec((1, wl), const),
                pl.BlockSpec((h1, k1p), const), pl.BlockSpec((h1, k1p), const)]
    args = [b, z, xg, bias_wide, m4r, m4i]
    wide = jax.ShapeDtypeStruct((B, h1, wl), BF16)
    if last:
        out_specs, out_shape = seq, wide
    else:
        in_specs.append(pl.BlockSpec((2 * k1p, h1), const))
        args.append(m1)
        out_specs = (seq, spec4)
        out_shape = (wide, jax.ShapeDtypeStruct((B, 2, k1p, wl), BF16))
    return pl.pallas_call(
        functools.partial(_hy_back_kernel, k1p=k1p, last=last),
        grid=(B,),
        in_specs=in_specs,
        out_specs=out_specs,
        out_shape=out_shape,
        compiler_params=_cparams(("parallel",)),
        name="hy_back_last" if last else "hy_back",
    )(*args)


def _hy_filter_kernel(f_ref, t_ref, w1_ref, b1_ref, w2_ref, b2_ref, w3_ref, fr_ref, ad_ref,
                      o_ref, *, tile):
    i = pl.program_id(0)
    freq = fr_ref[...]
    z = jnp.sin(freq * (_dot_f32(f_ref[...], w1_ref[...]) + b1_ref[...]))
    z = jnp.sin(freq * (_dot_f32(z, w2_ref[...]) + b2_ref[...]))
    decay = jnp.exp(-t_ref[...] * ad_ref[...])
    rowi = i * tile + lax.broadcasted_iota(jnp.int32, (tile, GROUP_W), 0)
    for j in range(2 * HYENA_ORDER):
        hj = _dot_f32(z, w3_ref[:, j * GROUP_W:(j + 1) * GROUP_W]) * decay
        if j % 2 == 1:
            hj = jnp.where(rowi == 0, 0.0, hj)
        o_ref[j] = hj.astype(BF16)


def _hy_filter(feats, tcol, w1, b1, w2, b2, w3, freq, absdelta, L):
    tile = math.gcd(L, ROW_TILE)
    const = lambda i: (0, 0)
    return pl.pallas_call(
        functools.partial(_hy_filter_kernel, tile=tile),
        grid=(L // tile,),
        in_specs=[pl.BlockSpec((tile, LANE), lambda i: (i, 0)),
                  pl.BlockSpec((tile, 1), lambda i: (i, 0)),
                  pl.BlockSpec((LANE, HYENA_HIDDEN), const),
                  pl.BlockSpec((1, HYENA_HIDDEN), const),
                  pl.BlockSpec((HYENA_HIDDEN, HYENA_HIDDEN), const),
                  pl.BlockSpec((1, HYENA_HIDDEN), const),
                  pl.BlockSpec((HYENA_HIDDEN, 2 * HYENA_ORDER * GROUP_W), const),
                  pl.BlockSpec((1, HYENA_HIDDEN), const),
                  pl.BlockSpec((1, GROUP_W), const)],
        out_specs=pl.BlockSpec((2 * HYENA_ORDER, tile, GROUP_W), lambda i: (0, i, 0)),
        out_shape=jax.ShapeDtypeStruct((2 * HYENA_ORDER, L, GROUP_W), BF16),
        compiler_params=_cparams(("parallel",)),
        name="hy_filter",
    )(feats, tcol, w1, b1, w2, b2, w3, freq, absdelta)


def _outproj_kernel(x_ref, oa_ref, ob_ref, oc_ref, od_ref, ox_ref,
                    ga_ref, gb_ref, gc_ref, gdo_ref, gdg_ref, gx_ref, w_ref, png_ref, out_ref):
    f = lambda r: r[...].astype(F32)
    branches = (
        f(oa_ref) * _silu(f(ga_ref)),
        f(ob_ref) * _silu(f(gb_ref)),
        f(oc_ref) * _silu(f(gc_ref)),
        f(od_ref) * _sigmoid(f(gdo_ref)) * _silu(f(gdg_ref)),
        f(ox_ref) * _silu(f(gx_ref)),
    )
    y = jnp.zeros(x_ref.shape, F32)
    for j, br in enumerate(branches):
        y = y + _dot(br.astype(BF16), w_ref[j * GROUP_W:(j + 1) * GROUP_W, :])
    ms = jnp.mean(y * y, axis=-1, keepdims=True)
    out_ref[...] = x_ref[...] + y * lax.rsqrt(ms + NORM_EPS) * png_ref[...]


def _outproj(x2d, outs, h_main, w_out, png):
    n_tok = x2d.shape[0]
    nt = n_tok // ROW_TILE
    row = lambda i: (i, 0)
    blk = lambda j: pl.BlockSpec((ROW_TILE, GROUP_W), lambda i: (i, j))
    return pl.pallas_call(
        _outproj_kernel,
        grid=(nt,),
        in_specs=[pl.BlockSpec((ROW_TILE, D_MODEL), row)]
        + [pl.BlockSpec((ROW_TILE, GROUP_W), row)] * 5
        + [blk(BLK_AG), blk(BLK_BG), blk(BLK_CG), blk(BLK_DO), blk(BLK_DG), blk(BLK_XG)]
        + [pl.BlockSpec((5 * GROUP_W, D_MODEL), lambda i: (0, 0)),
           pl.BlockSpec((1, D_MODEL), lambda i: (0, 0))],
        out_specs=pl.BlockSpec((ROW_TILE, D_MODEL), row),
        out_shape=jax.ShapeDtypeStruct((n_tok, D_MODEL), F32),
        compiler_params=_cparams(("parallel",)),
        name="outproj",
    )(x2d, *outs, h_main, h_main, h_main, h_main, h_main, h_main, w_out, png)


def _rope_tables(L, group, rot_dim):
    half = rot_dim // 2
    inv = 1.0 / (ROPE_THETA ** (jnp.arange(0, rot_dim, 2, dtype=F32) / rot_dim))
    ang = jnp.arange(L, dtype=F32)[:, None] * inv[None, :]
    cos, sin = jnp.cos(ang), jnp.sin(ang)
    lane = np.arange(GROUP_W) % group
    in_rot = lane < rot_dim
    idx = lane % half
    c = jnp.where(in_rot[None, :], cos[:, idx], 1.0)
    s = jnp.where(in_rot[None, :], sin[:, idx], 0.0)
    p = np.zeros((GROUP_W, GROUP_W), np.float32)
    for j in range(GROUP_W):
        if lane[j] < half:
            p[j + half, j] = -1.0
        elif lane[j] < rot_dim:
            p[j - half, j] = 1.0
    return c, s, jnp.asarray(p, BF16)


def _hyena_features(L):
    t = jnp.linspace(0.0, 1.0, L, dtype=F32)[:, None]
    bands = jnp.linspace(1e-4, HYENA_BANDS - 1, HYENA_BANDS, dtype=F32)
    ang = (2.0 * math.pi / L) * jnp.arange(L, dtype=F32)[:, None] * bands[None, :]
    feats = jnp.concatenate([t, jnp.cos(ang), -jnp.sin(ang)], axis=-1)
    return jnp.pad(feats, ((0, 0), (0, LANE - HYENA_EMB))), t


def _relayout_w_in(w):
    g = GROUP_W
    off_c, off_d = 8 * g, 12 * g
    off_gate = off_d + 5 * g
    off_x = off_gate + N_MLSTM_GATES
    main = jnp.concatenate([w[:, 0:off_c], w[:, off_c + 3 * g:off_c + 4 * g],
                            w[:, off_d:off_gate], w[:, off_x:off_x + 2 * g]], axis=1)
    hy = w[:, off_c:off_c + 3 * g]
    gate_i, gate_f = _split_gates(w[:, off_gate:off_x])
    return jnp.concatenate([main, hy, gate_i, gate_f], axis=1).astype(BF16)


def _split_gates(t):
    nh = N_HEADS
    pad = [(0, 0)] * (t.ndim - 1) + [(0, LANE - 2 * nh)]
    gi = jnp.concatenate([t[..., 0:nh], t[..., 2 * nh:3 * nh]], axis=-1)
    gf = jnp.concatenate([t[..., nh:2 * nh], t[..., 3 * nh:4 * nh]], axis=-1)
    return jnp.pad(gi, pad), jnp.pad(gf, pad)


def _trunk(x, mem, pre_norm_g, post_norm_g, w_in, w_out, diff_lambda, diff_subln_g,
           hy_conv_w, hy_conv_b, hy_ffn_w1, hy_ffn_b1, hy_ffn_w2, hy_ffn_b2, hy_ffn_w3,
           hy_freq, hy_bias, ml_conv_w, ml_conv_b, ml_gate_b, mem_norm_g, w_mem_kv):
    B, L, _ = x.shape
    M = mem.shape[1]
    depth = w_in.shape[0]
    g = GROUP_W
    h1, _, k1p, _ = _hy_dims(L)
    wl = HY_N2 * g

    ca, sa, pa = _rope_tables(L, HEAD_DIM, HEAD_DIM // ROPE_FRACTION)
    cb, sb, pb = _rope_tables(L, DIFF_QK_DIM, DIFF_QK_DIM // ROPE_FRACTION)
    rope_tabs = (ca, sa, cb, sb, pa, pb)
    dil_bias = jnp.asarray(_dil_bias_table())
    feats, tcol = _hyena_features(L)
    absdelta = jnp.abs(jnp.linspace(math.log(HYENA_TARGET) / HYENA_SLOW_DECAY,
                                    math.log(HYENA_TARGET) / HYENA_FAST_DECAY, g, dtype=F32))[None]
    m1, f3, f3i, m4r, m4i = _hy_tables(L)
    head_of = np.arange(g) // HEAD_DIM
    same_head = (head_of[:, None] == head_of[None, :]).astype(np.float32)
    hmean = jnp.asarray(same_head / HEAD_DIM, BF16)
    mlstm_consts = _mlstm_consts()

    x2d = x.reshape(B * L, D_MODEL)
    mem2d = mem.reshape(B * M, D_MODEL)
    for li in range(depth):
        lam_init = 0.8 - 0.6 * math.exp(-0.3 * li)
        w_all = _relayout_w_in(w_in[li])
        h_main, cv, cx1, cx2, gate_i, gate_f = _inproj(x2d, pre_norm_g[li][None], w_all,
                                                       rope_tabs, B, L)

        oa = _dilattn(h_main, dil_bias, B, L)
        ob = _diffattn(h_main, diff_lambda[li], jnp.tile(diff_subln_g[li], N_HEADS)[None],
                       hmean, lam_init, B, L)

        w1 = jnp.pad(hy_ffn_w1[li], ((0, LANE - HYENA_EMB), (0, 0)))
        hfilt = _hy_filter(feats, tcol, w1, hy_ffn_b1[li][None], hy_ffn_w2[li],
                           hy_ffn_b2[li][None], hy_ffn_w3[li], hy_freq[li][None], absdelta, L)
        a_filt = _hy_stage1_call(hfilt.reshape(2 * HYENA_ORDER, h1, wl), m1, L)
        g_spec = _hy_spectrum(a_filt, f3, L)
        cw = jnp.tile(hy_conv_w[li].reshape(3, 3, g).transpose(1, 0, 2).reshape(9, g), (1, HY_N2))
        cbw = jnp.tile(hy_conv_b[li].reshape(3, g), (1, HY_N2))
        z, x1c, x2c, a = _hy_front(cv.reshape(B, h1, wl), cx1.reshape(B, h1, wl),
                                   cx2.reshape(B, h1, wl), cw, cbw, m1, B, L)
        bias_w = jnp.tile(hy_bias[li], (1, HY_N2))
        bsp = _hy_mid(a, g_spec, 0, f3, f3i, B, L)
        z, a = _hy_back(bsp, z, x1c, bias_w[0:1], m4r, m4i, m1, B, L, last=False)
        bsp = _hy_mid(a, g_spec, 1, f3, f3i, B, L)
        oc = _hy_back(bsp, z, x2c, bias_w[1:2], m4r, m4i, m1, B, L, last=True).reshape(B * L, g)

        bias_i, bias_f = _split_gates(ml_gate_b[li][None])
        od = _mlstm(h_main, gate_i, gate_f, ml_conv_w[li], ml_conv_b[li][None], bias_i, bias_f,
                    mlstm_consts, B, L)

        mkv = _memkv(mem2d, mem_norm_g[li][None], w_mem_kv[li].astype(BF16))
        ox = _memattn(h_main, mkv, B, L, M)

        x2d = _outproj(x2d, (oa, ob, oc, od, ox), h_main, w_out[li].astype(BF16),
                       post_norm_g[li][None])
    return x2d.reshape(B, L, D_MODEL)


def kernel(x_prompt, x_sample, mem_prompt, mem_sample, pre_norm_g, post_norm_g, w_in, w_out,
           diff_lambda, diff_subln_g, hy_conv_w, hy_conv_b, hy_ffn_w1, hy_ffn_b1, hy_ffn_w2,
           hy_ffn_b2, hy_ffn_w3, hy_freq, hy_bias, ml_conv_w, ml_conv_b, ml_gate_b,
           mem_norm_g, w_mem_kv):
    nb = x_prompt.shape[0]
    x = jnp.concatenate([x_prompt, x_sample], axis=0)
    mem = jnp.concatenate([mem_prompt, mem_sample], axis=0)
    y = _trunk(x, mem, pre_norm_g, post_norm_g, w_in, w_out, diff_lambda, diff_subln_g,
               hy_conv_w, hy_conv_b, hy_ffn_w1, hy_ffn_b1, hy_ffn_w2, hy_ffn_b2, hy_ffn_w3,
               hy_freq, hy_bias, ml_conv_w, ml_conv_b, ml_gate_b, mem_norm_g, w_mem_kv)
    return y[:nb], y[nb:]
```

```python
import functools
import math

import numpy as np
import jax
import jax.numpy as jnp
from jax import lax
from jax.experimental import pallas as pl
from jax.experimental.pallas import tpu as pltpu

F32 = jnp.float32
BF16 = jnp.bfloat16

D_MODEL = 1024
HEAD_DIM = 64
GROUP_W = 256
N_HEADS = GROUP_W // HEAD_DIM
NORM_EPS = 1e-6
NEG_INF = -1e30
ROPE_THETA = 500000.0
ROPE_FRACTION = 4
DIL_PATTERNS = ((128, 1), (512, 4), (2048, 16))
DIFF_QK_DIM = HEAD_DIM // 2
DIFF_SUBLN_EPS = 1e-5
HYENA_ORDER = 2
HYENA_BANDS = 16
HYENA_EMB = 1 + 2 * HYENA_BANDS
HYENA_HIDDEN = 64
HYENA_FAST_DECAY = 0.3
HYENA_SLOW_DECAY = 1.5
HYENA_TARGET = 1e-2
N_MLSTM_GATES = 4 * N_HEADS

(BLK_AQ, BLK_AK, BLK_AV, BLK_AG, BLK_BQ, BLK_BK, BLK_BV, BLK_BG, BLK_CG,
 BLK_DQ, BLK_DK, BLK_DV, BLK_DO, BLK_DG, BLK_XQ, BLK_XG) = range(16)
N_MAIN_BLK = 16
MAIN_W = N_MAIN_BLK * GROUP_W
W_ALL = MAIN_W + 4 * GROUP_W

VMEM_LIMIT_BYTES = 56 * 1024 * 1024
LANE = 128
BF16_SUBLANES = 16

ROW_TILE = 512
DIL_Q_TILE = 256
DIL_PAD = 1024
DIFF_Q_TILE = 256
DIFF_K_CHUNK = 512
DIFF_VT_ROWS = HEAD_DIM + BF16_SUBLANES
LOG2E = 1.4426950408889634
MEM_Q_TILE = 512
MLSTM_T = 128
CONV_HALO = 16
HY_N2 = 128


def _cparams(sem, vmem=VMEM_LIMIT_BYTES):
    return pltpu.CompilerParams(dimension_semantics=sem, vmem_limit_bytes=vmem)


def _sigmoid(v):
    return 1.0 / (1.0 + jnp.exp(-v))


def _silu(v):
    return v * _sigmoid(v)


def _dot(a, b):
    return jnp.dot(a, b, preferred_element_type=F32)


def _dot_nt(a, b):
    return lax.dot_general(a, b, (((1,), (1,)), ((), ())), preferred_element_type=F32)


def _dot_tn(a, b):
    return lax.dot_general(a, b, (((0,), (0,)), ((), ())), preferred_element_type=F32)


def _head_mask(h, width, dtype):
    lane = lax.broadcasted_iota(jnp.int32, (1, GROUP_W), 1)
    return ((lane // width) == h).astype(dtype)


def _inproj_kernel(x_ref, g_ref, w_ref, ca_ref, sa_ref, cb_ref, sb_ref, pa_ref, pb_ref,
                   h_ref, cv_ref, cx1_ref, cx2_ref, gi_ref, gf_ref):
    x = x_ref[...]
    ms = jnp.mean(x * x, axis=-1, keepdims=True)
    xn = (x * lax.rsqrt(ms + NORM_EPS) * g_ref[...]).astype(BF16)

    def proj(j, width=GROUP_W):
        return _dot(xn, w_ref[:, j * GROUP_W:j * GROUP_W + width])

    def rope(acc, c_ref, s_ref, p_ref):
        partner = _dot(acc.astype(BF16), p_ref[...])
        return acc * c_ref[...] + partner * s_ref[...]

    for j in range(N_MAIN_BLK):
        acc = proj(j)
        if j in (BLK_AQ, BLK_AK):
            acc = rope(acc, ca_ref, sa_ref, pa_ref)
        if j in (BLK_BQ, BLK_BK):
            acc = rope(acc, cb_ref, sb_ref, pb_ref)
        if j in (BLK_AQ, BLK_XQ):
            acc = acc * (LOG2E / math.sqrt(HEAD_DIM))
        if j == BLK_BQ:
            acc = acc * (LOG2E / math.sqrt(DIFF_QK_DIM))
        h_ref[:, j * GROUP_W:(j + 1) * GROUP_W] = acc.astype(BF16)
    cv_ref[...] = proj(N_MAIN_BLK).astype(BF16)
    cx1_ref[...] = proj(N_MAIN_BLK + 1).astype(BF16)
    cx2_ref[...] = proj(N_MAIN_BLK + 2).astype(BF16)
    gates = proj(N_MAIN_BLK + 3)
    gi_ref[...] = gates[:, 0:LANE]
    gf_ref[...] = gates[:, LANE:2 * LANE]


def _inproj(x2d, g, w_all, rope_tabs, B, L):
    ca, sa, cb, sb, pa, pb = rope_tabs
    nt = L // ROW_TILE
    n_tok = B * L
    row = lambda i, b: (b * nt + i, 0)
    tab = lambda i, b: (i, 0)
    const = lambda i, b: (0, 0)
    out_shapes = (
        jax.ShapeDtypeStruct((n_tok, MAIN_W), BF16),
        jax.ShapeDtypeStruct((n_tok, GROUP_W), BF16),
        jax.ShapeDtypeStruct((n_tok, GROUP_W), BF16),
        jax.ShapeDtypeStruct((n_tok, GROUP_W), BF16),
        jax.ShapeDtypeStruct((n_tok, LANE), F32),
        jax.ShapeDtypeStruct((n_tok, LANE), F32),
    )
    return pl.pallas_call(
        _inproj_kernel,
        grid=(nt, B),
        in_specs=[
            pl.BlockSpec((ROW_TILE, D_MODEL), row),
            pl.BlockSpec((1, D_MODEL), const),
            pl.BlockSpec((D_MODEL, W_ALL), const),
            pl.BlockSpec((ROW_TILE, GROUP_W), tab),
            pl.BlockSpec((ROW_TILE, GROUP_W), tab),
            pl.BlockSpec((ROW_TILE, GROUP_W), tab),
            pl.BlockSpec((ROW_TILE, GROUP_W), tab),
            pl.BlockSpec((GROUP_W, GROUP_W), const),
            pl.BlockSpec((GROUP_W, GROUP_W), const),
        ],
        out_specs=(
            pl.BlockSpec((ROW_TILE, MAIN_W), row),
            pl.BlockSpec((ROW_TILE, GROUP_W), row),
            pl.BlockSpec((ROW_TILE, GROUP_W), row),
            pl.BlockSpec((ROW_TILE, GROUP_W), row),
            pl.BlockSpec((ROW_TILE, LANE), row),
            pl.BlockSpec((ROW_TILE, LANE), row),
        ),
        out_shape=out_shapes,
        compiler_params=_cparams(("parallel", "parallel")),
        name="inproj",
    )(x2d, g, w_all, ca, sa, cb, sb, pa, pb)


def _memkv_kernel(m_ref, g_ref, w_ref, o_ref):
    x = m_ref[...]
    ms = jnp.mean(x * x, axis=-1, keepdims=True)
    xn = (x * lax.rsqrt(ms + NORM_EPS) * g_ref[...]).astype(BF16)
    o_ref[...] = _dot(xn, w_ref[...]).astype(BF16)


def _memkv(mem2d, g, w):
    rows = mem2d.shape[0]
    tile = math.gcd(rows, ROW_TILE)
    return pl.pallas_call(
        _memkv_kernel,
        grid=(rows // tile,),
        in_specs=[
            pl.BlockSpec((tile, D_MODEL), lambda i: (i, 0)),
            pl.BlockSpec((1, D_MODEL), lambda i: (0, 0)),
            pl.BlockSpec((D_MODEL, 2 * GROUP_W), lambda i: (0, 0)),
        ],
        out_specs=pl.BlockSpec((tile, 2 * GROUP_W), lambda i: (i, 0)),
        out_shape=jax.ShapeDtypeStruct((rows, 2 * GROUP_W), BF16),
        compiler_params=_cparams(("parallel",)),
        name="memkv",
    )(mem2d, g, w)


def _memattn_kernel(q_ref, mk_ref, mv_ref, o_ref):
    q = q_ref[...]
    mk = mk_ref[...]
    mv = mv_ref[...]
    acc = jnp.zeros(q.shape, F32)
    for h in range(N_HEADS):
        hm = _head_mask(h, HEAD_DIM, BF16)
        s = _dot_nt(q * hm, mk)
        m = jnp.max(s, axis=-1, keepdims=True)
        p = jnp.exp2(s - m)
        l = jnp.sum(p, axis=-1, keepdims=True)
        acc = acc + _dot(p.astype(BF16), mv * hm) * (1.0 / l)
    o_ref[...] = acc.astype(BF16)


def _memattn(h_main, mkv, B, L, M):
    nq = L // MEM_Q_TILE
    return pl.pallas_call(
        _memattn_kernel,
        grid=(B, nq),
        in_specs=[
            pl.BlockSpec((MEM_Q_TILE, GROUP_W), lambda b, i: (b * nq + i, BLK_XQ)),
            pl.BlockSpec((M, GROUP_W), lambda b, i: (b, 0)),
            pl.BlockSpec((M, GROUP_W), lambda b, i: (b, 1)),
        ],
        out_specs=pl.BlockSpec((MEM_Q_TILE, GROUP_W), lambda b, i: (b * nq + i, 0)),
        out_shape=jax.ShapeDtypeStruct((B * L, GROUP_W), BF16),
        compiler_params=_cparams(("parallel", "parallel")),
        name="memattn",
    )(h_main, mkv, mkv)


def _dil_bias_table():
    w = DIL_Q_TILE + 2 * DIL_PAD
    d = np.arange(DIL_Q_TILE)[:, None] - np.arange(w)[None, :] + DIL_PAD
    count = np.zeros(d.shape, np.float64)
    for win, dil in DIL_PATTERNS:
        reach = (win // (2 * dil)) * dil
        count += (d % dil == 0) & (np.abs(d) <= reach)
    return np.where(count > 0, np.log2(np.maximum(count, 1.0)), NEG_INF).astype(np.float32)


def _dilattn_kernel(q_ref, k_ref, v_ref, bias_ref, o_ref, kpad, vpad, *, L):
    i = pl.program_id(1)
    w = DIL_Q_TILE + 2 * DIL_PAD

    @pl.when(i == 0)
    def _():
        zeros = jnp.zeros((DIL_PAD, GROUP_W), BF16)
        for pad, src in ((kpad, k_ref), (vpad, v_ref)):
            pad[0:DIL_PAD, :] = zeros
            pad[DIL_PAD + L:DIL_PAD + L + DIL_PAD, :] = zeros
            pad[DIL_PAD:DIL_PAD + L, :] = src[...]

    q0 = pl.multiple_of(i * DIL_Q_TILE, DIL_Q_TILE)
    q = q_ref[...]
    kw = kpad[pl.ds(q0, w), :]
    vw = vpad[pl.ds(q0, w), :]
    jpos = q0 - DIL_PAD + lax.broadcasted_iota(jnp.int32, (1, w), 1)
    in_seq = jnp.where(jpos >= 0, jnp.where(jpos < L, 0.0, NEG_INF), NEG_INF)
    bias = bias_ref[...] + in_seq
    acc = jnp.zeros(q.shape, F32)
    for h in range(N_HEADS):
        hm = _head_mask(h, HEAD_DIM, BF16)
        s = _dot_nt(q * hm, kw) + bias
        m = jnp.max(s, axis=-1, keepdims=True)
        p = jnp.exp2((s - m).astype(BF16))
        ones_next = _head_mask((h + 1) % N_HEADS, HEAD_DIM, BF16)
        o = _dot(p, vw * hm + ones_next)
        den = pltpu.roll(o, GROUP_W - HEAD_DIM, axis=1)
        in_head = _head_mask(h, HEAD_DIM, F32) > 0.0
        acc = acc + jnp.where(in_head, o / jnp.where(in_head, den, 1.0), 0.0)
    o_ref[...] = acc.astype(BF16)


def _dilattn(h_main, bias, B, L):
    nq = L // DIL_Q_TILE
    w = DIL_Q_TILE + 2 * DIL_PAD
    return pl.pallas_call(
        functools.partial(_dilattn_kernel, L=L),
        grid=(B, nq),
        in_specs=[
            pl.BlockSpec((DIL_Q_TILE, GROUP_W), lambda b, i: (b * nq + i, BLK_AQ)),
            pl.BlockSpec((L, GROUP_W), lambda b, i: (b, BLK_AK)),
            pl.BlockSpec((L, GROUP_W), lambda b, i: (b, BLK_AV)),
            pl.BlockSpec((DIL_Q_TILE, w), lambda b, i: (0, 0)),
        ],
        out_specs=pl.BlockSpec((DIL_Q_TILE, GROUP_W), lambda b, i: (b * nq + i, 0)),
        out_shape=jax.ShapeDtypeStruct((B * L, GROUP_W), BF16),
        scratch_shapes=[pltpu.VMEM((L + 2 * DIL_PAD, GROUP_W), BF16),
                        pltpu.VMEM((L + 2 * DIL_PAD, GROUP_W), BF16)],
        compiler_params=_cparams(("parallel", "arbitrary")),
        name="dilattn",
    )(h_main, h_main, h_main, bias)


def _split3(x):
    hi = x.astype(BF16)
    r1 = x - hi.astype(F32)
    mid = r1.astype(BF16)
    lo = (r1 - mid.astype(F32)).astype(BF16)
    return hi, mid, lo


def _split_dot(x, mat):
    return sum(_dot(t, mat) for t in _split3(x))


def _tri_dot(mat, x):
    return sum(_dot(mat, t) for t in _split3(x))


def _dot_f32(a, b):
    ah, am, al = _split3(a)
    bh, bm, bl = _split3(b)
    return (_dot(ah, bh) + (_dot(ah, bm) + _dot(am, bh))
            + (_dot(ah, bl) + _dot(am, bm) + _dot(al, bh)))


def _diffattn_kernel(q_ref, k_ref, v_ref, lam_ref, g_ref, hmean_ref, o_ref,
                     vt_ref, sa_ref, sb_ref, m_ref, oh_ref, *, lam_init, L):
    hd = HEAD_DIM

    @pl.when(pl.program_id(1) == 0)
    def _():
        tail = (lax.broadcasted_iota(jnp.int32, (DIFF_VT_ROWS - hd, DIFF_K_CHUNK), 0) == 0)
        for c in range(L // DIFF_K_CHUNK):
            lo, hi = c * DIFF_K_CHUNK, (c + 1) * DIFF_K_CHUNK
            vt = v_ref[lo:hi, :].astype(F32).T.astype(BF16)
            for h in range(N_HEADS):
                vt_ref[h, 0:hd, lo:hi] = vt[h * hd:(h + 1) * hd]
                vt_ref[h, hd:DIFF_VT_ROWS, lo:hi] = tail.astype(BF16)

    lp = lam_ref[...]
    lam = (jnp.exp(jnp.sum(lp[0:1] * lp[1:2], axis=-1, keepdims=True))
           - jnp.exp(jnp.sum(lp[2:3] * lp[3:4], axis=-1, keepdims=True)) + lam_init)
    qt = q_ref[...].astype(F32).T
    feat_group = lax.broadcasted_iota(jnp.int32, (GROUP_W, 1), 0) // DIFF_QK_DIM
    tq = qt.shape[1]
    chunks = [(c * DIFF_K_CHUNK, (c + 1) * DIFF_K_CHUNK) for c in range(L // DIFF_K_CHUNK)]

    def masked_qt(h):
        return jnp.concatenate([jnp.where(feat_group == 2 * h + c, qt, 0.0) for c in range(2)],
                               axis=1).astype(BF16)

    def stage(h_next, nxt, h_cur, cur):
        qt2 = masked_qt(h_next)
        mx = None
        if h_cur is not None:
            m = m_ref[cur, 0:1, :]
            o = jnp.zeros((DIFF_VT_ROWS, 2 * tq), F32)
        for lo, hi in chunks:
            s = _dot(k_ref[lo:hi, :], qt2)
            s_refs[nxt][lo:hi, :] = s
            cm = jnp.max(s, axis=0, keepdims=True)
            mx = cm if mx is None else jnp.maximum(mx, cm)
            if h_cur is not None:
                p = jnp.exp2((s_refs[cur][lo:hi, :] - m).astype(BF16))
                o = o + _dot(vt_ref[h_cur, :, lo:hi], p)
        m_ref[nxt] = jnp.broadcast_to(mx, m_ref.shape[1:])
        if h_cur is not None:
            on = o[0:hd] * (1.0 / o[hd:hd + 1])
            oh_ref[h_cur] = on[:, 0:tq] - on[:, tq:2 * tq] * lam

    s_refs = (sa_ref, sb_ref)
    stage(0, 0, None, None)

    def body(j, carry):
        stage(2 * j + 1, 1, 2 * j, 0)
        stage(2 * j + 2, 0, 2 * j + 1, 1)
        return carry

    lax.fori_loop(0, N_HEADS // 2, body, 0)
    acc = jnp.concatenate([oh_ref[h] for h in range(N_HEADS)], axis=0).T
    ms = _split_dot(acc * acc, hmean_ref[...])
    y = acc * lax.rsqrt(ms + DIFF_SUBLN_EPS) * g_ref[...] * (1.0 - lam_init)
    o_ref[...] = y.astype(BF16)


def _diffattn(h_main, lam_p, subln_g, hmean, lam_init, B, L):
    nq = L // DIFF_Q_TILE
    return pl.pallas_call(
        functools.partial(_diffattn_kernel, lam_init=lam_init, L=L),
        grid=(B, nq),
        in_specs=[
            pl.BlockSpec((DIFF_Q_TILE, GROUP_W), lambda b, i: (b * nq + i, BLK_BQ)),
            pl.BlockSpec((L, GROUP_W), lambda b, i: (b, BLK_BK)),
            pl.BlockSpec((L, GROUP_W), lambda b, i: (b, BLK_BV)),
            pl.BlockSpec((4, DIFF_QK_DIM), lambda b, i: (0, 0)),
            pl.BlockSpec((1, GROUP_W), lambda b, i: (0, 0)),
            pl.BlockSpec((GROUP_W, GROUP_W), lambda b, i: (0, 0)),
        ],
        out_specs=pl.BlockSpec((DIFF_Q_TILE, GROUP_W), lambda b, i: (b * nq + i, 0)),
        out_shape=jax.ShapeDtypeStruct((B * L, GROUP_W), BF16),
        scratch_shapes=[pltpu.VMEM((N_HEADS, DIFF_VT_ROWS, L), BF16),
                        pltpu.VMEM((L, 2 * DIFF_Q_TILE), F32),
                        pltpu.VMEM((L, 2 * DIFF_Q_TILE), F32),
                        pltpu.VMEM((2, 8, 2 * DIFF_Q_TILE), F32),
                        pltpu.VMEM((N_HEADS, HEAD_DIM, DIFF_Q_TILE), F32)],
        compiler_params=_cparams(("parallel", "arbitrary")),
        name="diffattn",
    )(h_main, h_main, h_main, lam_p, subln_g, hmean)


def _log_sigmoid(v):
    return jnp.minimum(v, 0.0) - jnp.log(1.0 + jnp.exp(-jnp.abs(v)))


def _mlstm_kernel(q_ref, k_ref, v_ref, gi_ref, gf_ref, cw_ref, cb_ref, bi_ref, bf_ref,
                  ltri_ref, utri_ref, ecol_ref, elane_ref, kmask_ref, vmask_ref, ones_ref,
                  bd_ref, hsum_ref, o_ref,
                  qpad, kpad, qs, ks, hfw, hbw, cst, nst, mst, *, L):
    T = MLSTM_T
    nc = L // T
    halo = CONV_HALO
    nh = N_HEADS

    zpad = jnp.zeros((halo, GROUP_W), BF16)
    for pad, src in ((qpad, q_ref), (kpad, k_ref)):
        pad[0:halo, :] = zpad
        pad[halo + L:halo + L + halo, :] = zpad
        pad[halo:halo + L, :] = src[...]

    def conv_body(c, carry):
        r0 = pl.multiple_of(c * T, T)
        for idx, (pad, dst, scale) in enumerate(((qpad, qs, 1.0),
                                                 (kpad, ks, 1.0 / math.sqrt(HEAD_DIM)))):
            xw = pad[pl.ds(r0, T + 2 * halo), :].astype(F32)
            xm = pltpu.roll(xw, 1, axis=0)[halo:halo + T]
            xp = pltpu.roll(xw, T + 2 * halo - 1, axis=0)[halo:halo + T]
            xc = xw[halo:halo + T]
            lo, hi = idx * GROUP_W, (idx + 1) * GROUP_W
            y = (xm * cw_ref[0:1, lo:hi] + xc * cw_ref[1:2, lo:hi] + xp * cw_ref[2:3, lo:hi]
                 + cb_ref[0:1, lo:hi])
            dst[pl.ds(r0, T), :] = (_silu(y) * scale).astype(BF16)
        return carry

    lax.fori_loop(0, nc, conv_body, 0)

    cst[...] = jnp.zeros(cst.shape, F32)
    nst[...] = jnp.zeros(nst.shape, F32)
    mst[...] = jnp.zeros(mst.shape, F32)

    is_fw = lax.broadcasted_iota(jnp.int32, (1, LANE), 1) < nh
    rowi = lax.broadcasted_iota(jnp.int32, (T, LANE), 0)
    row4 = lax.broadcasted_iota(jnp.int32, (T, nh * T), 0)
    col4 = lax.broadcasted_iota(jnp.int32, (T, nh * T), 1) % T
    causal = (col4 <= row4, col4 >= row4)

    def body(c, carry):
        rows = (pl.multiple_of(c * T, T), pl.multiple_of((nc - 1 - c) * T, T))
        gate_i = jnp.where(is_fw, gi_ref[pl.ds(rows[0], T), :], gi_ref[pl.ds(rows[1], T), :])
        gate_f = jnp.where(is_fw, gf_ref[pl.ds(rows[0], T), :], gf_ref[pl.ds(rows[1], T), :])
        gate_i = gate_i + bi_ref[...]
        parts = _split3(_log_sigmoid(gate_f + bf_ref[...]))
        pre = sum(_dot(ltri_ref[...], t) for t in parts)
        suf = sum(_dot(utri_ref[...], t) for t in parts)
        cum = jnp.where(is_fw, pre, suf)
        b_end = jnp.where(is_fw, pre[T - 1:T, :], suf[0:1, :])
        key_w = gate_i - cum

        pmax, smax = key_w, key_w
        sh = 1
        while sh < T:
            pmax = jnp.maximum(pmax, jnp.where(rowi >= sh, pltpu.roll(pmax, sh, axis=0), NEG_INF))
            smax = jnp.maximum(smax, jnp.where(rowi < T - sh, pltpu.roll(smax, T - sh, axis=0),
                                               NEG_INF))
            sh *= 2
        m_prev = mst[0:1, :]
        inter = cum + m_prev
        m_t = jnp.maximum(inter, cum + jnp.where(is_fw, pmax, smax))
        u = cum - m_t
        a = b_end + key_w
        m_new = jnp.maximum(b_end + m_prev, jnp.max(a, axis=0, keepdims=True))
        mst[0:1, :] = m_new
        stack = jnp.concatenate(
            [jnp.exp(inter - m_t), jnp.exp(-m_t), jnp.exp(a - m_new),
             jnp.broadcast_to(jnp.exp(b_end + m_prev - m_new), (8, LANE))], axis=0)
        st_hi, st_mid, _ = _split3(stack)
        key_w_t = key_w.T

        for d in range(2):
            r0 = rows[d]
            qc = qs[pl.ds(r0, T), :]
            kc = ks[pl.ds(r0, T), :]
            vc = v_ref[pl.ds(r0, T), :]
            ex = _dot(st_hi, elane_ref[d]) + _dot(st_mid, elane_ref[d])
            w_inter, floor, w_key = ex[0:T], ex[T:2 * T], ex[2 * T:3 * T]
            sp_row = ex[3 * T:3 * T + 1]

            r_row = jnp.concatenate([key_w_t[d * nh + h:d * nh + h + 1, :] for h in range(nh)],
                                    axis=1)
            dlog = _split_dot(u, ecol_ref[d]) + r_row
            decay = jnp.exp(jnp.where(causal[d], dlog, NEG_INF))

            kt = kc.astype(F32).T.astype(BF16)
            qk = _dot(qc, jnp.concatenate([kt] * nh, axis=1) * kmask_ref[...]) * decay
            qk_hi = qk.astype(BF16)
            qk_lo = (qk - qk_hi.astype(F32)).astype(BF16)
            vblk = jnp.concatenate([vc] * nh, axis=0) * vmask_ref[...]
            num = _dot(qk_hi, vblk)
            den = _dot(qk_hi, ones_ref[...]) + _dot(qk_lo, ones_ref[...])

            c_prev = cst[d]
            n_prev = nst[d:d + 1, :]
            num = num + w_inter * _dot(qc, c_prev.astype(BF16))
            den = den + w_inter * _dot((qc.astype(F32) * n_prev).astype(BF16), hsum_ref[...])
            h_out = num / jnp.maximum(jnp.abs(den), floor)
            if d == 0:
                hfw[pl.ds(r0, T), :] = h_out
            else:
                hbw[pl.ds(r0, T), :] = h_out

            kw = kc.astype(F32) * w_key
            cst[d] = c_prev * sp_row + _dot_tn(kw.astype(BF16), vc) * bd_ref[...]
            nst[d:d + 1, :] = n_prev * sp_row + jnp.sum(kw, axis=0, keepdims=True)
        return carry

    lax.fori_loop(0, nc, body, 0)
    o_ref[...] = (hfw[...] + hbw[...]).astype(BF16)


def _mlstm_consts():
    T, nh, g = MLSTM_T, N_HEADS, GROUP_W
    tri = np.tril(np.ones((T, T), np.float32))
    src = np.arange(LANE)[:, None]
    ecol = np.stack([(src == d * nh + np.arange(nh * T)[None, :] // T) for d in range(2)])
    elane = np.stack([(src == d * nh + np.arange(g)[None, :] // HEAD_DIM) for d in range(2)])
    head_of = np.arange(g) // HEAD_DIM
    blk_of = np.arange(nh * T) // T
    kmask = head_of[:, None] == blk_of[None, :]
    vmask = blk_of[:, None] == head_of[None, :]
    same_head = head_of[:, None] == head_of[None, :]
    b16 = lambda a: jnp.asarray(a.astype(np.float32), BF16)
    return (b16(tri), b16(tri.T), b16(ecol), b16(elane), b16(kmask), b16(vmask), b16(vmask),
            jnp.asarray(same_head.astype(np.float32)), b16(same_head))


def _mlstm(h_main, gate_i, gate_f, conv_w, conv_b, bias_i, bias_f, consts, B, L):
    T = MLSTM_T
    const = lambda b: (0, 0)
    const3 = lambda b: (0, 0, 0)
    return pl.pallas_call(
        functools.partial(_mlstm_kernel, L=L),
        grid=(B,),
        in_specs=[
            pl.BlockSpec((L, GROUP_W), lambda b: (b, BLK_DQ)),
            pl.BlockSpec((L, GROUP_W), lambda b: (b, BLK_DK)),
            pl.BlockSpec((L, GROUP_W), lambda b: (b, BLK_DV)),
            pl.BlockSpec((L, LANE), lambda b: (b, 0)),
            pl.BlockSpec((L, LANE), lambda b: (b, 0)),
            pl.BlockSpec((3, 2 * GROUP_W), const),
            pl.BlockSpec((1, 2 * GROUP_W), const),
            pl.BlockSpec((1, LANE), const),
            pl.BlockSpec((1, LANE), const),
            pl.BlockSpec((T, T), const),
            pl.BlockSpec((T, T), const),
            pl.BlockSpec((2, LANE, N_HEADS * T), const3),
            pl.BlockSpec((2, LANE, GROUP_W), const3),
            pl.BlockSpec((GROUP_W, N_HEADS * T), const),
            pl.BlockSpec((N_HEADS * T, GROUP_W), const),
            pl.BlockSpec((N_HEADS * T, GROUP_W), const),
            pl.BlockSpec((GROUP_W, GROUP_W), const),
            pl.BlockSpec((GROUP_W, GROUP_W), const),
        ],
        out_specs=pl.BlockSpec((L, GROUP_W), lambda b: (b, 0)),
        out_shape=jax.ShapeDtypeStruct((B * L, GROUP_W), BF16),
        scratch_shapes=[
            pltpu.VMEM((L + 2 * CONV_HALO, GROUP_W), BF16),
            pltpu.VMEM((L + 2 * CONV_HALO, GROUP_W), BF16),
            pltpu.VMEM((L, GROUP_W), BF16),
            pltpu.VMEM((L, GROUP_W), BF16),
            pltpu.VMEM((L, GROUP_W), F32),
            pltpu.VMEM((L, GROUP_W), F32),
            pltpu.VMEM((2, GROUP_W, GROUP_W), F32),
            pltpu.VMEM((8, GROUP_W), F32),
            pltpu.VMEM((8, LANE), F32),
        ],
        compiler_params=_cparams(("parallel",)),
        name="mlstm",
    )(h_main, h_main, h_main, gate_i, gate_f, conv_w, conv_b, bias_i, bias_f, *consts)


def _hy_dims(L):
    h1 = L // HY_N2
    k1h = h1 + 1
    k1p = -(-k1h // BF16_SUBLANES) * BF16_SUBLANES
    kg = 12 if k1p % 12 == 0 else BF16_SUBLANES
    return h1, k1h, k1p, kg


def _hy_tables(L):
    h1, k1h, k1p, _ = _hy_dims(L)
    n1_len = 2 * h1
    n = 2 * L
    k1 = np.arange(k1p)[:, None].astype(np.float64)
    live = (np.arange(k1p) < k1h)[:, None]
    n1 = np.arange(h1)[None, :].astype(np.float64)
    ang = 2.0 * np.pi * k1 * n1 / n1_len
    m1 = np.concatenate([np.where(live, np.cos(ang), 0.0), np.where(live, -np.sin(ang), 0.0)], 0)

    n2 = np.arange(HY_N2)[None, None, :].astype(np.float64)
    k2 = np.arange(HY_N2)[None, :, None].astype(np.float64)
    kk = np.arange(k1p)[:, None, None] + n1_len * k2
    th = 2.0 * np.pi * kk * n2 / n
    fr, fi = np.cos(th), -np.sin(th)
    f3 = np.concatenate([np.concatenate([fr, -fi], 2), np.concatenate([fi, fr], 2)], 1)
    er, ei = np.transpose(np.cos(th), (0, 2, 1)), np.transpose(np.sin(th), (0, 2, 1))
    f3i = np.concatenate([np.concatenate([er, -ei], 2), np.concatenate([ei, er], 2)], 1)
    live3 = (np.arange(k1p) < k1h)[:, None, None]
    f3 = np.where(live3, f3, 0.0)
    f3i = np.where(live3, f3i, 0.0)

    nn1 = np.arange(h1)[:, None].astype(np.float64)
    kc = np.arange(k1p)[None, :].astype(np.float64)
    ph = 2.0 * np.pi * nn1 * kc / n1_len
    edge = (np.arange(k1p) == 0) | (np.arange(k1p) == h1)
    livec = (np.arange(k1p) < k1h)[None, :]
    m4r = np.where(livec, np.where(edge[None, :], np.cos(ph), 2.0 * np.cos(ph)), 0.0) / n
    m4i = np.where(livec & ~edge[None, :], -2.0 * np.sin(ph), 0.0) / n
    return (jnp.asarray(m1, BF16), jnp.asarray(f3, BF16), jnp.asarray(f3i, BF16),
            jnp.asarray(m4r, BF16), jnp.asarray(m4i, BF16))


def _to_wide(x, h1):
    return x.astype(F32).reshape(h1, HY_N2 * GROUP_W)


def _hy_stage1(z_bf16, m1_ref, a_ref, k1p):
    a = _dot(m1_ref[...], z_bf16)
    a_ref[0, 0] = a[0:k1p].reshape(k1p, HY_N2, GROUP_W).astype(BF16)
    a_ref[0, 1] = a[k1p:2 * k1p].reshape(k1p, HY_N2, GROUP_W).astype(BF16)


def _hy_conv3_wide(x, w_ref, b_ref, j, h1):
    c = GROUP_W
    wl = HY_N2 * c
    rowi = lax.broadcasted_iota(jnp.int32, (h1, c), 0)
    tail = x[:, wl - c:wl]
    head = x[:, 0:c]
    prev_tail = jnp.where(rowi == 0, 0.0, pltpu.roll(tail, 1, axis=0))
    next_head = jnp.where(rowi == h1 - 1, 0.0, pltpu.roll(head, h1 - 1, axis=0))
    xm = jnp.concatenate([prev_tail, x[:, 0:wl - c]], axis=1)
    xp = jnp.concatenate([x[:, c:wl], next_head], axis=1)
    return (xm * w_ref[3 * j:3 * j + 1, :] + x * w_ref[3 * j + 1:3 * j + 2, :]
            + xp * w_ref[3 * j + 2:3 * j + 3, :] + b_ref[j:j + 1, :])


def _hy_front_kernel(v_ref, x1_ref, x2_ref, w_ref, b_ref, m1_ref,
                     z_ref, x1c_ref, x2c_ref, a_ref, *, h1, k1p):
    z = _hy_conv3_wide(_to_wide(v_ref[...], h1), w_ref, b_ref, 0, h1).astype(BF16)
    z_ref[0] = z
    x1c_ref[0] = _hy_conv3_wide(_to_wide(x1_ref[...], h1), w_ref, b_ref, 1, h1).astype(BF16)
    x2c_ref[0] = _hy_conv3_wide(_to_wide(x2_ref[...], h1), w_ref, b_ref, 2, h1).astype(BF16)
    _hy_stage1(z, m1_ref, a_ref, k1p)


def _hy_front(cv, cx1, cx2, w_wide, b_wide, m1, B, L):
    h1, _, k1p, _ = _hy_dims(L)
    wl = HY_N2 * GROUP_W
    nat = pl.BlockSpec((L, GROUP_W), lambda b: (b, 0))
    seq = pl.BlockSpec((1, h1, wl), lambda b: (b, 0, 0))
    wide = jax.ShapeDtypeStruct((B, h1, wl), BF16)
    slab = (1, 2, k1p, HY_N2, GROUP_W)
    return pl.pallas_call(
        functools.partial(_hy_front_kernel, h1=h1, k1p=k1p),
        grid=(B,),
        in_specs=[nat, nat, nat,
                  pl.BlockSpec((9, wl), lambda b: (0, 0)),
                  pl.BlockSpec((3, wl), lambda b: (0, 0)),
                  pl.BlockSpec((2 * k1p, h1), lambda b: (0, 0))],
        out_specs=(seq, seq, seq, pl.BlockSpec(slab, lambda b: (b, 0, 0, 0, 0))),
        out_shape=(wide, wide, wide,
                   jax.ShapeDtypeStruct((B, 2, k1p, HY_N2, GROUP_W), BF16)),
        compiler_params=_cparams(("parallel",)),
        name="hy_front",
    )(cv, cx1, cx2, w_wide, b_wide, m1)


def _hy_stage1_kernel(z_ref, m1_ref, a_ref, *, h1, k1p):
    _hy_stage1(_to_wide(z_ref[0], h1).astype(BF16), m1_ref, a_ref, k1p)


def _hy_stage1_call(z_nat, m1, L):
    h1, _, k1p, _ = _hy_dims(L)
    nb = z_nat.shape[0]
    return pl.pallas_call(
        functools.partial(_hy_stage1_kernel, h1=h1, k1p=k1p),
        grid=(nb,),
        in_specs=[pl.BlockSpec((1, L, GROUP_W), lambda b: (b, 0, 0)),
                  pl.BlockSpec((2 * k1p, h1), lambda b: (0, 0))],
        out_specs=pl.BlockSpec((1, 2, k1p, HY_N2, GROUP_W), lambda b: (b, 0, 0, 0, 0)),
        out_shape=jax.ShapeDtypeStruct((nb, 2, k1p, HY_N2, GROUP_W), BF16),
        compiler_params=_cparams(("parallel",)),
        name="hy_stage1",
    )(z_nat, m1)


def _hy_spectrum_kernel(af_ref, ab_ref, f3_ref, g_ref, *, kg, k1h):
    grp = pl.program_id(0)

    @pl.when(grp * kg < k1h)
    def _():
        for kk in range(kg):
            xs = []
            for a_ref in (af_ref, ab_ref):
                a2 = jnp.concatenate([a_ref[0, 0, kk], a_ref[0, 1, kk]], axis=0)
                xs.append(_dot(f3_ref[kk], a2))
            g_ref[0, 0, kk] = xs[0][0:HY_N2] + xs[1][0:HY_N2]
            g_ref[0, 1, kk] = xs[0][HY_N2:2 * HY_N2] - xs[1][HY_N2:2 * HY_N2]

    @pl.when(grp * kg >= k1h)
    def _():
        g_ref[...] = jnp.zeros(g_ref.shape, F32)


def _hy_spectrum(a_filt, f3, L):
    _, k1h, k1p, kg = _hy_dims(L)
    a5 = a_filt
    blk = (1, 2, kg, HY_N2, GROUP_W)
    return pl.pallas_call(
        functools.partial(_hy_spectrum_kernel, kg=kg, k1h=k1h),
        grid=(k1p // kg, HYENA_ORDER),
        in_specs=[pl.BlockSpec(blk, lambda g, o: (2 * o, 0, g, 0, 0)),
                  pl.BlockSpec(blk, lambda g, o: (2 * o + 1, 0, g, 0, 0)),
                  pl.BlockSpec((kg, 2 * HY_N2, 2 * HY_N2), lambda g, o: (g, 0, 0))],
        out_specs=pl.BlockSpec(blk, lambda g, o: (o, 0, g, 0, 0)),
        out_shape=jax.ShapeDtypeStruct((HYENA_ORDER, 2, k1p, HY_N2, GROUP_W), F32),
        compiler_params=_cparams(("parallel", "parallel")),
        name="hy_spectrum",
    )(a5, a5, f3)


def _hy_mid_kernel(a_ref, g_ref, f3_ref, f3i_ref, b_ref, *, kg, k1h):
    grp = pl.program_id(0)

    @pl.when(grp * kg < k1h)
    def _():
        for kk in range(kg):
            a2 = jnp.concatenate([a_ref[0, 0, kk], a_ref[0, 1, kk]], axis=0)
            x = _dot(f3_ref[kk], a2)
            xr, xi = x[0:HY_N2], x[HY_N2:2 * HY_N2]
            gr, gi = g_ref[0, 0, kk], g_ref[0, 1, kk]
            y2 = jnp.concatenate([xr * gr - xi * gi, xr * gi + xi * gr], axis=0).astype(BF16)
            bm = _dot(f3i_ref[kk], y2)
            b_ref[0, 0, kk] = bm[0:HY_N2].astype(BF16)
            b_ref[0, 1, kk] = bm[HY_N2:2 * HY_N2].astype(BF16)

    @pl.when(grp * kg >= k1h)
    def _():
        b_ref[...] = jnp.zeros(b_ref.shape, BF16)


def _hy_mid(a, g_spec, order, f3, f3i, B, L):
    _, k1h, k1p, kg = _hy_dims(L)
    blk = (1, 2, kg, HY_N2, GROUP_W)
    tab = pl.BlockSpec((kg, 2 * HY_N2, 2 * HY_N2), lambda g, b: (g, 0, 0))
    return pl.pallas_call(
        functools.partial(_hy_mid_kernel, kg=kg, k1h=k1h),
        grid=(k1p // kg, B),
        in_specs=[pl.BlockSpec(blk, lambda g, b: (b, 0, g, 0, 0)),
                  pl.BlockSpec(blk, lambda g, b: (order, 0, g, 0, 0)),
                  tab, tab],
        out_specs=pl.BlockSpec(blk, lambda g, b: (b, 0, g, 0, 0)),
        out_shape=jax.ShapeDtypeStruct((B, 2, k1p, HY_N2, GROUP_W), BF16),
        compiler_params=_cparams(("parallel", "parallel")),
        name="hy_mid",
    )(a, g_spec, f3, f3i)


def _hy_back_kernel(b_ref, z_ref, x_ref, bias_ref, m4r_ref, m4i_ref, *rest, k1p, last):
    wl = HY_N2 * GROUP_W
    br = b_ref[0, 0].astype(F32).reshape(k1p, wl).astype(BF16)
    bi = b_ref[0, 1].astype(F32).reshape(k1p, wl).astype(BF16)
    y = _dot(m4r_ref[...], br) + _dot(m4i_ref[...], bi)
    z_new = x_ref[0].astype(F32) * (y + z_ref[0].astype(F32) * bias_ref[...])
    zb = z_new.astype(BF16)
    if last:
        (o_ref,) = rest
        o_ref[...] = z_new.reshape(o_ref.shape).astype(BF16)
    else:
        m1_ref, o_ref, a_ref = rest
        o_ref[0] = zb
        _hy_stage1(zb, m1_ref, a_ref, k1p)


def _hy_back(b, z, xg, bias_wide, m4r, m4i, m1, B, L, last):
    h1, _, k1p, _ = _hy_dims(L)
    wl = HY_N2 * GROUP_W
    seq = pl.BlockSpec((1, h1, wl), lambda i: (i, 0, 0))
    slab = pl.BlockSpec((1, 2, k1p, HY_N2, GROUP_W), lambda i: (i, 0, 0, 0, 0))
    const = lambda i: (0, 0)
    in_specs = [slab, seq, seq, pl.BlockSpec((1, wl), const),
                pl.BlockSpec((h1, k1p), const), pl.BlockSpec((h1, k1p), const)]
    args = [b, z, xg, bias_wide, m4r, m4i]
    if last:
        out_specs = pl.BlockSpec((L, GROUP_W), lambda i: (i, 0))
        out_shape = jax.ShapeDtypeStruct((B * L, GROUP_W), BF16)
    else:
        in_specs.append(pl.BlockSpec((2 * k1p, h1), const))
        args.append(m1)
        out_specs = (seq, slab)
        out_shape = (jax.ShapeDtypeStruct((B, h1, wl), BF16),
                     jax.ShapeDtypeStruct((B, 2, k1p, HY_N2, GROUP_W), BF16))
    return pl.pallas_call(
        functools.partial(_hy_back_kernel, k1p=k1p, last=last),
        grid=(B,),
        in_specs=in_specs,
        out_specs=out_specs,
        out_shape=out_shape,
        compiler_params=_cparams(("parallel",)),
        name="hy_back_last" if last else "hy_back",
    )(*args)


def _hy_filter_kernel(f_ref, t_ref, w1_ref, b1_ref, w2_ref, b2_ref, w3_ref, fr_ref, ad_ref,
                      o_ref, *, tile):
    i = pl.program_id(0)
    freq = fr_ref[...]
    z = jnp.sin(freq * (_dot_f32(f_ref[...], w1_ref[...]) + b1_ref[...]))
    z = jnp.sin(freq * (_dot_f32(z, w2_ref[...]) + b2_ref[...]))
    decay = jnp.exp(-t_ref[...] * ad_ref[...])
    rowi = i * tile + lax.broadcasted_iota(jnp.int32, (tile, GROUP_W), 0)
    for j in range(2 * HYENA_ORDER):
        hj = _dot_f32(z, w3_ref[:, j * GROUP_W:(j + 1) * GROUP_W]) * decay
        if j % 2 == 1:
            hj = jnp.where(rowi == 0, 0.0, hj)
        o_ref[j] = hj.astype(BF16)


def _hy_filter(feats, tcol, w1, b1, w2, b2, w3, freq, absdelta, L):
    tile = math.gcd(L, ROW_TILE)
    const = lambda i: (0, 0)
    return pl.pallas_call(
        functools.partial(_hy_filter_kernel, tile=tile),
        grid=(L // tile,),
        in_specs=[pl.BlockSpec((tile, LANE), lambda i: (i, 0)),
                  pl.BlockSpec((tile, 1), lambda i: (i, 0)),
                  pl.BlockSpec((LANE, HYENA_HIDDEN), const),
                  pl.BlockSpec((1, HYENA_HIDDEN), const),
                  pl.BlockSpec((HYENA_HIDDEN, HYENA_HIDDEN), const),
                  pl.BlockSpec((1, HYENA_HIDDEN), const),
                  pl.BlockSpec((HYENA_HIDDEN, 2 * HYENA_ORDER * GROUP_W), const),
                  pl.BlockSpec((1, HYENA_HIDDEN), const),
                  pl.BlockSpec((1, GROUP_W), const)],
        out_specs=pl.BlockSpec((2 * HYENA_ORDER, tile, GROUP_W), lambda i: (0, i, 0)),
        out_shape=jax.ShapeDtypeStruct((2 * HYENA_ORDER, L, GROUP_W), BF16),
        compiler_params=_cparams(("parallel",)),
        name="hy_filter",
    )(feats, tcol, w1, b1, w2, b2, w3, freq, absdelta)


def _outproj_kernel(x_ref, oa_ref, ob_ref, oc_ref, od_ref, ox_ref,
                    ga_ref, gb_ref, gc_ref, gdo_ref, gdg_ref, gx_ref, w_ref, png_ref, out_ref):
    f = lambda r: r[...].astype(F32)
    branches = (
        f(oa_ref) * _silu(f(ga_ref)),
        f(ob_ref) * _silu(f(gb_ref)),
        f(oc_ref) * _silu(f(gc_ref)),
        f(od_ref) * _sigmoid(f(gdo_ref)) * _silu(f(gdg_ref)),
        f(ox_ref) * _silu(f(gx_ref)),
    )
    y = jnp.zeros(x_ref.shape, F32)
    for j, br in enumerate(branches):
        y = y + _dot(br.astype(BF16), w_ref[j * GROUP_W:(j + 1) * GROUP_W, :])
    ms = jnp.mean(y * y, axis=-1, keepdims=True)
    out_ref[...] = x_ref[...] + y * lax.rsqrt(ms + NORM_EPS) * png_ref[...]


def _outproj(x2d, outs, h_main, w_out, png):
    n_tok = x2d.shape[0]
    nt = n_tok // ROW_TILE
    row = lambda i: (i, 0)
    blk = lambda j: pl.BlockSpec((ROW_TILE, GROUP_W), lambda i: (i, j))
    return pl.pallas_call(
        _outproj_kernel,
        grid=(nt,),
        in_specs=[pl.BlockSpec((ROW_TILE, D_MODEL), row)]
        + [pl.BlockSpec((ROW_TILE, GROUP_W), row)] * 5
        + [blk(BLK_AG), blk(BLK_BG), blk(BLK_CG), blk(BLK_DO), blk(BLK_DG), blk(BLK_XG)]
        + [pl.BlockSpec((5 * GROUP_W, D_MODEL), lambda i: (0, 0)),
           pl.BlockSpec((1, D_MODEL), lambda i: (0, 0))],
        out_specs=pl.BlockSpec((ROW_TILE, D_MODEL), row),
        out_shape=jax.ShapeDtypeStruct((n_tok, D_MODEL), F32),
        compiler_params=_cparams(("parallel",)),
        name="outproj",
    )(x2d, *outs, h_main, h_main, h_main, h_main, h_main, h_main, w_out, png)


def _rope_tables(L, group, rot_dim):
    half = rot_dim // 2
    inv = 1.0 / (ROPE_THETA ** (jnp.arange(0, rot_dim, 2, dtype=F32) / rot_dim))
    ang = jnp.arange(L, dtype=F32)[:, None] * inv[None, :]
    cos, sin = jnp.cos(ang), jnp.sin(ang)
    lane = np.arange(GROUP_W) % group
    in_rot = lane < rot_dim
    idx = lane % half
    c = jnp.where(in_rot[None, :], cos[:, idx], 1.0)
    s = jnp.where(in_rot[None, :], sin[:, idx], 0.0)
    p = np.zeros((GROUP_W, GROUP_W), np.float32)
    for j in range(GROUP_W):
        if lane[j] < half:
            p[j + half, j] = -1.0
        elif lane[j] < rot_dim:
            p[j - half, j] = 1.0
    return c, s, jnp.asarray(p, BF16)


def _hyena_features(L):
    t = jnp.linspace(0.0, 1.0, L, dtype=F32)[:, None]
    bands = jnp.linspace(1e-4, HYENA_BANDS - 1, HYENA_BANDS, dtype=F32)
    ang = (2.0 * math.pi / L) * jnp.arange(L, dtype=F32)[:, None] * bands[None, :]
    feats = jnp.concatenate([t, jnp.cos(ang), -jnp.sin(ang)], axis=-1)
    return jnp.pad(feats, ((0, 0), (0, LANE - HYENA_EMB))), t


def _relayout_w_in(w):
    g = GROUP_W
    off_c, off_d = 8 * g, 12 * g
    off_gate = off_d + 5 * g
    off_x = off_gate + N_MLSTM_GATES
    main = jnp.concatenate([w[:, 0:off_c], w[:, off_c + 3 * g:off_c + 4 * g],
                            w[:, off_d:off_gate], w[:, off_x:off_x + 2 * g]], axis=1)
    hy = w[:, off_c:off_c + 3 * g]
    gate_i, gate_f = _split_gates(w[:, off_gate:off_x])
    return jnp.concatenate([main, hy, gate_i, gate_f], axis=1).astype(BF16)


def _split_gates(t):
    nh = N_HEADS
    pad = [(0, 0)] * (t.ndim - 1) + [(0, LANE - 2 * nh)]
    gi = jnp.concatenate([t[..., 0:nh], t[..., 2 * nh:3 * nh]], axis=-1)
    gf = jnp.concatenate([t[..., nh:2 * nh], t[..., 3 * nh:4 * nh]], axis=-1)
    return jnp.pad(gi, pad), jnp.pad(gf, pad)


def _trunk(x, mem, pre_norm_g, post_norm_g, w_in, w_out, diff_lambda, diff_subln_g,
           hy_conv_w, hy_conv_b, hy_ffn_w1, hy_ffn_b1, hy_ffn_w2, hy_ffn_b2, hy_ffn_w3,
           hy_freq, hy_bias, ml_conv_w, ml_conv_b, ml_gate_b, mem_norm_g, w_mem_kv):
    B, L, _ = x.shape
    M = mem.shape[1]
    depth = w_in.shape[0]
    g = GROUP_W
    h1, _, k1p, _ = _hy_dims(L)
    wl = HY_N2 * g

    ca, sa, pa = _rope_tables(L, HEAD_DIM, HEAD_DIM // ROPE_FRACTION)
    cb, sb, pb = _rope_tables(L, DIFF_QK_DIM, DIFF_QK_DIM // ROPE_FRACTION)
    rope_tabs = (ca, sa, cb, sb, pa, pb)
    dil_bias = jnp.asarray(_dil_bias_table())
    feats, tcol = _hyena_features(L)
    absdelta = jnp.abs(jnp.linspace(math.log(HYENA_TARGET) / HYENA_SLOW_DECAY,
                                    math.log(HYENA_TARGET) / HYENA_FAST_DECAY, g, dtype=F32))[None]
    m1, f3, f3i, m4r, m4i = _hy_tables(L)
    head_of = np.arange(g) // HEAD_DIM
    same_head = (head_of[:, None] == head_of[None, :]).astype(np.float32)
    hmean = jnp.asarray(same_head / HEAD_DIM, BF16)
    mlstm_consts = _mlstm_consts()

    x2d = x.reshape(B * L, D_MODEL)
    mem2d = mem.reshape(B * M, D_MODEL)
    for li in range(depth):
        lam_init = 0.8 - 0.6 * math.exp(-0.3 * li)
        w_all = _relayout_w_in(w_in[li])
        h_main, cv, cx1, cx2, gate_i, gate_f = _inproj(x2d, pre_norm_g[li][None], w_all,
                                                       rope_tabs, B, L)

        oa = _dilattn(h_main, dil_bias, B, L)
        ob = _diffattn(h_main, diff_lambda[li], jnp.tile(diff_subln_g[li], N_HEADS)[None],
                       hmean, lam_init, B, L)

        w1 = jnp.pad(hy_ffn_w1[li], ((0, LANE - HYENA_EMB), (0, 0)))
        hfilt = _hy_filter(feats, tcol, w1, hy_ffn_b1[li][None], hy_ffn_w2[li],
                           hy_ffn_b2[li][None], hy_ffn_w3[li], hy_freq[li][None], absdelta, L)
        a_filt = _hy_stage1_call(hfilt, m1, L)
        g_spec = _hy_spectrum(a_filt, f3, L)
        cw = jnp.tile(hy_conv_w[li].reshape(3, 3, g).transpose(1, 0, 2).reshape(9, g), (1, HY_N2))
        cbw = jnp.tile(hy_conv_b[li].reshape(3, g), (1, HY_N2))
        z, x1c, x2c, a = _hy_front(cv, cx1, cx2, cw, cbw, m1, B, L)
        bias_w = jnp.tile(hy_bias[li], (1, HY_N2))
        bsp = _hy_mid(a, g_spec, 0, f3, f3i, B, L)
        z, a = _hy_back(bsp, z, x1c, bias_w[0:1], m4r, m4i, m1, B, L, last=False)
        bsp = _hy_mid(a, g_spec, 1, f3, f3i, B, L)
        oc = _hy_back(bsp, z, x2c, bias_w[1:2], m4r, m4i, m1, B, L, last=True)

        bias_i, bias_f = _split_gates(ml_gate_b[li][None])
        od = _mlstm(h_main, gate_i, gate_f, ml_conv_w[li], ml_conv_b[li][None], bias_i, bias_f,
                    mlstm_consts, B, L)

        mkv = _memkv(mem2d, mem_norm_g[li][None], w_mem_kv[li].astype(BF16))
        ox = _memattn(h_main, mkv, B, L, M)

        x2d = _outproj(x2d, (oa, ob, oc, od, ox), h_main, w_out[li].astype(BF16),
                       post_norm_g[li][None])
    return x2d.reshape(B, L, D_MODEL)


def kernel(x_prompt, x_sample, mem_prompt, mem_sample, pre_norm_g, post_norm_g, w_in, w_out,
           diff_lambda, diff_subln_g, hy_conv_w, hy_conv_b, hy_ffn_w1, hy_ffn_b1, hy_ffn_w2,
           hy_ffn_b2, hy_ffn_w3, hy_freq, hy_bias, ml_conv_w, ml_conv_b, ml_gate_b,
           mem_norm_g, w_mem_kv):
    nb = x_prompt.shape[0]
    x = jnp.concatenate([x_prompt, x_sample], axis=0)
    mem = jnp.concatenate([mem_prompt, mem_sample], axis=0)
    y = _trunk(x, mem, pre_norm_g, post_norm_g, w_in, w_out, diff_lambda, diff_subln_g,
               hy_conv_w, hy_conv_b, hy_ffn_w1, hy_ffn_b1, hy_ffn_w2, hy_ffn_b2, hy_ffn_w3,
               hy_freq, hy_bias, ml_conv_w, ml_conv_b, ml_gate_b, mem_norm_g, w_mem_kv)
    return y[:nb], y[nb:]
```

```python
import functools
import math

import numpy as np
import jax
import jax.numpy as jnp
from jax import lax
from jax.experimental import pallas as pl
from jax.experimental.pallas import tpu as pltpu

F32 = jnp.float32
BF16 = jnp.bfloat16

D_MODEL = 1024
HEAD_DIM = 64
GROUP_W = 256
N_HEADS = GROUP_W // HEAD_DIM
NORM_EPS = 1e-6
NEG_INF = -1e30
ROPE_THETA = 500000.0
ROPE_FRACTION = 4
DIL_PATTERNS = ((128, 1), (512, 4), (2048, 16))
DIFF_QK_DIM = HEAD_DIM // 2
DIFF_SUBLN_EPS = 1e-5
HYENA_ORDER = 2
HYENA_BANDS = 16
HYENA_EMB = 1 + 2 * HYENA_BANDS
HYENA_HIDDEN = 64
HYENA_FAST_DECAY = 0.3
HYENA_SLOW_DECAY = 1.5
HYENA_TARGET = 1e-2
N_MLSTM_GATES = 4 * N_HEADS

(BLK_AQ, BLK_AK, BLK_AV, BLK_AG, BLK_BQ, BLK_BK, BLK_BV, BLK_BG, BLK_CG,
 BLK_DQ, BLK_DK, BLK_DV, BLK_DO, BLK_DG, BLK_XQ, BLK_XG) = range(16)
N_MAIN_BLK = 16
MAIN_W = N_MAIN_BLK * GROUP_W
W_ALL = MAIN_W + 4 * GROUP_W

VMEM_LIMIT_BYTES = 56 * 1024 * 1024
LANE = 128
BF16_SUBLANES = 16

ROW_TILE = 512
DIL_Q_TILE = 256
DIL_PAD = 1024
DIFF_Q_TILE = 256
DIFF_K_CHUNK = 512
DIFF_VT_ROWS = HEAD_DIM + BF16_SUBLANES
LOG2E = 1.4426950408889634
MEM_Q_TILE = 512
MLSTM_T = 128
CONV_HALO = 16
HY_N2 = 128


def _cparams(sem, vmem=VMEM_LIMIT_BYTES):
    return pltpu.CompilerParams(dimension_semantics=sem, vmem_limit_bytes=vmem)


def _sigmoid(v):
    return 1.0 / (1.0 + jnp.exp(-v))


def _silu(v):
    return v * _sigmoid(v)


def _dot(a, b):
    return jnp.dot(a, b, preferred_element_type=F32)


def _dot_nt(a, b):
    return lax.dot_general(a, b, (((1,), (1,)), ((), ())), preferred_element_type=F32)


def _dot_tn(a, b):
    return lax.dot_general(a, b, (((0,), (0,)), ((), ())), preferred_element_type=F32)


def _head_mask(h, width, dtype):
    lane = lax.broadcasted_iota(jnp.int32, (1, GROUP_W), 1)
    return ((lane // width) == h).astype(dtype)


def _inproj_kernel(x_ref, g_ref, w_ref, ca_ref, sa_ref, cb_ref, sb_ref, pa_ref, pb_ref,
                   h_ref, cv_ref, cx1_ref, cx2_ref, gi_ref, gf_ref):
    x = x_ref[...]
    ms = jnp.mean(x * x, axis=-1, keepdims=True)
    xn = (x * lax.rsqrt(ms + NORM_EPS) * g_ref[...]).astype(BF16)

    def proj(j, width=GROUP_W):
        return _dot(xn, w_ref[:, j * GROUP_W:j * GROUP_W + width])

    def rope(acc, c_ref, s_ref, p_ref):
        partner = _dot(acc.astype(BF16), p_ref[...])
        return acc * c_ref[...] + partner * s_ref[...]

    def finish(j, acc):
        if j in (BLK_AQ, BLK_AK):
            acc = rope(acc, ca_ref, sa_ref, pa_ref)
        if j in (BLK_BQ, BLK_BK):
            acc = rope(acc, cb_ref, sb_ref, pb_ref)
        if j in (BLK_AQ, BLK_XQ):
            acc = acc * (LOG2E / math.sqrt(HEAD_DIM))
        if j == BLK_BQ:
            acc = acc * (LOG2E / math.sqrt(DIFF_QK_DIM))
        if j < N_MAIN_BLK:
            h_ref[:, j * GROUP_W:(j + 1) * GROUP_W] = acc.astype(BF16)
        elif j < N_MAIN_BLK + 3:
            (cv_ref, cx1_ref, cx2_ref)[j - N_MAIN_BLK][...] = acc.astype(BF16)
        else:
            gi_ref[...] = acc[:, 0:LANE]
            gf_ref[...] = acc[:, LANE:2 * LANE]

    for j in range(0, W_ALL // GROUP_W, 2):
        acc2 = proj(j, 2 * GROUP_W)
        finish(j, acc2[:, 0:GROUP_W])
        finish(j + 1, acc2[:, GROUP_W:2 * GROUP_W])


def _inproj(x2d, g, w_all, rope_tabs, B, L):
    ca, sa, cb, sb, pa, pb = rope_tabs
    nt = L // ROW_TILE
    n_tok = B * L
    row = lambda i, b: (b * nt + i, 0)
    tab = lambda i, b: (i, 0)
    const = lambda i, b: (0, 0)
    out_shapes = (
        jax.ShapeDtypeStruct((n_tok, MAIN_W), BF16),
        jax.ShapeDtypeStruct((n_tok, GROUP_W), BF16),
        jax.ShapeDtypeStruct((n_tok, GROUP_W), BF16),
        jax.ShapeDtypeStruct((n_tok, GROUP_W), BF16),
        jax.ShapeDtypeStruct((n_tok, LANE), F32),
        jax.ShapeDtypeStruct((n_tok, LANE), F32),
    )
    return pl.pallas_call(
        _inproj_kernel,
        grid=(nt, B),
        in_specs=[
            pl.BlockSpec((ROW_TILE, D_MODEL), row),
            pl.BlockSpec((1, D_MODEL), const),
            pl.BlockSpec((D_MODEL, W_ALL), const),
            pl.BlockSpec((ROW_TILE, GROUP_W), tab),
            pl.BlockSpec((ROW_TILE, GROUP_W), tab),
            pl.BlockSpec((ROW_TILE, GROUP_W), tab),
            pl.BlockSpec((ROW_TILE, GROUP_W), tab),
            pl.BlockSpec((GROUP_W, GROUP_W), const),
            pl.BlockSpec((GROUP_W, GROUP_W), const),
        ],
        out_specs=(
            pl.BlockSpec((ROW_TILE, MAIN_W), row),
            pl.BlockSpec((ROW_TILE, GROUP_W), row),
            pl.BlockSpec((ROW_TILE, GROUP_W), row),
            pl.BlockSpec((ROW_TILE, GROUP_W), row),
            pl.BlockSpec((ROW_TILE, LANE), row),
            pl.BlockSpec((ROW_TILE, LANE), row),
        ),
        out_shape=out_shapes,
        compiler_params=_cparams(("parallel", "parallel")),
        name="inproj",
    )(x2d, g, w_all, ca, sa, cb, sb, pa, pb)


def _memkv_kernel(m_ref, g_ref, w_ref, o_ref):
    x = m_ref[...]
    ms = jnp.mean(x * x, axis=-1, keepdims=True)
    xn = (x * lax.rsqrt(ms + NORM_EPS) * g_ref[...]).astype(BF16)
    o_ref[...] = _dot(xn, w_ref[...]).astype(BF16)


def _memkv(mem2d, g, w):
    rows = mem2d.shape[0]
    tile = math.gcd(rows, ROW_TILE)
    return pl.pallas_call(
        _memkv_kernel,
        grid=(rows // tile,),
        in_specs=[
            pl.BlockSpec((tile, D_MODEL), lambda i: (i, 0)),
            pl.BlockSpec((1, D_MODEL), lambda i: (0, 0)),
            pl.BlockSpec((D_MODEL, 2 * GROUP_W), lambda i: (0, 0)),
        ],
        out_specs=pl.BlockSpec((tile, 2 * GROUP_W), lambda i: (i, 0)),
        out_shape=jax.ShapeDtypeStruct((rows, 2 * GROUP_W), BF16),
        compiler_params=_cparams(("parallel",)),
        name="memkv",
    )(mem2d, g, w)


def _memattn_kernel(q_ref, mk_ref, mv_ref, o_ref):
    q = q_ref[...]
    mk = mk_ref[...]
    mv = mv_ref[...]
    acc = jnp.zeros(q.shape, F32)
    for h in range(N_HEADS):
        hm = _head_mask(h, HEAD_DIM, BF16)
        s = _dot_nt(q * hm, mk)
        m = jnp.max(s, axis=-1, keepdims=True)
        p = jnp.exp2(s - m)
        l = jnp.sum(p, axis=-1, keepdims=True)
        acc = acc + _dot(p.astype(BF16), mv * hm) * (1.0 / l)
    o_ref[...] = acc.astype(BF16)


def _memattn(h_main, mkv, B, L, M):
    nq = L // MEM_Q_TILE
    return pl.pallas_call(
        _memattn_kernel,
        grid=(B, nq),
        in_specs=[
            pl.BlockSpec((MEM_Q_TILE, GROUP_W), lambda b, i: (b * nq + i, BLK_XQ)),
            pl.BlockSpec((M, GROUP_W), lambda b, i: (b, 0)),
            pl.BlockSpec((M, GROUP_W), lambda b, i: (b, 1)),
        ],
        out_specs=pl.BlockSpec((MEM_Q_TILE, GROUP_W), lambda b, i: (b * nq + i, 0)),
        out_shape=jax.ShapeDtypeStruct((B * L, GROUP_W), BF16),
        compiler_params=_cparams(("parallel", "parallel")),
        name="memattn",
    )(h_main, mkv, mkv)


def _dil_bias_table():
    w = DIL_Q_TILE + 2 * DIL_PAD
    d = np.arange(DIL_Q_TILE)[:, None] - np.arange(w)[None, :] + DIL_PAD
    count = np.zeros(d.shape, np.float64)
    for win, dil in DIL_PATTERNS:
        reach = (win // (2 * dil)) * dil
        count += (d % dil == 0) & (np.abs(d) <= reach)
    return np.where(count > 0, np.log2(np.maximum(count, 1.0)), NEG_INF).astype(np.float32)


def _dilattn_kernel(q_ref, k_ref, v_ref, bias_ref, o_ref, kpad, vpad, *, L):
    i = pl.program_id(1)
    w = DIL_Q_TILE + 2 * DIL_PAD

    @pl.when(i == 0)
    def _():
        zeros = jnp.zeros((DIL_PAD, GROUP_W), BF16)
        for pad, src in ((kpad, k_ref), (vpad, v_ref)):
            pad[0:DIL_PAD, :] = zeros
            pad[DIL_PAD + L:DIL_PAD + L + DIL_PAD, :] = zeros
            pad[DIL_PAD:DIL_PAD + L, :] = src[...]

    q0 = pl.multiple_of(i * DIL_Q_TILE, DIL_Q_TILE)
    q = q_ref[...]
    kw = kpad[pl.ds(q0, w), :]
    vw = vpad[pl.ds(q0, w), :]
    jpos = q0 - DIL_PAD + lax.broadcasted_iota(jnp.int32, (1, w), 1)
    in_seq = jnp.where(jpos >= 0, jnp.where(jpos < L, 0.0, NEG_INF), NEG_INF)
    bias = bias_ref[...] + in_seq
    acc = jnp.zeros(q.shape, F32)
    for h in range(N_HEADS):
        hm = _head_mask(h, HEAD_DIM, BF16)
        s = _dot_nt(q * hm, kw) + bias
        m = jnp.max(s, axis=-1, keepdims=True)
        p = jnp.exp2((s - m).astype(BF16))
        ones_next = _head_mask((h + 1) % N_HEADS, HEAD_DIM, BF16)
        o = _dot(p, vw * hm + ones_next)
        den = pltpu.roll(o, GROUP_W - HEAD_DIM, axis=1)
        in_head = _head_mask(h, HEAD_DIM, F32) > 0.0
        acc = acc + jnp.where(in_head, o / jnp.where(in_head, den, 1.0), 0.0)
    o_ref[...] = acc.astype(BF16)


def _dilattn(h_main, bias, B, L):
    nq = L // DIL_Q_TILE
    w = DIL_Q_TILE + 2 * DIL_PAD
    return pl.pallas_call(
        functools.partial(_dilattn_kernel, L=L),
        grid=(B, nq),
        in_specs=[
            pl.BlockSpec((DIL_Q_TILE, GROUP_W), lambda b, i: (b * nq + i, BLK_AQ)),
            pl.BlockSpec((L, GROUP_W), lambda b, i: (b, BLK_AK)),
            pl.BlockSpec((L, GROUP_W), lambda b, i: (b, BLK_AV)),
            pl.BlockSpec((DIL_Q_TILE, w), lambda b, i: (0, 0)),
        ],
        out_specs=pl.BlockSpec((DIL_Q_TILE, GROUP_W), lambda b, i: (b * nq + i, 0)),
        out_shape=jax.ShapeDtypeStruct((B * L, GROUP_W), BF16),
        scratch_shapes=[pltpu.VMEM((L + 2 * DIL_PAD, GROUP_W), BF16),
                        pltpu.VMEM((L + 2 * DIL_PAD, GROUP_W), BF16)],
        compiler_params=_cparams(("parallel", "arbitrary")),
        name="dilattn",
    )(h_main, h_main, h_main, bias)


def _split3(x):
    hi = x.astype(BF16)
    r1 = x - hi.astype(F32)
    mid = r1.astype(BF16)
    lo = (r1 - mid.astype(F32)).astype(BF16)
    return hi, mid, lo


def _split_dot(x, mat):
    return sum(_dot(t, mat) for t in _split3(x))


def _tri_dot(mat, x):
    return sum(_dot(mat, t) for t in _split3(x))


def _dot_f32(a, b):
    ah, am, al = _split3(a)
    bh, bm, bl = _split3(b)
    return (_dot(ah, bh) + (_dot(ah, bm) + _dot(am, bh))
            + (_dot(ah, bl) + _dot(am, bm) + _dot(al, bh)))


def _diffattn_kernel(q_ref, k_ref, v_ref, lam_ref, g_ref, hmean_ref, o_ref,
                     vt_ref, sa_ref, sb_ref, m_ref, oh_ref, *, lam_init, L):
    hd = HEAD_DIM

    @pl.when(pl.program_id(1) == 0)
    def _():
        tail = (lax.broadcasted_iota(jnp.int32, (DIFF_VT_ROWS - hd, DIFF_K_CHUNK), 0) == 0)
        for c in range(L // DIFF_K_CHUNK):
            lo, hi = c * DIFF_K_CHUNK, (c + 1) * DIFF_K_CHUNK
            vt = v_ref[lo:hi, :].astype(F32).T.astype(BF16)
            for h in range(N_HEADS):
                vt_ref[h, 0:hd, lo:hi] = vt[h * hd:(h + 1) * hd]
                vt_ref[h, hd:DIFF_VT_ROWS, lo:hi] = tail.astype(BF16)

    lp = lam_ref[...]
    lam = (jnp.exp(jnp.sum(lp[0:1] * lp[1:2], axis=-1, keepdims=True))
           - jnp.exp(jnp.sum(lp[2:3] * lp[3:4], axis=-1, keepdims=True)) + lam_init)
    qt = q_ref[...].astype(F32).T
    feat_group = lax.broadcasted_iota(jnp.int32, (GROUP_W, 1), 0) // DIFF_QK_DIM
    tq = qt.shape[1]
    chunks = [(c * DIFF_K_CHUNK, (c + 1) * DIFF_K_CHUNK) for c in range(L // DIFF_K_CHUNK)]

    def masked_qt(h):
        return jnp.concatenate([jnp.where(feat_group == 2 * h + c, qt, 0.0) for c in range(2)],
                               axis=1).astype(BF16)

    def stage(h_next, nxt, h_cur, cur):
        qt2 = masked_qt(h_next)
        mx = None
        if h_cur is not None:
            m = m_ref[cur, 0:1, :]
            o = jnp.zeros((DIFF_VT_ROWS, 2 * tq), F32)
        for lo, hi in chunks:
            s = _dot(k_ref[lo:hi, :], qt2)
            s_refs[nxt][lo:hi, :] = s
            cm = jnp.max(s, axis=0, keepdims=True)
            mx = cm if mx is None else jnp.maximum(mx, cm)
            if h_cur is not None:
                p = jnp.exp2((s_refs[cur][lo:hi, :] - m).astype(BF16))
                o = o + _dot(vt_ref[h_cur, :, lo:hi], p)
        m_ref[nxt] = jnp.broadcast_to(mx, m_ref.shape[1:])
        if h_cur is not None:
            on = o[0:hd] * (1.0 / o[hd:hd + 1])
            oh_ref[h_cur] = on[:, 0:tq] - on[:, tq:2 * tq] * lam

    s_refs = (sa_ref, sb_ref)
    stage(0, 0, None, None)

    def body(j, carry):
        stage(2 * j + 1, 1, 2 * j, 0)
        stage(2 * j + 2, 0, 2 * j + 1, 1)
        return carry

    lax.fori_loop(0, N_HEADS // 2, body, 0)
    acc = jnp.concatenate([oh_ref[h] for h in range(N_HEADS)], axis=0).T
    ms = _split_dot(acc * acc, hmean_ref[...])
    y = acc * lax.rsqrt(ms + DIFF_SUBLN_EPS) * g_ref[...] * (1.0 - lam_init)
    o_ref[...] = y.astype(BF16)


def _diffattn(h_main, lam_p, subln_g, hmean, lam_init, B, L):
    nq = L // DIFF_Q_TILE
    return pl.pallas_call(
        functools.partial(_diffattn_kernel, lam_init=lam_init, L=L),
        grid=(B, nq),
        in_specs=[
            pl.BlockSpec((DIFF_Q_TILE, GROUP_W), lambda b, i: (b * nq + i, BLK_BQ)),
            pl.BlockSpec((L, GROUP_W), lambda b, i: (b, BLK_BK)),
            pl.BlockSpec((L, GROUP_W), lambda b, i: (b, BLK_BV)),
            pl.BlockSpec((4, DIFF_QK_DIM), lambda b, i: (0, 0)),
            pl.BlockSpec((1, GROUP_W), lambda b, i: (0, 0)),
            pl.BlockSpec((GROUP_W, GROUP_W), lambda b, i: (0, 0)),
        ],
        out_specs=pl.BlockSpec((DIFF_Q_TILE, GROUP_W), lambda b, i: (b * nq + i, 0)),
        out_shape=jax.ShapeDtypeStruct((B * L, GROUP_W), BF16),
        scratch_shapes=[pltpu.VMEM((N_HEADS, DIFF_VT_ROWS, L), BF16),
                        pltpu.VMEM((L, 2 * DIFF_Q_TILE), F32),
                        pltpu.VMEM((L, 2 * DIFF_Q_TILE), F32),
                        pltpu.VMEM((2, 8, 2 * DIFF_Q_TILE), F32),
                        pltpu.VMEM((N_HEADS, HEAD_DIM, DIFF_Q_TILE), F32)],
        compiler_params=_cparams(("parallel", "arbitrary")),
        name="diffattn",
    )(h_main, h_main, h_main, lam_p, subln_g, hmean)


def _log_sigmoid(v):
    return jnp.minimum(v, 0.0) - jnp.log(1.0 + jnp.exp(-jnp.abs(v)))


def _mlstm_kernel(q_ref, k_ref, v_ref, gi_ref, gf_ref, cw_ref, cb_ref, bi_ref, bf_ref,
                  ltri_ref, utri_ref, ecol_ref, elane_ref, kmask_ref, vmask_ref, ones_ref,
                  bd_ref, hsum_ref, o_ref,
                  qpad, kpad, qs, ks, hfw, hbw, cst, nst, mst, *, L):
    T = MLSTM_T
    nc = L // T
    halo = CONV_HALO
    nh = N_HEADS

    zpad = jnp.zeros((halo, GROUP_W), BF16)
    for pad, src in ((qpad, q_ref), (kpad, k_ref)):
        pad[0:halo, :] = zpad
        pad[halo + L:halo + L + halo, :] = zpad
        pad[halo:halo + L, :] = src[...]

    def conv_body(c, carry):
        r0 = pl.multiple_of(c * T, T)
        for idx, (pad, dst, scale) in enumerate(((qpad, qs, 1.0),
                                                 (kpad, ks, 1.0 / math.sqrt(HEAD_DIM)))):
            xw = pad[pl.ds(r0, T + 2 * halo), :].astype(F32)
            xm = pltpu.roll(xw, 1, axis=0)[halo:halo + T]
            xp = pltpu.roll(xw, T + 2 * halo - 1, axis=0)[halo:halo + T]
            xc = xw[halo:halo + T]
            lo, hi = idx * GROUP_W, (idx + 1) * GROUP_W
            y = (xm * cw_ref[0:1, lo:hi] + xc * cw_ref[1:2, lo:hi] + xp * cw_ref[2:3, lo:hi]
                 + cb_ref[0:1, lo:hi])
            dst[pl.ds(r0, T), :] = (_silu(y) * scale).astype(BF16)
        return carry

    lax.fori_loop(0, nc, conv_body, 0)

    cst[...] = jnp.zeros(cst.shape, F32)
    nst[...] = jnp.zeros(nst.shape, F32)
    mst[...] = jnp.zeros(mst.shape, F32)

    is_fw = lax.broadcasted_iota(jnp.int32, (1, LANE), 1) < nh
    rowi = lax.broadcasted_iota(jnp.int32, (T, LANE), 0)
    row4 = lax.broadcasted_iota(jnp.int32, (T, nh * T), 0)
    col4 = lax.broadcasted_iota(jnp.int32, (T, nh * T), 1) % T
    causal = (col4 <= row4, col4 >= row4)

    def body(c, carry):
        rows = (pl.multiple_of(c * T, T), pl.multiple_of((nc - 1 - c) * T, T))
        gate_i = jnp.where(is_fw, gi_ref[pl.ds(rows[0], T), :], gi_ref[pl.ds(rows[1], T), :])
        gate_f = jnp.where(is_fw, gf_ref[pl.ds(rows[0], T), :], gf_ref[pl.ds(rows[1], T), :])
        gate_i = gate_i + bi_ref[...]
        parts = _split3(_log_sigmoid(gate_f + bf_ref[...]))
        pre = sum(_dot(ltri_ref[...], t) for t in parts)
        suf = sum(_dot(utri_ref[...], t) for t in parts)
        cum = jnp.where(is_fw, pre, suf)
        b_end = jnp.where(is_fw, pre[T - 1:T, :], suf[0:1, :])
        key_w = gate_i - cum

        pmax, smax = key_w, key_w
        sh = 1
        while sh < T:
            pmax = jnp.maximum(pmax, jnp.where(rowi >= sh, pltpu.roll(pmax, sh, axis=0), NEG_INF))
            smax = jnp.maximum(smax, jnp.where(rowi < T - sh, pltpu.roll(smax, T - sh, axis=0),
                                               NEG_INF))
            sh *= 2
        m_prev = mst[0:1, :]
        inter = cum + m_prev
        m_t = jnp.maximum(inter, cum + jnp.where(is_fw, pmax, smax))
        u = cum - m_t
        a = b_end + key_w
        m_new = jnp.maximum(b_end + m_prev, jnp.max(a, axis=0, keepdims=True))
        mst[0:1, :] = m_new
        stack = jnp.concatenate(
            [jnp.exp(inter - m_t), jnp.exp(-m_t), jnp.exp(a - m_new),
             jnp.broadcast_to(jnp.exp(b_end + m_prev - m_new), (8, LANE))], axis=0)
        st_hi, st_mid, _ = _split3(stack)
        key_w_t = key_w.T

        for d in range(2):
            r0 = rows[d]
            qc = qs[pl.ds(r0, T), :]
            kc = ks[pl.ds(r0, T), :]
            vc = v_ref[pl.ds(r0, T), :]
            ex = _dot(st_hi, elane_ref[d]) + _dot(st_mid, elane_ref[d])
            w_inter, floor, w_key = ex[0:T], ex[T:2 * T], ex[2 * T:3 * T]
            sp_row = ex[3 * T:3 * T + 1]

            r_row = jnp.concatenate([key_w_t[d * nh + h:d * nh + h + 1, :] for h in range(nh)],
                                    axis=1)
            dlog = _split_dot(u, ecol_ref[d]) + r_row
            decay = jnp.exp(jnp.where(causal[d], dlog, NEG_INF))

            kt = kc.astype(F32).T.astype(BF16)
            qk = _dot(qc, jnp.concatenate([kt] * nh, axis=1) * kmask_ref[...]) * decay
            qk_hi = qk.astype(BF16)
            qk_lo = (qk - qk_hi.astype(F32)).astype(BF16)
            vblk = jnp.concatenate([vc] * nh, axis=0) * vmask_ref[...]
            num = _dot(qk_hi, vblk)
            den = _dot(qk_hi, ones_ref[...]) + _dot(qk_lo, ones_ref[...])

            c_prev = cst[d]
            n_prev = nst[d:d + 1, :]
            num = num + w_inter * _dot(qc, c_prev.astype(BF16))
            den = den + w_inter * _dot((qc.astype(F32) * n_prev).astype(BF16), hsum_ref[...])
            h_out = num / jnp.maximum(jnp.abs(den), floor)
            if d == 0:
                hfw[pl.ds(r0, T), :] = h_out
            else:
                hbw[pl.ds(r0, T), :] = h_out

            kw = kc.astype(F32) * w_key
            cst[d] = c_prev * sp_row + _dot_tn(kw.astype(BF16), vc) * bd_ref[...]
            nst[d:d + 1, :] = n_prev * sp_row + jnp.sum(kw, axis=0, keepdims=True)
        return carry

    lax.fori_loop(0, nc, body, 0)
    o_ref[...] = (hfw[...] + hbw[...]).astype(BF16)


def _mlstm_consts():
    T, nh, g = MLSTM_T, N_HEADS, GROUP_W
    tri = np.tril(np.ones((T, T), np.float32))
    src = np.arange(LANE)[:, None]
    ecol = np.stack([(src == d * nh + np.arange(nh * T)[None, :] // T) for d in range(2)])
    elane = np.stack([(src == d * nh + np.arange(g)[None, :] // HEAD_DIM) for d in range(2)])
    head_of = np.arange(g) // HEAD_DIM
    blk_of = np.arange(nh * T) // T
    kmask = head_of[:, None] == blk_of[None, :]
    vmask = blk_of[:, None] == head_of[None, :]
    same_head = head_of[:, None] == head_of[None, :]
    b16 = lambda a: jnp.asarray(a.astype(np.float32), BF16)
    return (b16(tri), b16(tri.T), b16(ecol), b16(elane), b16(kmask), b16(vmask), b16(vmask),
            jnp.asarray(same_head.astype(np.float32)), b16(same_head))


def _mlstm(h_main, gate_i, gate_f, conv_w, conv_b, bias_i, bias_f, consts, B, L):
    T = MLSTM_T
    const = lambda b: (0, 0)
    const3 = lambda b: (0, 0, 0)
    return pl.pallas_call(
        functools.partial(_mlstm_kernel, L=L),
        grid=(B,),
        in_specs=[
            pl.BlockSpec((L, GROUP_W), lambda b: (b, BLK_DQ)),
            pl.BlockSpec((L, GROUP_W), lambda b: (b, BLK_DK)),
            pl.BlockSpec((L, GROUP_W), lambda b: (b, BLK_DV)),
            pl.BlockSpec((L, LANE), lambda b: (b, 0)),
            pl.BlockSpec((L, LANE), lambda b: (b, 0)),
            pl.BlockSpec((3, 2 * GROUP_W), const),
            pl.BlockSpec((1, 2 * GROUP_W), const),
            pl.BlockSpec((1, LANE), const),
            pl.BlockSpec((1, LANE), const),
            pl.BlockSpec((T, T), const),
            pl.BlockSpec((T, T), const),
            pl.BlockSpec((2, LANE, N_HEADS * T), const3),
            pl.BlockSpec((2, LANE, GROUP_W), const3),
            pl.BlockSpec((GROUP_W, N_HEADS * T), const),
            pl.BlockSpec((N_HEADS * T, GROUP_W), const),
            pl.BlockSpec((N_HEADS * T, GROUP_W), const),
            pl.BlockSpec((GROUP_W, GROUP_W), const),
            pl.BlockSpec((GROUP_W, GROUP_W), const),
        ],
        out_specs=pl.BlockSpec((L, GROUP_W), lambda b: (b, 0)),
        out_shape=jax.ShapeDtypeStruct((B * L, GROUP_W), BF16),
        scratch_shapes=[
            pltpu.VMEM((L + 2 * CONV_HALO, GROUP_W), BF16),
            pltpu.VMEM((L + 2 * CONV_HALO, GROUP_W), BF16),
            pltpu.VMEM((L, GROUP_W), BF16),
            pltpu.VMEM((L, GROUP_W), BF16),
            pltpu.VMEM((L, GROUP_W), F32),
            pltpu.VMEM((L, GROUP_W), F32),
            pltpu.VMEM((2, GROUP_W, GROUP_W), F32),
            pltpu.VMEM((8, GROUP_W), F32),
            pltpu.VMEM((8, LANE), F32),
        ],
        compiler_params=_cparams(("parallel",)),
        name="mlstm",
    )(h_main, h_main, h_main, gate_i, gate_f, conv_w, conv_b, bias_i, bias_f, *consts)


def _hy_dims(L):
    h1 = L // HY_N2
    k1h = h1 + 1
    k1p = -(-k1h // BF16_SUBLANES) * BF16_SUBLANES
    kg = 12 if k1p % 12 == 0 else BF16_SUBLANES
    return h1, k1h, k1p, kg


def _hy_tables(L):
    h1, k1h, k1p, _ = _hy_dims(L)
    n1_len = 2 * h1
    n = 2 * L
    k1 = np.arange(k1p)[:, None].astype(np.float64)
    live = (np.arange(k1p) < k1h)[:, None]
    n1 = np.arange(h1)[None, :].astype(np.float64)
    ang = 2.0 * np.pi * k1 * n1 / n1_len
    m1 = np.concatenate([np.where(live, np.cos(ang), 0.0), np.where(live, -np.sin(ang), 0.0)], 0)

    n2 = np.arange(HY_N2)[None, None, :].astype(np.float64)
    k2 = np.arange(HY_N2)[None, :, None].astype(np.float64)
    kk = np.arange(k1p)[:, None, None] + n1_len * k2
    th = 2.0 * np.pi * kk * n2 / n
    fr, fi = np.cos(th), -np.sin(th)
    f3 = np.concatenate([np.concatenate([fr, -fi], 2), np.concatenate([fi, fr], 2)], 1)
    er, ei = np.transpose(np.cos(th), (0, 2, 1)), np.transpose(np.sin(th), (0, 2, 1))
    f3i = np.concatenate([np.concatenate([er, -ei], 2), np.concatenate([ei, er], 2)], 1)
    live3 = (np.arange(k1p) < k1h)[:, None, None]
    f3 = np.where(live3, f3, 0.0)
    f3i = np.where(live3, f3i, 0.0)

    nn1 = np.arange(h1)[:, None].astype(np.float64)
    kc = np.arange(k1p)[None, :].astype(np.float64)
    ph = 2.0 * np.pi * nn1 * kc / n1_len
    edge = (np.arange(k1p) == 0) | (np.arange(k1p) == h1)
    livec = (np.arange(k1p) < k1h)[None, :]
    m4r = np.where(livec, np.where(edge[None, :], np.cos(ph), 2.0 * np.cos(ph)), 0.0) / n
    m4i = np.where(livec & ~edge[None, :], -2.0 * np.sin(ph), 0.0) / n
    return (jnp.asarray(m1, BF16), jnp.asarray(f3, BF16), jnp.asarray(f3i, BF16),
            jnp.asarray(m4r, BF16), jnp.asarray(m4i, BF16))


def _to_wide(x, h1):
    return x.astype(F32).reshape(h1, HY_N2 * GROUP_W)


def _hy_stage1(z_bf16, m1_ref, a_ref, k1p):
    a = _dot(m1_ref[...], z_bf16)
    a_ref[0, 0] = a[0:k1p].reshape(k1p, HY_N2, GROUP_W).astype(BF16)
    a_ref[0, 1] = a[k1p:2 * k1p].reshape(k1p, HY_N2, GROUP_W).astype(BF16)


def _hy_conv3_wide(x, w_ref, b_ref, j, h1):
    c = GROUP_W
    wl = HY_N2 * c
    rowi = lax.broadcasted_iota(jnp.int32, (h1, c), 0)
    tail = x[:, wl - c:wl]
    head = x[:, 0:c]
    prev_tail = jnp.where(rowi == 0, 0.0, pltpu.roll(tail, 1, axis=0))
    next_head = jnp.where(rowi == h1 - 1, 0.0, pltpu.roll(head, h1 - 1, axis=0))
    xm = jnp.concatenate([prev_tail, x[:, 0:wl - c]], axis=1)
    xp = jnp.concatenate([x[:, c:wl], next_head], axis=1)
    return (xm * w_ref[3 * j:3 * j + 1, :] + x * w_ref[3 * j + 1:3 * j + 2, :]
            + xp * w_ref[3 * j + 2:3 * j + 3, :] + b_ref[j:j + 1, :])


def _hy_front_kernel(v_ref, x1_ref, x2_ref, w_ref, b_ref, m1_ref,
                     z_ref, x1c_ref, x2c_ref, a_ref, *, h1, k1p):
    z = _hy_conv3_wide(_to_wide(v_ref[...], h1), w_ref, b_ref, 0, h1).astype(BF16)
    z_ref[0] = z
    x1c_ref[0] = _hy_conv3_wide(_to_wide(x1_ref[...], h1), w_ref, b_ref, 1, h1).astype(BF16)
    x2c_ref[0] = _hy_conv3_wide(_to_wide(x2_ref[...], h1), w_ref, b_ref, 2, h1).astype(BF16)
    _hy_stage1(z, m1_ref, a_ref, k1p)


def _hy_front(cv, cx1, cx2, w_wide, b_wide, m1, B, L):
    h1, _, k1p, _ = _hy_dims(L)
    wl = HY_N2 * GROUP_W
    nat = pl.BlockSpec((L, GROUP_W), lambda b: (b, 0))
    seq = pl.BlockSpec((1, h1, wl), lambda b: (b, 0, 0))
    wide = jax.ShapeDtypeStruct((B, h1, wl), BF16)
    slab = (1, 2, k1p, HY_N2, GROUP_W)
    return pl.pallas_call(
        functools.partial(_hy_front_kernel, h1=h1, k1p=k1p),
        grid=(B,),
        in_specs=[nat, nat, nat,
                  pl.BlockSpec((9, wl), lambda b: (0, 0)),
                  pl.BlockSpec((3, wl), lambda b: (0, 0)),
                  pl.BlockSpec((2 * k1p, h1), lambda b: (0, 0))],
        out_specs=(seq, seq, seq, pl.BlockSpec(slab, lambda b: (b, 0, 0, 0, 0))),
        out_shape=(wide, wide, wide,
                   jax.ShapeDtypeStruct((B, 2, k1p, HY_N2, GROUP_W), BF16)),
        compiler_params=_cparams(("parallel",)),
        name="hy_front",
    )(cv, cx1, cx2, w_wide, b_wide, m1)


def _hy_stage1_kernel(z_ref, m1_ref, a_ref, *, h1, k1p):
    _hy_stage1(_to_wide(z_ref[0], h1).astype(BF16), m1_ref, a_ref, k1p)


def _hy_stage1_call(z_nat, m1, L):
    h1, _, k1p, _ = _hy_dims(L)
    nb = z_nat.shape[0]
    return pl.pallas_call(
        functools.partial(_hy_stage1_kernel, h1=h1, k1p=k1p),
        grid=(nb,),
        in_specs=[pl.BlockSpec((1, L, GROUP_W), lambda b: (b, 0, 0)),
                  pl.BlockSpec((2 * k1p, h1), lambda b: (0, 0))],
        out_specs=pl.BlockSpec((1, 2, k1p, HY_N2, GROUP_W), lambda b: (b, 0, 0, 0, 0)),
        out_shape=jax.ShapeDtypeStruct((nb, 2, k1p, HY_N2, GROUP_W), BF16),
        compiler_params=_cparams(("parallel",)),
        name="hy_stage1",
    )(z_nat, m1)


def _hy_spectrum_kernel(af_ref, ab_ref, f3_ref, g_ref, *, kg, k1h):
    grp = pl.program_id(0)

    @pl.when(grp * kg < k1h)
    def _():
        for kk in range(kg):
            xs = []
            for a_ref in (af_ref, ab_ref):
                a2 = jnp.concatenate([a_ref[0, 0, kk], a_ref[0, 1, kk]], axis=0)
                xs.append(_dot(f3_ref[kk], a2))
            g_ref[0, 0, kk] = xs[0][0:HY_N2] + xs[1][0:HY_N2]
            g_ref[0, 1, kk] = xs[0][HY_N2:2 * HY_N2] - xs[1][HY_N2:2 * HY_N2]

    @pl.when(grp * kg >= k1h)
    def _():
        g_ref[...] = jnp.zeros(g_ref.shape, F32)


def _hy_spectrum(a_filt, f3, L):
    _, k1h, k1p, kg = _hy_dims(L)
    a5 = a_filt
    blk = (1, 2, kg, HY_N2, GROUP_W)
    return pl.pallas_call(
        functools.partial(_hy_spectrum_kernel, kg=kg, k1h=k1h),
        grid=(k1p // kg, HYENA_ORDER),
        in_specs=[pl.BlockSpec(blk, lambda g, o: (2 * o, 0, g, 0, 0)),
                  pl.BlockSpec(blk, lambda g, o: (2 * o + 1, 0, g, 0, 0)),
                  pl.BlockSpec((kg, 2 * HY_N2, 2 * HY_N2), lambda g, o: (g, 0, 0))],
        out_specs=pl.BlockSpec(blk, lambda g, o: (o, 0, g, 0, 0)),
        out_shape=jax.ShapeDtypeStruct((HYENA_ORDER, 2, k1p, HY_N2, GROUP_W), F32),
        compiler_params=_cparams(("parallel", "parallel")),
        name="hy_spectrum",
    )(a5, a5, f3)


def _hy_mid_kernel(a_ref, g_ref, f3_ref, f3i_ref, b_ref, *, kg, k1h):
    grp = pl.program_id(0)

    @pl.when(grp * kg < k1h)
    def _():
        for kk in range(kg):
            a2 = jnp.concatenate([a_ref[0, 0, kk], a_ref[0, 1, kk]], axis=0)
            x = _dot(f3_ref[kk], a2)
            xr, xi = x[0:HY_N2], x[HY_N2:2 * HY_N2]
            gr, gi = g_ref[0, 0, kk], g_ref[0, 1, kk]
            y2 = jnp.concatenate([xr * gr - xi * gi, xr * gi + xi * gr], axis=0).astype(BF16)
            bm = _dot(f3i_ref[kk], y2)
            b_ref[0, 0, kk] = bm[0:HY_N2].astype(BF16)
            b_ref[0, 1, kk] = bm[HY_N2:2 * HY_N2].astype(BF16)

    @pl.when(grp * kg >= k1h)
    def _():
        b_ref[...] = jnp.zeros(b_ref.shape, BF16)


def _hy_mid(a, g_spec, order, f3, f3i, B, L):
    _, k1h, k1p, kg = _hy_dims(L)
    blk = (1, 2, kg, HY_N2, GROUP_W)
    tab = pl.BlockSpec((kg, 2 * HY_N2, 2 * HY_N2), lambda g, b: (g, 0, 0))
    return pl.pallas_call(
        functools.partial(_hy_mid_kernel, kg=kg, k1h=k1h),
        grid=(k1p // kg, B),
        in_specs=[pl.BlockSpec(blk, lambda g, b: (b, 0, g, 0, 0)),
                  pl.BlockSpec(blk, lambda g, b: (order, 0, g, 0, 0)),
                  tab, tab],
        out_specs=pl.BlockSpec(blk, lambda g, b: (b, 0, g, 0, 0)),
        out_shape=jax.ShapeDtypeStruct((B, 2, k1p, HY_N2, GROUP_W), BF16),
        compiler_params=_cparams(("parallel", "parallel")),
        name="hy_mid",
    )(a, g_spec, f3, f3i)


def _hy_back_kernel(b_ref, z_ref, x_ref, bias_ref, m4r_ref, m4i_ref, *rest, k1p, last):
    wl = HY_N2 * GROUP_W
    br = b_ref[0, 0].astype(F32).reshape(k1p, wl).astype(BF16)
    bi = b_ref[0, 1].astype(F32).reshape(k1p, wl).astype(BF16)
    y = _dot(m4r_ref[...], br) + _dot(m4i_ref[...], bi)
    z_new = x_ref[0].astype(F32) * (y + z_ref[0].astype(F32) * bias_ref[...])
    zb = z_new.astype(BF16)
    if last:
        (o_ref,) = rest
        o_ref[...] = z_new.reshape(o_ref.shape).astype(BF16)
    else:
        m1_ref, o_ref, a_ref = rest
        o_ref[0] = zb
        _hy_stage1(zb, m1_ref, a_ref, k1p)


def _hy_back(b, z, xg, bias_wide, m4r, m4i, m1, B, L, last):
    h1, _, k1p, _ = _hy_dims(L)
    wl = HY_N2 * GROUP_W
    seq = pl.BlockSpec((1, h1, wl), lambda i: (i, 0, 0))
    slab = pl.BlockSpec((1, 2, k1p, HY_N2, GROUP_W), lambda i: (i, 0, 0, 0, 0))
    const = lambda i: (0, 0)
    in_specs = [slab, seq, seq, pl.BlockSpec((1, wl), const),
                pl.BlockSpec((h1, k1p), const), pl.BlockSpec((h1, k1p), const)]
    args = [b, z, xg, bias_wide, m4r, m4i]
    if last:
        out_specs = pl.BlockSpec((L, GROUP_W), lambda i: (i, 0))
        out_shape = jax.ShapeDtypeStruct((B * L, GROUP_W), BF16)
    else:
        in_specs.append(pl.BlockSpec((2 * k1p, h1), const))
        args.append(m1)
        out_specs = (seq, slab)
        out_shape = (jax.ShapeDtypeStruct((B, h1, wl), BF16),
                     jax.ShapeDtypeStruct((B, 2, k1p, HY_N2, GROUP_W), BF16))
    return pl.pallas_call(
        functools.partial(_hy_back_kernel, k1p=k1p, last=last),
        grid=(B,),
        in_specs=in_specs,
        out_specs=out_specs,
        out_shape=out_shape,
        compiler_params=_cparams(("parallel",)),
        name="hy_back_last" if last else "hy_back",
    )(*args)


def _hy_filter_kernel(f_ref, t_ref, w1_ref, b1_ref, w2_ref, b2_ref, w3_ref, fr_ref, ad_ref,
                      o_ref, *, tile):
    i = pl.program_id(0)
    freq = fr_ref[...]
    z = jnp.sin(freq * (_dot_f32(f_ref[...], w1_ref[...]) + b1_ref[...]))
    z = jnp.sin(freq * (_dot_f32(z, w2_ref[...]) + b2_ref[...]))
    decay = jnp.exp(-t_ref[...] * ad_ref[...])
    rowi = i * tile + lax.broadcasted_iota(jnp.int32, (tile, GROUP_W), 0)
    for j in range(2 * HYENA_ORDER):
        hj = _dot_f32(z, w3_ref[:, j * GROUP_W:(j + 1) * GROUP_W]) * decay
        if j % 2 == 1:
            hj = jnp.where(rowi == 0, 0.0, hj)
        o_ref[j] = hj.astype(BF16)


def _hy_filter(feats, tcol, w1, b1, w2, b2, w3, freq, absdelta, L):
    tile = math.gcd(L, ROW_TILE)
    const = lambda i: (0, 0)
    return pl.pallas_call(
        functools.partial(_hy_filter_kernel, tile=tile),
        grid=(L // tile,),
        in_specs=[pl.BlockSpec((tile, LANE), lambda i: (i, 0)),
                  pl.BlockSpec((tile, 1), lambda i: (i, 0)),
                  pl.BlockSpec((LANE, HYENA_HIDDEN), const),
                  pl.BlockSpec((1, HYENA_HIDDEN), const),
                  pl.BlockSpec((HYENA_HIDDEN, HYENA_HIDDEN), const),
                  pl.BlockSpec((1, HYENA_HIDDEN), const),
                  pl.BlockSpec((HYENA_HIDDEN, 2 * HYENA_ORDER * GROUP_W), const),
                  pl.BlockSpec((1, HYENA_HIDDEN), const),
                  pl.BlockSpec((1, GROUP_W), const)],
        out_specs=pl.BlockSpec((2 * HYENA_ORDER, tile, GROUP_W), lambda i: (0, i, 0)),
        out_shape=jax.ShapeDtypeStruct((2 * HYENA_ORDER, L, GROUP_W), BF16),
        compiler_params=_cparams(("parallel",)),
        name="hy_filter",
    )(feats, tcol, w1, b1, w2, b2, w3, freq, absdelta)


def _outproj_kernel(x_ref, oa_ref, ob_ref, oc_ref, od_ref, ox_ref,
                    ga_ref, gb_ref, gc_ref, gdo_ref, gdg_ref, gx_ref, w_ref, png_ref, out_ref):
    f = lambda r: r[...].astype(F32)
    branches = (
        f(oa_ref) * _silu(f(ga_ref)),
        f(ob_ref) * _silu(f(gb_ref)),
        f(oc_ref) * _silu(f(gc_ref)),
        f(od_ref) * _sigmoid(f(gdo_ref)) * _silu(f(gdg_ref)),
        f(ox_ref) * _silu(f(gx_ref)),
    )
    y = jnp.zeros(x_ref.shape, F32)
    for j, br in enumerate(branches):
        y = y + _dot(br.astype(BF16), w_ref[j * GROUP_W:(j + 1) * GROUP_W, :])
    ms = jnp.mean(y * y, axis=-1, keepdims=True)
    out_ref[...] = x_ref[...] + y * lax.rsqrt(ms + NORM_EPS) * png_ref[...]


def _outproj(x2d, outs, h_main, w_out, png, row0=0, n_rows=None):
    n_rows = x2d.shape[0] if n_rows is None else n_rows
    nt = n_rows // ROW_TILE
    t0 = row0 // ROW_TILE
    row = lambda i: (t0 + i, 0)
    blk = lambda j: pl.BlockSpec((ROW_TILE, GROUP_W), lambda i: (t0 + i, j))
    return pl.pallas_call(
        _outproj_kernel,
        grid=(nt,),
        in_specs=[pl.BlockSpec((ROW_TILE, D_MODEL), row)]
        + [pl.BlockSpec((ROW_TILE, GROUP_W), row)] * 5
        + [blk(BLK_AG), blk(BLK_BG), blk(BLK_CG), blk(BLK_DO), blk(BLK_DG), blk(BLK_XG)]
        + [pl.BlockSpec((5 * GROUP_W, D_MODEL), lambda i: (0, 0)),
           pl.BlockSpec((1, D_MODEL), lambda i: (0, 0))],
        out_specs=pl.BlockSpec((ROW_TILE, D_MODEL), lambda i: (i, 0)),
        out_shape=jax.ShapeDtypeStruct((n_rows, D_MODEL), F32),
        compiler_params=_cparams(("parallel",)),
        name="outproj",
    )(x2d, *outs, h_main, h_main, h_main, h_main, h_main, h_main, w_out, png)


def _rope_tables(L, group, rot_dim):
    half = rot_dim // 2
    inv = 1.0 / (ROPE_THETA ** (jnp.arange(0, rot_dim, 2, dtype=F32) / rot_dim))
    ang = jnp.arange(L, dtype=F32)[:, None] * inv[None, :]
    cos, sin = jnp.cos(ang), jnp.sin(ang)
    lane = np.arange(GROUP_W) % group
    in_rot = lane < rot_dim
    idx = lane % half
    c = jnp.where(in_rot[None, :], cos[:, idx], 1.0)
    s = jnp.where(in_rot[None, :], sin[:, idx], 0.0)
    p = np.zeros((GROUP_W, GROUP_W), np.float32)
    for j in range(GROUP_W):
        if lane[j] < half:
            p[j + half, j] = -1.0
        elif lane[j] < rot_dim:
            p[j - half, j] = 1.0
    return c, s, jnp.asarray(p, BF16)


def _hyena_features(L):
    t = jnp.linspace(0.0, 1.0, L, dtype=F32)[:, None]
    bands = jnp.linspace(1e-4, HYENA_BANDS - 1, HYENA_BANDS, dtype=F32)
    ang = (2.0 * math.pi / L) * jnp.arange(L, dtype=F32)[:, None] * bands[None, :]
    feats = jnp.concatenate([t, jnp.cos(ang), -jnp.sin(ang)], axis=-1)
    return jnp.pad(feats, ((0, 0), (0, LANE - HYENA_EMB))), t


def _relayout_w_in(w):
    g = GROUP_W
    off_c, off_d = 8 * g, 12 * g
    off_gate = off_d + 5 * g
    off_x = off_gate + N_MLSTM_GATES
    main = jnp.concatenate([w[:, 0:off_c], w[:, off_c + 3 * g:off_c + 4 * g],
                            w[:, off_d:off_gate], w[:, off_x:off_x + 2 * g]], axis=1)
    hy = w[:, off_c:off_c + 3 * g]
    gate_i, gate_f = _split_gates(w[:, off_gate:off_x])
    return jnp.concatenate([main, hy, gate_i, gate_f], axis=1).astype(BF16)


def _split_gates(t):
    nh = N_HEADS
    pad = [(0, 0)] * (t.ndim - 1) + [(0, LANE - 2 * nh)]
    gi = jnp.concatenate([t[..., 0:nh], t[..., 2 * nh:3 * nh]], axis=-1)
    gf = jnp.concatenate([t[..., nh:2 * nh], t[..., 3 * nh:4 * nh]], axis=-1)
    return jnp.pad(gi, pad), jnp.pad(gf, pad)


def _trunk(x, mem, splits, pre_norm_g, post_norm_g, w_in, w_out, diff_lambda, diff_subln_g,
           hy_conv_w, hy_conv_b, hy_ffn_w1, hy_ffn_b1, hy_ffn_w2, hy_ffn_b2, hy_ffn_w3,
           hy_freq, hy_bias, ml_conv_w, ml_conv_b, ml_gate_b, mem_norm_g, w_mem_kv):
    B, L, _ = x.shape
    M = mem.shape[1]
    depth = w_in.shape[0]
    g = GROUP_W
    h1, _, k1p, _ = _hy_dims(L)
    wl = HY_N2 * g

    ca, sa, pa = _rope_tables(L, HEAD_DIM, HEAD_DIM // ROPE_FRACTION)
    cb, sb, pb = _rope_tables(L, DIFF_QK_DIM, DIFF_QK_DIM // ROPE_FRACTION)
    rope_tabs = (ca, sa, cb, sb, pa, pb)
    dil_bias = jnp.asarray(_dil_bias_table())
    feats, tcol = _hyena_features(L)
    absdelta = jnp.abs(jnp.linspace(math.log(HYENA_TARGET) / HYENA_SLOW_DECAY,
                                    math.log(HYENA_TARGET) / HYENA_FAST_DECAY, g, dtype=F32))[None]
    m1, f3, f3i, m4r, m4i = _hy_tables(L)
    head_of = np.arange(g) // HEAD_DIM
    same_head = (head_of[:, None] == head_of[None, :]).astype(np.float32)
    hmean = jnp.asarray(same_head / HEAD_DIM, BF16)
    mlstm_consts = _mlstm_consts()

    x2d = x.reshape(B * L, D_MODEL)
    mem2d = mem.reshape(B * M, D_MODEL)
    for li in range(depth):
        lam_init = 0.8 - 0.6 * math.exp(-0.3 * li)
        w_all = _relayout_w_in(w_in[li])
        h_main, cv, cx1, cx2, gate_i, gate_f = _inproj(x2d, pre_norm_g[li][None], w_all,
                                                       rope_tabs, B, L)

        oa = _dilattn(h_main, dil_bias, B, L)
        ob = _diffattn(h_main, diff_lambda[li], jnp.tile(diff_subln_g[li], N_HEADS)[None],
                       hmean, lam_init, B, L)

        w1 = jnp.pad(hy_ffn_w1[li], ((0, LANE - HYENA_EMB), (0, 0)))
        hfilt = _hy_filter(feats, tcol, w1, hy_ffn_b1[li][None], hy_ffn_w2[li],
                           hy_ffn_b2[li][None], hy_ffn_w3[li], hy_freq[li][None], absdelta, L)
        a_filt = _hy_stage1_call(hfilt, m1, L)
        g_spec = _hy_spectrum(a_filt, f3, L)
        cw = jnp.tile(hy_conv_w[li].reshape(3, 3, g).transpose(1, 0, 2).reshape(9, g), (1, HY_N2))
        cbw = jnp.tile(hy_conv_b[li].reshape(3, g), (1, HY_N2))
        z, x1c, x2c, a = _hy_front(cv, cx1, cx2, cw, cbw, m1, B, L)
        bias_w = jnp.tile(hy_bias[li], (1, HY_N2))
        bsp = _hy_mid(a, g_spec, 0, f3, f3i, B, L)
        z, a = _hy_back(bsp, z, x1c, bias_w[0:1], m4r, m4i, m1, B, L, last=False)
        bsp = _hy_mid(a, g_spec, 1, f3, f3i, B, L)
        oc = _hy_back(bsp, z, x2c, bias_w[1:2], m4r, m4i, m1, B, L, last=True)

        bias_i, bias_f = _split_gates(ml_gate_b[li][None])
        od = _mlstm(h_main, gate_i, gate_f, ml_conv_w[li], ml_conv_b[li][None], bias_i, bias_f,
                    mlstm_consts, B, L)

        mkv = _memkv(mem2d, mem_norm_g[li][None], w_mem_kv[li].astype(BF16))
        ox = _memattn(h_main, mkv, B, L, M)

        branch_outs = (oa, ob, oc, od, ox)
        w_o, png = w_out[li].astype(BF16), post_norm_g[li][None]
        if li + 1 < depth:
            x2d = _outproj(x2d, branch_outs, h_main, w_o, png)
    ys, b0 = [], 0
    for nb in splits:
        y = _outproj(x2d, branch_outs, h_main, w_o, png, row0=b0 * L, n_rows=nb * L)
        ys.append(y.reshape(nb, L, D_MODEL))
        b0 += nb
    return tuple(ys)


def kernel(x_prompt, x_sample, mem_prompt, mem_sample, pre_norm_g, post_norm_g, w_in, w_out,
           diff_lambda, diff_subln_g, hy_conv_w, hy_conv_b, hy_ffn_w1, hy_ffn_b1, hy_ffn_w2,
           hy_ffn_b2, hy_ffn_w3, hy_freq, hy_bias, ml_conv_w, ml_conv_b, ml_gate_b,
           mem_norm_g, w_mem_kv):
    x = jnp.concatenate([x_prompt, x_sample], axis=0)
    mem = jnp.concatenate([mem_prompt, mem_sample], axis=0)
    return _trunk(x, mem, (x_prompt.shape[0], x_sample.shape[0]), pre_norm_g, post_norm_g, w_in,
                  w_out, diff_lambda, diff_subln_g, hy_conv_w, hy_conv_b, hy_ffn_w1, hy_ffn_b1,
                  hy_ffn_w2, hy_ffn_b2, hy_ffn_w3, hy_freq, hy_bias, ml_conv_w, ml_conv_b,
                  ml_gate_b, mem_norm_g, w_mem_kv)
```

```python
import functools
import math

import numpy as np
import jax
import jax.numpy as jnp
from jax import lax
from jax.experimental import pallas as pl
from jax.experimental.pallas import tpu as pltpu

F32 = jnp.float32
BF16 = jnp.bfloat16

D_MODEL = 1024
HEAD_DIM = 64
GROUP_W = 256
N_HEADS = GROUP_W // HEAD_DIM
NORM_EPS = 1e-6
NEG_INF = -1e30
ROPE_THETA = 500000.0
ROPE_FRACTION = 4
DIL_PATTERNS = ((128, 1), (512, 4), (2048, 16))
DIFF_QK_DIM = HEAD_DIM // 2
DIFF_SUBLN_EPS = 1e-5
HYENA_ORDER = 2
HYENA_BANDS = 16
HYENA_EMB = 1 + 2 * HYENA_BANDS
HYENA_HIDDEN = 64
HYENA_FAST_DECAY = 0.3
HYENA_SLOW_DECAY = 1.5
HYENA_TARGET = 1e-2
N_MLSTM_GATES = 4 * N_HEADS

(BLK_AQ, BLK_AK, BLK_AV, BLK_AG, BLK_BQ, BLK_BK, BLK_BV, BLK_BG, BLK_CG,
 BLK_DQ, BLK_DK, BLK_DV, BLK_DO, BLK_DG, BLK_XQ, BLK_XG) = range(16)
N_MAIN_BLK = 16
MAIN_W = N_MAIN_BLK * GROUP_W
W_ALL = MAIN_W + 4 * GROUP_W

VMEM_LIMIT_BYTES = 56 * 1024 * 1024
LANE = 128
BF16_SUBLANES = 16

ROW_TILE = 512
DIL_Q_TILE = 256
DIL_PAD = 1024
DIFF_Q_TILE = 256
DIFF_K_CHUNK = 512
DIFF_VT_ROWS = HEAD_DIM + BF16_SUBLANES
LOG2E = 1.4426950408889634
MEM_Q_TILE = 512
MLSTM_T = 128
CONV_HALO = 16
HY_N2 = 128


def _cparams(sem, vmem=VMEM_LIMIT_BYTES):
    return pltpu.CompilerParams(dimension_semantics=sem, vmem_limit_bytes=vmem)


def _sigmoid(v):
    return 1.0 / (1.0 + jnp.exp(-v))


def _silu(v):
    return v * _sigmoid(v)


def _dot(a, b):
    return jnp.dot(a, b, preferred_element_type=F32)


def _dot_nt(a, b):
    return lax.dot_general(a, b, (((1,), (1,)), ((), ())), preferred_element_type=F32)


def _dot_tn(a, b):
    return lax.dot_general(a, b, (((0,), (0,)), ((), ())), preferred_element_type=F32)


def _head_mask(h, width, dtype):
    lane = lax.broadcasted_iota(jnp.int32, (1, GROUP_W), 1)
    return ((lane // width) == h).astype(dtype)


def _inproj_kernel(x_ref, g_ref, w_ref, ca_ref, sa_ref, cb_ref, sb_ref, pa_ref, pb_ref,
                   h_ref, cv_ref, cx1_ref, cx2_ref, gi_ref, gf_ref):
    x = x_ref[...]
    ms = jnp.mean(x * x, axis=-1, keepdims=True)
    xn = (x * lax.rsqrt(ms + NORM_EPS) * g_ref[...]).astype(BF16)

    def proj(j, width=GROUP_W):
        return _dot(xn, w_ref[:, j * GROUP_W:j * GROUP_W + width])

    def rope(acc, c_ref, s_ref, p_ref):
        partner = _dot(acc.astype(BF16), p_ref[...])
        return acc * c_ref[...] + partner * s_ref[...]

    def finish(j, acc):
        if j in (BLK_AQ, BLK_AK):
            acc = rope(acc, ca_ref, sa_ref, pa_ref)
        if j in (BLK_BQ, BLK_BK):
            acc = rope(acc, cb_ref, sb_ref, pb_ref)
        if j in (BLK_AQ, BLK_XQ):
            acc = acc * (LOG2E / math.sqrt(HEAD_DIM))
        if j == BLK_BQ:
            acc = acc * (LOG2E / math.sqrt(DIFF_QK_DIM))
        if j < N_MAIN_BLK:
            h_ref[:, j * GROUP_W:(j + 1) * GROUP_W] = acc.astype(BF16)
        elif j < N_MAIN_BLK + 3:
            (cv_ref, cx1_ref, cx2_ref)[j - N_MAIN_BLK][...] = acc.astype(BF16)
        else:
            gi_ref[...] = acc[:, 0:LANE]
            gf_ref[...] = acc[:, LANE:2 * LANE]

    for j in range(0, W_ALL // GROUP_W, 2):
        acc2 = proj(j, 2 * GROUP_W)
        finish(j, acc2[:, 0:GROUP_W])
        finish(j + 1, acc2[:, GROUP_W:2 * GROUP_W])


def _inproj(x2d, g, w_all, rope_tabs, B, L):
    ca, sa, cb, sb, pa, pb = rope_tabs
    nt = L // ROW_TILE
    n_tok = B * L
    row = lambda i, b: (b * nt + i, 0)
    tab = lambda i, b: (i, 0)
    const = lambda i, b: (0, 0)
    out_shapes = (
        jax.ShapeDtypeStruct((n_tok, MAIN_W), BF16),
        jax.ShapeDtypeStruct((n_tok, GROUP_W), BF16),
        jax.ShapeDtypeStruct((n_tok, GROUP_W), BF16),
        jax.ShapeDtypeStruct((n_tok, GROUP_W), BF16),
        jax.ShapeDtypeStruct((n_tok, LANE), F32),
        jax.ShapeDtypeStruct((n_tok, LANE), F32),
    )
    return pl.pallas_call(
        _inproj_kernel,
        grid=(nt, B),
        in_specs=[
            pl.BlockSpec((ROW_TILE, D_MODEL), row),
            pl.BlockSpec((1, D_MODEL), const),
            pl.BlockSpec((D_MODEL, W_ALL), const),
            pl.BlockSpec((ROW_TILE, GROUP_W), tab),
            pl.BlockSpec((ROW_TILE, GROUP_W), tab),
            pl.BlockSpec((ROW_TILE, GROUP_W), tab),
            pl.BlockSpec((ROW_TILE, GROUP_W), tab),
            pl.BlockSpec((GROUP_W, GROUP_W), const),
            pl.BlockSpec((GROUP_W, GROUP_W), const),
        ],
        out_specs=(
            pl.BlockSpec((ROW_TILE, MAIN_W), row),
            pl.BlockSpec((ROW_TILE, GROUP_W), row),
            pl.BlockSpec((ROW_TILE, GROUP_W), row),
            pl.BlockSpec((ROW_TILE, GROUP_W), row),
            pl.BlockSpec((ROW_TILE, LANE), row),
            pl.BlockSpec((ROW_TILE, LANE), row),
        ),
        out_shape=out_shapes,
        compiler_params=_cparams(("parallel", "parallel")),
        name="inproj",
    )(x2d, g, w_all, ca, sa, cb, sb, pa, pb)


def _memkv_kernel(m_ref, g_ref, w_ref, o_ref):
    x = m_ref[...]
    ms = jnp.mean(x * x, axis=-1, keepdims=True)
    xn = (x * lax.rsqrt(ms + NORM_EPS) * g_ref[...]).astype(BF16)
    o_ref[...] = _dot(xn, w_ref[...]).astype(BF16)


def _memkv(mem2d, g, w):
    rows = mem2d.shape[0]
    tile = math.gcd(rows, ROW_TILE)
    return pl.pallas_call(
        _memkv_kernel,
        grid=(rows // tile,),
        in_specs=[
            pl.BlockSpec((tile, D_MODEL), lambda i: (i, 0)),
            pl.BlockSpec((1, D_MODEL), lambda i: (0, 0)),
            pl.BlockSpec((D_MODEL, 2 * GROUP_W), lambda i: (0, 0)),
        ],
        out_specs=pl.BlockSpec((tile, 2 * GROUP_W), lambda i: (i, 0)),
        out_shape=jax.ShapeDtypeStruct((rows, 2 * GROUP_W), BF16),
        compiler_params=_cparams(("parallel",)),
        name="memkv",
    )(mem2d, g, w)


def _memattn_kernel(q_ref, mk_ref, mv_ref, o_ref):
    q = q_ref[...]
    mk = mk_ref[...]
    mv = mv_ref[...]
    acc = jnp.zeros(q.shape, F32)
    for h in range(N_HEADS):
        hm = _head_mask(h, HEAD_DIM, BF16)
        s = _dot_nt(q * hm, mk)
        m = jnp.max(s, axis=-1, keepdims=True)
        p = jnp.exp2(s - m)
        l = jnp.sum(p, axis=-1, keepdims=True)
        acc = acc + _dot(p.astype(BF16), mv * hm) * (1.0 / l)
    o_ref[...] = acc.astype(BF16)


def _memattn(h_main, mkv, B, L, M):
    nq = L // MEM_Q_TILE
    return pl.pallas_call(
        _memattn_kernel,
        grid=(B, nq),
        in_specs=[
            pl.BlockSpec((MEM_Q_TILE, GROUP_W), lambda b, i: (b * nq + i, BLK_XQ)),
            pl.BlockSpec((M, GROUP_W), lambda b, i: (b, 0)),
            pl.BlockSpec((M, GROUP_W), lambda b, i: (b, 1)),
        ],
        out_specs=pl.BlockSpec((MEM_Q_TILE, GROUP_W), lambda b, i: (b * nq + i, 0)),
        out_shape=jax.ShapeDtypeStruct((B * L, GROUP_W), BF16),
        compiler_params=_cparams(("parallel", "parallel")),
        name="memattn",
    )(h_main, mkv, mkv)


def _dil_bias_table():
    w = DIL_Q_TILE + 2 * DIL_PAD
    d = np.arange(DIL_Q_TILE)[:, None] - np.arange(w)[None, :] + DIL_PAD
    count = np.zeros(d.shape, np.float64)
    for win, dil in DIL_PATTERNS:
        reach = (win // (2 * dil)) * dil
        count += (d % dil == 0) & (np.abs(d) <= reach)
    return np.where(count > 0, np.log2(np.maximum(count, 1.0)), NEG_INF).astype(np.float32)


def _dilattn_kernel(q_ref, k_ref, v_ref, bias_ref, o_ref, kpad, vpad, *, L):
    i = pl.program_id(1)
    w = DIL_Q_TILE + 2 * DIL_PAD

    @pl.when(i == 0)
    def _():
        zeros = jnp.zeros((DIL_PAD, GROUP_W), BF16)
        for pad, src in ((kpad, k_ref), (vpad, v_ref)):
            pad[0:DIL_PAD, :] = zeros
            pad[DIL_PAD + L:DIL_PAD + L + DIL_PAD, :] = zeros
            pad[DIL_PAD:DIL_PAD + L, :] = src[...]

    q0 = pl.multiple_of(i * DIL_Q_TILE, DIL_Q_TILE)
    q = q_ref[...]
    kw = kpad[pl.ds(q0, w), :]
    vw = vpad[pl.ds(q0, w), :]
    jpos = q0 - DIL_PAD + lax.broadcasted_iota(jnp.int32, (1, w), 1)
    in_seq = jnp.where(jpos >= 0, jnp.where(jpos < L, 0.0, NEG_INF), NEG_INF)
    bias = bias_ref[...] + in_seq
    acc = jnp.zeros(q.shape, F32)
    for h in range(N_HEADS):
        hm = _head_mask(h, HEAD_DIM, BF16)
        s = _dot_nt(q * hm, kw) + bias
        m = jnp.max(s, axis=-1, keepdims=True)
        p = jnp.exp2((s - m).astype(BF16))
        ones_next = _head_mask((h + 1) % N_HEADS, HEAD_DIM, BF16)
        o = _dot(p, vw * hm + ones_next)
        den = pltpu.roll(o, GROUP_W - HEAD_DIM, axis=1)
        in_head = _head_mask(h, HEAD_DIM, F32) > 0.0
        acc = acc + jnp.where(in_head, o / jnp.where(in_head, den, 1.0), 0.0)
    o_ref[...] = acc.astype(BF16)


def _dilattn(h_main, bias, B, L):
    nq = L // DIL_Q_TILE
    w = DIL_Q_TILE + 2 * DIL_PAD
    return pl.pallas_call(
        functools.partial(_dilattn_kernel, L=L),
        grid=(B, nq),
        in_specs=[
            pl.BlockSpec((DIL_Q_TILE, GROUP_W), lambda b, i: (b * nq + i, BLK_AQ)),
            pl.BlockSpec((L, GROUP_W), lambda b, i: (b, BLK_AK)),
            pl.BlockSpec((L, GROUP_W), lambda b, i: (b, BLK_AV)),
            pl.BlockSpec((DIL_Q_TILE, w), lambda b, i: (0, 0)),
        ],
        out_specs=pl.BlockSpec((DIL_Q_TILE, GROUP_W), lambda b, i: (b * nq + i, 0)),
        out_shape=jax.ShapeDtypeStruct((B * L, GROUP_W), BF16),
        scratch_shapes=[pltpu.VMEM((L + 2 * DIL_PAD, GROUP_W), BF16),
                        pltpu.VMEM((L + 2 * DIL_PAD, GROUP_W), BF16)],
        compiler_params=_cparams(("parallel", "arbitrary")),
        name="dilattn",
    )(h_main, h_main, h_main, bias)


def _split3(x):
    hi = x.astype(BF16)
    r1 = x - hi.astype(F32)
    mid = r1.astype(BF16)
    lo = (r1 - mid.astype(F32)).astype(BF16)
    return hi, mid, lo


def _split_dot(x, mat):
    return sum(_dot(t, mat) for t in _split3(x))


def _tri_dot(mat, x):
    return sum(_dot(mat, t) for t in _split3(x))


def _dot_f32(a, b):
    ah, am, al = _split3(a)
    bh, bm, bl = _split3(b)
    return (_dot(ah, bh) + (_dot(ah, bm) + _dot(am, bh))
            + (_dot(ah, bl) + _dot(am, bm) + _dot(al, bh)))


def _diffattn_kernel(q_ref, qn_ref, k_ref, v_ref, lam_ref, g_ref, hmean_ref, o_ref,
                     vt_ref, sa_ref, sb_ref, m_ref, oh_ref, *, lam_init, L):
    hd = HEAD_DIM
    first_tile = pl.program_id(1) == 0

    @pl.when(first_tile)
    def _():
        tail = (lax.broadcasted_iota(jnp.int32, (DIFF_VT_ROWS - hd, DIFF_K_CHUNK), 0) == 0)
        for c in range(L // DIFF_K_CHUNK):
            lo, hi = c * DIFF_K_CHUNK, (c + 1) * DIFF_K_CHUNK
            vt = v_ref[lo:hi, :].astype(F32).T.astype(BF16)
            for h in range(N_HEADS):
                vt_ref[h, 0:hd, lo:hi] = vt[h * hd:(h + 1) * hd]
                vt_ref[h, hd:DIFF_VT_ROWS, lo:hi] = tail.astype(BF16)

    lp = lam_ref[...]
    lam = (jnp.exp(jnp.sum(lp[0:1] * lp[1:2], axis=-1, keepdims=True))
           - jnp.exp(jnp.sum(lp[2:3] * lp[3:4], axis=-1, keepdims=True)) + lam_init)
    qt = q_ref[...].astype(F32).T
    qt_next = qn_ref[...].astype(F32).T
    feat_group = lax.broadcasted_iota(jnp.int32, (GROUP_W, 1), 0) // DIFF_QK_DIM
    tq = qt.shape[1]
    chunks = [(c * DIFF_K_CHUNK, (c + 1) * DIFF_K_CHUNK) for c in range(L // DIFF_K_CHUNK)]

    def masked_qt(h):
        wraps = h == N_HEADS
        src = jnp.where(wraps, qt_next, qt)
        hh = jnp.where(wraps, 0, h)
        return jnp.concatenate([jnp.where(feat_group == 2 * hh + c, src, 0.0) for c in range(2)],
                               axis=1).astype(BF16)

    def stage(h_next, nxt, h_cur, cur):
        qt2 = masked_qt(h_next)
        mx = None
        if h_cur is not None:
            m = m_ref[cur, 0:1, :]
            o = jnp.zeros((DIFF_VT_ROWS, 2 * tq), F32)
        for lo, hi in chunks:
            s = _dot(k_ref[lo:hi, :], qt2)
            s_refs[nxt][lo:hi, :] = s
            cm = jnp.max(s, axis=0, keepdims=True)
            mx = cm if mx is None else jnp.maximum(mx, cm)
            if h_cur is not None:
                p = jnp.exp2((s_refs[cur][lo:hi, :] - m).astype(BF16))
                o = o + _dot(vt_ref[h_cur, :, lo:hi], p)
        m_ref[nxt] = jnp.broadcast_to(mx, m_ref.shape[1:])
        if h_cur is not None:
            on = o[0:hd] * (1.0 / o[hd:hd + 1])
            oh_ref[h_cur] = on[:, 0:tq] - on[:, tq:2 * tq] * lam

    s_refs = (sa_ref, sb_ref)

    @pl.when(first_tile)
    def _():
        stage(0, 0, None, None)

    def body(j, carry):
        stage(2 * j + 1, 1, 2 * j, 0)
        stage(2 * j + 2, 0, 2 * j + 1, 1)
        return carry

    lax.fori_loop(0, N_HEADS // 2, body, 0)
    acc = jnp.concatenate([oh_ref[h] for h in range(N_HEADS)], axis=0).T
    ms = _split_dot(acc * acc, hmean_ref[...])
    y = acc * lax.rsqrt(ms + DIFF_SUBLN_EPS) * g_ref[...] * (1.0 - lam_init)
    o_ref[...] = y.astype(BF16)


def _diffattn(h_main, lam_p, subln_g, hmean, lam_init, B, L):
    nq = L // DIFF_Q_TILE
    return pl.pallas_call(
        functools.partial(_diffattn_kernel, lam_init=lam_init, L=L),
        grid=(B, nq),
        in_specs=[
            pl.BlockSpec((DIFF_Q_TILE, GROUP_W), lambda b, i: (b * nq + i, BLK_BQ)),
            pl.BlockSpec((DIFF_Q_TILE, GROUP_W),
                         lambda b, i: (b * nq + jnp.minimum(i + 1, nq - 1), BLK_BQ)),
            pl.BlockSpec((L, GROUP_W), lambda b, i: (b, BLK_BK)),
            pl.BlockSpec((L, GROUP_W), lambda b, i: (b, BLK_BV)),
            pl.BlockSpec((4, DIFF_QK_DIM), lambda b, i: (0, 0)),
            pl.BlockSpec((1, GROUP_W), lambda b, i: (0, 0)),
            pl.BlockSpec((GROUP_W, GROUP_W), lambda b, i: (0, 0)),
        ],
        out_specs=pl.BlockSpec((DIFF_Q_TILE, GROUP_W), lambda b, i: (b * nq + i, 0)),
        out_shape=jax.ShapeDtypeStruct((B * L, GROUP_W), BF16),
        scratch_shapes=[pltpu.VMEM((N_HEADS, DIFF_VT_ROWS, L), BF16),
                        pltpu.VMEM((L, 2 * DIFF_Q_TILE), F32),
                        pltpu.VMEM((L, 2 * DIFF_Q_TILE), F32),
                        pltpu.VMEM((2, 8, 2 * DIFF_Q_TILE), F32),
                        pltpu.VMEM((N_HEADS, HEAD_DIM, DIFF_Q_TILE), F32)],
        compiler_params=_cparams(("parallel", "arbitrary")),
        name="diffattn",
    )(h_main, h_main, h_main, h_main, lam_p, subln_g, hmean)


def _log_sigmoid(v):
    return jnp.minimum(v, 0.0) - jnp.log(1.0 + jnp.exp(-jnp.abs(v)))


def _mlstm_kernel(q_ref, k_ref, v_ref, gi_ref, gf_ref, cw_ref, cb_ref, bi_ref, bf_ref,
                  ltri_ref, utri_ref, ecol_ref, elane_ref, kmask_ref, vmask_ref, ones_ref,
                  bd_ref, hsum_ref, o_ref,
                  qpad, kpad, qs, ks, hfw, hbw, cst, nst, mst, *, L):
    T = MLSTM_T
    nc = L // T
    halo = CONV_HALO
    nh = N_HEADS

    zpad = jnp.zeros((halo, GROUP_W), BF16)
    for pad, src in ((qpad, q_ref), (kpad, k_ref)):
        pad[0:halo, :] = zpad
        pad[halo + L:halo + L + halo, :] = zpad
        pad[halo:halo + L, :] = src[...]

    def conv_body(c, carry):
        r0 = pl.multiple_of(c * T, T)
        for idx, (pad, dst, scale) in enumerate(((qpad, qs, 1.0),
                                                 (kpad, ks, 1.0 / math.sqrt(HEAD_DIM)))):
            xw = pad[pl.ds(r0, T + 2 * halo), :].astype(F32)
            xm = pltpu.roll(xw, 1, axis=0)[halo:halo + T]
            xp = pltpu.roll(xw, T + 2 * halo - 1, axis=0)[halo:halo + T]
            xc = xw[halo:halo + T]
            lo, hi = idx * GROUP_W, (idx + 1) * GROUP_W
            y = (xm * cw_ref[0:1, lo:hi] + xc * cw_ref[1:2, lo:hi] + xp * cw_ref[2:3, lo:hi]
                 + cb_ref[0:1, lo:hi])
            dst[pl.ds(r0, T), :] = (_silu(y) * scale).astype(BF16)
        return carry

    lax.fori_loop(0, nc, conv_body, 0)

    cst[...] = jnp.zeros(cst.shape, F32)
    nst[...] = jnp.zeros(nst.shape, F32)
    mst[...] = jnp.zeros(mst.shape, F32)

    is_fw = lax.broadcasted_iota(jnp.int32, (1, LANE), 1) < nh
    rowi = lax.broadcasted_iota(jnp.int32, (T, LANE), 0)
    row4 = lax.broadcasted_iota(jnp.int32, (T, nh * T), 0)
    col4 = lax.broadcasted_iota(jnp.int32, (T, nh * T), 1) % T
    causal = (col4 <= row4, col4 >= row4)

    def body(c, carry):
        rows = (pl.multiple_of(c * T, T), pl.multiple_of((nc - 1 - c) * T, T))
        gate_i = jnp.where(is_fw, gi_ref[pl.ds(rows[0], T), :], gi_ref[pl.ds(rows[1], T), :])
        gate_f = jnp.where(is_fw, gf_ref[pl.ds(rows[0], T), :], gf_ref[pl.ds(rows[1], T), :])
        gate_i = gate_i + bi_ref[...]
        parts = _split3(_log_sigmoid(gate_f + bf_ref[...]))
        pre = sum(_dot(ltri_ref[...], t) for t in parts)
        suf = sum(_dot(utri_ref[...], t) for t in parts)
        cum = jnp.where(is_fw, pre, suf)
        b_end = jnp.where(is_fw, pre[T - 1:T, :], suf[0:1, :])
        key_w = gate_i - cum

        pmax, smax = key_w, key_w
        sh = 1
        while sh < T:
            pmax = jnp.maximum(pmax, jnp.where(rowi >= sh, pltpu.roll(pmax, sh, axis=0), NEG_INF))
            smax = jnp.maximum(smax, jnp.where(rowi < T - sh, pltpu.roll(smax, T - sh, axis=0),
                                               NEG_INF))
            sh *= 2
        m_prev = mst[0:1, :]
        inter = cum + m_prev
        m_t = jnp.maximum(inter, cum + jnp.where(is_fw, pmax, smax))
        u = cum - m_t
        a = b_end + key_w
        m_new = jnp.maximum(b_end + m_prev, jnp.max(a, axis=0, keepdims=True))
        mst[0:1, :] = m_new
        stack = jnp.concatenate(
            [jnp.exp(inter - m_t), jnp.exp(-m_t), jnp.exp(a - m_new),
             jnp.broadcast_to(jnp.exp(b_end + m_prev - m_new), (8, LANE))], axis=0)
        st_hi, st_mid, _ = _split3(stack)
        key_w_t = key_w.T

        for d in range(2):
            r0 = rows[d]
            qc = qs[pl.ds(r0, T), :]
            kc = ks[pl.ds(r0, T), :]
            vc = v_ref[pl.ds(r0, T), :]
            ex = _dot(st_hi, elane_ref[d]) + _dot(st_mid, elane_ref[d])
            w_inter, floor, w_key = ex[0:T], ex[T:2 * T], ex[2 * T:3 * T]
            sp_row = ex[3 * T:3 * T + 1]

            r_row = jnp.concatenate([key_w_t[d * nh + h:d * nh + h + 1, :] for h in range(nh)],
                                    axis=1)
            dlog = _split_dot(u, ecol_ref[d]) + r_row
            decay = jnp.exp(jnp.where(causal[d], dlog, NEG_INF))

            kt = kc.astype(F32).T.astype(BF16)
            qk = _dot(qc, jnp.concatenate([kt] * nh, axis=1) * kmask_ref[...]) * decay
            qk_hi = qk.astype(BF16)
            qk_lo = (qk - qk_hi.astype(F32)).astype(BF16)
            vblk = jnp.concatenate([vc] * nh, axis=0) * vmask_ref[...]
            num = _dot(qk_hi, vblk)
            den = _dot(qk_hi, ones_ref[...]) + _dot(qk_lo, ones_ref[...])

            c_prev = cst[d]
            n_prev = nst[d:d + 1, :]
            num = num + w_inter * _dot(qc, c_prev.astype(BF16))
            den = den + w_inter * _dot((qc.astype(F32) * n_prev).astype(BF16), hsum_ref[...])
            h_out = num / jnp.maximum(jnp.abs(den), floor)
            if d == 0:
                hfw[pl.ds(r0, T), :] = h_out
            else:
                hbw[pl.ds(r0, T), :] = h_out

            kw = kc.astype(F32) * w_key
            cst[d] = c_prev * sp_row + _dot_tn(kw.astype(BF16), vc) * bd_ref[...]
            nst[d:d + 1, :] = n_prev * sp_row + jnp.sum(kw, axis=0, keepdims=True)
        return carry

    lax.fori_loop(0, nc, body, 0)
    o_ref[...] = (hfw[...] + hbw[...]).astype(BF16)


def _mlstm_consts():
    T, nh, g = MLSTM_T, N_HEADS, GROUP_W
    tri = np.tril(np.ones((T, T), np.float32))
    src = np.arange(LANE)[:, None]
    ecol = np.stack([(src == d * nh + np.arange(nh * T)[None, :] // T) for d in range(2)])
    elane = np.stack([(src == d * nh + np.arange(g)[None, :] // HEAD_DIM) for d in range(2)])
    head_of = np.arange(g) // HEAD_DIM
    blk_of = np.arange(nh * T) // T
    kmask = head_of[:, None] == blk_of[None, :]
    vmask = blk_of[:, None] == head_of[None, :]
    same_head = head_of[:, None] == head_of[None, :]
    b16 = lambda a: jnp.asarray(a.astype(np.float32), BF16)
    return (b16(tri), b16(tri.T), b16(ecol), b16(elane), b16(kmask), b16(vmask), b16(vmask),
            jnp.asarray(same_head.astype(np.float32)), b16(same_head))


def _mlstm(h_main, gate_i, gate_f, conv_w, conv_b, bias_i, bias_f, consts, B, L):
    T = MLSTM_T
    const = lambda b: (0, 0)
    const3 = lambda b: (0, 0, 0)
    return pl.pallas_call(
        functools.partial(_mlstm_kernel, L=L),
        grid=(B,),
        in_specs=[
            pl.BlockSpec((L, GROUP_W), lambda b: (b, BLK_DQ)),
            pl.BlockSpec((L, GROUP_W), lambda b: (b, BLK_DK)),
            pl.BlockSpec((L, GROUP_W), lambda b: (b, BLK_DV)),
            pl.BlockSpec((L, LANE), lambda b: (b, 0)),
            pl.BlockSpec((L, LANE), lambda b: (b, 0)),
            pl.BlockSpec((3, 2 * GROUP_W), const),
            pl.BlockSpec((1, 2 * GROUP_W), const),
            pl.BlockSpec((1, LANE), const),
            pl.BlockSpec((1, LANE), const),
            pl.BlockSpec((T, T), const),
            pl.BlockSpec((T, T), const),
            pl.BlockSpec((2, LANE, N_HEADS * T), const3),
            pl.BlockSpec((2, LANE, GROUP_W), const3),
            pl.BlockSpec((GROUP_W, N_HEADS * T), const),
            pl.BlockSpec((N_HEADS * T, GROUP_W), const),
            pl.BlockSpec((N_HEADS * T, GROUP_W), const),
            pl.BlockSpec((GROUP_W, GROUP_W), const),
            pl.BlockSpec((GROUP_W, GROUP_W), const),
        ],
        out_specs=pl.BlockSpec((L, GROUP_W), lambda b: (b, 0)),
        out_shape=jax.ShapeDtypeStruct((B * L, GROUP_W), BF16),
        scratch_shapes=[
            pltpu.VMEM((L + 2 * CONV_HALO, GROUP_W), BF16),
            pltpu.VMEM((L + 2 * CONV_HALO, GROUP_W), BF16),
            pltpu.VMEM((L, GROUP_W), BF16),
            pltpu.VMEM((L, GROUP_W), BF16),
            pltpu.VMEM((L, GROUP_W), F32),
            pltpu.VMEM((L, GROUP_W), F32),
            pltpu.VMEM((2, GROUP_W, GROUP_W), F32),
            pltpu.VMEM((8, GROUP_W), F32),
            pltpu.VMEM((8, LANE), F32),
        ],
        compiler_params=_cparams(("parallel",)),
        name="mlstm",
    )(h_main, h_main, h_main, gate_i, gate_f, conv_w, conv_b, bias_i, bias_f, *consts)


def _hy_dims(L):
    h1 = L // HY_N2
    k1h = h1 + 1
    k1p = -(-k1h // BF16_SUBLANES) * BF16_SUBLANES
    kg = 12 if k1p % 12 == 0 else BF16_SUBLANES
    return h1, k1h, k1p, kg


def _hy_tables(L):
    h1, k1h, k1p, _ = _hy_dims(L)
    n1_len = 2 * h1
    n = 2 * L
    k1 = np.arange(k1p)[:, None].astype(np.float64)
    live = (np.arange(k1p) < k1h)[:, None]
    n1 = np.arange(h1)[None, :].astype(np.float64)
    ang = 2.0 * np.pi * k1 * n1 / n1_len
    m1 = np.concatenate([np.where(live, np.cos(ang), 0.0), np.where(live, -np.sin(ang), 0.0)], 0)

    n2 = np.arange(HY_N2)[None, None, :].astype(np.float64)
    k2 = np.arange(HY_N2)[None, :, None].astype(np.float64)
    kk = np.arange(k1p)[:, None, None] + n1_len * k2
    th = 2.0 * np.pi * kk * n2 / n
    fr, fi = np.cos(th), -np.sin(th)
    f3 = np.concatenate([np.concatenate([fr, -fi], 2), np.concatenate([fi, fr], 2)], 1)
    er, ei = np.transpose(np.cos(th), (0, 2, 1)), np.transpose(np.sin(th), (0, 2, 1))
    f3i = np.concatenate([np.concatenate([er, -ei], 2), np.concatenate([ei, er], 2)], 1)
    live3 = (np.arange(k1p) < k1h)[:, None, None]
    f3 = np.where(live3, f3, 0.0)
    f3i = np.where(live3, f3i, 0.0)

    nn1 = np.arange(h1)[:, None].astype(np.float64)
    kc = np.arange(k1p)[None, :].astype(np.float64)
    ph = 2.0 * np.pi * nn1 * kc / n1_len
    edge = (np.arange(k1p) == 0) | (np.arange(k1p) == h1)
    livec = (np.arange(k1p) < k1h)[None, :]
    m4r = np.where(livec, np.where(edge[None, :], np.cos(ph), 2.0 * np.cos(ph)), 0.0) / n
    m4i = np.where(livec & ~edge[None, :], -2.0 * np.sin(ph), 0.0) / n
    return (jnp.asarray(m1, BF16), jnp.asarray(f3, BF16), jnp.asarray(f3i, BF16),
            jnp.asarray(m4r, BF16), jnp.asarray(m4i, BF16))


def _to_wide(x, h1):
    return x.astype(F32).reshape(h1, HY_N2 * GROUP_W)


def _hy_stage1(z_bf16, m1_ref, a_ref, k1p):
    a = _dot(m1_ref[...], z_bf16)
    a_ref[0, 0] = a[0:k1p].reshape(k1p, HY_N2, GROUP_W).astype(BF16)
    a_ref[0, 1] = a[k1p:2 * k1p].reshape(k1p, HY_N2, GROUP_W).astype(BF16)


def _hy_conv3_wide(x, w_ref, b_ref, j, h1):
    c = GROUP_W
    wl = HY_N2 * c
    rowi = lax.broadcasted_iota(jnp.int32, (h1, c), 0)
    tail = x[:, wl - c:wl]
    head = x[:, 0:c]
    prev_tail = jnp.where(rowi == 0, 0.0, pltpu.roll(tail, 1, axis=0))
    next_head = jnp.where(rowi == h1 - 1, 0.0, pltpu.roll(head, h1 - 1, axis=0))
    xm = jnp.concatenate([prev_tail, x[:, 0:wl - c]], axis=1)
    xp = jnp.concatenate([x[:, c:wl], next_head], axis=1)
    return (xm * w_ref[3 * j:3 * j + 1, :] + x * w_ref[3 * j + 1:3 * j + 2, :]
            + xp * w_ref[3 * j + 2:3 * j + 3, :] + b_ref[j:j + 1, :])


def _hy_front_kernel(v_ref, x1_ref, x2_ref, w_ref, b_ref, m1_ref,
                     z_ref, x1c_ref, x2c_ref, a_ref, *, h1, k1p):
    z = _hy_conv3_wide(_to_wide(v_ref[...], h1), w_ref, b_ref, 0, h1).astype(BF16)
    z_ref[0] = z
    x1c_ref[0] = _hy_conv3_wide(_to_wide(x1_ref[...], h1), w_ref, b_ref, 1, h1).astype(BF16)
    x2c_ref[0] = _hy_conv3_wide(_to_wide(x2_ref[...], h1), w_ref, b_ref, 2, h1).astype(BF16)
    _hy_stage1(z, m1_ref, a_ref, k1p)


def _hy_front(cv, cx1, cx2, w_wide, b_wide, m1, B, L):
    h1, _, k1p, _ = _hy_dims(L)
    wl = HY_N2 * GROUP_W
    nat = pl.BlockSpec((L, GROUP_W), lambda b: (b, 0))
    seq = pl.BlockSpec((1, h1, wl), lambda b: (b, 0, 0))
    wide = jax.ShapeDtypeStruct((B, h1, wl), BF16)
    slab = (1, 2, k1p, HY_N2, GROUP_W)
    return pl.pallas_call(
        functools.partial(_hy_front_kernel, h1=h1, k1p=k1p),
        grid=(B,),
        in_specs=[nat, nat, nat,
                  pl.BlockSpec((9, wl), lambda b: (0, 0)),
                  pl.BlockSpec((3, wl), lambda b: (0, 0)),
                  pl.BlockSpec((2 * k1p, h1), lambda b: (0, 0))],
        out_specs=(seq, seq, seq, pl.BlockSpec(slab, lambda b: (b, 0, 0, 0, 0))),
        out_shape=(wide, wide, wide,
                   jax.ShapeDtypeStruct((B, 2, k1p, HY_N2, GROUP_W), BF16)),
        compiler_params=_cparams(("parallel",)),
        name="hy_front",
    )(cv, cx1, cx2, w_wide, b_wide, m1)


def _hy_stage1_kernel(z_ref, m1_ref, a_ref, *, h1, k1p):
    _hy_stage1(_to_wide(z_ref[0], h1).astype(BF16), m1_ref, a_ref, k1p)


def _hy_stage1_call(z_nat, m1, L):
    h1, _, k1p, _ = _hy_dims(L)
    nb = z_nat.shape[0]
    return pl.pallas_call(
        functools.partial(_hy_stage1_kernel, h1=h1, k1p=k1p),
        grid=(nb,),
        in_specs=[pl.BlockSpec((1, L, GROUP_W), lambda b: (b, 0, 0)),
                  pl.BlockSpec((2 * k1p, h1), lambda b: (0, 0))],
        out_specs=pl.BlockSpec((1, 2, k1p, HY_N2, GROUP_W), lambda b: (b, 0, 0, 0, 0)),
        out_shape=jax.ShapeDtypeStruct((nb, 2, k1p, HY_N2, GROUP_W), BF16),
        compiler_params=_cparams(("parallel",)),
        name="hy_stage1",
    )(z_nat, m1)


def _hy_spectrum_kernel(af_ref, ab_ref, f3_ref, g_ref, *, kg, k1h):
    grp = pl.program_id(0)

    @pl.when(grp * kg < k1h)
    def _():
        for kk in range(kg):
            xs = []
            for a_ref in (af_ref, ab_ref):
                a2 = jnp.concatenate([a_ref[0, 0, kk], a_ref[0, 1, kk]], axis=0)
                xs.append(_dot(f3_ref[kk], a2))
            g_ref[0, 0, kk] = xs[0][0:HY_N2] + xs[1][0:HY_N2]
            g_ref[0, 1, kk] = xs[0][HY_N2:2 * HY_N2] - xs[1][HY_N2:2 * HY_N2]

    @pl.when(grp * kg >= k1h)
    def _():
        g_ref[...] = jnp.zeros(g_ref.shape, F32)


def _hy_spectrum(a_filt, f3, L):
    _, k1h, k1p, kg = _hy_dims(L)
    a5 = a_filt
    blk = (1, 2, kg, HY_N2, GROUP_W)
    return pl.pallas_call(
        functools.partial(_hy_spectrum_kernel, kg=kg, k1h=k1h),
        grid=(k1p // kg, HYENA_ORDER),
        in_specs=[pl.BlockSpec(blk, lambda g, o: (2 * o, 0, g, 0, 0)),
                  pl.BlockSpec(blk, lambda g, o: (2 * o + 1, 0, g, 0, 0)),
                  pl.BlockSpec((kg, 2 * HY_N2, 2 * HY_N2), lambda g, o: (g, 0, 0))],
        out_specs=pl.BlockSpec(blk, lambda g, o: (o, 0, g, 0, 0)),
        out_shape=jax.ShapeDtypeStruct((HYENA_ORDER, 2, k1p, HY_N2, GROUP_W), F32),
        compiler_params=_cparams(("parallel", "parallel")),
        name="hy_spectrum",
    )(a5, a5, f3)


def _hy_mid_kernel(a_ref, g_ref, f3_ref, f3i_ref, b_ref, *, kg, k1h):
    grp = pl.program_id(0)

    @pl.when(grp * kg < k1h)
    def _():
        for kk in range(kg):
            a2 = jnp.concatenate([a_ref[0, 0, kk], a_ref[0, 1, kk]], axis=0)
            x = _dot(f3_ref[kk], a2)
            xr, xi = x[0:HY_N2], x[HY_N2:2 * HY_N2]
            gr, gi = g_ref[0, 0, kk], g_ref[0, 1, kk]
            y2 = jnp.concatenate([xr * gr - xi * gi, xr * gi + xi * gr], axis=0).astype(BF16)
            bm = _dot(f3i_ref[kk], y2)
            b_ref[0, 0, kk] = bm[0:HY_N2].astype(BF16)
            b_ref[0, 1, kk] = bm[HY_N2:2 * HY_N2].astype(BF16)

    @pl.when(grp * kg >= k1h)
    def _():
        b_ref[...] = jnp.zeros(b_ref.shape, BF16)


def _hy_mid(a, g_spec, order, f3, f3i, B, L):
    _, k1h, k1p, kg = _hy_dims(L)
    blk = (1, 2, kg, HY_N2, GROUP_W)
    tab = pl.BlockSpec((kg, 2 * HY_N2, 2 * HY_N2), lambda g, b: (g, 0, 0))
    return pl.pallas_call(
        functools.partial(_hy_mid_kernel, kg=kg, k1h=k1h),
        grid=(k1p // kg, B),
        in_specs=[pl.BlockSpec(blk, lambda g, b: (b, 0, g, 0, 0)),
                  pl.BlockSpec(blk, lambda g, b: (order, 0, g, 0, 0)),
                  tab, tab],
        out_specs=pl.BlockSpec(blk, lambda g, b: (b, 0, g, 0, 0)),
        out_shape=jax.ShapeDtypeStruct((B, 2, k1p, HY_N2, GROUP_W), BF16),
        compiler_params=_cparams(("parallel", "parallel")),
        name="hy_mid",
    )(a, g_spec, f3, f3i)


def _hy_back_kernel(b_ref, z_ref, x_ref, bias_ref, m4r_ref, m4i_ref, *rest, k1p, last):
    wl = HY_N2 * GROUP_W
    br = b_ref[0, 0].astype(F32).reshape(k1p, wl).astype(BF16)
    bi = b_ref[0, 1].astype(F32).reshape(k1p, wl).astype(BF16)
    y = _dot(m4r_ref[...], br) + _dot(m4i_ref[...], bi)
    z_new = x_ref[0].astype(F32) * (y + z_ref[0].astype(F32) * bias_ref[...])
    zb = z_new.astype(BF16)
    if last:
        (o_ref,) = rest
        o_ref[...] = z_new.reshape(o_ref.shape).astype(BF16)
    else:
        m1_ref, o_ref, a_ref = rest
        o_ref[0] = zb
        _hy_stage1(zb, m1_ref, a_ref, k1p)


def _hy_back(b, z, xg, bias_wide, m4r, m4i, m1, B, L, last):
    h1, _, k1p, _ = _hy_dims(L)
    wl = HY_N2 * GROUP_W
    seq = pl.BlockSpec((1, h1, wl), lambda i: (i, 0, 0))
    slab = pl.BlockSpec((1, 2, k1p, HY_N2, GROUP_W), lambda i: (i, 0, 0, 0, 0))
    const = lambda i: (0, 0)
    in_specs = [slab, seq, seq, pl.BlockSpec((1, wl), const),
                pl.BlockSpec((h1, k1p), const), pl.BlockSpec((h1, k1p), const)]
    args = [b, z, xg, bias_wide, m4r, m4i]
    if last:
        out_specs = pl.BlockSpec((L, GROUP_W), lambda i: (i, 0))
        out_shape = jax.ShapeDtypeStruct((B * L, GROUP_W), BF16)
    else:
        in_specs.append(pl.BlockSpec((2 * k1p, h1), const))
        args.append(m1)
        out_specs = (seq, slab)
        out_shape = (jax.ShapeDtypeStruct((B, h1, wl), BF16),
                     jax.ShapeDtypeStruct((B, 2, k1p, HY_N2, GROUP_W), BF16))
    return pl.pallas_call(
        functools.partial(_hy_back_kernel, k1p=k1p, last=last),
        grid=(B,),
        in_specs=in_specs,
        out_specs=out_specs,
        out_shape=out_shape,
        compiler_params=_cparams(("parallel",)),
        name="hy_back_last" if last else "hy_back",
    )(*args)


def _hy_filter_kernel(f_ref, t_ref, w1_ref, b1_ref, w2_ref, b2_ref, w3_ref, fr_ref, ad_ref,
                      o_ref, *, tile):
    i = pl.program_id(0)
    freq = fr_ref[...]
    z = jnp.sin(freq * (_dot_f32(f_ref[...], w1_ref[...]) + b1_ref[...]))
    z = jnp.sin(freq * (_dot_f32(z, w2_ref[...]) + b2_ref[...]))
    decay = jnp.exp(-t_ref[...] * ad_ref[...])
    rowi = i * tile + lax.broadcasted_iota(jnp.int32, (tile, GROUP_W), 0)
    for j in range(2 * HYENA_ORDER):
        hj = _dot_f32(z, w3_ref[:, j * GROUP_W:(j + 1) * GROUP_W]) * decay
        if j % 2 == 1:
            hj = jnp.where(rowi == 0, 0.0, hj)
        o_ref[j] = hj.astype(BF16)


def _hy_filter(feats, tcol, w1, b1, w2, b2, w3, freq, absdelta, L):
    tile = math.gcd(L, ROW_TILE)
    const = lambda i: (0, 0)
    return pl.pallas_call(
        functools.partial(_hy_filter_kernel, tile=tile),
        grid=(L // tile,),
        in_specs=[pl.BlockSpec((tile, LANE), lambda i: (i, 0)),
                  pl.BlockSpec((tile, 1), lambda i: (i, 0)),
                  pl.BlockSpec((LANE, HYENA_HIDDEN), const),
                  pl.BlockSpec((1, HYENA_HIDDEN), const),
                  pl.BlockSpec((HYENA_HIDDEN, HYENA_HIDDEN), const),
                  pl.BlockSpec((1, HYENA_HIDDEN), const),
                  pl.BlockSpec((HYENA_HIDDEN, 2 * HYENA_ORDER * GROUP_W), const),
                  pl.BlockSpec((1, HYENA_HIDDEN), const),
                  pl.BlockSpec((1, GROUP_W), const)],
        out_specs=pl.BlockSpec((2 * HYENA_ORDER, tile, GROUP_W), lambda i: (0, i, 0)),
        out_shape=jax.ShapeDtypeStruct((2 * HYENA_ORDER, L, GROUP_W), BF16),
        compiler_params=_cparams(("parallel",)),
        name="hy_filter",
    )(feats, tcol, w1, b1, w2, b2, w3, freq, absdelta)


def _outproj_kernel(x_ref, oa_ref, ob_ref, oc_ref, od_ref, ox_ref,
                    ga_ref, gb_ref, gc_ref, gdo_ref, gdg_ref, gx_ref, w_ref, png_ref, out_ref):
    f = lambda r: r[...].astype(F32)
    branches = (
        f(oa_ref) * _silu(f(ga_ref)),
        f(ob_ref) * _silu(f(gb_ref)),
        f(oc_ref) * _silu(f(gc_ref)),
        f(od_ref) * _sigmoid(f(gdo_ref)) * _silu(f(gdg_ref)),
        f(ox_ref) * _silu(f(gx_ref)),
    )
    y = jnp.zeros(x_ref.shape, F32)
    for j, br in enumerate(branches):
        y = y + _dot(br.astype(BF16), w_ref[j * GROUP_W:(j + 1) * GROUP_W, :])
    ms = jnp.mean(y * y, axis=-1, keepdims=True)
    out_ref[...] = x_ref[...] + y * lax.rsqrt(ms + NORM_EPS) * png_ref[...]


def _outproj(x2d, outs, h_main, w_out, png, row0=0, n_rows=None):
    n_rows = x2d.shape[0] if n_rows is None else n_rows
    nt = n_rows // ROW_TILE
    t0 = row0 // ROW_TILE
    row = lambda i: (t0 + i, 0)
    blk = lambda j: pl.BlockSpec((ROW_TILE, GROUP_W), lambda i: (t0 + i, j))
    return pl.pallas_call(
        _outproj_kernel,
        grid=(nt,),
        in_specs=[pl.BlockSpec((ROW_TILE, D_MODEL), row)]
        + [pl.BlockSpec((ROW_TILE, GROUP_W), row)] * 5
        + [blk(BLK_AG), blk(BLK_BG), blk(BLK_CG), blk(BLK_DO), blk(BLK_DG), blk(BLK_XG)]
        + [pl.BlockSpec((5 * GROUP_W, D_MODEL), lambda i: (0, 0)),
           pl.BlockSpec((1, D_MODEL), lambda i: (0, 0))],
        out_specs=pl.BlockSpec((ROW_TILE, D_MODEL), lambda i: (i, 0)),
        out_shape=jax.ShapeDtypeStruct((n_rows, D_MODEL), F32),
        compiler_params=_cparams(("parallel",)),
        name="outproj",
    )(x2d, *outs, h_main, h_main, h_main, h_main, h_main, h_main, w_out, png)


def _rope_tables(L, group, rot_dim):
    half = rot_dim // 2
    inv = 1.0 / (ROPE_THETA ** (jnp.arange(0, rot_dim, 2, dtype=F32) / rot_dim))
    ang = jnp.arange(L, dtype=F32)[:, None] * inv[None, :]
    cos, sin = jnp.cos(ang), jnp.sin(ang)
    lane = np.arange(GROUP_W) % group
    in_rot = lane < rot_dim
    idx = lane % half
    c = jnp.where(in_rot[None, :], cos[:, idx], 1.0)
    s = jnp.where(in_rot[None, :], sin[:, idx], 0.0)
    p = np.zeros((GROUP_W, GROUP_W), np.float32)
    for j in range(GROUP_W):
        if lane[j] < half:
            p[j + half, j] = -1.0
        elif lane[j] < rot_dim:
            p[j - half, j] = 1.0
    return c, s, jnp.asarray(p, BF16)


def _hyena_features(L):
    t = jnp.linspace(0.0, 1.0, L, dtype=F32)[:, None]
    bands = jnp.linspace(1e-4, HYENA_BANDS - 1, HYENA_BANDS, dtype=F32)
    ang = (2.0 * math.pi / L) * jnp.arange(L, dtype=F32)[:, None] * bands[None, :]
    feats = jnp.concatenate([t, jnp.cos(ang), -jnp.sin(ang)], axis=-1)
    return jnp.pad(feats, ((0, 0), (0, LANE - HYENA_EMB))), t


def _relayout_w_in(w):
    g = GROUP_W
    off_c, off_d = 8 * g, 12 * g
    off_gate = off_d + 5 * g
    off_x = off_gate + N_MLSTM_GATES
    main = jnp.concatenate([w[:, 0:off_c], w[:, off_c + 3 * g:off_c + 4 * g],
                            w[:, off_d:off_gate], w[:, off_x:off_x + 2 * g]], axis=1)
    hy = w[:, off_c:off_c + 3 * g]
    gate_i, gate_f = _split_gates(w[:, off_gate:off_x])
    return jnp.concatenate([main, hy, gate_i, gate_f], axis=1).astype(BF16)


def _split_gates(t):
    nh = N_HEADS
    pad = [(0, 0)] * (t.ndim - 1) + [(0, LANE - 2 * nh)]
    gi = jnp.concatenate([t[..., 0:nh], t[..., 2 * nh:3 * nh]], axis=-1)
    gf = jnp.concatenate([t[..., nh:2 * nh], t[..., 3 * nh:4 * nh]], axis=-1)
    return jnp.pad(gi, pad), jnp.pad(gf, pad)


def _trunk(x, mem, splits, pre_norm_g, post_norm_g, w_in, w_out, diff_lambda, diff_subln_g,
           hy_conv_w, hy_conv_b, hy_ffn_w1, hy_ffn_b1, hy_ffn_w2, hy_ffn_b2, hy_ffn_w3,
           hy_freq, hy_bias, ml_conv_w, ml_conv_b, ml_gate_b, mem_norm_g, w_mem_kv):
    B, L, _ = x.shape
    M = mem.shape[1]
    depth = w_in.shape[0]
    g = GROUP_W
    h1, _, k1p, _ = _hy_dims(L)
    wl = HY_N2 * g

    ca, sa, pa = _rope_tables(L, HEAD_DIM, HEAD_DIM // ROPE_FRACTION)
    cb, sb, pb = _rope_tables(L, DIFF_QK_DIM, DIFF_QK_DIM // ROPE_FRACTION)
    rope_tabs = (ca, sa, cb, sb, pa, pb)
    dil_bias = jnp.asarray(_dil_bias_table())
    feats, tcol = _hyena_features(L)
    absdelta = jnp.abs(jnp.linspace(math.log(HYENA_TARGET) / HYENA_SLOW_DECAY,
                                    math.log(HYENA_TARGET) / HYENA_FAST_DECAY, g, dtype=F32))[None]
    m1, f3, f3i, m4r, m4i = _hy_tables(L)
    head_of = np.arange(g) // HEAD_DIM
    same_head = (head_of[:, None] == head_of[None, :]).astype(np.float32)
    hmean = jnp.asarray(same_head / HEAD_DIM, BF16)
    mlstm_consts = _mlstm_consts()

    x2d = x.reshape(B * L, D_MODEL)
    mem2d = mem.reshape(B * M, D_MODEL)
    for li in range(depth):
        lam_init = 0.8 - 0.6 * math.exp(-0.3 * li)
        w_all = _relayout_w_in(w_in[li])
        h_main, cv, cx1, cx2, gate_i, gate_f = _inproj(x2d, pre_norm_g[li][None], w_all,
                                                       rope_tabs, B, L)

        oa = _dilattn(h_main, dil_bias, B, L)
        ob = _diffattn(h_main, diff_lambda[li], jnp.tile(diff_subln_g[li], N_HEADS)[None],
                       hmean, lam_init, B, L)

        w1 = jnp.pad(hy_ffn_w1[li], ((0, LANE - HYENA_EMB), (0, 0)))
        hfilt = _hy_filter(feats, tcol, w1, hy_ffn_b1[li][None], hy_ffn_w2[li],
                           hy_ffn_b2[li][None], hy_ffn_w3[li], hy_freq[li][None], absdelta, L)
        a_filt = _hy_stage1_call(hfilt, m1, L)
        g_spec = _hy_spectrum(a_filt, f3, L)
        cw = jnp.tile(hy_conv_w[li].reshape(3, 3, g).transpose(1, 0, 2).reshape(9, g), (1, HY_N2))
        cbw = jnp.tile(hy_conv_b[li].reshape(3, g), (1, HY_N2))
        z, x1c, x2c, a = _hy_front(cv, cx1, cx2, cw, cbw, m1, B, L)
        bias_w = jnp.tile(hy_bias[li], (1, HY_N2))
        bsp = _hy_mid(a, g_spec, 0, f3, f3i, B, L)
        z, a = _hy_back(bsp, z, x1c, bias_w[0:1], m4r, m4i, m1, B, L, last=False)
        bsp = _hy_mid(a, g_spec, 1, f3, f3i, B, L)
        oc = _hy_back(bsp, z, x2c, bias_w[1:2], m4r, m4i, m1, B, L, last=True)

        bias_i, bias_f = _split_gates(ml_gate_b[li][None])
        od = _mlstm(h_main, gate_i, gate_f, ml_conv_w[li], ml_conv_b[li][None], bias_i, bias_f,
                    mlstm_consts, B, L)

        mkv = _memkv(mem2d, mem_norm_g[li][None], w_mem_kv[li].astype(BF16))
        ox = _memattn(h_main, mkv, B, L, M)

        branch_outs = (oa, ob, oc, od, ox)
        w_o, png = w_out[li].astype(BF16), post_norm_g[li][None]
        if li + 1 < depth:
            x2d = _outproj(x2d, branch_outs, h_main, w_o, png)
    ys, b0 = [], 0
    for nb in splits:
        y = _outproj(x2d, branch_outs, h_main, w_o, png, row0=b0 * L, n_rows=nb * L)
        ys.append(y.reshape(nb, L, D_MODEL))
        b0 += nb
    return tuple(ys)


def kernel(x_prompt, x_sample, mem_prompt, mem_sample, pre_norm_g, post_norm_g, w_in, w_out,
           diff_lambda, diff_subln_g, hy_conv_w, hy_conv_b, hy_ffn_w1, hy_ffn_b1, hy_ffn_w2,
           hy_ffn_b2, hy_ffn_w3, hy_freq, hy_bias, ml_conv_w, ml_conv_b, ml_gate_b,
           mem_norm_g, w_mem_kv):
    x = jnp.concatenate([x_prompt, x_sample], axis=0)
    mem = jnp.concatenate([mem_prompt, mem_sample], axis=0)
    return _trunk(x, mem, (x_prompt.shape[0], x_sample.shape[0]), pre_norm_g, post_norm_g, w_in,
                  w_out, diff_lambda, diff_subln_g, hy_conv_w, hy_conv_b, hy_ffn_w1, hy_ffn_b1,
                  hy_ffn_w2, hy_ffn_b2, hy_ffn_w3, hy_freq, hy_bias, ml_conv_w, ml_conv_b,
                  ml_gate_b, mem_norm_g, w_mem_kv)
```

```python
import functools
import math

import numpy as np
import jax
import jax.numpy as jnp
from jax import lax
from jax.experimental import pallas as pl
from jax.experimental.pallas import tpu as pltpu

F32 = jnp.float32
BF16 = jnp.bfloat16

D_MODEL = 1024
HEAD_DIM = 64
GROUP_W = 256
N_HEADS = GROUP_W // HEAD_DIM
NORM_EPS = 1e-6
NEG_INF = -1e30
ROPE_THETA = 500000.0
ROPE_FRACTION = 4
DIL_PATTERNS = ((128, 1), (512, 4), (2048, 16))
DIFF_QK_DIM = HEAD_DIM // 2
DIFF_SUBLN_EPS = 1e-5
HYENA_ORDER = 2
HYENA_BANDS = 16
HYENA_EMB = 1 + 2 * HYENA_BANDS
HYENA_HIDDEN = 64
HYENA_FAST_DECAY = 0.3
HYENA_SLOW_DECAY = 1.5
HYENA_TARGET = 1e-2
N_MLSTM_GATES = 4 * N_HEADS

(BLK_AQ, BLK_AK, BLK_AV, BLK_AG, BLK_BQ, BLK_BK, BLK_BV, BLK_BG, BLK_CG,
 BLK_DQ, BLK_DK, BLK_DV, BLK_DO, BLK_DG, BLK_XQ, BLK_XG) = range(16)
N_MAIN_BLK = 16
MAIN_W = N_MAIN_BLK * GROUP_W
W_ALL = MAIN_W + 4 * GROUP_W

VMEM_LIMIT_BYTES = 56 * 1024 * 1024
LANE = 128
BF16_SUBLANES = 16

ROW_TILE = 512
DIL_Q_TILE = 256
DIL_PAD = 1024
DIL_K_TILE = 256
DIL_VT_ROWS = HEAD_DIM + BF16_SUBLANES
DIFF_Q_TILE = 256
DIFF_K_CHUNK = 512
DIFF_VT_ROWS = HEAD_DIM + BF16_SUBLANES
LOG2E = 1.4426950408889634
MEM_Q_TILE = 512
MLSTM_T = 128
CONV_HALO = 16
HY_N2 = 128


def _cparams(sem, vmem=VMEM_LIMIT_BYTES):
    return pltpu.CompilerParams(dimension_semantics=sem, vmem_limit_bytes=vmem)


def _sigmoid(v):
    return 1.0 / (1.0 + jnp.exp(-v))


def _silu(v):
    return v * _sigmoid(v)


def _dot(a, b):
    return jnp.dot(a, b, preferred_element_type=F32)


def _dot_nt(a, b):
    return lax.dot_general(a, b, (((1,), (1,)), ((), ())), preferred_element_type=F32)


def _dot_tn(a, b):
    return lax.dot_general(a, b, (((0,), (0,)), ((), ())), preferred_element_type=F32)


def _head_mask(h, width, dtype):
    lane = lax.broadcasted_iota(jnp.int32, (1, GROUP_W), 1)
    return ((lane // width) == h).astype(dtype)


def _inproj_kernel(x_ref, g_ref, w_ref, ca_ref, sa_ref, cb_ref, sb_ref, pa_ref, pb_ref,
                   h_ref, cv_ref, cx1_ref, cx2_ref, gi_ref, gf_ref):
    x = x_ref[...]
    ms = jnp.mean(x * x, axis=-1, keepdims=True)
    xn = (x * lax.rsqrt(ms + NORM_EPS) * g_ref[...]).astype(BF16)

    def proj(j, width=GROUP_W):
        return _dot(xn, w_ref[:, j * GROUP_W:j * GROUP_W + width])

    def rope(acc, c_ref, s_ref, p_ref):
        partner = _dot(acc.astype(BF16), p_ref[...])
        return acc * c_ref[...] + partner * s_ref[...]

    def finish(j, acc):
        if j in (BLK_AQ, BLK_AK):
            acc = rope(acc, ca_ref, sa_ref, pa_ref)
        if j in (BLK_BQ, BLK_BK):
            acc = rope(acc, cb_ref, sb_ref, pb_ref)
        if j in (BLK_AQ, BLK_XQ):
            acc = acc * (LOG2E / math.sqrt(HEAD_DIM))
        if j == BLK_BQ:
            acc = acc * (LOG2E / math.sqrt(DIFF_QK_DIM))
        if j < N_MAIN_BLK:
            h_ref[:, j * GROUP_W:(j + 1) * GROUP_W] = acc.astype(BF16)
        elif j < N_MAIN_BLK + 3:
            (cv_ref, cx1_ref, cx2_ref)[j - N_MAIN_BLK][...] = acc.astype(BF16)
        else:
            gi_ref[...] = acc[:, 0:LANE]
            gf_ref[...] = acc[:, LANE:2 * LANE]

    for j in range(0, W_ALL // GROUP_W, 2):
        acc2 = proj(j, 2 * GROUP_W)
        finish(j, acc2[:, 0:GROUP_W])
        finish(j + 1, acc2[:, GROUP_W:2 * GROUP_W])


def _inproj(x2d, g, w_all, rope_tabs, B, L):
    ca, sa, cb, sb, pa, pb = rope_tabs
    nt = L // ROW_TILE
    n_tok = B * L
    row = lambda i, b: (b * nt + i, 0)
    tab = lambda i, b: (i, 0)
    const = lambda i, b: (0, 0)
    out_shapes = (
        jax.ShapeDtypeStruct((n_tok, MAIN_W), BF16),
        jax.ShapeDtypeStruct((n_tok, GROUP_W), BF16),
        jax.ShapeDtypeStruct((n_tok, GROUP_W), BF16),
        jax.ShapeDtypeStruct((n_tok, GROUP_W), BF16),
        jax.ShapeDtypeStruct((n_tok, LANE), F32),
        jax.ShapeDtypeStruct((n_tok, LANE), F32),
    )
    return pl.pallas_call(
        _inproj_kernel,
        grid=(nt, B),
        in_specs=[
            pl.BlockSpec((ROW_TILE, D_MODEL), row),
            pl.BlockSpec((1, D_MODEL), const),
            pl.BlockSpec((D_MODEL, W_ALL), const),
            pl.BlockSpec((ROW_TILE, GROUP_W), tab),
            pl.BlockSpec((ROW_TILE, GROUP_W), tab),
            pl.BlockSpec((ROW_TILE, GROUP_W), tab),
            pl.BlockSpec((ROW_TILE, GROUP_W), tab),
            pl.BlockSpec((GROUP_W, GROUP_W), const),
            pl.BlockSpec((GROUP_W, GROUP_W), const),
        ],
        out_specs=(
            pl.BlockSpec((ROW_TILE, MAIN_W), row),
            pl.BlockSpec((ROW_TILE, GROUP_W), row),
            pl.BlockSpec((ROW_TILE, GROUP_W), row),
            pl.BlockSpec((ROW_TILE, GROUP_W), row),
            pl.BlockSpec((ROW_TILE, LANE), row),
            pl.BlockSpec((ROW_TILE, LANE), row),
        ),
        out_shape=out_shapes,
        compiler_params=_cparams(("parallel", "parallel")),
        name="inproj",
    )(x2d, g, w_all, ca, sa, cb, sb, pa, pb)


def _memkv_kernel(m_ref, g_ref, w_ref, o_ref):
    x = m_ref[...]
    ms = jnp.mean(x * x, axis=-1, keepdims=True)
    xn = (x * lax.rsqrt(ms + NORM_EPS) * g_ref[...]).astype(BF16)
    o_ref[...] = _dot(xn, w_ref[...]).astype(BF16)


def _memkv(mem2d, g, w):
    rows = mem2d.shape[0]
    tile = math.gcd(rows, ROW_TILE)
    return pl.pallas_call(
        _memkv_kernel,
        grid=(rows // tile,),
        in_specs=[
            pl.BlockSpec((tile, D_MODEL), lambda i: (i, 0)),
            pl.BlockSpec((1, D_MODEL), lambda i: (0, 0)),
            pl.BlockSpec((D_MODEL, 2 * GROUP_W), lambda i: (0, 0)),
        ],
        out_specs=pl.BlockSpec((tile, 2 * GROUP_W), lambda i: (i, 0)),
        out_shape=jax.ShapeDtypeStruct((rows, 2 * GROUP_W), BF16),
        compiler_params=_cparams(("parallel",)),
        name="memkv",
    )(mem2d, g, w)


def _memattn_kernel(q_ref, mk_ref, mv_ref, o_ref):
    q = q_ref[...]
    mk = mk_ref[...]
    mv = mv_ref[...]
    acc = jnp.zeros(q.shape, F32)
    for h in range(N_HEADS):
        hm = _head_mask(h, HEAD_DIM, BF16)
        s = _dot_nt(q * hm, mk)
        m = jnp.max(s, axis=-1, keepdims=True)
        p = jnp.exp2(s - m)
        l = jnp.sum(p, axis=-1, keepdims=True)
        acc = acc + _dot(p.astype(BF16), mv * hm) * (1.0 / l)
    o_ref[...] = acc.astype(BF16)


def _memattn(h_main, mkv, B, L, M):
    nq = L // MEM_Q_TILE
    return pl.pallas_call(
        _memattn_kernel,
        grid=(B, nq),
        in_specs=[
            pl.BlockSpec((MEM_Q_TILE, GROUP_W), lambda b, i: (b * nq + i, BLK_XQ)),
            pl.BlockSpec((M, GROUP_W), lambda b, i: (b, 0)),
            pl.BlockSpec((M, GROUP_W), lambda b, i: (b, 1)),
        ],
        out_specs=pl.BlockSpec((MEM_Q_TILE, GROUP_W), lambda b, i: (b * nq + i, 0)),
        out_shape=jax.ShapeDtypeStruct((B * L, GROUP_W), BF16),
        compiler_params=_cparams(("parallel", "parallel")),
        name="memattn",
    )(h_main, mkv, mkv)


def _dil_bias_table():
    w = DIL_Q_TILE + 2 * DIL_PAD
    d = np.arange(DIL_Q_TILE)[None, :] - np.arange(w)[:, None] + DIL_PAD
    count = np.zeros(d.shape, np.float64)
    for win, dil in DIL_PATTERNS:
        reach = (win // (2 * dil)) * dil
        count += (d % dil == 0) & (np.abs(d) <= reach)
    return np.where(count > 0, np.log2(np.maximum(count, 1.0)), NEG_INF).astype(np.float32)


def _dilattn_kernel(q_ref, k_ref, v_ref, bias_ref, o_ref, kpad, vt_ref, *, L):
    i = pl.program_id(1)
    tq, hd, kt = DIL_Q_TILE, HEAD_DIM, DIL_K_TILE
    w = tq + 2 * DIL_PAD
    n_pad = DIL_PAD // kt

    @pl.when(i == 0)
    def _():
        zeros = jnp.zeros((DIL_PAD, GROUP_W), BF16)
        kpad[0:DIL_PAD, :] = zeros
        kpad[DIL_PAD + L:DIL_PAD + L + DIL_PAD, :] = zeros
        kpad[DIL_PAD:DIL_PAD + L, :] = k_ref[...]
        tail = (lax.broadcasted_iota(jnp.int32, (DIL_VT_ROWS - hd, kt), 0) == 0).astype(BF16)
        for c in range(L // kt + 2 * n_pad):
            inside = n_pad <= c < n_pad + L // kt
            if inside:
                lo = (c - n_pad) * kt
                vt = v_ref[lo:lo + kt, :].astype(F32).T.astype(BF16)
            for h in range(N_HEADS):
                vt_ref[c, h, 0:hd, :] = (vt[h * hd:(h + 1) * hd] if inside
                                         else jnp.zeros((hd, kt), BF16))
                vt_ref[c, h, hd:DIL_VT_ROWS, :] = tail

    q0 = pl.multiple_of(i * tq, tq)
    qt = q_ref[...].astype(F32).T
    feat_head = lax.broadcasted_iota(jnp.int32, (GROUP_W, 1), 0) // hd
    kw = kpad[pl.ds(q0, w), :]
    jpos = q0 - DIL_PAD + lax.broadcasted_iota(jnp.int32, (w, tq), 0)
    bias = bias_ref[...] + jnp.where(jpos >= 0, jnp.where(jpos < L, 0.0, NEG_INF), NEG_INF)
    bias2 = jnp.concatenate([bias, bias], axis=1)
    heads = []
    for pair in range(N_HEADS // 2):
        qt2 = jnp.concatenate([jnp.where(feat_head == 2 * pair + c, qt, 0.0) for c in range(2)],
                              axis=1).astype(BF16)
        s = _dot(kw, qt2) + bias2
        m = jnp.max(s, axis=0, keepdims=True)
        p = jnp.exp2((s - m).astype(BF16))
        for c in range(2):
            h = 2 * pair + c
            o = jnp.zeros((DIL_VT_ROWS, tq), F32)
            for t in range(w // kt):
                o = o + _dot(vt_ref[i * (tq // kt) + t, h],
                             p[t * kt:(t + 1) * kt, c * tq:(c + 1) * tq])
            heads.append(o[0:hd] * (1.0 / o[hd:hd + 1]))
    o_ref[...] = jnp.concatenate(heads, axis=0).T.astype(BF16)


def _dilattn(h_main, bias, B, L):
    nq = L // DIL_Q_TILE
    w = DIL_Q_TILE + 2 * DIL_PAD
    return pl.pallas_call(
        functools.partial(_dilattn_kernel, L=L),
        grid=(B, nq),
        in_specs=[
            pl.BlockSpec((DIL_Q_TILE, GROUP_W), lambda b, i: (b * nq + i, BLK_AQ)),
            pl.BlockSpec((L, GROUP_W), lambda b, i: (b, BLK_AK)),
            pl.BlockSpec((L, GROUP_W), lambda b, i: (b, BLK_AV)),
            pl.BlockSpec((w, DIL_Q_TILE), lambda b, i: (0, 0)),
        ],
        out_specs=pl.BlockSpec((DIL_Q_TILE, GROUP_W), lambda b, i: (b * nq + i, 0)),
        out_shape=jax.ShapeDtypeStruct((B * L, GROUP_W), BF16),
        scratch_shapes=[pltpu.VMEM((L + 2 * DIL_PAD, GROUP_W), BF16),
                        pltpu.VMEM(((L + 2 * DIL_PAD) // DIL_K_TILE, N_HEADS, DIL_VT_ROWS,
                                    DIL_K_TILE), BF16)],
        compiler_params=_cparams(("parallel", "arbitrary")),
        name="dilattn",
    )(h_main, h_main, h_main, bias)


def _split3(x):
    hi = x.astype(BF16)
    r1 = x - hi.astype(F32)
    mid = r1.astype(BF16)
    lo = (r1 - mid.astype(F32)).astype(BF16)
    return hi, mid, lo


def _split_dot(x, mat):
    return sum(_dot(t, mat) for t in _split3(x))


def _tri_dot(mat, x):
    return sum(_dot(mat, t) for t in _split3(x))


def _dot_f32(a, b):
    ah, am, al = _split3(a)
    bh, bm, bl = _split3(b)
    return (_dot(ah, bh) + (_dot(ah, bm) + _dot(am, bh))
            + (_dot(ah, bl) + _dot(am, bm) + _dot(al, bh)))


def _diffattn_kernel(q_ref, qn_ref, k_ref, v_ref, lam_ref, g_ref, hmean_ref, o_ref,
                     vt_ref, sa_ref, sb_ref, m_ref, oh_ref, *, lam_init, L):
    hd = HEAD_DIM
    first_tile = pl.program_id(1) == 0

    @pl.when(first_tile)
    def _():
        tail = (lax.broadcasted_iota(jnp.int32, (DIFF_VT_ROWS - hd, DIFF_K_CHUNK), 0) == 0)
        for c in range(L // DIFF_K_CHUNK):
            lo, hi = c * DIFF_K_CHUNK, (c + 1) * DIFF_K_CHUNK
            vt = v_ref[lo:hi, :].astype(F32).T.astype(BF16)
            for h in range(N_HEADS):
                vt_ref[h, 0:hd, lo:hi] = vt[h * hd:(h + 1) * hd]
                vt_ref[h, hd:DIFF_VT_ROWS, lo:hi] = tail.astype(BF16)

    lp = lam_ref[...]
    lam = (jnp.exp(jnp.sum(lp[0:1] * lp[1:2], axis=-1, keepdims=True))
           - jnp.exp(jnp.sum(lp[2:3] * lp[3:4], axis=-1, keepdims=True)) + lam_init)
    qt = q_ref[...].astype(F32).T
    qt_next = qn_ref[...].astype(F32).T
    feat_group = lax.broadcasted_iota(jnp.int32, (GROUP_W, 1), 0) // DIFF_QK_DIM
    tq = qt.shape[1]
    chunks = [(c * DIFF_K_CHUNK, (c + 1) * DIFF_K_CHUNK) for c in range(L // DIFF_K_CHUNK)]

    def masked_qt(h):
        wraps = h == N_HEADS
        src = jnp.where(wraps, qt_next, qt)
        hh = jnp.where(wraps, 0, h)
        return jnp.concatenate([jnp.where(feat_group == 2 * hh + c, src, 0.0) for c in range(2)],
                               axis=1).astype(BF16)

    def stage(h_next, nxt, h_cur, cur):
        qt2 = masked_qt(h_next)
        mx = None
        if h_cur is not None:
            m = m_ref[cur, 0:1, :]
            o = jnp.zeros((DIFF_VT_ROWS, 2 * tq), F32)
        for lo, hi in chunks:
            s = _dot(k_ref[lo:hi, :], qt2)
            s_refs[nxt][lo:hi, :] = s
            cm = jnp.max(s, axis=0, keepdims=True)
            mx = cm if mx is None else jnp.maximum(mx, cm)
            if h_cur is not None:
                p = jnp.exp2((s_refs[cur][lo:hi, :] - m).astype(BF16))
                o = o + _dot(vt_ref[h_cur, :, lo:hi], p)
        m_ref[nxt] = jnp.broadcast_to(mx, m_ref.shape[1:])
        if h_cur is not None:
            on = o[0:hd] * (1.0 / o[hd:hd + 1])
            oh_ref[h_cur] = on[:, 0:tq] - on[:, tq:2 * tq] * lam

    s_refs = (sa_ref, sb_ref)

    @pl.when(first_tile)
    def _():
        stage(0, 0, None, None)

    def body(j, carry):
        stage(2 * j + 1, 1, 2 * j, 0)
        stage(2 * j + 2, 0, 2 * j + 1, 1)
        return carry

    lax.fori_loop(0, N_HEADS // 2, body, 0)
    acc = jnp.concatenate([oh_ref[h] for h in range(N_HEADS)], axis=0).T
    ms = _split_dot(acc * acc, hmean_ref[...])
    y = acc * lax.rsqrt(ms + DIFF_SUBLN_EPS) * g_ref[...] * (1.0 - lam_init)
    o_ref[...] = y.astype(BF16)


def _diffattn(h_main, lam_p, subln_g, hmean, lam_init, B, L):
    nq = L // DIFF_Q_TILE
    return pl.pallas_call(
        functools.partial(_diffattn_kernel, lam_init=lam_init, L=L),
        grid=(B, nq),
        in_specs=[
            pl.BlockSpec((DIFF_Q_TILE, GROUP_W), lambda b, i: (b * nq + i, BLK_BQ)),
            pl.BlockSpec((DIFF_Q_TILE, GROUP_W),
                         lambda b, i: (b * nq + jnp.minimum(i + 1, nq - 1), BLK_BQ)),
            pl.BlockSpec((L, GROUP_W), lambda b, i: (b, BLK_BK)),
            pl.BlockSpec((L, GROUP_W), lambda b, i: (b, BLK_BV)),
            pl.BlockSpec((4, DIFF_QK_DIM), lambda b, i: (0, 0)),
            pl.BlockSpec((1, GROUP_W), lambda b, i: (0, 0)),
            pl.BlockSpec((GROUP_W, GROUP_W), lambda b, i: (0, 0)),
        ],
        out_specs=pl.BlockSpec((DIFF_Q_TILE, GROUP_W), lambda b, i: (b * nq + i, 0)),
        out_shape=jax.ShapeDtypeStruct((B * L, GROUP_W), BF16),
        scratch_shapes=[pltpu.VMEM((N_HEADS, DIFF_VT_ROWS, L), BF16),
                        pltpu.VMEM((L, 2 * DIFF_Q_TILE), F32),
                        pltpu.VMEM((L, 2 * DIFF_Q_TILE), F32),
                        pltpu.VMEM((2, 8, 2 * DIFF_Q_TILE), F32),
                        pltpu.VMEM((N_HEADS, HEAD_DIM, DIFF_Q_TILE), F32)],
        compiler_params=_cparams(("parallel", "arbitrary")),
        name="diffattn",
    )(h_main, h_main, h_main, h_main, lam_p, subln_g, hmean)


def _log_sigmoid(v):
    return jnp.minimum(v, 0.0) - jnp.log(1.0 + jnp.exp(-jnp.abs(v)))


def _mlstm_kernel(q_ref, k_ref, v_ref, gi_ref, gf_ref, cw_ref, cb_ref, bi_ref, bf_ref,
                  ltri_ref, utri_ref, ecol_ref, elane_ref, kmask_ref, vmask_ref, ones_ref,
                  bd_ref, hsum_ref, o_ref,
                  qpad, kpad, qs, ks, hfw, hbw, cst, nst, mst, *, L):
    T = MLSTM_T
    nc = L // T
    halo = CONV_HALO
    nh = N_HEADS

    zpad = jnp.zeros((halo, GROUP_W), BF16)
    for pad, src in ((qpad, q_ref), (kpad, k_ref)):
        pad[0:halo, :] = zpad
        pad[halo + L:halo + L + halo, :] = zpad
        pad[halo:halo + L, :] = src[...]

    def conv_body(c, carry):
        r0 = pl.multiple_of(c * T, T)
        for idx, (pad, dst, scale) in enumerate(((qpad, qs, 1.0),
                                                 (kpad, ks, 1.0 / math.sqrt(HEAD_DIM)))):
            xw = pad[pl.ds(r0, T + 2 * halo), :].astype(F32)
            xm = pltpu.roll(xw, 1, axis=0)[halo:halo + T]
            xp = pltpu.roll(xw, T + 2 * halo - 1, axis=0)[halo:halo + T]
            xc = xw[halo:halo + T]
            lo, hi = idx * GROUP_W, (idx + 1) * GROUP_W
            y = (xm * cw_ref[0:1, lo:hi] + xc * cw_ref[1:2, lo:hi] + xp * cw_ref[2:3, lo:hi]
                 + cb_ref[0:1, lo:hi])
            dst[pl.ds(r0, T), :] = (_silu(y) * scale).astype(BF16)
        return carry

    lax.fori_loop(0, nc, conv_body, 0)

    cst[...] = jnp.zeros(cst.shape, F32)
    nst[...] = jnp.zeros(nst.shape, F32)
    mst[...] = jnp.zeros(mst.shape, F32)

    is_fw = lax.broadcasted_iota(jnp.int32, (1, LANE), 1) < nh
    rowi = lax.broadcasted_iota(jnp.int32, (T, LANE), 0)
    row4 = lax.broadcasted_iota(jnp.int32, (T, nh * T), 0)
    col4 = lax.broadcasted_iota(jnp.int32, (T, nh * T), 1) % T
    causal = (col4 <= row4, col4 >= row4)

    def body(c, carry):
        rows = (pl.multiple_of(c * T, T), pl.multiple_of((nc - 1 - c) * T, T))
        gate_i = jnp.where(is_fw, gi_ref[pl.ds(rows[0], T), :], gi_ref[pl.ds(rows[1], T), :])
        gate_f = jnp.where(is_fw, gf_ref[pl.ds(rows[0], T), :], gf_ref[pl.ds(rows[1], T), :])
        gate_i = gate_i + bi_ref[...]
        parts = _split3(_log_sigmoid(gate_f + bf_ref[...]))
        pre = sum(_dot(ltri_ref[...], t) for t in parts)
        suf = sum(_dot(utri_ref[...], t) for t in parts)
        cum = jnp.where(is_fw, pre, suf)
        b_end = jnp.where(is_fw, pre[T - 1:T, :], suf[0:1, :])
        key_w = gate_i - cum

        pmax, smax = key_w, key_w
        sh = 1
        while sh < T:
            pmax = jnp.maximum(pmax, jnp.where(rowi >= sh, pltpu.roll(pmax, sh, axis=0), NEG_INF))
            smax = jnp.maximum(smax, jnp.where(rowi < T - sh, pltpu.roll(smax, T - sh, axis=0),
                                               NEG_INF))
            sh *= 2
        m_prev = mst[0:1, :]
        inter = cum + m_prev
        m_t = jnp.maximum(inter, cum + jnp.where(is_fw, pmax, smax))
        u = cum - m_t
        a = b_end + key_w
        m_new = jnp.maximum(b_end + m_prev, jnp.max(a, axis=0, keepdims=True))
        mst[0:1, :] = m_new
        stack = jnp.concatenate(
            [jnp.exp(inter - m_t), jnp.exp(-m_t), jnp.exp(a - m_new),
             jnp.broadcast_to(jnp.exp(b_end + m_prev - m_new), (8, LANE))], axis=0)
        st_hi, st_mid, _ = _split3(stack)
        key_w_t = key_w.T

        for d in range(2):
            r0 = rows[d]
            qc = qs[pl.ds(r0, T), :]
            kc = ks[pl.ds(r0, T), :]
            vc = v_ref[pl.ds(r0, T), :]
            ex = _dot(st_hi, elane_ref[d]) + _dot(st_mid, elane_ref[d])
            w_inter, floor, w_key = ex[0:T], ex[T:2 * T], ex[2 * T:3 * T]
            sp_row = ex[3 * T:3 * T + 1]

            r_row = jnp.concatenate([key_w_t[d * nh + h:d * nh + h + 1, :] for h in range(nh)],
                                    axis=1)
            dlog = _split_dot(u, ecol_ref[d]) + r_row
            decay = jnp.exp(jnp.where(causal[d], dlog, NEG_INF))

            kt = kc.astype(F32).T.astype(BF16)
            qk = _dot(qc, jnp.concatenate([kt] * nh, axis=1) * kmask_ref[...]) * decay
            qk_hi = qk.astype(BF16)
            qk_lo = (qk - qk_hi.astype(F32)).astype(BF16)
            vblk = jnp.concatenate([vc] * nh, axis=0) * vmask_ref[...]
            num = _dot(qk_hi, vblk)
            den = _dot(qk_hi, ones_ref[...]) + _dot(qk_lo, ones_ref[...])

            c_prev = cst[d]
            n_prev = nst[d:d + 1, :]
            num = num + w_inter * _dot(qc, c_prev.astype(BF16))
            den = den + w_inter * _dot((qc.astype(F32) * n_prev).astype(BF16), hsum_ref[...])
            h_out = num / jnp.maximum(jnp.abs(den), floor)
            if d == 0:
                hfw[pl.ds(r0, T), :] = h_out
            else:
                hbw[pl.ds(r0, T), :] = h_out

            kw = kc.astype(F32) * w_key
            cst[d] = c_prev * sp_row + _dot_tn(kw.astype(BF16), vc) * bd_ref[...]
            nst[d:d + 1, :] = n_prev * sp_row + jnp.sum(kw, axis=0, keepdims=True)
        return carry

    lax.fori_loop(0, nc, body, 0)
    o_ref[...] = (hfw[...] + hbw[...]).astype(BF16)


def _mlstm_consts():
    T, nh, g = MLSTM_T, N_HEADS, GROUP_W
    tri = np.tril(np.ones((T, T), np.float32))
    src = np.arange(LANE)[:, None]
    ecol = np.stack([(src == d * nh + np.arange(nh * T)[None, :] // T) for d in range(2)])
    elane = np.stack([(src == d * nh + np.arange(g)[None, :] // HEAD_DIM) for d in range(2)])
    head_of = np.arange(g) // HEAD_DIM
    blk_of = np.arange(nh * T) // T
    kmask = head_of[:, None] == blk_of[None, :]
    vmask = blk_of[:, None] == head_of[None, :]
    same_head = head_of[:, None] == head_of[None, :]
    b16 = lambda a: jnp.asarray(a.astype(np.float32), BF16)
    return (b16(tri), b16(tri.T), b16(ecol), b16(elane), b16(kmask), b16(vmask), b16(vmask),
            jnp.asarray(same_head.astype(np.float32)), b16(same_head))


def _mlstm(h_main, gate_i, gate_f, conv_w, conv_b, bias_i, bias_f, consts, B, L):
    T = MLSTM_T
    const = lambda b: (0, 0)
    const3 = lambda b: (0, 0, 0)
    return pl.pallas_call(
        functools.partial(_mlstm_kernel, L=L),
        grid=(B,),
        in_specs=[
            pl.BlockSpec((L, GROUP_W), lambda b: (b, BLK_DQ)),
            pl.BlockSpec((L, GROUP_W), lambda b: (b, BLK_DK)),
            pl.BlockSpec((L, GROUP_W), lambda b: (b, BLK_DV)),
            pl.BlockSpec((L, LANE), lambda b: (b, 0)),
            pl.BlockSpec((L, LANE), lambda b: (b, 0)),
            pl.BlockSpec((3, 2 * GROUP_W), const),
            pl.BlockSpec((1, 2 * GROUP_W), const),
            pl.BlockSpec((1, LANE), const),
            pl.BlockSpec((1, LANE), const),
            pl.BlockSpec((T, T), const),
            pl.BlockSpec((T, T), const),
            pl.BlockSpec((2, LANE, N_HEADS * T), const3),
            pl.BlockSpec((2, LANE, GROUP_W), const3),
            pl.BlockSpec((GROUP_W, N_HEADS * T), const),
            pl.BlockSpec((N_HEADS * T, GROUP_W), const),
            pl.BlockSpec((N_HEADS * T, GROUP_W), const),
            pl.BlockSpec((GROUP_W, GROUP_W), const),
            pl.BlockSpec((GROUP_W, GROUP_W), const),
        ],
        out_specs=pl.BlockSpec((L, GROUP_W), lambda b: (b, 0)),
        out_shape=jax.ShapeDtypeStruct((B * L, GROUP_W), BF16),
        scratch_shapes=[
            pltpu.VMEM((L + 2 * CONV_HALO, GROUP_W), BF16),
            pltpu.VMEM((L + 2 * CONV_HALO, GROUP_W), BF16),
            pltpu.VMEM((L, GROUP_W), BF16),
            pltpu.VMEM((L, GROUP_W), BF16),
            pltpu.VMEM((L, GROUP_W), F32),
            pltpu.VMEM((L, GROUP_W), F32),
            pltpu.VMEM((2, GROUP_W, GROUP_W), F32),
            pltpu.VMEM((8, GROUP_W), F32),
            pltpu.VMEM((8, LANE), F32),
        ],
        compiler_params=_cparams(("parallel",)),
        name="mlstm",
    )(h_main, h_main, h_main, gate_i, gate_f, conv_w, conv_b, bias_i, bias_f, *consts)


def _hy_dims(L):
    h1 = L // HY_N2
    k1h = h1 + 1
    k1p = -(-k1h // BF16_SUBLANES) * BF16_SUBLANES
    kg = 12 if k1p % 12 == 0 else BF16_SUBLANES
    return h1, k1h, k1p, kg


def _hy_tables(L):
    h1, k1h, k1p, _ = _hy_dims(L)
    n1_len = 2 * h1
    n = 2 * L
    k1 = np.arange(k1p)[:, None].astype(np.float64)
    live = (np.arange(k1p) < k1h)[:, None]
    n1 = np.arange(h1)[None, :].astype(np.float64)
    ang = 2.0 * np.pi * k1 * n1 / n1_len
    m1 = np.concatenate([np.where(live, np.cos(ang), 0.0), np.where(live, -np.sin(ang), 0.0)], 0)

    n2 = np.arange(HY_N2)[None, None, :].astype(np.float64)
    k2 = np.arange(HY_N2)[None, :, None].astype(np.float64)
    kk = np.arange(k1p)[:, None, None] + n1_len * k2
    th = 2.0 * np.pi * kk * n2 / n
    fr, fi = np.cos(th), -np.sin(th)
    f3 = np.concatenate([np.concatenate([fr, -fi], 2), np.concatenate([fi, fr], 2)], 1)
    er, ei = np.transpose(np.cos(th), (0, 2, 1)), np.transpose(np.sin(th), (0, 2, 1))
    f3i = np.concatenate([np.concatenate([er, -ei], 2), np.concatenate([ei, er], 2)], 1)
    live3 = (np.arange(k1p) < k1h)[:, None, None]
    f3 = np.where(live3, f3, 0.0)
    f3i = np.where(live3, f3i, 0.0)

    nn1 = np.arange(h1)[:, None].astype(np.float64)
    kc = np.arange(k1p)[None, :].astype(np.float64)
    ph = 2.0 * np.pi * nn1 * kc / n1_len
    edge = (np.arange(k1p) == 0) | (np.arange(k1p) == h1)
    livec = (np.arange(k1p) < k1h)[None, :]
    m4r = np.where(livec, np.where(edge[None, :], np.cos(ph), 2.0 * np.cos(ph)), 0.0) / n
    m4i = np.where(livec & ~edge[None, :], -2.0 * np.sin(ph), 0.0) / n
    return (jnp.asarray(m1, BF16), jnp.asarray(f3, BF16), jnp.asarray(f3i, BF16),
            jnp.asarray(m4r, BF16), jnp.asarray(m4i, BF16))


def _to_wide(x, h1):
    return x.astype(F32).reshape(h1, HY_N2 * GROUP_W)


def _hy_stage1(z_bf16, m1_ref, a_ref, k1p):
    a = _dot(m1_ref[...], z_bf16)
    a_ref[0, 0] = a[0:k1p].reshape(k1p, HY_N2, GROUP_W).astype(BF16)
    a_ref[0, 1] = a[k1p:2 * k1p].reshape(k1p, HY_N2, GROUP_W).astype(BF16)


def _hy_conv3_wide(x, w_ref, b_ref, j, h1):
    c = GROUP_W
    wl = HY_N2 * c
    rowi = lax.broadcasted_iota(jnp.int32, (h1, c), 0)
    tail = x[:, wl - c:wl]
    head = x[:, 0:c]
    prev_tail = jnp.where(rowi == 0, 0.0, pltpu.roll(tail, 1, axis=0))
    next_head = jnp.where(rowi == h1 - 1, 0.0, pltpu.roll(head, h1 - 1, axis=0))
    xm = jnp.concatenate([prev_tail, x[:, 0:wl - c]], axis=1)
    xp = jnp.concatenate([x[:, c:wl], next_head], axis=1)
    return (xm * w_ref[3 * j:3 * j + 1, :] + x * w_ref[3 * j + 1:3 * j + 2, :]
            + xp * w_ref[3 * j + 2:3 * j + 3, :] + b_ref[j:j + 1, :])


def _hy_front_kernel(v_ref, x1_ref, x2_ref, w_ref, b_ref, m1_ref,
                     z_ref, x1c_ref, x2c_ref, a_ref, *, h1, k1p):
    z = _hy_conv3_wide(_to_wide(v_ref[...], h1), w_ref, b_ref, 0, h1).astype(BF16)
    z_ref[0] = z
    x1c_ref[0] = _hy_conv3_wide(_to_wide(x1_ref[...], h1), w_ref, b_ref, 1, h1).astype(BF16)
    x2c_ref[0] = _hy_conv3_wide(_to_wide(x2_ref[...], h1), w_ref, b_ref, 2, h1).astype(BF16)
    _hy_stage1(z, m1_ref, a_ref, k1p)


def _hy_front(cv, cx1, cx2, w_wide, b_wide, m1, B, L):
    h1, _, k1p, _ = _hy_dims(L)
    wl = HY_N2 * GROUP_W
    nat = pl.BlockSpec((L, GROUP_W), lambda b: (b, 0))
    seq = pl.BlockSpec((1, h1, wl), lambda b: (b, 0, 0))
    wide = jax.ShapeDtypeStruct((B, h1, wl), BF16)
    slab = (1, 2, k1p, HY_N2, GROUP_W)
    return pl.pallas_call(
        functools.partial(_hy_front_kernel, h1=h1, k1p=k1p),
        grid=(B,),
        in_specs=[nat, nat, nat,
                  pl.BlockSpec((9, wl), lambda b: (0, 0)),
                  pl.BlockSpec((3, wl), lambda b: (0, 0)),
                  pl.BlockSpec((2 * k1p, h1), lambda b: (0, 0))],
        out_specs=(seq, seq, seq, pl.BlockSpec(slab, lambda b: (b, 0, 0, 0, 0))),
        out_shape=(wide, wide, wide,
                   jax.ShapeDtypeStruct((B, 2, k1p, HY_N2, GROUP_W), BF16)),
        compiler_params=_cparams(("parallel",)),
        name="hy_front",
    )(cv, cx1, cx2, w_wide, b_wide, m1)


def _hy_stage1_kernel(z_ref, m1_ref, a_ref, *, h1, k1p):
    _hy_stage1(_to_wide(z_ref[0], h1).astype(BF16), m1_ref, a_ref, k1p)


def _hy_stage1_call(z_nat, m1, L):
    h1, _, k1p, _ = _hy_dims(L)
    nb = z_nat.shape[0]
    return pl.pallas_call(
        functools.partial(_hy_stage1_kernel, h1=h1, k1p=k1p),
        grid=(nb,),
        in_specs=[pl.BlockSpec((1, L, GROUP_W), lambda b: (b, 0, 0)),
                  pl.BlockSpec((2 * k1p, h1), lambda b: (0, 0))],
        out_specs=pl.BlockSpec((1, 2, k1p, HY_N2, GROUP_W), lambda b: (b, 0, 0, 0, 0)),
        out_shape=jax.ShapeDtypeStruct((nb, 2, k1p, HY_N2, GROUP_W), BF16),
        compiler_params=_cparams(("parallel",)),
        name="hy_stage1",
    )(z_nat, m1)


def _hy_spectrum_kernel(af_ref, ab_ref, f3_ref, g_ref, *, kg, k1h):
    grp = pl.program_id(0)

    @pl.when(grp * kg < k1h)
    def _():
        for kk in range(kg):
            xs = []
            for a_ref in (af_ref, ab_ref):
                a2 = jnp.concatenate([a_ref[0, 0, kk], a_ref[0, 1, kk]], axis=0)
                xs.append(_dot(f3_ref[kk], a2))
            g_ref[0, 0, kk] = xs[0][0:HY_N2] + xs[1][0:HY_N2]
            g_ref[0, 1, kk] = xs[0][HY_N2:2 * HY_N2] - xs[1][HY_N2:2 * HY_N2]

    @pl.when(grp * kg >= k1h)
    def _():
        g_ref[...] = jnp.zeros(g_ref.shape, F32)


def _hy_spectrum(a_filt, f3, L):
    _, k1h, k1p, kg = _hy_dims(L)
    a5 = a_filt
    blk = (1, 2, kg, HY_N2, GROUP_W)
    return pl.pallas_call(
        functools.partial(_hy_spectrum_kernel, kg=kg, k1h=k1h),
        grid=(k1p // kg, HYENA_ORDER),
        in_specs=[pl.BlockSpec(blk, lambda g, o: (2 * o, 0, g, 0, 0)),
                  pl.BlockSpec(blk, lambda g, o: (2 * o + 1, 0, g, 0, 0)),
                  pl.BlockSpec((kg, 2 * HY_N2, 2 * HY_N2), lambda g, o: (g, 0, 0))],
        out_specs=pl.BlockSpec(blk, lambda g, o: (o, 0, g, 0, 0)),
        out_shape=jax.ShapeDtypeStruct((HYENA_ORDER, 2, k1p, HY_N2, GROUP_W), F32),
        compiler_params=_cparams(("parallel", "parallel")),
        name="hy_spectrum",
    )(a5, a5, f3)


def _hy_mid_kernel(a_ref, g_ref, f3_ref, f3i_ref, b_ref, *, kg, k1h):
    grp = pl.program_id(0)

    @pl.when(grp * kg < k1h)
    def _():
        for kk in range(kg):
            a2 = jnp.concatenate([a_ref[0, 0, kk], a_ref[0, 1, kk]], axis=0)
            x = _dot(f3_ref[kk], a2)
            xr, xi = x[0:HY_N2], x[HY_N2:2 * HY_N2]
            gr, gi = g_ref[0, 0, kk], g_ref[0, 1, kk]
            y2 = jnp.concatenate([xr * gr - xi * gi, xr * gi + xi * gr], axis=0).astype(BF16)
            bm = _dot(f3i_ref[kk], y2)
            b_ref[0, 0, kk] = bm[0:HY_N2].astype(BF16)
            b_ref[0, 1, kk] = bm[HY_N2:2 * HY_N2].astype(BF16)

    @pl.when(grp * kg >= k1h)
    def _():
        b_ref[...] = jnp.zeros(b_ref.shape, BF16)


def _hy_mid(a, g_spec, order, f3, f3i, B, L):
    _, k1h, k1p, kg = _hy_dims(L)
    blk = (1, 2, kg, HY_N2, GROUP_W)
    tab = pl.BlockSpec((kg, 2 * HY_N2, 2 * HY_N2), lambda g, b: (g, 0, 0))
    return pl.pallas_call(
        functools.partial(_hy_mid_kernel, kg=kg, k1h=k1h),
        grid=(k1p // kg, B),
        in_specs=[pl.BlockSpec(blk, lambda g, b: (b, 0, g, 0, 0)),
                  pl.BlockSpec(blk, lambda g, b: (order, 0, g, 0, 0)),
                  tab, tab],
        out_specs=pl.BlockSpec(blk, lambda g, b: (b, 0, g, 0, 0)),
        out_shape=jax.ShapeDtypeStruct((B, 2, k1p, HY_N2, GROUP_W), BF16),
        compiler_params=_cparams(("parallel", "parallel")),
        name="hy_mid",
    )(a, g_spec, f3, f3i)


def _hy_back_kernel(b_ref, z_ref, x_ref, bias_ref, m4r_ref, m4i_ref, *rest, k1p, last):
    wl = HY_N2 * GROUP_W
    br = b_ref[0, 0].astype(F32).reshape(k1p, wl).astype(BF16)
    bi = b_ref[0, 1].astype(F32).reshape(k1p, wl).astype(BF16)
    y = _dot(m4r_ref[...], br) + _dot(m4i_ref[...], bi)
    z_new = x_ref[0].astype(F32) * (y + z_ref[0].astype(F32) * bias_ref[...])
    zb = z_new.astype(BF16)
    if last:
        (o_ref,) = rest
        o_ref[...] = z_new.reshape(o_ref.shape).astype(BF16)
    else:
        m1_ref, o_ref, a_ref = rest
        o_ref[0] = zb
        _hy_stage1(zb, m1_ref, a_ref, k1p)


def _hy_back(b, z, xg, bias_wide, m4r, m4i, m1, B, L, last):
    h1, _, k1p, _ = _hy_dims(L)
    wl = HY_N2 * GROUP_W
    seq = pl.BlockSpec((1, h1, wl), lambda i: (i, 0, 0))
    slab = pl.BlockSpec((1, 2, k1p, HY_N2, GROUP_W), lambda i: (i, 0, 0, 0, 0))
    const = lambda i: (0, 0)
    in_specs = [slab, seq, seq, pl.BlockSpec((1, wl), const),
                pl.BlockSpec((h1, k1p), const), pl.BlockSpec((h1, k1p), const)]
    args = [b, z, xg, bias_wide, m4r, m4i]
    if last:
        out_specs = pl.BlockSpec((L, GROUP_W), lambda i: (i, 0))
        out_shape = jax.ShapeDtypeStruct((B * L, GROUP_W), BF16)
    else:
        in_specs.append(pl.BlockSpec((2 * k1p, h1), const))
        args.append(m1)
        out_specs = (seq, slab)
        out_shape = (jax.ShapeDtypeStruct((B, h1, wl), BF16),
                     jax.ShapeDtypeStruct((B, 2, k1p, HY_N2, GROUP_W), BF16))
    return pl.pallas_call(
        functools.partial(_hy_back_kernel, k1p=k1p, last=last),
        grid=(B,),
        in_specs=in_specs,
        out_specs=out_specs,
        out_shape=out_shape,
        compiler_params=_cparams(("parallel",)),
        name="hy_back_last" if last else "hy_back",
    )(*args)


def _hy_filter_kernel(f_ref, t_ref, w1_ref, b1_ref, w2_ref, b2_ref, w3_ref, fr_ref, ad_ref,
                      o_ref, *, tile):
    i = pl.program_id(0)
    freq = fr_ref[...]
    z = jnp.sin(freq * (_dot_f32(f_ref[...], w1_ref[...]) + b1_ref[...]))
    z = jnp.sin(freq * (_dot_f32(z, w2_ref[...]) + b2_ref[...]))
    decay = jnp.exp(-t_ref[...] * ad_ref[...])
    rowi = i * tile + lax.broadcasted_iota(jnp.int32, (tile, GROUP_W), 0)
    for j in range(2 * HYENA_ORDER):
        hj = _dot_f32(z, w3_ref[:, j * GROUP_W:(j + 1) * GROUP_W]) * decay
        if j % 2 == 1:
            hj = jnp.where(rowi == 0, 0.0, hj)
        o_ref[j] = hj.astype(BF16)


def _hy_filter(feats, tcol, w1, b1, w2, b2, w3, freq, absdelta, L):
    tile = math.gcd(L, ROW_TILE)
    const = lambda i: (0, 0)
    return pl.pallas_call(
        functools.partial(_hy_filter_kernel, tile=tile),
        grid=(L // tile,),
        in_specs=[pl.BlockSpec((tile, LANE), lambda i: (i, 0)),
                  pl.BlockSpec((tile, 1), lambda i: (i, 0)),
                  pl.BlockSpec((LANE, HYENA_HIDDEN), const),
                  pl.BlockSpec((1, HYENA_HIDDEN), const),
                  pl.BlockSpec((HYENA_HIDDEN, HYENA_HIDDEN), const),
                  pl.BlockSpec((1, HYENA_HIDDEN), const),
                  pl.BlockSpec((HYENA_HIDDEN, 2 * HYENA_ORDER * GROUP_W), const),
                  pl.BlockSpec((1, HYENA_HIDDEN), const),
                  pl.BlockSpec((1, GROUP_W), const)],
        out_specs=pl.BlockSpec((2 * HYENA_ORDER, tile, GROUP_W), lambda i: (0, i, 0)),
        out_shape=jax.ShapeDtypeStruct((2 * HYENA_ORDER, L, GROUP_W), BF16),
        compiler_params=_cparams(("parallel",)),
        name="hy_filter",
    )(feats, tcol, w1, b1, w2, b2, w3, freq, absdelta)


def _outproj_kernel(x_ref, oa_ref, ob_ref, oc_ref, od_ref, ox_ref,
                    ga_ref, gb_ref, gc_ref, gdo_ref, gdg_ref, gx_ref, w_ref, png_ref, out_ref):
    f = lambda r: r[...].astype(F32)
    branches = (
        f(oa_ref) * _silu(f(ga_ref)),
        f(ob_ref) * _silu(f(gb_ref)),
        f(oc_ref) * _silu(f(gc_ref)),
        f(od_ref) * _sigmoid(f(gdo_ref)) * _silu(f(gdg_ref)),
        f(ox_ref) * _silu(f(gx_ref)),
    )
    y = jnp.zeros(x_ref.shape, F32)
    for j, br in enumerate(branches):
        y = y + _dot(br.astype(BF16), w_ref[j * GROUP_W:(j + 1) * GROUP_W, :])
    ms = jnp.mean(y * y, axis=-1, keepdims=True)
    out_ref[...] = x_ref[...] + y * lax.rsqrt(ms + NORM_EPS) * png_ref[...]


def _outproj(x2d, outs, h_main, w_out, png, row0=0, n_rows=None):
    n_rows = x2d.shape[0] if n_rows is None else n_rows
    nt = n_rows // ROW_TILE
    t0 = row0 // ROW_TILE
    row = lambda i: (t0 + i, 0)
    blk = lambda j: pl.BlockSpec((ROW_TILE, GROUP_W), lambda i: (t0 + i, j))
    return pl.pallas_call(
        _outproj_kernel,
        grid=(nt,),
        in_specs=[pl.BlockSpec((ROW_TILE, D_MODEL), row)]
        + [pl.BlockSpec((ROW_TILE, GROUP_W), row)] * 5
        + [blk(BLK_AG), blk(BLK_BG), blk(BLK_CG), blk(BLK_DO), blk(BLK_DG), blk(BLK_XG)]
        + [pl.BlockSpec((5 * GROUP_W, D_MODEL), lambda i: (0, 0)),
           pl.BlockSpec((1, D_MODEL), lambda i: (0, 0))],
        out_specs=pl.BlockSpec((ROW_TILE, D_MODEL), lambda i: (i, 0)),
        out_shape=jax.ShapeDtypeStruct((n_rows, D_MODEL), F32),
        compiler_params=_cparams(("parallel",)),
        name="outproj",
    )(x2d, *outs, h_main, h_main, h_main, h_main, h_main, h_main, w_out, png)


def _rope_tables(L, group, rot_dim):
    half = rot_dim // 2
    inv = 1.0 / (ROPE_THETA ** (jnp.arange(0, rot_dim, 2, dtype=F32) / rot_dim))
    ang = jnp.arange(L, dtype=F32)[:, None] * inv[None, :]
    cos, sin = jnp.cos(ang), jnp.sin(ang)
    lane = np.arange(GROUP_W) % group
    in_rot = lane < rot_dim
    idx = lane % half
    c = jnp.where(in_rot[None, :], cos[:, idx], 1.0)
    s = jnp.where(in_rot[None, :], sin[:, idx], 0.0)
    p = np.zeros((GROUP_W, GROUP_W), np.float32)
    for j in range(GROUP_W):
        if lane[j] < half:
            p[j + half, j] = -1.0
        elif lane[j] < rot_dim:
            p[j - half, j] = 1.0
    return c, s, jnp.asarray(p, BF16)


def _hyena_features(L):
    t = jnp.linspace(0.0, 1.0, L, dtype=F32)[:, None]
    bands = jnp.linspace(1e-4, HYENA_BANDS - 1, HYENA_BANDS, dtype=F32)
    ang = (2.0 * math.pi / L) * jnp.arange(L, dtype=F32)[:, None] * bands[None, :]
    feats = jnp.concatenate([t, jnp.cos(ang), -jnp.sin(ang)], axis=-1)
    return jnp.pad(feats, ((0, 0), (0, LANE - HYENA_EMB))), t


def _relayout_w_in(w):
    g = GROUP_W
    off_c, off_d = 8 * g, 12 * g
    off_gate = off_d + 5 * g
    off_x = off_gate + N_MLSTM_GATES
    main = jnp.concatenate([w[:, 0:off_c], w[:, off_c + 3 * g:off_c + 4 * g],
                            w[:, off_d:off_gate], w[:, off_x:off_x + 2 * g]], axis=1)
    hy = w[:, off_c:off_c + 3 * g]
    gate_i, gate_f = _split_gates(w[:, off_gate:off_x])
    return jnp.concatenate([main, hy, gate_i, gate_f], axis=1).astype(BF16)


def _split_gates(t):
    nh = N_HEADS
    pad = [(0, 0)] * (t.ndim - 1) + [(0, LANE - 2 * nh)]
    gi = jnp.concatenate([t[..., 0:nh], t[..., 2 * nh:3 * nh]], axis=-1)
    gf = jnp.concatenate([t[..., nh:2 * nh], t[..., 3 * nh:4 * nh]], axis=-1)
    return jnp.pad(gi, pad), jnp.pad(gf, pad)


def _trunk(x, mem, splits, pre_norm_g, post_norm_g, w_in, w_out, diff_lambda, diff_subln_g,
           hy_conv_w, hy_conv_b, hy_ffn_w1, hy_ffn_b1, hy_ffn_w2, hy_ffn_b2, hy_ffn_w3,
           hy_freq, hy_bias, ml_conv_w, ml_conv_b, ml_gate_b, mem_norm_g, w_mem_kv):
    B, L, _ = x.shape
    M = mem.shape[1]
    depth = w_in.shape[0]
    g = GROUP_W
    h1, _, k1p, _ = _hy_dims(L)
    wl = HY_N2 * g

    ca, sa, pa = _rope_tables(L, HEAD_DIM, HEAD_DIM // ROPE_FRACTION)
    cb, sb, pb = _rope_tables(L, DIFF_QK_DIM, DIFF_QK_DIM // ROPE_FRACTION)
    rope_tabs = (ca, sa, cb, sb, pa, pb)
    dil_bias = jnp.asarray(_dil_bias_table())
    feats, tcol = _hyena_features(L)
    absdelta = jnp.abs(jnp.linspace(math.log(HYENA_TARGET) / HYENA_SLOW_DECAY,
                                    math.log(HYENA_TARGET) / HYENA_FAST_DECAY, g, dtype=F32))[None]
    m1, f3, f3i, m4r, m4i = _hy_tables(L)
    head_of = np.arange(g) // HEAD_DIM
    same_head = (head_of[:, None] == head_of[None, :]).astype(np.float32)
    hmean = jnp.asarray(same_head / HEAD_DIM, BF16)
    mlstm_consts = _mlstm_consts()

    x2d = x.reshape(B * L, D_MODEL)
    mem2d = mem.reshape(B * M, D_MODEL)
    for li in range(depth):
        lam_init = 0.8 - 0.6 * math.exp(-0.3 * li)
        w_all = _relayout_w_in(w_in[li])
        h_main, cv, cx1, cx2, gate_i, gate_f = _inproj(x2d, pre_norm_g[li][None], w_all,
                                                       rope_tabs, B, L)

        oa = _dilattn(h_main, dil_bias, B, L)
        ob = _diffattn(h_main, diff_lambda[li], jnp.tile(diff_subln_g[li], N_HEADS)[None],
                       hmean, lam_init, B, L)

        w1 = jnp.pad(hy_ffn_w1[li], ((0, LANE - HYENA_EMB), (0, 0)))
        hfilt = _hy_filter(feats, tcol, w1, hy_ffn_b1[li][None], hy_ffn_w2[li],
                           hy_ffn_b2[li][None], hy_ffn_w3[li], hy_freq[li][None], absdelta, L)
        a_filt = _hy_stage1_call(hfilt, m1, L)
        g_spec = _hy_spectrum(a_filt, f3, L)
        cw = jnp.tile(hy_conv_w[li].reshape(3, 3, g).transpose(1, 0, 2).reshape(9, g), (1, HY_N2))
        cbw = jnp.tile(hy_conv_b[li].reshape(3, g), (1, HY_N2))
        z, x1c, x2c, a = _hy_front(cv, cx1, cx2, cw, cbw, m1, B, L)
        bias_w = jnp.tile(hy_bias[li], (1, HY_N2))
        bsp = _hy_mid(a, g_spec, 0, f3, f3i, B, L)
        z, a = _hy_back(bsp, z, x1c, bias_w[0:1], m4r, m4i, m1, B, L, last=False)
        bsp = _hy_mid(a, g_spec, 1, f3, f3i, B, L)
        oc = _hy_back(bsp, z, x2c, bias_w[1:2], m4r, m4i, m1, B, L, last=True)

        bias_i, bias_f = _split_gates(ml_gate_b[li][None])
        od = _mlstm(h_main, gate_i, gate_f, ml_conv_w[li], ml_conv_b[li][None], bias_i, bias_f,
                    mlstm_consts, B, L)

        mkv = _memkv(mem2d, mem_norm_g[li][None], w_mem_kv[li].astype(BF16))
        ox = _memattn(h_main, mkv, B, L, M)

        branch_outs = (oa, ob, oc, od, ox)
        w_o, png = w_out[li].astype(BF16), post_norm_g[li][None]
        if li + 1 < depth:
            x2d = _outproj(x2d, branch_outs, h_main, w_o, png)
    ys, b0 = [], 0
    for nb in splits:
        y = _outproj(x2d, branch_outs, h_main, w_o, png, row0=b0 * L, n_rows=nb * L)
        ys.append(y.reshape(nb, L, D_MODEL))
        b0 += nb
    return tuple(ys)


def kernel(x_prompt, x_sample, mem_prompt, mem_sample, pre_norm_g, post_norm_g, w_in, w_out,
           diff_lambda, diff_subln_g, hy_conv_w, hy_conv_b, hy_ffn_w1, hy_ffn_b1, hy_ffn_w2,
           hy_ffn_b2, hy_ffn_w3, hy_freq, hy_bias, ml_conv_w, ml_conv_b, ml_gate_b,
           mem_norm_g, w_mem_kv):
    x = jnp.concatenate([x_prompt, x_sample], axis=0)
    mem = jnp.concatenate([mem_prompt, mem_sample], axis=0)
    return _trunk(x, mem, (x_prompt.shape[0], x_sample.shape[0]), pre_norm_g, post_norm_g, w_in,
                  w_out, diff_lambda, diff_subln_g, hy_conv_w, hy_conv_b, hy_ffn_w1, hy_ffn_b1,
                  hy_ffn_w2, hy_ffn_b2, hy_ffn_w3, hy_freq, hy_bias, ml_conv_w, ml_conv_b,
                  ml_gate_b, mem_norm_g, w_mem_kv)
```

```python
import functools
import math

import numpy as np
import jax
import jax.numpy as jnp
from jax import lax
from jax.experimental import pallas as pl
from jax.experimental.pallas import tpu as pltpu

F32 = jnp.float32
BF16 = jnp.bfloat16

D_MODEL = 1024
HEAD_DIM = 64
GROUP_W = 256
N_HEADS = GROUP_W // HEAD_DIM
NORM_EPS = 1e-6
NEG_INF = -1e30
ROPE_THETA = 500000.0
ROPE_FRACTION = 4
DIL_PATTERNS = ((128, 1), (512, 4), (2048, 16))
DIFF_QK_DIM = HEAD_DIM // 2
DIFF_SUBLN_EPS = 1e-5
HYENA_ORDER = 2
HYENA_BANDS = 16
HYENA_EMB = 1 + 2 * HYENA_BANDS
HYENA_HIDDEN = 64
HYENA_FAST_DECAY = 0.3
HYENA_SLOW_DECAY = 1.5
HYENA_TARGET = 1e-2
N_MLSTM_GATES = 4 * N_HEADS

(BLK_AQ, BLK_AK, BLK_AV, BLK_AG, BLK_BQ, BLK_BK, BLK_BV, BLK_BG, BLK_CG,
 BLK_DQ, BLK_DK, BLK_DV, BLK_DO, BLK_DG, BLK_XQ, BLK_XG) = range(16)
N_MAIN_BLK = 16
MAIN_W = N_MAIN_BLK * GROUP_W
W_ALL = MAIN_W + 4 * GROUP_W

VMEM_LIMIT_BYTES = 56 * 1024 * 1024
LANE = 128
BF16_SUBLANES = 16

ROW_TILE = 512
DIL_Q_TILE = 256
DIL_PAD = 1024
DIL_K_TILE = 256
DIL_VT_ROWS = HEAD_DIM + BF16_SUBLANES
DIFF_Q_TILE = 256
DIFF_K_CHUNK = 512
DIFF_VT_ROWS = HEAD_DIM + BF16_SUBLANES
LOG2E = 1.4426950408889634
MEM_Q_TILE = 512
MLSTM_T = 128
CONV_HALO = 16
HY_N2 = 128


def _cparams(sem, vmem=VMEM_LIMIT_BYTES):
    return pltpu.CompilerParams(dimension_semantics=sem, vmem_limit_bytes=vmem)


def _sigmoid(v):
    return 1.0 / (1.0 + jnp.exp(-v))


def _silu(v):
    return v * _sigmoid(v)


def _dot(a, b):
    return jnp.dot(a, b, preferred_element_type=F32)


def _dot_nt(a, b):
    return lax.dot_general(a, b, (((1,), (1,)), ((), ())), preferred_element_type=F32)


def _dot_tn(a, b):
    return lax.dot_general(a, b, (((0,), (0,)), ((), ())), preferred_element_type=F32)


def _head_mask(h, width, dtype):
    lane = lax.broadcasted_iota(jnp.int32, (1, GROUP_W), 1)
    return ((lane // width) == h).astype(dtype)


def _inproj_kernel(x_ref, g_ref, w_ref, ca_ref, sa_ref, cb_ref, sb_ref, pa_ref, pb_ref,
                   h_ref, cv_ref, cx1_ref, cx2_ref, gi_ref, gf_ref):
    x = x_ref[...]
    ms = jnp.mean(x * x, axis=-1, keepdims=True)
    xn = (x * lax.rsqrt(ms + NORM_EPS) * g_ref[...]).astype(BF16)

    def proj(j, width=GROUP_W):
        return _dot(xn, w_ref[:, j * GROUP_W:j * GROUP_W + width])

    def rope(acc, c_ref, s_ref, p_ref):
        partner = _dot(acc.astype(BF16), p_ref[...])
        return acc * c_ref[...] + partner * s_ref[...]

    def finish(j, acc):
        if j in (BLK_AQ, BLK_AK):
            acc = rope(acc, ca_ref, sa_ref, pa_ref)
        if j in (BLK_BQ, BLK_BK):
            acc = rope(acc, cb_ref, sb_ref, pb_ref)
        if j in (BLK_AQ, BLK_XQ):
            acc = acc * (LOG2E / math.sqrt(HEAD_DIM))
        if j == BLK_BQ:
            acc = acc * (LOG2E / math.sqrt(DIFF_QK_DIM))
        if j < N_MAIN_BLK:
            h_ref[:, j * GROUP_W:(j + 1) * GROUP_W] = acc.astype(BF16)
        elif j < N_MAIN_BLK + 3:
            (cv_ref, cx1_ref, cx2_ref)[j - N_MAIN_BLK][...] = acc.astype(BF16)
        else:
            gi_ref[...] = acc[:, 0:LANE]
            gf_ref[...] = acc[:, LANE:2 * LANE]

    for j in range(0, W_ALL // GROUP_W, 2):
        acc2 = proj(j, 2 * GROUP_W)
        finish(j, acc2[:, 0:GROUP_W])
        finish(j + 1, acc2[:, GROUP_W:2 * GROUP_W])


def _inproj(x2d, g, w_all, rope_tabs, B, L):
    ca, sa, cb, sb, pa, pb = rope_tabs
    nt = L // ROW_TILE
    n_tok = B * L
    row = lambda i, b: (b * nt + i, 0)
    tab = lambda i, b: (i, 0)
    const = lambda i, b: (0, 0)
    out_shapes = (
        jax.ShapeDtypeStruct((n_tok, MAIN_W), BF16),
        jax.ShapeDtypeStruct((n_tok, GROUP_W), BF16),
        jax.ShapeDtypeStruct((n_tok, GROUP_W), BF16),
        jax.ShapeDtypeStruct((n_tok, GROUP_W), BF16),
        jax.ShapeDtypeStruct((n_tok, LANE), F32),
        jax.ShapeDtypeStruct((n_tok, LANE), F32),
    )
    return pl.pallas_call(
        _inproj_kernel,
        grid=(nt, B),
        in_specs=[
            pl.BlockSpec((ROW_TILE, D_MODEL), row),
            pl.BlockSpec((1, D_MODEL), const),
            pl.BlockSpec((D_MODEL, W_ALL), const),
            pl.BlockSpec((ROW_TILE, GROUP_W), tab),
            pl.BlockSpec((ROW_TILE, GROUP_W), tab),
            pl.BlockSpec((ROW_TILE, GROUP_W), tab),
            pl.BlockSpec((ROW_TILE, GROUP_W), tab),
            pl.BlockSpec((GROUP_W, GROUP_W), const),
            pl.BlockSpec((GROUP_W, GROUP_W), const),
        ],
        out_specs=(
            pl.BlockSpec((ROW_TILE, MAIN_W), row),
            pl.BlockSpec((ROW_TILE, GROUP_W), row),
            pl.BlockSpec((ROW_TILE, GROUP_W), row),
            pl.BlockSpec((ROW_TILE, GROUP_W), row),
            pl.BlockSpec((ROW_TILE, LANE), row),
            pl.BlockSpec((ROW_TILE, LANE), row),
        ),
        out_shape=out_shapes,
        compiler_params=_cparams(("parallel", "parallel")),
        name="inproj",
    )(x2d, g, w_all, ca, sa, cb, sb, pa, pb)


def _memkv_kernel(m_ref, g_ref, w_ref, o_ref):
    x = m_ref[...]
    ms = jnp.mean(x * x, axis=-1, keepdims=True)
    xn = (x * lax.rsqrt(ms + NORM_EPS) * g_ref[...]).astype(BF16)
    o_ref[...] = _dot(xn, w_ref[...]).astype(BF16)


def _memkv(mem2d, g, w):
    rows = mem2d.shape[0]
    tile = math.gcd(rows, ROW_TILE)
    return pl.pallas_call(
        _memkv_kernel,
        grid=(rows // tile,),
        in_specs=[
            pl.BlockSpec((tile, D_MODEL), lambda i: (i, 0)),
            pl.BlockSpec((1, D_MODEL), lambda i: (0, 0)),
            pl.BlockSpec((D_MODEL, 2 * GROUP_W), lambda i: (0, 0)),
        ],
        out_specs=pl.BlockSpec((tile, 2 * GROUP_W), lambda i: (i, 0)),
        out_shape=jax.ShapeDtypeStruct((rows, 2 * GROUP_W), BF16),
        compiler_params=_cparams(("parallel",)),
        name="memkv",
    )(mem2d, g, w)


def _memattn_kernel(q_ref, mk_ref, mv_ref, o_ref):
    q = q_ref[...]
    mk = mk_ref[...]
    mv = mv_ref[...]
    acc = jnp.zeros(q.shape, F32)
    for h in range(N_HEADS):
        hm = _head_mask(h, HEAD_DIM, BF16)
        s = _dot_nt(q * hm, mk)
        m = jnp.max(s, axis=-1, keepdims=True)
        p = jnp.exp2(s - m)
        l = jnp.sum(p, axis=-1, keepdims=True)
        acc = acc + _dot(p.astype(BF16), mv * hm) * (1.0 / l)
    o_ref[...] = acc.astype(BF16)


def _memattn(h_main, mkv, B, L, M):
    nq = L // MEM_Q_TILE
    return pl.pallas_call(
        _memattn_kernel,
        grid=(B, nq),
        in_specs=[
            pl.BlockSpec((MEM_Q_TILE, GROUP_W), lambda b, i: (b * nq + i, BLK_XQ)),
            pl.BlockSpec((M, GROUP_W), lambda b, i: (b, 0)),
            pl.BlockSpec((M, GROUP_W), lambda b, i: (b, 1)),
        ],
        out_specs=pl.BlockSpec((MEM_Q_TILE, GROUP_W), lambda b, i: (b * nq + i, 0)),
        out_shape=jax.ShapeDtypeStruct((B * L, GROUP_W), BF16),
        compiler_params=_cparams(("parallel", "parallel")),
        name="memattn",
    )(h_main, mkv, mkv)


def _dil_bias_table():
    w = DIL_Q_TILE + 2 * DIL_PAD
    d = np.arange(DIL_Q_TILE)[None, :] - np.arange(w)[:, None] + DIL_PAD
    count = np.zeros(d.shape, np.float64)
    for win, dil in DIL_PATTERNS:
        reach = (win // (2 * dil)) * dil
        count += (d % dil == 0) & (np.abs(d) <= reach)
    return np.where(count > 0, np.log2(np.maximum(count, 1.0)), NEG_INF).astype(np.float32)


def _dilattn_kernel(q_ref, k_ref, v_ref, bias_ref, o_ref, kpad, vt_ref, *, L):
    i = pl.program_id(1)
    tq, hd, kt = DIL_Q_TILE, HEAD_DIM, DIL_K_TILE
    w = tq + 2 * DIL_PAD
    n_pad = DIL_PAD // kt

    @pl.when(i == 0)
    def _():
        zeros = jnp.zeros((DIL_PAD, GROUP_W), BF16)
        kpad[0:DIL_PAD, :] = zeros
        kpad[DIL_PAD + L:DIL_PAD + L + DIL_PAD, :] = zeros
        kpad[DIL_PAD:DIL_PAD + L, :] = k_ref[...]
        tail = (lax.broadcasted_iota(jnp.int32, (DIL_VT_ROWS - hd, kt), 0) == 0).astype(BF16)
        for c in range(L // kt + 2 * n_pad):
            inside = n_pad <= c < n_pad + L // kt
            if inside:
                lo = (c - n_pad) * kt
                vt = v_ref[lo:lo + kt, :].astype(F32).T.astype(BF16)
            for h in range(N_HEADS):
                vt_ref[c, h, 0:hd, :] = (vt[h * hd:(h + 1) * hd] if inside
                                         else jnp.zeros((hd, kt), BF16))
                vt_ref[c, h, hd:DIL_VT_ROWS, :] = tail

    q0 = pl.multiple_of(i * tq, tq)
    qt = q_ref[...].astype(F32).T
    feat_head = lax.broadcasted_iota(jnp.int32, (GROUP_W, 1), 0) // hd
    kw = kpad[pl.ds(q0, w), :]
    jpos = q0 - DIL_PAD + lax.broadcasted_iota(jnp.int32, (w, tq), 0)
    bias = bias_ref[...] + jnp.where(jpos >= 0, jnp.where(jpos < L, 0.0, NEG_INF), NEG_INF)
    bias2 = jnp.concatenate([bias, bias], axis=1)
    heads = []
    for pair in range(N_HEADS // 2):
        qt2 = jnp.concatenate([jnp.where(feat_head == 2 * pair + c, qt, 0.0) for c in range(2)],
                              axis=1).astype(BF16)
        s = _dot(kw, qt2) + bias2
        m = jnp.max(s, axis=0, keepdims=True)
        p = jnp.exp2((s - m).astype(BF16))
        for c in range(2):
            h = 2 * pair + c
            o = jnp.zeros((DIL_VT_ROWS, tq), F32)
            for t in range(w // kt):
                o = o + _dot(vt_ref[i * (tq // kt) + t, h],
                             p[t * kt:(t + 1) * kt, c * tq:(c + 1) * tq])
            heads.append(o[0:hd] * (1.0 / o[hd:hd + 1]))
    o_ref[...] = jnp.concatenate(heads, axis=0).T.astype(BF16)


def _dilattn(h_main, bias, B, L):
    nq = L // DIL_Q_TILE
    w = DIL_Q_TILE + 2 * DIL_PAD
    return pl.pallas_call(
        functools.partial(_dilattn_kernel, L=L),
        grid=(B, nq),
        in_specs=[
            pl.BlockSpec((DIL_Q_TILE, GROUP_W), lambda b, i: (b * nq + i, BLK_AQ)),
            pl.BlockSpec((L, GROUP_W), lambda b, i: (b, BLK_AK)),
            pl.BlockSpec((L, GROUP_W), lambda b, i: (b, BLK_AV)),
            pl.BlockSpec((w, DIL_Q_TILE), lambda b, i: (0, 0)),
        ],
        out_specs=pl.BlockSpec((DIL_Q_TILE, GROUP_W), lambda b, i: (b * nq + i, 0)),
        out_shape=jax.ShapeDtypeStruct((B * L, GROUP_W), BF16),
        scratch_shapes=[pltpu.VMEM((L + 2 * DIL_PAD, GROUP_W), BF16),
                        pltpu.VMEM(((L + 2 * DIL_PAD) // DIL_K_TILE, N_HEADS, DIL_VT_ROWS,
                                    DIL_K_TILE), BF16)],
        compiler_params=_cparams(("parallel", "arbitrary")),
        name="dilattn",
    )(h_main, h_main, h_main, bias)


def _split3(x):
    hi = x.astype(BF16)
    r1 = x - hi.astype(F32)
    mid = r1.astype(BF16)
    lo = (r1 - mid.astype(F32)).astype(BF16)
    return hi, mid, lo


def _split_dot(x, mat):
    return sum(_dot(t, mat) for t in _split3(x))


def _tri_dot(mat, x):
    return sum(_dot(mat, t) for t in _split3(x))


def _dot_f32(a, b):
    ah, am, al = _split3(a)
    bh, bm, bl = _split3(b)
    return (_dot(ah, bh) + (_dot(ah, bm) + _dot(am, bh))
            + (_dot(ah, bl) + _dot(am, bm) + _dot(al, bh)))


def _diffattn_kernel(q_ref, qn_ref, k_ref, v_ref, lam_ref, g_ref, hmean_ref, o_ref,
                     vt_ref, sa_ref, sb_ref, m_ref, oh_ref, *, lam_init, L):
    hd = HEAD_DIM
    first_tile = pl.program_id(1) == 0

    @pl.when(first_tile)
    def _():
        tail = (lax.broadcasted_iota(jnp.int32, (DIFF_VT_ROWS - hd, DIFF_K_CHUNK), 0) == 0)
        for c in range(L // DIFF_K_CHUNK):
            lo, hi = c * DIFF_K_CHUNK, (c + 1) * DIFF_K_CHUNK
            vt = v_ref[lo:hi, :].astype(F32).T.astype(BF16)
            for h in range(N_HEADS):
                vt_ref[h, 0:hd, lo:hi] = vt[h * hd:(h + 1) * hd]
                vt_ref[h, hd:DIFF_VT_ROWS, lo:hi] = tail.astype(BF16)

    lp = lam_ref[...]
    lam = (jnp.exp(jnp.sum(lp[0:1] * lp[1:2], axis=-1, keepdims=True))
           - jnp.exp(jnp.sum(lp[2:3] * lp[3:4], axis=-1, keepdims=True)) + lam_init)
    qt = q_ref[...].astype(F32).T
    qt_next = qn_ref[...].astype(F32).T
    feat_group = lax.broadcasted_iota(jnp.int32, (GROUP_W, 1), 0) // DIFF_QK_DIM
    tq = qt.shape[1]
    chunks = [(c * DIFF_K_CHUNK, (c + 1) * DIFF_K_CHUNK) for c in range(L // DIFF_K_CHUNK)]

    def masked_qt(h):
        wraps = h == N_HEADS
        src = jnp.where(wraps, qt_next, qt)
        hh = jnp.where(wraps, 0, h)
        return jnp.concatenate([jnp.where(feat_group == 2 * hh + c, src, 0.0) for c in range(2)],
                               axis=1).astype(BF16)

    def stage(h_next, nxt, h_cur, cur):
        qt2 = masked_qt(h_next)
        m_run = None
        if h_cur is not None:
            m_fin = m_ref[cur, len(chunks) - 1, 0:1, :]
            o = jnp.zeros((DIFF_VT_ROWS, 2 * tq), F32)
        for c, (lo, hi) in enumerate(chunks):
            s = _dot(k_ref[lo:hi, :], qt2)
            cm = jnp.max(s, axis=0, keepdims=True)
            m_run = cm if m_run is None else jnp.maximum(m_run, cm)
            s_refs[nxt][lo:hi, :] = (s - m_run).astype(BF16)
            m_ref[nxt, c] = jnp.broadcast_to(m_run, m_ref.shape[2:])
            if h_cur is not None:
                oc = _dot(vt_ref[h_cur, :, lo:hi], jnp.exp2(s_refs[cur][lo:hi, :]))
                o = o + oc * jnp.exp2(m_ref[cur, c, 0:1, :] - m_fin)
        if h_cur is not None:
            on = o[0:hd] * (1.0 / o[hd:hd + 1])
            oh_ref[h_cur] = on[:, 0:tq] - on[:, tq:2 * tq] * lam

    s_refs = (sa_ref, sb_ref)

    @pl.when(first_tile)
    def _():
        stage(0, 0, None, None)

    def body(j, carry):
        stage(2 * j + 1, 1, 2 * j, 0)
        stage(2 * j + 2, 0, 2 * j + 1, 1)
        return carry

    lax.fori_loop(0, N_HEADS // 2, body, 0)
    acc = jnp.concatenate([oh_ref[h] for h in range(N_HEADS)], axis=0).T
    ms = _split_dot(acc * acc, hmean_ref[...])
    y = acc * lax.rsqrt(ms + DIFF_SUBLN_EPS) * g_ref[...] * (1.0 - lam_init)
    o_ref[...] = y.astype(BF16)


def _diffattn(h_main, lam_p, subln_g, hmean, lam_init, B, L):
    nq = L // DIFF_Q_TILE
    return pl.pallas_call(
        functools.partial(_diffattn_kernel, lam_init=lam_init, L=L),
        grid=(B, nq),
        in_specs=[
            pl.BlockSpec((DIFF_Q_TILE, GROUP_W), lambda b, i: (b * nq + i, BLK_BQ)),
            pl.BlockSpec((DIFF_Q_TILE, GROUP_W),
                         lambda b, i: (b * nq + jnp.minimum(i + 1, nq - 1), BLK_BQ)),
            pl.BlockSpec((L, GROUP_W), lambda b, i: (b, BLK_BK)),
            pl.BlockSpec((L, GROUP_W), lambda b, i: (b, BLK_BV)),
            pl.BlockSpec((4, DIFF_QK_DIM), lambda b, i: (0, 0)),
            pl.BlockSpec((1, GROUP_W), lambda b, i: (0, 0)),
            pl.BlockSpec((GROUP_W, GROUP_W), lambda b, i: (0, 0)),
        ],
        out_specs=pl.BlockSpec((DIFF_Q_TILE, GROUP_W), lambda b, i: (b * nq + i, 0)),
        out_shape=jax.ShapeDtypeStruct((B * L, GROUP_W), BF16),
        scratch_shapes=[pltpu.VMEM((N_HEADS, DIFF_VT_ROWS, L), BF16),
                        pltpu.VMEM((L, 2 * DIFF_Q_TILE), BF16),
                        pltpu.VMEM((L, 2 * DIFF_Q_TILE), BF16),
                        pltpu.VMEM((2, L // DIFF_K_CHUNK, 8, 2 * DIFF_Q_TILE), F32),
                        pltpu.VMEM((N_HEADS, HEAD_DIM, DIFF_Q_TILE), F32)],
        compiler_params=_cparams(("parallel", "arbitrary")),
        name="diffattn",
    )(h_main, h_main, h_main, h_main, lam_p, subln_g, hmean)


def _log_sigmoid(v):
    return jnp.minimum(v, 0.0) - jnp.log(1.0 + jnp.exp(-jnp.abs(v)))


def _mlstm_kernel(q_ref, k_ref, v_ref, gi_ref, gf_ref, cw_ref, cb_ref, bi_ref, bf_ref,
                  ltri_ref, utri_ref, ecol_ref, elane_ref, kmask_ref, vmask_ref, ones_ref,
                  bd_ref, hsum_ref, o_ref,
                  qpad, kpad, qs, ks, hfw, hbw, cst, nst, mst, *, L):
    T = MLSTM_T
    nc = L // T
    halo = CONV_HALO
    nh = N_HEADS

    zpad = jnp.zeros((halo, GROUP_W), BF16)
    for pad, src in ((qpad, q_ref), (kpad, k_ref)):
        pad[0:halo, :] = zpad
        pad[halo + L:halo + L + halo, :] = zpad
        pad[halo:halo + L, :] = src[...]

    def conv_body(c, carry):
        r0 = pl.multiple_of(c * T, T)
        for idx, (pad, dst, scale) in enumerate(((qpad, qs, 1.0),
                                                 (kpad, ks, 1.0 / math.sqrt(HEAD_DIM)))):
            xw = pad[pl.ds(r0, T + 2 * halo), :].astype(F32)
            xm = pltpu.roll(xw, 1, axis=0)[halo:halo + T]
            xp = pltpu.roll(xw, T + 2 * halo - 1, axis=0)[halo:halo + T]
            xc = xw[halo:halo + T]
            lo, hi = idx * GROUP_W, (idx + 1) * GROUP_W
            y = (xm * cw_ref[0:1, lo:hi] + xc * cw_ref[1:2, lo:hi] + xp * cw_ref[2:3, lo:hi]
                 + cb_ref[0:1, lo:hi])
            dst[pl.ds(r0, T), :] = (_silu(y) * scale).astype(BF16)
        return carry

    lax.fori_loop(0, nc, conv_body, 0)

    cst[...] = jnp.zeros(cst.shape, F32)
    nst[...] = jnp.zeros(nst.shape, F32)
    mst[...] = jnp.zeros(mst.shape, F32)

    is_fw = lax.broadcasted_iota(jnp.int32, (1, LANE), 1) < nh
    rowi = lax.broadcasted_iota(jnp.int32, (T, LANE), 0)
    row4 = lax.broadcasted_iota(jnp.int32, (T, nh * T), 0)
    col4 = lax.broadcasted_iota(jnp.int32, (T, nh * T), 1) % T
    causal = (col4 <= row4, col4 >= row4)

    def body(c, carry):
        rows = (pl.multiple_of(c * T, T), pl.multiple_of((nc - 1 - c) * T, T))
        gate_i = jnp.where(is_fw, gi_ref[pl.ds(rows[0], T), :], gi_ref[pl.ds(rows[1], T), :])
        gate_f = jnp.where(is_fw, gf_ref[pl.ds(rows[0], T), :], gf_ref[pl.ds(rows[1], T), :])
        gate_i = gate_i + bi_ref[...]
        parts = _split3(_log_sigmoid(gate_f + bf_ref[...]))
        pre = sum(_dot(ltri_ref[...], t) for t in parts)
        suf = sum(_dot(utri_ref[...], t) for t in parts)
        cum = jnp.where(is_fw, pre, suf)
        b_end = jnp.where(is_fw, pre[T - 1:T, :], suf[0:1, :])
        key_w = gate_i - cum

        pmax, smax = key_w, key_w
        sh = 1
        while sh < T:
            pmax = jnp.maximum(pmax, jnp.where(rowi >= sh, pltpu.roll(pmax, sh, axis=0), NEG_INF))
            smax = jnp.maximum(smax, jnp.where(rowi < T - sh, pltpu.roll(smax, T - sh, axis=0),
                                               NEG_INF))
            sh *= 2
        m_prev = mst[0:1, :]
        inter = cum + m_prev
        m_t = jnp.maximum(inter, cum + jnp.where(is_fw, pmax, smax))
        u = cum - m_t
        a = b_end + key_w
        m_new = jnp.maximum(b_end + m_prev, jnp.max(a, axis=0, keepdims=True))
        mst[0:1, :] = m_new
        stack = jnp.concatenate(
            [jnp.exp(inter - m_t), jnp.exp(-m_t), jnp.exp(a - m_new),
             jnp.broadcast_to(jnp.exp(b_end + m_prev - m_new), (8, LANE))], axis=0)
        st_hi, st_mid, _ = _split3(stack)
        key_w_t = key_w.T

        for d in range(2):
            r0 = rows[d]
            qc = qs[pl.ds(r0, T), :]
            kc = ks[pl.ds(r0, T), :]
            vc = v_ref[pl.ds(r0, T), :]
            ex = _dot(st_hi, elane_ref[d]) + _dot(st_mid, elane_ref[d])
            w_inter, floor, w_key = ex[0:T], ex[T:2 * T], ex[2 * T:3 * T]
            sp_row = ex[3 * T:3 * T + 1]

            r_row = jnp.concatenate([key_w_t[d * nh + h:d * nh + h + 1, :] for h in range(nh)],
                                    axis=1)
            dlog = _split_dot(u, ecol_ref[d]) + r_row
            decay = jnp.exp(jnp.where(causal[d], dlog, NEG_INF))

            kt = kc.astype(F32).T.astype(BF16)
            qk = _dot(qc, jnp.concatenate([kt] * nh, axis=1) * kmask_ref[...]) * decay
            qk_hi = qk.astype(BF16)
            qk_lo = (qk - qk_hi.astype(F32)).astype(BF16)
            vblk = jnp.concatenate([vc] * nh, axis=0) * vmask_ref[...]
            num = _dot(qk_hi, vblk)
            den = _dot(qk_hi, ones_ref[...]) + _dot(qk_lo, ones_ref[...])

            c_prev = cst[d]
            n_prev = nst[d:d + 1, :]
            num = num + w_inter * _dot(qc, c_prev.astype(BF16))
            den = den + w_inter * _dot((qc.astype(F32) * n_prev).astype(BF16), hsum_ref[...])
            h_out = num / jnp.maximum(jnp.abs(den), floor)
            if d == 0:
                hfw[pl.ds(r0, T), :] = h_out
            else:
                hbw[pl.ds(r0, T), :] = h_out

            kw = kc.astype(F32) * w_key
            cst[d] = c_prev * sp_row + _dot_tn(kw.astype(BF16), vc) * bd_ref[...]
            nst[d:d + 1, :] = n_prev * sp_row + jnp.sum(kw, axis=0, keepdims=True)
        return carry

    lax.fori_loop(0, nc, body, 0)
    o_ref[...] = (hfw[...] + hbw[...]).astype(BF16)


def _mlstm_consts():
    T, nh, g = MLSTM_T, N_HEADS, GROUP_W
    tri = np.tril(np.ones((T, T), np.float32))
    src = np.arange(LANE)[:, None]
    ecol = np.stack([(src == d * nh + np.arange(nh * T)[None, :] // T) for d in range(2)])
    elane = np.stack([(src == d * nh + np.arange(g)[None, :] // HEAD_DIM) for d in range(2)])
    head_of = np.arange(g) // HEAD_DIM
    blk_of = np.arange(nh * T) // T
    kmask = head_of[:, None] == blk_of[None, :]
    vmask = blk_of[:, None] == head_of[None, :]
    same_head = head_of[:, None] == head_of[None, :]
    b16 = lambda a: jnp.asarray(a.astype(np.float32), BF16)
    return (b16(tri), b16(tri.T), b16(ecol), b16(elane), b16(kmask), b16(vmask), b16(vmask),
            jnp.asarray(same_head.astype(np.float32)), b16(same_head))


def _mlstm(h_main, gate_i, gate_f, conv_w, conv_b, bias_i, bias_f, consts, B, L):
    T = MLSTM_T
    const = lambda b: (0, 0)
    const3 = lambda b: (0, 0, 0)
    return pl.pallas_call(
        functools.partial(_mlstm_kernel, L=L),
        grid=(B,),
        in_specs=[
            pl.BlockSpec((L, GROUP_W), lambda b: (b, BLK_DQ)),
            pl.BlockSpec((L, GROUP_W), lambda b: (b, BLK_DK)),
            pl.BlockSpec((L, GROUP_W), lambda b: (b, BLK_DV)),
            pl.BlockSpec((L, LANE), lambda b: (b, 0)),
            pl.BlockSpec((L, LANE), lambda b: (b, 0)),
            pl.BlockSpec((3, 2 * GROUP_W), const),
            pl.BlockSpec((1, 2 * GROUP_W), const),
            pl.BlockSpec((1, LANE), const),
            pl.BlockSpec((1, LANE), const),
            pl.BlockSpec((T, T), const),
            pl.BlockSpec((T, T), const),
            pl.BlockSpec((2, LANE, N_HEADS * T), const3),
            pl.BlockSpec((2, LANE, GROUP_W), const3),
            pl.BlockSpec((GROUP_W, N_HEADS * T), const),
            pl.BlockSpec((N_HEADS * T, GROUP_W), const),
            pl.BlockSpec((N_HEADS * T, GROUP_W), const),
            pl.BlockSpec((GROUP_W, GROUP_W), const),
            pl.BlockSpec((GROUP_W, GROUP_W), const),
        ],
        out_specs=pl.BlockSpec((L, GROUP_W), lambda b: (b, 0)),
        out_shape=jax.ShapeDtypeStruct((B * L, GROUP_W), BF16),
        scratch_shapes=[
            pltpu.VMEM((L + 2 * CONV_HALO, GROUP_W), BF16),
            pltpu.VMEM((L + 2 * CONV_HALO, GROUP_W), BF16),
            pltpu.VMEM((L, GROUP_W), BF16),
            pltpu.VMEM((L, GROUP_W), BF16),
            pltpu.VMEM((L, GROUP_W), F32),
            pltpu.VMEM((L, GROUP_W), F32),
            pltpu.VMEM((2, GROUP_W, GROUP_W), F32),
            pltpu.VMEM((8, GROUP_W), F32),
            pltpu.VMEM((8, LANE), F32),
        ],
        compiler_params=_cparams(("parallel",)),
        name="mlstm",
    )(h_main, h_main, h_main, gate_i, gate_f, conv_w, conv_b, bias_i, bias_f, *consts)


def _hy_dims(L):
    h1 = L // HY_N2
    k1h = h1 + 1
    k1p = -(-k1h // BF16_SUBLANES) * BF16_SUBLANES
    kg = 12 if k1p % 12 == 0 else BF16_SUBLANES
    return h1, k1h, k1p, kg


def _hy_tables(L):
    h1, k1h, k1p, _ = _hy_dims(L)
    n1_len = 2 * h1
    n = 2 * L
    k1 = np.arange(k1p)[:, None].astype(np.float64)
    live = (np.arange(k1p) < k1h)[:, None]
    n1 = np.arange(h1)[None, :].astype(np.float64)
    ang = 2.0 * np.pi * k1 * n1 / n1_len
    m1 = np.concatenate([np.where(live, np.cos(ang), 0.0), np.where(live, -np.sin(ang), 0.0)], 0)

    n2 = np.arange(HY_N2)[None, None, :].astype(np.float64)
    k2 = np.arange(HY_N2)[None, :, None].astype(np.float64)
    kk = np.arange(k1p)[:, None, None] + n1_len * k2
    th = 2.0 * np.pi * kk * n2 / n
    fr, fi = np.cos(th), -np.sin(th)
    f3 = np.concatenate([np.concatenate([fr, -fi], 2), np.concatenate([fi, fr], 2)], 1)
    er, ei = np.transpose(np.cos(th), (0, 2, 1)), np.transpose(np.sin(th), (0, 2, 1))
    f3i = np.concatenate([np.concatenate([er, -ei], 2), np.concatenate([ei, er], 2)], 1)
    live3 = (np.arange(k1p) < k1h)[:, None, None]
    f3 = np.where(live3, f3, 0.0)
    f3i = np.where(live3, f3i, 0.0)

    nn1 = np.arange(h1)[:, None].astype(np.float64)
    kc = np.arange(k1p)[None, :].astype(np.float64)
    ph = 2.0 * np.pi * nn1 * kc / n1_len
    edge = (np.arange(k1p) == 0) | (np.arange(k1p) == h1)
    livec = (np.arange(k1p) < k1h)[None, :]
    m4r = np.where(livec, np.where(edge[None, :], np.cos(ph), 2.0 * np.cos(ph)), 0.0) / n
    m4i = np.where(livec & ~edge[None, :], -2.0 * np.sin(ph), 0.0) / n
    return (jnp.asarray(m1, BF16), jnp.asarray(f3, BF16), jnp.asarray(f3i, BF16),
            jnp.asarray(m4r, BF16), jnp.asarray(m4i, BF16))


def _to_wide(x, h1):
    return x.astype(F32).reshape(h1, HY_N2 * GROUP_W)


def _hy_stage1(z_bf16, m1_ref, a_ref, k1p):
    a = _dot(m1_ref[...], z_bf16)
    a_ref[0, 0] = a[0:k1p].reshape(k1p, HY_N2, GROUP_W).astype(BF16)
    a_ref[0, 1] = a[k1p:2 * k1p].reshape(k1p, HY_N2, GROUP_W).astype(BF16)


def _hy_conv3_wide(x, w_ref, b_ref, j, h1):
    c = GROUP_W
    wl = HY_N2 * c
    rowi = lax.broadcasted_iota(jnp.int32, (h1, c), 0)
    tail = x[:, wl - c:wl]
    head = x[:, 0:c]
    prev_tail = jnp.where(rowi == 0, 0.0, pltpu.roll(tail, 1, axis=0))
    next_head = jnp.where(rowi == h1 - 1, 0.0, pltpu.roll(head, h1 - 1, axis=0))
    xm = jnp.concatenate([prev_tail, x[:, 0:wl - c]], axis=1)
    xp = jnp.concatenate([x[:, c:wl], next_head], axis=1)
    return (xm * w_ref[3 * j:3 * j + 1, :] + x * w_ref[3 * j + 1:3 * j + 2, :]
            + xp * w_ref[3 * j + 2:3 * j + 3, :] + b_ref[j:j + 1, :])


def _hy_front_kernel(v_ref, x1_ref, x2_ref, w_ref, b_ref, m1_ref,
                     z_ref, x1c_ref, x2c_ref, a_ref, *, h1, k1p):
    z = _hy_conv3_wide(_to_wide(v_ref[...], h1), w_ref, b_ref, 0, h1).astype(BF16)
    z_ref[0] = z
    x1c_ref[0] = _hy_conv3_wide(_to_wide(x1_ref[...], h1), w_ref, b_ref, 1, h1).astype(BF16)
    x2c_ref[0] = _hy_conv3_wide(_to_wide(x2_ref[...], h1), w_ref, b_ref, 2, h1).astype(BF16)
    _hy_stage1(z, m1_ref, a_ref, k1p)


def _hy_front(cv, cx1, cx2, w_wide, b_wide, m1, B, L):
    h1, _, k1p, _ = _hy_dims(L)
    wl = HY_N2 * GROUP_W
    nat = pl.BlockSpec((L, GROUP_W), lambda b: (b, 0))
    seq = pl.BlockSpec((1, h1, wl), lambda b: (b, 0, 0))
    wide = jax.ShapeDtypeStruct((B, h1, wl), BF16)
    slab = (1, 2, k1p, HY_N2, GROUP_W)
    return pl.pallas_call(
        functools.partial(_hy_front_kernel, h1=h1, k1p=k1p),
        grid=(B,),
        in_specs=[nat, nat, nat,
                  pl.BlockSpec((9, wl), lambda b: (0, 0)),
                  pl.BlockSpec((3, wl), lambda b: (0, 0)),
                  pl.BlockSpec((2 * k1p, h1), lambda b: (0, 0))],
        out_specs=(seq, seq, seq, pl.BlockSpec(slab, lambda b: (b, 0, 0, 0, 0))),
        out_shape=(wide, wide, wide,
                   jax.ShapeDtypeStruct((B, 2, k1p, HY_N2, GROUP_W), BF16)),
        compiler_params=_cparams(("parallel",)),
        name="hy_front",
    )(cv, cx1, cx2, w_wide, b_wide, m1)


def _hy_stage1_kernel(z_ref, m1_ref, a_ref, *, h1, k1p):
    _hy_stage1(_to_wide(z_ref[0], h1).astype(BF16), m1_ref, a_ref, k1p)


def _hy_stage1_call(z_nat, m1, L):
    h1, _, k1p, _ = _hy_dims(L)
    nb = z_nat.shape[0]
    return pl.pallas_call(
        functools.partial(_hy_stage1_kernel, h1=h1, k1p=k1p),
        grid=(nb,),
        in_specs=[pl.BlockSpec((1, L, GROUP_W), lambda b: (b, 0, 0)),
                  pl.BlockSpec((2 * k1p, h1), lambda b: (0, 0))],
        out_specs=pl.BlockSpec((1, 2, k1p, HY_N2, GROUP_W), lambda b: (b, 0, 0, 0, 0)),
        out_shape=jax.ShapeDtypeStruct((nb, 2, k1p, HY_N2, GROUP_W), BF16),
        compiler_params=_cparams(("parallel",)),
        name="hy_stage1",
    )(z_nat, m1)


def _hy_spectrum_kernel(af_ref, ab_ref, f3_ref, g_ref, *, kg, k1h):
    grp = pl.program_id(0)

    @pl.when(grp * kg < k1h)
    def _():
        for kk in range(kg):
            xs = []
            for a_ref in (af_ref, ab_ref):
                a2 = jnp.concatenate([a_ref[0, 0, kk], a_ref[0, 1, kk]], axis=0)
                xs.append(_dot(f3_ref[kk], a2))
            g_ref[0, 0, kk] = xs[0][0:HY_N2] + xs[1][0:HY_N2]
            g_ref[0, 1, kk] = xs[0][HY_N2:2 * HY_N2] - xs[1][HY_N2:2 * HY_N2]

    @pl.when(grp * kg >= k1h)
    def _():
        g_ref[...] = jnp.zeros(g_ref.shape, F32)


def _hy_spectrum(a_filt, f3, L):
    _, k1h, k1p, kg = _hy_dims(L)
    a5 = a_filt
    blk = (1, 2, kg, HY_N2, GROUP_W)
    return pl.pallas_call(
        functools.partial(_hy_spectrum_kernel, kg=kg, k1h=k1h),
        grid=(k1p // kg, HYENA_ORDER),
        in_specs=[pl.BlockSpec(blk, lambda g, o: (2 * o, 0, g, 0, 0)),
                  pl.BlockSpec(blk, lambda g, o: (2 * o + 1, 0, g, 0, 0)),
                  pl.BlockSpec((kg, 2 * HY_N2, 2 * HY_N2), lambda g, o: (g, 0, 0))],
        out_specs=pl.BlockSpec(blk, lambda g, o: (o, 0, g, 0, 0)),
        out_shape=jax.ShapeDtypeStruct((HYENA_ORDER, 2, k1p, HY_N2, GROUP_W), F32),
        compiler_params=_cparams(("parallel", "parallel")),
        name="hy_spectrum",
    )(a5, a5, f3)


def _hy_mid_kernel(a_ref, g_ref, f3_ref, f3i_ref, b_ref, *, kg, k1h):
    grp = pl.program_id(0)

    @pl.when(grp * kg < k1h)
    def _():
        for kk in range(kg):
            a2 = jnp.concatenate([a_ref[0, 0, kk], a_ref[0, 1, kk]], axis=0)
            x = _dot(f3_ref[kk], a2)
            xr, xi = x[0:HY_N2], x[HY_N2:2 * HY_N2]
            gr, gi = g_ref[0, 0, kk], g_ref[0, 1, kk]
            y2 = jnp.concatenate([xr * gr - xi * gi, xr * gi + xi * gr], axis=0).astype(BF16)
            bm = _dot(f3i_ref[kk], y2)
            b_ref[0, 0, kk] = bm[0:HY_N2].astype(BF16)
            b_ref[0, 1, kk] = bm[HY_N2:2 * HY_N2].astype(BF16)

    @pl.when(grp * kg >= k1h)
    def _():
        b_ref[...] = jnp.zeros(b_ref.shape, BF16)


def _hy_mid(a, g_spec, order, f3, f3i, B, L):
    _, k1h, k1p, kg = _hy_dims(L)
    blk = (1, 2, kg, HY_N2, GROUP_W)
    tab = pl.BlockSpec((kg, 2 * HY_N2, 2 * HY_N2), lambda g, b: (g, 0, 0))
    return pl.pallas_call(
        functools.partial(_hy_mid_kernel, kg=kg, k1h=k1h),
        grid=(k1p // kg, B),
        in_specs=[pl.BlockSpec(blk, lambda g, b: (b, 0, g, 0, 0)),
                  pl.BlockSpec(blk, lambda g, b: (order, 0, g, 0, 0)),
                  tab, tab],
        out_specs=pl.BlockSpec(blk, lambda g, b: (b, 0, g, 0, 0)),
        out_shape=jax.ShapeDtypeStruct((B, 2, k1p, HY_N2, GROUP_W), BF16),
        compiler_params=_cparams(("parallel", "parallel")),
        name="hy_mid",
    )(a, g_spec, f3, f3i)


def _hy_back_kernel(b_ref, z_ref, x_ref, bias_ref, m4r_ref, m4i_ref, *rest, k1p, last):
    wl = HY_N2 * GROUP_W
    br = b_ref[0, 0].astype(F32).reshape(k1p, wl).astype(BF16)
    bi = b_ref[0, 1].astype(F32).reshape(k1p, wl).astype(BF16)
    y = _dot(m4r_ref[...], br) + _dot(m4i_ref[...], bi)
    z_new = x_ref[0].astype(F32) * (y + z_ref[0].astype(F32) * bias_ref[...])
    zb = z_new.astype(BF16)
    if last:
        (o_ref,) = rest
        o_ref[...] = z_new.reshape(o_ref.shape).astype(BF16)
    else:
        m1_ref, o_ref, a_ref = rest
        o_ref[0] = zb
        _hy_stage1(zb, m1_ref, a_ref, k1p)


def _hy_back(b, z, xg, bias_wide, m4r, m4i, m1, B, L, last):
    h1, _, k1p, _ = _hy_dims(L)
    wl = HY_N2 * GROUP_W
    seq = pl.BlockSpec((1, h1, wl), lambda i: (i, 0, 0))
    slab = pl.BlockSpec((1, 2, k1p, HY_N2, GROUP_W), lambda i: (i, 0, 0, 0, 0))
    const = lambda i: (0, 0)
    in_specs = [slab, seq, seq, pl.BlockSpec((1, wl), const),
                pl.BlockSpec((h1, k1p), const), pl.BlockSpec((h1, k1p), const)]
    args = [b, z, xg, bias_wide, m4r, m4i]
    if last:
        out_specs = pl.BlockSpec((L, GROUP_W), lambda i: (i, 0))
        out_shape = jax.ShapeDtypeStruct((B * L, GROUP_W), BF16)
    else:
        in_specs.append(pl.BlockSpec((2 * k1p, h1), const))
        args.append(m1)
        out_specs = (seq, slab)
        out_shape = (jax.ShapeDtypeStruct((B, h1, wl), BF16),
                     jax.ShapeDtypeStruct((B, 2, k1p, HY_N2, GROUP_W), BF16))
    return pl.pallas_call(
        functools.partial(_hy_back_kernel, k1p=k1p, last=last),
        grid=(B,),
        in_specs=in_specs,
        out_specs=out_specs,
        out_shape=out_shape,
        compiler_params=_cparams(("parallel",)),
        name="hy_back_last" if last else "hy_back",
    )(*args)


def _hy_filter_kernel(f_ref, t_ref, w1_ref, b1_ref, w2_ref, b2_ref, w3_ref, fr_ref, ad_ref,
                      o_ref, *, tile):
    i = pl.program_id(0)
    freq = fr_ref[...]
    z = jnp.sin(freq * (_dot_f32(f_ref[...], w1_ref[...]) + b1_ref[...]))
    z = jnp.sin(freq * (_dot_f32(z, w2_ref[...]) + b2_ref[...]))
    decay = jnp.exp(-t_ref[...] * ad_ref[...])
    rowi = i * tile + lax.broadcasted_iota(jnp.int32, (tile, GROUP_W), 0)
    for j in range(2 * HYENA_ORDER):
        hj = _dot_f32(z, w3_ref[:, j * GROUP_W:(j + 1) * GROUP_W]) * decay
        if j % 2 == 1:
            hj = jnp.where(rowi == 0, 0.0, hj)
        o_ref[j] = hj.astype(BF16)


def _hy_filter(feats, tcol, w1, b1, w2, b2, w3, freq, absdelta, L):
    tile = math.gcd(L, ROW_TILE)
    const = lambda i: (0, 0)
    return pl.pallas_call(
        functools.partial(_hy_filter_kernel, tile=tile),
        grid=(L // tile,),
        in_specs=[pl.BlockSpec((tile, LANE), lambda i: (i, 0)),
                  pl.BlockSpec((tile, 1), lambda i: (i, 0)),
                  pl.BlockSpec((LANE, HYENA_HIDDEN), const),
                  pl.BlockSpec((1, HYENA_HIDDEN), const),
                  pl.BlockSpec((HYENA_HIDDEN, HYENA_HIDDEN), const),
                  pl.BlockSpec((1, HYENA_HIDDEN), const),
                  pl.BlockSpec((HYENA_HIDDEN, 2 * HYENA_ORDER * GROUP_W), const),
                  pl.BlockSpec((1, HYENA_HIDDEN), const),
                  pl.BlockSpec((1, GROUP_W), const)],
        out_specs=pl.BlockSpec((2 * HYENA_ORDER, tile, GROUP_W), lambda i: (0, i, 0)),
        out_shape=jax.ShapeDtypeStruct((2 * HYENA_ORDER, L, GROUP_W), BF16),
        compiler_params=_cparams(("parallel",)),
        name="hy_filter",
    )(feats, tcol, w1, b1, w2, b2, w3, freq, absdelta)


def _outproj_kernel(x_ref, oa_ref, ob_ref, oc_ref, od_ref, ox_ref,
                    ga_ref, gb_ref, gc_ref, gdo_ref, gdg_ref, gx_ref, w_ref, png_ref, out_ref):
    f = lambda r: r[...].astype(F32)
    branches = (
        f(oa_ref) * _silu(f(ga_ref)),
        f(ob_ref) * _silu(f(gb_ref)),
        f(oc_ref) * _silu(f(gc_ref)),
        f(od_ref) * _sigmoid(f(gdo_ref)) * _silu(f(gdg_ref)),
        f(ox_ref) * _silu(f(gx_ref)),
    )
    y = jnp.zeros(x_ref.shape, F32)
    for j, br in enumerate(branches):
        y = y + _dot(br.astype(BF16), w_ref[j * GROUP_W:(j + 1) * GROUP_W, :])
    ms = jnp.mean(y * y, axis=-1, keepdims=True)
    out_ref[...] = x_ref[...] + y * lax.rsqrt(ms + NORM_EPS) * png_ref[...]


def _outproj(x2d, outs, h_main, w_out, png, row0=0, n_rows=None):
    n_rows = x2d.shape[0] if n_rows is None else n_rows
    nt = n_rows // ROW_TILE
    t0 = row0 // ROW_TILE
    row = lambda i: (t0 + i, 0)
    blk = lambda j: pl.BlockSpec((ROW_TILE, GROUP_W), lambda i: (t0 + i, j))
    return pl.pallas_call(
        _outproj_kernel,
        grid=(nt,),
        in_specs=[pl.BlockSpec((ROW_TILE, D_MODEL), row)]
        + [pl.BlockSpec((ROW_TILE, GROUP_W), row)] * 5
        + [blk(BLK_AG), blk(BLK_BG), blk(BLK_CG), blk(BLK_DO), blk(BLK_DG), blk(BLK_XG)]
        + [pl.BlockSpec((5 * GROUP_W, D_MODEL), lambda i: (0, 0)),
           pl.BlockSpec((1, D_MODEL), lambda i: (0, 0))],
        out_specs=pl.BlockSpec((ROW_TILE, D_MODEL), lambda i: (i, 0)),
        out_shape=jax.ShapeDtypeStruct((n_rows, D_MODEL), F32),
        compiler_params=_cparams(("parallel",)),
        name="outproj",
    )(x2d, *outs, h_main, h_main, h_main, h_main, h_main, h_main, w_out, png)


def _rope_tables(L, group, rot_dim):
    half = rot_dim // 2
    inv = 1.0 / (ROPE_THETA ** (jnp.arange(0, rot_dim, 2, dtype=F32) / rot_dim))
    ang = jnp.arange(L, dtype=F32)[:, None] * inv[None, :]
    cos, sin = jnp.cos(ang), jnp.sin(ang)
    lane = np.arange(GROUP_W) % group
    in_rot = lane < rot_dim
    idx = lane % half
    c = jnp.where(in_rot[None, :], cos[:, idx], 1.0)
    s = jnp.where(in_rot[None, :], sin[:, idx], 0.0)
    p = np.zeros((GROUP_W, GROUP_W), np.float32)
    for j in range(GROUP_W):
        if lane[j] < half:
            p[j + half, j] = -1.0
        elif lane[j] < rot_dim:
            p[j - half, j] = 1.0
    return c, s, jnp.asarray(p, BF16)


def _hyena_features(L):
    t = jnp.linspace(0.0, 1.0, L, dtype=F32)[:, None]
    bands = jnp.linspace(1e-4, HYENA_BANDS - 1, HYENA_BANDS, dtype=F32)
    ang = (2.0 * math.pi / L) * jnp.arange(L, dtype=F32)[:, None] * bands[None, :]
    feats = jnp.concatenate([t, jnp.cos(ang), -jnp.sin(ang)], axis=-1)
    return jnp.pad(feats, ((0, 0), (0, LANE - HYENA_EMB))), t


def _relayout_w_in(w):
    g = GROUP_W
    off_c, off_d = 8 * g, 12 * g
    off_gate = off_d + 5 * g
    off_x = off_gate + N_MLSTM_GATES
    main = jnp.concatenate([w[:, 0:off_c], w[:, off_c + 3 * g:off_c + 4 * g],
                            w[:, off_d:off_gate], w[:, off_x:off_x + 2 * g]], axis=1)
    hy = w[:, off_c:off_c + 3 * g]
    gate_i, gate_f = _split_gates(w[:, off_gate:off_x])
    return jnp.concatenate([main, hy, gate_i, gate_f], axis=1).astype(BF16)


def _split_gates(t):
    nh = N_HEADS
    pad = [(0, 0)] * (t.ndim - 1) + [(0, LANE - 2 * nh)]
    gi = jnp.concatenate([t[..., 0:nh], t[..., 2 * nh:3 * nh]], axis=-1)
    gf = jnp.concatenate([t[..., nh:2 * nh], t[..., 3 * nh:4 * nh]], axis=-1)
    return jnp.pad(gi, pad), jnp.pad(gf, pad)


def _trunk(x, mem, splits, pre_norm_g, post_norm_g, w_in, w_out, diff_lambda, diff_subln_g,
           hy_conv_w, hy_conv_b, hy_ffn_w1, hy_ffn_b1, hy_ffn_w2, hy_ffn_b2, hy_ffn_w3,
           hy_freq, hy_bias, ml_conv_w, ml_conv_b, ml_gate_b, mem_norm_g, w_mem_kv):
    B, L, _ = x.shape
    M = mem.shape[1]
    depth = w_in.shape[0]
    g = GROUP_W
    h1, _, k1p, _ = _hy_dims(L)
    wl = HY_N2 * g

    ca, sa, pa = _rope_tables(L, HEAD_DIM, HEAD_DIM // ROPE_FRACTION)
    cb, sb, pb = _rope_tables(L, DIFF_QK_DIM, DIFF_QK_DIM // ROPE_FRACTION)
    rope_tabs = (ca, sa, cb, sb, pa, pb)
    dil_bias = jnp.asarray(_dil_bias_table())
    feats, tcol = _hyena_features(L)
    absdelta = jnp.abs(jnp.linspace(math.log(HYENA_TARGET) / HYENA_SLOW_DECAY,
                                    math.log(HYENA_TARGET) / HYENA_FAST_DECAY, g, dtype=F32))[None]
    m1, f3, f3i, m4r, m4i = _hy_tables(L)
    head_of = np.arange(g) // HEAD_DIM
    same_head = (head_of[:, None] == head_of[None, :]).astype(np.float32)
    hmean = jnp.asarray(same_head / HEAD_DIM, BF16)
    mlstm_consts = _mlstm_consts()

    x2d = x.reshape(B * L, D_MODEL)
    mem2d = mem.reshape(B * M, D_MODEL)
    for li in range(depth):
        lam_init = 0.8 - 0.6 * math.exp(-0.3 * li)
        w_all = _relayout_w_in(w_in[li])
        h_main, cv, cx1, cx2, gate_i, gate_f = _inproj(x2d, pre_norm_g[li][None], w_all,
                                                       rope_tabs, B, L)

        oa = _dilattn(h_main, dil_bias, B, L)
        ob = _diffattn(h_main, diff_lambda[li], jnp.tile(diff_subln_g[li], N_HEADS)[None],
                       hmean, lam_init, B, L)

        w1 = jnp.pad(hy_ffn_w1[li], ((0, LANE - HYENA_EMB), (0, 0)))
        hfilt = _hy_filter(feats, tcol, w1, hy_ffn_b1[li][None], hy_ffn_w2[li],
                           hy_ffn_b2[li][None], hy_ffn_w3[li], hy_freq[li][None], absdelta, L)
        a_filt = _hy_stage1_call(hfilt, m1, L)
        g_spec = _hy_spectrum(a_filt, f3, L)
        cw = jnp.tile(hy_conv_w[li].reshape(3, 3, g).transpose(1, 0, 2).reshape(9, g), (1, HY_N2))
        cbw = jnp.tile(hy_conv_b[li].reshape(3, g), (1, HY_N2))
        z, x1c, x2c, a = _hy_front(cv, cx1, cx2, cw, cbw, m1, B, L)
        bias_w = jnp.tile(hy_bias[li], (1, HY_N2))
        bsp = _hy_mid(a, g_spec, 0, f3, f3i, B, L)
        z, a = _hy_back(bsp, z, x1c, bias_w[0:1], m4r, m4i, m1, B, L, last=False)
        bsp = _hy_mid(a, g_spec, 1, f3, f3i, B, L)
        oc = _hy_back(bsp, z, x2c, bias_w[1:2], m4r, m4i, m1, B, L, last=True)

        bias_i, bias_f = _split_gates(ml_gate_b[li][None])
        od = _mlstm(h_main, gate_i, gate_f, ml_conv_w[li], ml_conv_b[li][None], bias_i, bias_f,
                    mlstm_consts, B, L)

        mkv = _memkv(mem2d, mem_norm_g[li][None], w_mem_kv[li].astype(BF16))
        ox = _memattn(h_main, mkv, B, L, M)

        branch_outs = (oa, ob, oc, od, ox)
        w_o, png = w_out[li].astype(BF16), post_norm_g[li][None]
        if li + 1 < depth:
            x2d = _outproj(x2d, branch_outs, h_main, w_o, png)
    ys, b0 = [], 0
    for nb in splits:
        y = _outproj(x2d, branch_outs, h_main, w_o, png, row0=b0 * L, n_rows=nb * L)
        ys.append(y.reshape(nb, L, D_MODEL))
        b0 += nb
    return tuple(ys)


def kernel(x_prompt, x_sample, mem_prompt, mem_sample, pre_norm_g, post_norm_g, w_in, w_out,
           diff_lambda, diff_subln_g, hy_conv_w, hy_conv_b, hy_ffn_w1, hy_ffn_b1, hy_ffn_w2,
           hy_ffn_b2, hy_ffn_w3, hy_freq, hy_bias, ml_conv_w, ml_conv_b, ml_gate_b,
           mem_norm_g, w_mem_kv):
    x = jnp.concatenate([x_prompt, x_sample], axis=0)
    mem = jnp.concatenate([mem_prompt, mem_sample], axis=0)
    return _trunk(x, mem, (x_prompt.shape[0], x_sample.shape[0]), pre_norm_g, post_norm_g, w_in,
                  w_out, diff_lambda, diff_subln_g, hy_conv_w, hy_conv_b, hy_ffn_w1, hy_ffn_b1,
                  hy_ffn_w2, hy_ffn_b2, hy_ffn_w3, hy_freq, hy_bias, ml_conv_w, ml_conv_b,
                  ml_gate_b, mem_norm_g, w_mem_kv)
```

```python
import functools
import math

import numpy as np
import jax
import jax.numpy as jnp
from jax import lax
from jax.experimental import pallas as pl
from jax.experimental.pallas import tpu as pltpu

F32 = jnp.float32
BF16 = jnp.bfloat16

D_MODEL = 1024
HEAD_DIM = 64
GROUP_W = 256
N_HEADS = GROUP_W // HEAD_DIM
NORM_EPS = 1e-6
NEG_INF = -1e30
ROPE_THETA = 500000.0
ROPE_FRACTION = 4
DIL_PATTERNS = ((128, 1), (512, 4), (2048, 16))
DIFF_QK_DIM = HEAD_DIM // 2
DIFF_SUBLN_EPS = 1e-5
HYENA_ORDER = 2
HYENA_BANDS = 16
HYENA_EMB = 1 + 2 * HYENA_BANDS
HYENA_HIDDEN = 64
HYENA_FAST_DECAY = 0.3
HYENA_SLOW_DECAY = 1.5
HYENA_TARGET = 1e-2
N_MLSTM_GATES = 4 * N_HEADS

(BLK_AQ, BLK_AK, BLK_AV, BLK_AG, BLK_BQ, BLK_BK, BLK_BV, BLK_BG, BLK_CG,
 BLK_DQ, BLK_DK, BLK_DV, BLK_DO, BLK_DG, BLK_XQ, BLK_XG) = range(16)
N_MAIN_BLK = 16
MAIN_W = N_MAIN_BLK * GROUP_W
W_ALL = MAIN_W + 4 * GROUP_W

VMEM_LIMIT_BYTES = 56 * 1024 * 1024
LANE = 128
BF16_SUBLANES = 16

ROW_TILE = 512
DIL_Q_TILE = 256
DIL_PAD = 1024
DIL_K_TILE = 256
DIL_VT_ROWS = HEAD_DIM + BF16_SUBLANES
DIFF_Q_TILE = 256
DIFF_K_CHUNK = 512
DIFF_VT_ROWS = HEAD_DIM + BF16_SUBLANES
LOG2E = 1.4426950408889634
MEM_Q_TILE = 512
MLSTM_T = 128
CONV_HALO = 16
HY_N2 = 128


def _cparams(sem, vmem=VMEM_LIMIT_BYTES):
    return pltpu.CompilerParams(dimension_semantics=sem, vmem_limit_bytes=vmem)


def _sigmoid(v):
    return 1.0 / (1.0 + jnp.exp(-v))


def _silu(v):
    return v * _sigmoid(v)


def _dot(a, b):
    return jnp.dot(a, b, preferred_element_type=F32)


def _dot_nt(a, b):
    return lax.dot_general(a, b, (((1,), (1,)), ((), ())), preferred_element_type=F32)


def _dot_tn(a, b):
    return lax.dot_general(a, b, (((0,), (0,)), ((), ())), preferred_element_type=F32)


def _head_mask(h, width, dtype):
    lane = lax.broadcasted_iota(jnp.int32, (1, GROUP_W), 1)
    return ((lane // width) == h).astype(dtype)


def _inproj_kernel(x_ref, g_ref, w_ref, ca_ref, sa_ref, cb_ref, sb_ref, pa_ref, pb_ref,
                   h_ref, cv_ref, cx1_ref, cx2_ref, gi_ref, gf_ref):
    x = x_ref[...]
    ms = jnp.mean(x * x, axis=-1, keepdims=True)
    xn = (x * lax.rsqrt(ms + NORM_EPS) * g_ref[...]).astype(BF16)

    def proj(j, width=GROUP_W):
        return _dot(xn, w_ref[:, j * GROUP_W:j * GROUP_W + width])

    def rope(acc, c_ref, s_ref, p_ref):
        partner = _dot(acc.astype(BF16), p_ref[...])
        return acc * c_ref[...] + partner * s_ref[...]

    def finish(j, acc):
        if j in (BLK_AQ, BLK_AK):
            acc = rope(acc, ca_ref, sa_ref, pa_ref)
        if j in (BLK_BQ, BLK_BK):
            acc = rope(acc, cb_ref, sb_ref, pb_ref)
        if j in (BLK_AQ, BLK_XQ):
            acc = acc * (LOG2E / math.sqrt(HEAD_DIM))
        if j == BLK_BQ:
            acc = acc * (LOG2E / math.sqrt(DIFF_QK_DIM))
        if j < N_MAIN_BLK:
            h_ref[:, j * GROUP_W:(j + 1) * GROUP_W] = acc.astype(BF16)
        elif j < N_MAIN_BLK + 3:
            (cv_ref, cx1_ref, cx2_ref)[j - N_MAIN_BLK][...] = acc.astype(BF16)
        else:
            gi_ref[...] = acc[:, 0:LANE]
            gf_ref[...] = acc[:, LANE:2 * LANE]

    for j in range(0, W_ALL // GROUP_W, 2):
        acc2 = proj(j, 2 * GROUP_W)
        finish(j, acc2[:, 0:GROUP_W])
        finish(j + 1, acc2[:, GROUP_W:2 * GROUP_W])


def _inproj(x2d, g, w_all, rope_tabs, B, L):
    ca, sa, cb, sb, pa, pb = rope_tabs
    nt = L // ROW_TILE
    n_tok = B * L
    row = lambda i, b: (b * nt + i, 0)
    tab = lambda i, b: (i, 0)
    const = lambda i, b: (0, 0)
    out_shapes = (
        jax.ShapeDtypeStruct((n_tok, MAIN_W), BF16),
        jax.ShapeDtypeStruct((n_tok, GROUP_W), BF16),
        jax.ShapeDtypeStruct((n_tok, GROUP_W), BF16),
        jax.ShapeDtypeStruct((n_tok, GROUP_W), BF16),
        jax.ShapeDtypeStruct((n_tok, LANE), F32),
        jax.ShapeDtypeStruct((n_tok, LANE), F32),
    )
    return pl.pallas_call(
        _inproj_kernel,
        grid=(nt, B),
        in_specs=[
            pl.BlockSpec((ROW_TILE, D_MODEL), row),
            pl.BlockSpec((1, D_MODEL), const),
            pl.BlockSpec((D_MODEL, W_ALL), const),
            pl.BlockSpec((ROW_TILE, GROUP_W), tab),
            pl.BlockSpec((ROW_TILE, GROUP_W), tab),
            pl.BlockSpec((ROW_TILE, GROUP_W), tab),
            pl.BlockSpec((ROW_TILE, GROUP_W), tab),
            pl.BlockSpec((GROUP_W, GROUP_W), const),
            pl.BlockSpec((GROUP_W, GROUP_W), const),
        ],
        out_specs=(
            pl.BlockSpec((ROW_TILE, MAIN_W), row),
            pl.BlockSpec((ROW_TILE, GROUP_W), row),
            pl.BlockSpec((ROW_TILE, GROUP_W), row),
            pl.BlockSpec((ROW_TILE, GROUP_W), row),
            pl.BlockSpec((ROW_TILE, LANE), row),
            pl.BlockSpec((ROW_TILE, LANE), row),
        ),
        out_shape=out_shapes,
        compiler_params=_cparams(("parallel", "parallel")),
        name="inproj",
    )(x2d, g, w_all, ca, sa, cb, sb, pa, pb)


def _memkv_kernel(m_ref, g_ref, w_ref, o_ref):
    x = m_ref[...]
    ms = jnp.mean(x * x, axis=-1, keepdims=True)
    xn = (x * lax.rsqrt(ms + NORM_EPS) * g_ref[...]).astype(BF16)
    o_ref[...] = _dot(xn, w_ref[...]).astype(BF16)


def _memkv(mem2d, g, w):
    rows = mem2d.shape[0]
    tile = math.gcd(rows, ROW_TILE)
    return pl.pallas_call(
        _memkv_kernel,
        grid=(rows // tile,),
        in_specs=[
            pl.BlockSpec((tile, D_MODEL), lambda i: (i, 0)),
            pl.BlockSpec((1, D_MODEL), lambda i: (0, 0)),
            pl.BlockSpec((D_MODEL, 2 * GROUP_W), lambda i: (0, 0)),
        ],
        out_specs=pl.BlockSpec((tile, 2 * GROUP_W), lambda i: (i, 0)),
        out_shape=jax.ShapeDtypeStruct((rows, 2 * GROUP_W), BF16),
        compiler_params=_cparams(("parallel",)),
        name="memkv",
    )(mem2d, g, w)


def _memattn_kernel(q_ref, mk_ref, mv_ref, o_ref):
    q = q_ref[...]
    mk = mk_ref[...]
    mv = mv_ref[...]
    acc = jnp.zeros(q.shape, F32)
    for h in range(N_HEADS):
        hm = _head_mask(h, HEAD_DIM, BF16)
        s = _dot_nt(q * hm, mk)
        m = jnp.max(s, axis=-1, keepdims=True)
        p = jnp.exp2(s - m)
        l = jnp.sum(p, axis=-1, keepdims=True)
        acc = acc + _dot(p.astype(BF16), mv * hm) * (1.0 / l)
    o_ref[...] = acc.astype(BF16)


def _memattn(h_main, mkv, B, L, M):
    nq = L // MEM_Q_TILE
    return pl.pallas_call(
        _memattn_kernel,
        grid=(B, nq),
        in_specs=[
            pl.BlockSpec((MEM_Q_TILE, GROUP_W), lambda b, i: (b * nq + i, BLK_XQ)),
            pl.BlockSpec((M, GROUP_W), lambda b, i: (b, 0)),
            pl.BlockSpec((M, GROUP_W), lambda b, i: (b, 1)),
        ],
        out_specs=pl.BlockSpec((MEM_Q_TILE, GROUP_W), lambda b, i: (b * nq + i, 0)),
        out_shape=jax.ShapeDtypeStruct((B * L, GROUP_W), BF16),
        compiler_params=_cparams(("parallel", "parallel")),
        name="memattn",
    )(h_main, mkv, mkv)


def _dil_bias_table():
    w = DIL_Q_TILE + 2 * DIL_PAD
    d = np.arange(DIL_Q_TILE)[None, :] - np.arange(w)[:, None] + DIL_PAD
    count = np.zeros(d.shape, np.float64)
    for win, dil in DIL_PATTERNS:
        reach = (win // (2 * dil)) * dil
        count += (d % dil == 0) & (np.abs(d) <= reach)
    return np.where(count > 0, np.log2(np.maximum(count, 1.0)), NEG_INF).astype(np.float32)


def _dilattn_kernel(q_ref, qn_ref, k_ref, v_ref, bias_ref, o_ref, kpad, vt_ref, kmask,
                    sa_ref, sb_ref, m_ref, *, L):
    i = pl.program_id(1)
    nq = pl.num_programs(1)
    tq, hd, kt = DIL_Q_TILE, HEAD_DIM, DIL_K_TILE
    w = tq + 2 * DIL_PAD
    n_pad = DIL_PAD // kt

    @pl.when(i == 0)
    def _():
        zeros = jnp.zeros((DIL_PAD, GROUP_W), BF16)
        kpad[0:DIL_PAD, :] = zeros
        kpad[DIL_PAD + L:DIL_PAD + L + DIL_PAD, :] = zeros
        kpad[DIL_PAD:DIL_PAD + L, :] = k_ref[...]
        off = jnp.full((DIL_PAD, LANE), NEG_INF, F32)
        kmask[0:DIL_PAD, :] = off
        kmask[DIL_PAD + L:DIL_PAD + L + DIL_PAD, :] = off
        kmask[DIL_PAD:DIL_PAD + L, :] = jnp.zeros((L, LANE), F32)
        tail = (lax.broadcasted_iota(jnp.int32, (DIL_VT_ROWS - hd, kt), 0) == 0).astype(BF16)
        for c in range(L // kt + 2 * n_pad):
            inside = n_pad <= c < n_pad + L // kt
            if inside:
                lo = (c - n_pad) * kt
                vt = v_ref[lo:lo + kt, :].astype(F32).T.astype(BF16)
            for h in range(N_HEADS):
                vt_ref[c, h, 0:hd, :] = (vt[h * hd:(h + 1) * hd] if inside
                                         else jnp.zeros((hd, kt), BF16))
                vt_ref[c, h, hd:DIL_VT_ROWS, :] = tail

    feat_head = lax.broadcasted_iota(jnp.int32, (GROUP_W, 1), 0) // hd
    s_refs = (sa_ref, sb_ref)
    tiles = [(t * kt, (t + 1) * kt) for t in range(w // kt)]

    def stage(src_ref, tile, pair_next, nxt, pair_cur, cur):
        q0n = pl.multiple_of(tile * tq, tq)
        qt = src_ref[...].astype(F32).T
        qt2 = jnp.concatenate([jnp.where(feat_head == 2 * pair_next + c, qt, 0.0)
                               for c in range(2)], axis=1).astype(BF16)
        mx = None
        if pair_cur is not None:
            m = m_ref[cur, 0:1, :]
            acc = [jnp.zeros((DIL_VT_ROWS, tq), F32) for _ in range(2)]
        for t, (lo, hi) in enumerate(tiles):
            rows = pl.ds(q0n + lo, kt)
            km = kmask[rows, :]
            bias = bias_ref[lo:hi, :] + jnp.concatenate([km, km], axis=1)
            s = _dot(kpad[rows, :], qt2) + jnp.concatenate([bias, bias], axis=1)
            s_refs[nxt][lo:hi, :] = s
            cm = jnp.max(s, axis=0, keepdims=True)
            mx = cm if mx is None else jnp.maximum(mx, cm)
            if pair_cur is not None:
                p = jnp.exp2((s_refs[cur][lo:hi, :] - m).astype(BF16))
                for c in range(2):
                    acc[c] = acc[c] + _dot(vt_ref[i * (tq // kt) + t, 2 * pair_cur + c],
                                           p[:, c * tq:(c + 1) * tq])
        m_ref[nxt] = jnp.broadcast_to(mx, m_ref.shape[1:])
        if pair_cur is not None:
            return [o[0:hd] * (1.0 / o[hd:hd + 1]) for o in acc]

    @pl.when(i == 0)
    def _():
        stage(q_ref, i, 0, 0, None, None)

    heads = stage(q_ref, i, 1, 1, 0, 0)
    heads += stage(qn_ref, jnp.minimum(i + 1, nq - 1), 0, 0, 1, 1)
    o_ref[...] = jnp.concatenate(heads, axis=0).T.astype(BF16)


def _dilattn(h_main, bias, B, L):
    nq = L // DIL_Q_TILE
    w = DIL_Q_TILE + 2 * DIL_PAD
    return pl.pallas_call(
        functools.partial(_dilattn_kernel, L=L),
        grid=(B, nq),
        in_specs=[
            pl.BlockSpec((DIL_Q_TILE, GROUP_W), lambda b, i: (b * nq + i, BLK_AQ)),
            pl.BlockSpec((DIL_Q_TILE, GROUP_W),
                         lambda b, i: (b * nq + jnp.minimum(i + 1, nq - 1), BLK_AQ)),
            pl.BlockSpec((L, GROUP_W), lambda b, i: (b, BLK_AK)),
            pl.BlockSpec((L, GROUP_W), lambda b, i: (b, BLK_AV)),
            pl.BlockSpec((w, DIL_Q_TILE), lambda b, i: (0, 0)),
        ],
        out_specs=pl.BlockSpec((DIL_Q_TILE, GROUP_W), lambda b, i: (b * nq + i, 0)),
        out_shape=jax.ShapeDtypeStruct((B * L, GROUP_W), BF16),
        scratch_shapes=[pltpu.VMEM((L + 2 * DIL_PAD, GROUP_W), BF16),
                        pltpu.VMEM(((L + 2 * DIL_PAD) // DIL_K_TILE, N_HEADS, DIL_VT_ROWS,
                                    DIL_K_TILE), BF16),
                        pltpu.VMEM((L + 2 * DIL_PAD, LANE), F32),
                        pltpu.VMEM((w, 2 * DIL_Q_TILE), F32),
                        pltpu.VMEM((w, 2 * DIL_Q_TILE), F32),
                        pltpu.VMEM((2, 8, 2 * DIL_Q_TILE), F32)],
        compiler_params=_cparams(("parallel", "arbitrary")),
        name="dilattn",
    )(h_main, h_main, h_main, h_main, bias)


def _split3(x):
    hi = x.astype(BF16)
    r1 = x - hi.astype(F32)
    mid = r1.astype(BF16)
    lo = (r1 - mid.astype(F32)).astype(BF16)
    return hi, mid, lo


def _split_dot(x, mat):
    return sum(_dot(t, mat) for t in _split3(x))


def _tri_dot(mat, x):
    return sum(_dot(mat, t) for t in _split3(x))


def _dot_f32(a, b):
    ah, am, al = _split3(a)
    bh, bm, bl = _split3(b)
    return (_dot(ah, bh) + (_dot(ah, bm) + _dot(am, bh))
            + (_dot(ah, bl) + _dot(am, bm) + _dot(al, bh)))


def _diffattn_kernel(q_ref, qn_ref, k_ref, v_ref, lam_ref, g_ref, hmean_ref, o_ref,
                     vt_ref, sa_ref, sb_ref, m_ref, oh_ref, *, lam_init, L):
    hd = HEAD_DIM
    first_tile = pl.program_id(1) == 0

    @pl.when(first_tile)
    def _():
        tail = (lax.broadcasted_iota(jnp.int32, (DIFF_VT_ROWS - hd, DIFF_K_CHUNK), 0) == 0)
        for c in range(L // DIFF_K_CHUNK):
            lo, hi = c * DIFF_K_CHUNK, (c + 1) * DIFF_K_CHUNK
            vt = v_ref[lo:hi, :].astype(F32).T.astype(BF16)
            for h in range(N_HEADS):
                vt_ref[h, 0:hd, lo:hi] = vt[h * hd:(h + 1) * hd]
                vt_ref[h, hd:DIFF_VT_ROWS, lo:hi] = tail.astype(BF16)

    lp = lam_ref[...]
    lam = (jnp.exp(jnp.sum(lp[0:1] * lp[1:2], axis=-1, keepdims=True))
           - jnp.exp(jnp.sum(lp[2:3] * lp[3:4], axis=-1, keepdims=True)) + lam_init)
    qt = q_ref[...].astype(F32).T
    qt_next = qn_ref[...].astype(F32).T
    feat_group = lax.broadcasted_iota(jnp.int32, (GROUP_W, 1), 0) // DIFF_QK_DIM
    tq = qt.shape[1]
    chunks = [(c * DIFF_K_CHUNK, (c + 1) * DIFF_K_CHUNK) for c in range(L // DIFF_K_CHUNK)]

    def masked_qt(h):
        wraps = h == N_HEADS
        src = jnp.where(wraps, qt_next, qt)
        hh = jnp.where(wraps, 0, h)
        return jnp.concatenate([jnp.where(feat_group == 2 * hh + c, src, 0.0) for c in range(2)],
                               axis=1).astype(BF16)

    def stage(h_next, nxt, h_cur, cur):
        qt2 = masked_qt(h_next)
        mx = None
        if h_cur is not None:
            m = m_ref[cur, 0:1, :]
            o = jnp.zeros((DIFF_VT_ROWS, 2 * tq), F32)
        for lo, hi in chunks:
            s = _dot(k_ref[lo:hi, :], qt2)
            s_refs[nxt][lo:hi, :] = s
            cm = jnp.max(s, axis=0, keepdims=True)
            mx = cm if mx is None else jnp.maximum(mx, cm)
            if h_cur is not None:
                p = jnp.exp2((s_refs[cur][lo:hi, :] - m).astype(BF16))
                o = o + _dot(vt_ref[h_cur, :, lo:hi], p)
        m_ref[nxt] = jnp.broadcast_to(mx, m_ref.shape[1:])
        if h_cur is not None:
            on = o[0:hd] * (1.0 / o[hd:hd + 1])
            oh_ref[h_cur] = on[:, 0:tq] - on[:, tq:2 * tq] * lam

    s_refs = (sa_ref, sb_ref)

    @pl.when(first_tile)
    def _():
        stage(0, 0, None, None)

    def body(j, carry):
        stage(2 * j + 1, 1, 2 * j, 0)
        stage(2 * j + 2, 0, 2 * j + 1, 1)
        return carry

    lax.fori_loop(0, N_HEADS // 2, body, 0)
    acc = jnp.concatenate([oh_ref[h] for h in range(N_HEADS)], axis=0).T
    ms = _split_dot(acc * acc, hmean_ref[...])
    y = acc * lax.rsqrt(ms + DIFF_SUBLN_EPS) * g_ref[...] * (1.0 - lam_init)
    o_ref[...] = y.astype(BF16)


def _diffattn(h_main, lam_p, subln_g, hmean, lam_init, B, L):
    nq = L // DIFF_Q_TILE
    return pl.pallas_call(
        functools.partial(_diffattn_kernel, lam_init=lam_init, L=L),
        grid=(B, nq),
        in_specs=[
            pl.BlockSpec((DIFF_Q_TILE, GROUP_W), lambda b, i: (b * nq + i, BLK_BQ)),
            pl.BlockSpec((DIFF_Q_TILE, GROUP_W),
                         lambda b, i: (b * nq + jnp.minimum(i + 1, nq - 1), BLK_BQ)),
            pl.BlockSpec((L, GROUP_W), lambda b, i: (b, BLK_BK)),
            pl.BlockSpec((L, GROUP_W), lambda b, i: (b, BLK_BV)),
            pl.BlockSpec((4, DIFF_QK_DIM), lambda b, i: (0, 0)),
            pl.BlockSpec((1, GROUP_W), lambda b, i: (0, 0)),
            pl.BlockSpec((GROUP_W, GROUP_W), lambda b, i: (0, 0)),
        ],
        out_specs=pl.BlockSpec((DIFF_Q_TILE, GROUP_W), lambda b, i: (b * nq + i, 0)),
        out_shape=jax.ShapeDtypeStruct((B * L, GROUP_W), BF16),
        scratch_shapes=[pltpu.VMEM((N_HEADS, DIFF_VT_ROWS, L), BF16),
                        pltpu.VMEM((L, 2 * DIFF_Q_TILE), F32),
                        pltpu.VMEM((L, 2 * DIFF_Q_TILE), F32),
                        pltpu.VMEM((2, 8, 2 * DIFF_Q_TILE), F32),
                        pltpu.VMEM((N_HEADS, HEAD_DIM, DIFF_Q_TILE), F32)],
        compiler_params=_cparams(("parallel", "arbitrary")),
        name="diffattn",
    )(h_main, h_main, h_main, h_main, lam_p, subln_g, hmean)


def _log_sigmoid(v):
    return jnp.minimum(v, 0.0) - jnp.log(1.0 + jnp.exp(-jnp.abs(v)))


def _mlstm_kernel(q_ref, k_ref, v_ref, gi_ref, gf_ref, cw_ref, cb_ref, bi_ref, bf_ref,
                  ltri_ref, utri_ref, ecol_ref, elane_ref, kmask_ref, vmask_ref, ones_ref,
                  bd_ref, hsum_ref, o_ref,
                  qpad, kpad, qs, ks, hfw, hbw, cst, nst, mst, *, L):
    T = MLSTM_T
    nc = L // T
    halo = CONV_HALO
    nh = N_HEADS

    zpad = jnp.zeros((halo, GROUP_W), BF16)
    for pad, src in ((qpad, q_ref), (kpad, k_ref)):
        pad[0:halo, :] = zpad
        pad[halo + L:halo + L + halo, :] = zpad
        pad[halo:halo + L, :] = src[...]

    def conv_body(c, carry):
        r0 = pl.multiple_of(c * T, T)
        for idx, (pad, dst, scale) in enumerate(((qpad, qs, 1.0),
                                                 (kpad, ks, 1.0 / math.sqrt(HEAD_DIM)))):
            xw = pad[pl.ds(r0, T + 2 * halo), :].astype(F32)
            xm = pltpu.roll(xw, 1, axis=0)[halo:halo + T]
            xp = pltpu.roll(xw, T + 2 * halo - 1, axis=0)[halo:halo + T]
            xc = xw[halo:halo + T]
            lo, hi = idx * GROUP_W, (idx + 1) * GROUP_W
            y = (xm * cw_ref[0:1, lo:hi] + xc * cw_ref[1:2, lo:hi] + xp * cw_ref[2:3, lo:hi]
                 + cb_ref[0:1, lo:hi])
            dst[pl.ds(r0, T), :] = (_silu(y) * scale).astype(BF16)
        return carry

    lax.fori_loop(0, nc, conv_body, 0)

    cst[...] = jnp.zeros(cst.shape, F32)
    nst[...] = jnp.zeros(nst.shape, F32)
    mst[...] = jnp.zeros(mst.shape, F32)

    is_fw = lax.broadcasted_iota(jnp.int32, (1, LANE), 1) < nh
    rowi = lax.broadcasted_iota(jnp.int32, (T, LANE), 0)
    row4 = lax.broadcasted_iota(jnp.int32, (T, nh * T), 0)
    col4 = lax.broadcasted_iota(jnp.int32, (T, nh * T), 1) % T
    causal = (col4 <= row4, col4 >= row4)

    def body(c, carry):
        rows = (pl.multiple_of(c * T, T), pl.multiple_of((nc - 1 - c) * T, T))
        gate_i = jnp.where(is_fw, gi_ref[pl.ds(rows[0], T), :], gi_ref[pl.ds(rows[1], T), :])
        gate_f = jnp.where(is_fw, gf_ref[pl.ds(rows[0], T), :], gf_ref[pl.ds(rows[1], T), :])
        gate_i = gate_i + bi_ref[...]
        parts = _split3(_log_sigmoid(gate_f + bf_ref[...]))
        pre = sum(_dot(ltri_ref[...], t) for t in parts)
        suf = sum(_dot(utri_ref[...], t) for t in parts)
        cum = jnp.where(is_fw, pre, suf)
        b_end = jnp.where(is_fw, pre[T - 1:T, :], suf[0:1, :])
        key_w = gate_i - cum

        pmax, smax = key_w, key_w
        sh = 1
        while sh < T:
            pmax = jnp.maximum(pmax, jnp.where(rowi >= sh, pltpu.roll(pmax, sh, axis=0), NEG_INF))
            smax = jnp.maximum(smax, jnp.where(rowi < T - sh, pltpu.roll(smax, T - sh, axis=0),
                                               NEG_INF))
            sh *= 2
        m_prev = mst[0:1, :]
        inter = cum + m_prev
        m_t = jnp.maximum(inter, cum + jnp.where(is_fw, pmax, smax))
        u = cum - m_t
        a = b_end + key_w
        m_new = jnp.maximum(b_end + m_prev, jnp.max(a, axis=0, keepdims=True))
        mst[0:1, :] = m_new
        stack = jnp.concatenate(
            [jnp.exp(inter - m_t), jnp.exp(-m_t), jnp.exp(a - m_new),
             jnp.broadcast_to(jnp.exp(b_end + m_prev - m_new), (8, LANE))], axis=0)
        st_hi, st_mid, _ = _split3(stack)
        key_w_t = key_w.T

        for d in range(2):
            r0 = rows[d]
            qc = qs[pl.ds(r0, T), :]
            kc = ks[pl.ds(r0, T), :]
            vc = v_ref[pl.ds(r0, T), :]
            ex = _dot(st_hi, elane_ref[d]) + _dot(st_mid, elane_ref[d])
            w_inter, floor, w_key = ex[0:T], ex[T:2 * T], ex[2 * T:3 * T]
            sp_row = ex[3 * T:3 * T + 1]

            r_row = jnp.concatenate([key_w_t[d * nh + h:d * nh + h + 1, :] for h in range(nh)],
                                    axis=1)
            dlog = _split_dot(u, ecol_ref[d]) + r_row
            decay = jnp.exp(jnp.where(causal[d], dlog, NEG_INF))

            kt = kc.astype(F32).T.astype(BF16)
            qk = _dot(qc, jnp.concatenate([kt] * nh, axis=1) * kmask_ref[...]) * decay
            qk_hi = qk.astype(BF16)
            qk_lo = (qk - qk_hi.astype(F32)).astype(BF16)
            vblk = jnp.concatenate([vc] * nh, axis=0) * vmask_ref[...]
            num = _dot(qk_hi, vblk)
            den = _dot(qk_hi, ones_ref[...]) + _dot(qk_lo, ones_ref[...])

            c_prev = cst[d]
            n_prev = nst[d:d + 1, :]
            num = num + w_inter * _dot(qc, c_prev.astype(BF16))
            den = den + w_inter * _dot((qc.astype(F32) * n_prev).astype(BF16), hsum_ref[...])
            h_out = num / jnp.maximum(jnp.abs(den), floor)
            if d == 0:
                hfw[pl.ds(r0, T), :] = h_out
            else:
                hbw[pl.ds(r0, T), :] = h_out

            kw = kc.astype(F32) * w_key
            cst[d] = c_prev * sp_row + _dot_tn(kw.astype(BF16), vc) * bd_ref[...]
            nst[d:d + 1, :] = n_prev * sp_row + jnp.sum(kw, axis=0, keepdims=True)
        return carry

    lax.fori_loop(0, nc, body, 0)
    o_ref[...] = (hfw[...] + hbw[...]).astype(BF16)


def _mlstm_consts():
    T, nh, g = MLSTM_T, N_HEADS, GROUP_W
    tri = np.tril(np.ones((T, T), np.float32))
    src = np.arange(LANE)[:, None]
    ecol = np.stack([(src == d * nh + np.arange(nh * T)[None, :] // T) for d in range(2)])
    elane = np.stack([(src == d * nh + np.arange(g)[None, :] // HEAD_DIM) for d in range(2)])
    head_of = np.arange(g) // HEAD_DIM
    blk_of = np.arange(nh * T) // T
    kmask = head_of[:, None] == blk_of[None, :]
    vmask = blk_of[:, None] == head_of[None, :]
    same_head = head_of[:, None] == head_of[None, :]
    b16 = lambda a: jnp.asarray(a.astype(np.float32), BF16)
    return (b16(tri), b16(tri.T), b16(ecol), b16(elane), b16(kmask), b16(vmask), b16(vmask),
            jnp.asarray(same_head.astype(np.float32)), b16(same_head))


def _mlstm(h_main, gate_i, gate_f, conv_w, conv_b, bias_i, bias_f, consts, B, L):
    T = MLSTM_T
    const = lambda b: (0, 0)
    const3 = lambda b: (0, 0, 0)
    return pl.pallas_call(
        functools.partial(_mlstm_kernel, L=L),
        grid=(B,),
        in_specs=[
            pl.BlockSpec((L, GROUP_W), lambda b: (b, BLK_DQ)),
            pl.BlockSpec((L, GROUP_W), lambda b: (b, BLK_DK)),
            pl.BlockSpec((L, GROUP_W), lambda b: (b, BLK_DV)),
            pl.BlockSpec((L, LANE), lambda b: (b, 0)),
            pl.BlockSpec((L, LANE), lambda b: (b, 0)),
            pl.BlockSpec((3, 2 * GROUP_W), const),
            pl.BlockSpec((1, 2 * GROUP_W), const),
            pl.BlockSpec((1, LANE), const),
            pl.BlockSpec((1, LANE), const),
            pl.BlockSpec((T, T), const),
            pl.BlockSpec((T, T), const),
            pl.BlockSpec((2, LANE, N_HEADS * T), const3),
            pl.BlockSpec((2, LANE, GROUP_W), const3),
            pl.BlockSpec((GROUP_W, N_HEADS * T), const),
            pl.BlockSpec((N_HEADS * T, GROUP_W), const),
            pl.BlockSpec((N_HEADS * T, GROUP_W), const),
            pl.BlockSpec((GROUP_W, GROUP_W), const),
            pl.BlockSpec((GROUP_W, GROUP_W), const),
        ],
        out_specs=pl.BlockSpec((L, GROUP_W), lambda b: (b, 0)),
        out_shape=jax.ShapeDtypeStruct((B * L, GROUP_W), BF16),
        scratch_shapes=[
            pltpu.VMEM((L + 2 * CONV_HALO, GROUP_W), BF16),
            pltpu.VMEM((L + 2 * CONV_HALO, GROUP_W), BF16),
            pltpu.VMEM((L, GROUP_W), BF16),
            pltpu.VMEM((L, GROUP_W), BF16),
            pltpu.VMEM((L, GROUP_W), F32),
            pltpu.VMEM((L, GROUP_W), F32),
            pltpu.VMEM((2, GROUP_W, GROUP_W), F32),
            pltpu.VMEM((8, GROUP_W), F32),
            pltpu.VMEM((8, LANE), F32),
        ],
        compiler_params=_cparams(("parallel",)),
        name="mlstm",
    )(h_main, h_main, h_main, gate_i, gate_f, conv_w, conv_b, bias_i, bias_f, *consts)


def _hy_dims(L):
    h1 = L // HY_N2
    k1h = h1 + 1
    k1p = -(-k1h // BF16_SUBLANES) * BF16_SUBLANES
    kg = 12 if k1p % 12 == 0 else BF16_SUBLANES
    return h1, k1h, k1p, kg


def _hy_tables(L):
    h1, k1h, k1p, _ = _hy_dims(L)
    n1_len = 2 * h1
    n = 2 * L
    k1 = np.arange(k1p)[:, None].astype(np.float64)
    live = (np.arange(k1p) < k1h)[:, None]
    n1 = np.arange(h1)[None, :].astype(np.float64)
    ang = 2.0 * np.pi * k1 * n1 / n1_len
    m1 = np.concatenate([np.where(live, np.cos(ang), 0.0), np.where(live, -np.sin(ang), 0.0)], 0)

    n2 = np.arange(HY_N2)[None, None, :].astype(np.float64)
    k2 = np.arange(HY_N2)[None, :, None].astype(np.float64)
    kk = np.arange(k1p)[:, None, None] + n1_len * k2
    th = 2.0 * np.pi * kk * n2 / n
    fr, fi = np.cos(th), -np.sin(th)
    f3 = np.concatenate([np.concatenate([fr, -fi], 2), np.concatenate([fi, fr], 2)], 1)
    er, ei = np.transpose(np.cos(th), (0, 2, 1)), np.transpose(np.sin(th), (0, 2, 1))
    f3i = np.concatenate([np.concatenate([er, -ei], 2), np.concatenate([ei, er], 2)], 1)
    live3 = (np.arange(k1p) < k1h)[:, None, None]
    f3 = np.where(live3, f3, 0.0)
    f3i = np.where(live3, f3i, 0.0)

    nn1 = np.arange(h1)[:, None].astype(np.float64)
    kc = np.arange(k1p)[None, :].astype(np.float64)
    ph = 2.0 * np.pi * nn1 * kc / n1_len
    edge = (np.arange(k1p) == 0) | (np.arange(k1p) == h1)
    livec = (np.arange(k1p) < k1h)[None, :]
    m4r = np.where(livec, np.where(edge[None, :], np.cos(ph), 2.0 * np.cos(ph)), 0.0) / n
    m4i = np.where(livec & ~edge[None, :], -2.0 * np.sin(ph), 0.0) / n
    return (jnp.asarray(m1, BF16), jnp.asarray(f3, BF16), jnp.asarray(f3i, BF16),
            jnp.asarray(m4r, BF16), jnp.asarray(m4i, BF16))


def _to_wide(x, h1):
    return x.astype(F32).reshape(h1, HY_N2 * GROUP_W)


def _hy_stage1(z_bf16, m1_ref, a_ref, k1p):
    a = _dot(m1_ref[...], z_bf16)
    a_ref[0, 0] = a[0:k1p].reshape(k1p, HY_N2, GROUP_W).astype(BF16)
    a_ref[0, 1] = a[k1p:2 * k1p].reshape(k1p, HY_N2, GROUP_W).astype(BF16)


def _hy_conv3_wide(x, w_ref, b_ref, j, h1):
    c = GROUP_W
    wl = HY_N2 * c
    rowi = lax.broadcasted_iota(jnp.int32, (h1, c), 0)
    tail = x[:, wl - c:wl]
    head = x[:, 0:c]
    prev_tail = jnp.where(rowi == 0, 0.0, pltpu.roll(tail, 1, axis=0))
    next_head = jnp.where(rowi == h1 - 1, 0.0, pltpu.roll(head, h1 - 1, axis=0))
    xm = jnp.concatenate([prev_tail, x[:, 0:wl - c]], axis=1)
    xp = jnp.concatenate([x[:, c:wl], next_head], axis=1)
    return (xm * w_ref[3 * j:3 * j + 1, :] + x * w_ref[3 * j + 1:3 * j + 2, :]
            + xp * w_ref[3 * j + 2:3 * j + 3, :] + b_ref[j:j + 1, :])


def _hy_front_kernel(v_ref, x1_ref, x2_ref, w_ref, b_ref, m1_ref,
                     z_ref, x1c_ref, x2c_ref, a_ref, *, h1, k1p):
    z = _hy_conv3_wide(_to_wide(v_ref[...], h1), w_ref, b_ref, 0, h1).astype(BF16)
    z_ref[0] = z
    x1c_ref[0] = _hy_conv3_wide(_to_wide(x1_ref[...], h1), w_ref, b_ref, 1, h1).astype(BF16)
    x2c_ref[0] = _hy_conv3_wide(_to_wide(x2_ref[...], h1), w_ref, b_ref, 2, h1).astype(BF16)
    _hy_stage1(z, m1_ref, a_ref, k1p)


def _hy_front(cv, cx1, cx2, w_wide, b_wide, m1, B, L):
    h1, _, k1p, _ = _hy_dims(L)
    wl = HY_N2 * GROUP_W
    nat = pl.BlockSpec((L, GROUP_W), lambda b: (b, 0))
    seq = pl.BlockSpec((1, h1, wl), lambda b: (b, 0, 0))
    wide = jax.ShapeDtypeStruct((B, h1, wl), BF16)
    slab = (1, 2, k1p, HY_N2, GROUP_W)
    return pl.pallas_call(
        functools.partial(_hy_front_kernel, h1=h1, k1p=k1p),
        grid=(B,),
        in_specs=[nat, nat, nat,
                  pl.BlockSpec((9, wl), lambda b: (0, 0)),
                  pl.BlockSpec((3, wl), lambda b: (0, 0)),
                  pl.BlockSpec((2 * k1p, h1), lambda b: (0, 0))],
        out_specs=(seq, seq, seq, pl.BlockSpec(slab, lambda b: (b, 0, 0, 0, 0))),
        out_shape=(wide, wide, wide,
                   jax.ShapeDtypeStruct((B, 2, k1p, HY_N2, GROUP_W), BF16)),
        compiler_params=_cparams(("parallel",)),
        name="hy_front",
    )(cv, cx1, cx2, w_wide, b_wide, m1)


def _hy_stage1_kernel(z_ref, m1_ref, a_ref, *, h1, k1p):
    _hy_stage1(_to_wide(z_ref[0], h1).astype(BF16), m1_ref, a_ref, k1p)


def _hy_stage1_call(z_nat, m1, L):
    h1, _, k1p, _ = _hy_dims(L)
    nb = z_nat.shape[0]
    return pl.pallas_call(
        functools.partial(_hy_stage1_kernel, h1=h1, k1p=k1p),
        grid=(nb,),
        in_specs=[pl.BlockSpec((1, L, GROUP_W), lambda b: (b, 0, 0)),
                  pl.BlockSpec((2 * k1p, h1), lambda b: (0, 0))],
        out_specs=pl.BlockSpec((1, 2, k1p, HY_N2, GROUP_W), lambda b: (b, 0, 0, 0, 0)),
        out_shape=jax.ShapeDtypeStruct((nb, 2, k1p, HY_N2, GROUP_W), BF16),
        compiler_params=_cparams(("parallel",)),
        name="hy_stage1",
    )(z_nat, m1)


def _hy_spectrum_kernel(af_ref, ab_ref, f3_ref, g_ref, *, kg, k1h):
    grp = pl.program_id(0)

    @pl.when(grp * kg < k1h)
    def _():
        for kk in range(kg):
            xs = []
            for a_ref in (af_ref, ab_ref):
                a2 = jnp.concatenate([a_ref[0, 0, kk], a_ref[0, 1, kk]], axis=0)
                xs.append(_dot(f3_ref[kk], a2))
            g_ref[0, 0, kk] = xs[0][0:HY_N2] + xs[1][0:HY_N2]
            g_ref[0, 1, kk] = xs[0][HY_N2:2 * HY_N2] - xs[1][HY_N2:2 * HY_N2]

    @pl.when(grp * kg >= k1h)
    def _():
        g_ref[...] = jnp.zeros(g_ref.shape, F32)


def _hy_spectrum(a_filt, f3, L):
    _, k1h, k1p, kg = _hy_dims(L)
    a5 = a_filt
    blk = (1, 2, kg, HY_N2, GROUP_W)
    return pl.pallas_call(
        functools.partial(_hy_spectrum_kernel, kg=kg, k1h=k1h),
        grid=(k1p // kg, HYENA_ORDER),
        in_specs=[pl.BlockSpec(blk, lambda g, o: (2 * o, 0, g, 0, 0)),
                  pl.BlockSpec(blk, lambda g, o: (2 * o + 1, 0, g, 0, 0)),
                  pl.BlockSpec((kg, 2 * HY_N2, 2 * HY_N2), lambda g, o: (g, 0, 0))],
        out_specs=pl.BlockSpec(blk, lambda g, o: (o, 0, g, 0, 0)),
        out_shape=jax.ShapeDtypeStruct((HYENA_ORDER, 2, k1p, HY_N2, GROUP_W), F32),
        compiler_params=_cparams(("parallel", "parallel")),
        name="hy_spectrum",
    )(a5, a5, f3)


def _hy_mid_kernel(a_ref, g_ref, f3_ref, f3i_ref, b_ref, *, kg, k1h):
    grp = pl.program_id(0)

    @pl.when(grp * kg < k1h)
    def _():
        for kk in range(kg):
            a2 = jnp.concatenate([a_ref[0, 0, kk], a_ref[0, 1, kk]], axis=0)
            x = _dot(f3_ref[kk], a2)
            xr, xi = x[0:HY_N2], x[HY_N2:2 * HY_N2]
            gr, gi = g_ref[0, 0, kk], g_ref[0, 1, kk]
            y2 = jnp.concatenate([xr * gr - xi * gi, xr * gi + xi * gr], axis=0).astype(BF16)
            bm = _dot(f3i_ref[kk], y2)
            b_ref[0, 0, kk] = bm[0:HY_N2].astype(BF16)
            b_ref[0, 1, kk] = bm[HY_N2:2 * HY_N2].astype(BF16)

    @pl.when(grp * kg >= k1h)
    def _():
        b_ref[...] = jnp.zeros(b_ref.shape, BF16)


def _hy_mid(a, g_spec, order, f3, f3i, B, L):
    _, k1h, k1p, kg = _hy_dims(L)
    blk = (1, 2, kg, HY_N2, GROUP_W)
    tab = pl.BlockSpec((kg, 2 * HY_N2, 2 * HY_N2), lambda g, b: (g, 0, 0))
    return pl.pallas_call(
        functools.partial(_hy_mid_kernel, kg=kg, k1h=k1h),
        grid=(k1p // kg, B),
        in_specs=[pl.BlockSpec(blk, lambda g, b: (b, 0, g, 0, 0)),
                  pl.BlockSpec(blk, lambda g, b: (order, 0, g, 0, 0)),
                  tab, tab],
        out_specs=pl.BlockSpec(blk, lambda g, b: (b, 0, g, 0, 0)),
        out_shape=jax.ShapeDtypeStruct((B, 2, k1p, HY_N2, GROUP_W), BF16),
        compiler_params=_cparams(("parallel", "parallel")),
        name="hy_mid",
    )(a, g_spec, f3, f3i)


def _hy_back_kernel(b_ref, z_ref, x_ref, bias_ref, m4r_ref, m4i_ref, *rest, k1p, last):
    wl = HY_N2 * GROUP_W
    br = b_ref[0, 0].astype(F32).reshape(k1p, wl).astype(BF16)
    bi = b_ref[0, 1].astype(F32).reshape(k1p, wl).astype(BF16)
    y = _dot(m4r_ref[...], br) + _dot(m4i_ref[...], bi)
    z_new = x_ref[0].astype(F32) * (y + z_ref[0].astype(F32) * bias_ref[...])
    zb = z_new.astype(BF16)
    if last:
        (o_ref,) = rest
        o_ref[...] = z_new.reshape(o_ref.shape).astype(BF16)
    else:
        m1_ref, o_ref, a_ref = rest
        o_ref[0] = zb
        _hy_stage1(zb, m1_ref, a_ref, k1p)


def _hy_back(b, z, xg, bias_wide, m4r, m4i, m1, B, L, last):
    h1, _, k1p, _ = _hy_dims(L)
    wl = HY_N2 * GROUP_W
    seq = pl.BlockSpec((1, h1, wl), lambda i: (i, 0, 0))
    slab = pl.BlockSpec((1, 2, k1p, HY_N2, GROUP_W), lambda i: (i, 0, 0, 0, 0))
    const = lambda i: (0, 0)
    in_specs = [slab, seq, seq, pl.BlockSpec((1, wl), const),
                pl.BlockSpec((h1, k1p), const), pl.BlockSpec((h1, k1p), const)]
    args = [b, z, xg, bias_wide, m4r, m4i]
    if last:
        out_specs = pl.BlockSpec((L, GROUP_W), lambda i: (i, 0))
        out_shape = jax.ShapeDtypeStruct((B * L, GROUP_W), BF16)
    else:
        in_specs.append(pl.BlockSpec((2 * k1p, h1), const))
        args.append(m1)
        out_specs = (seq, slab)
        out_shape = (jax.ShapeDtypeStruct((B, h1, wl), BF16),
                     jax.ShapeDtypeStruct((B, 2, k1p, HY_N2, GROUP_W), BF16))
    return pl.pallas_call(
        functools.partial(_hy_back_kernel, k1p=k1p, last=last),
        grid=(B,),
        in_specs=in_specs,
        out_specs=out_specs,
        out_shape=out_shape,
        compiler_params=_cparams(("parallel",)),
        name="hy_back_last" if last else "hy_back",
    )(*args)


def _hy_filter_kernel(f_ref, t_ref, w1_ref, b1_ref, w2_ref, b2_ref, w3_ref, fr_ref, ad_ref,
                      o_ref, *, tile):
    i = pl.program_id(0)
    freq = fr_ref[...]
    z = jnp.sin(freq * (_dot_f32(f_ref[...], w1_ref[...]) + b1_ref[...]))
    z = jnp.sin(freq * (_dot_f32(z, w2_ref[...]) + b2_ref[...]))
    decay = jnp.exp(-t_ref[...] * ad_ref[...])
    rowi = i * tile + lax.broadcasted_iota(jnp.int32, (tile, GROUP_W), 0)
    for j in range(2 * HYENA_ORDER):
        hj = _dot_f32(z, w3_ref[:, j * GROUP_W:(j + 1) * GROUP_W]) * decay
        if j % 2 == 1:
            hj = jnp.where(rowi == 0, 0.0, hj)
        o_ref[j] = hj.astype(BF16)


def _hy_filter(feats, tcol, w1, b1, w2, b2, w3, freq, absdelta, L):
    tile = math.gcd(L, ROW_TILE)
    const = lambda i: (0, 0)
    return pl.pallas_call(
        functools.partial(_hy_filter_kernel, tile=tile),
        grid=(L // tile,),
        in_specs=[pl.BlockSpec((tile, LANE), lambda i: (i, 0)),
                  pl.BlockSpec((tile, 1), lambda i: (i, 0)),
                  pl.BlockSpec((LANE, HYENA_HIDDEN), const),
                  pl.BlockSpec((1, HYENA_HIDDEN), const),
                  pl.BlockSpec((HYENA_HIDDEN, HYENA_HIDDEN), const),
                  pl.BlockSpec((1, HYENA_HIDDEN), const),
                  pl.BlockSpec((HYENA_HIDDEN, 2 * HYENA_ORDER * GROUP_W), const),
                  pl.BlockSpec((1, HYENA_HIDDEN), const),
                  pl.BlockSpec((1, GROUP_W), const)],
        out_specs=pl.BlockSpec((2 * HYENA_ORDER, tile, GROUP_W), lambda i: (0, i, 0)),
        out_shape=jax.ShapeDtypeStruct((2 * HYENA_ORDER, L, GROUP_W), BF16),
        compiler_params=_cparams(("parallel",)),
        name="hy_filter",
    )(feats, tcol, w1, b1, w2, b2, w3, freq, absdelta)


def _outproj_kernel(x_ref, oa_ref, ob_ref, oc_ref, od_ref, ox_ref,
                    ga_ref, gb_ref, gc_ref, gdo_ref, gdg_ref, gx_ref, w_ref, png_ref, out_ref):
    f = lambda r: r[...].astype(F32)
    branches = (
        f(oa_ref) * _silu(f(ga_ref)),
        f(ob_ref) * _silu(f(gb_ref)),
        f(oc_ref) * _silu(f(gc_ref)),
        f(od_ref) * _sigmoid(f(gdo_ref)) * _silu(f(gdg_ref)),
        f(ox_ref) * _silu(f(gx_ref)),
    )
    y = jnp.zeros(x_ref.shape, F32)
    for j, br in enumerate(branches):
        y = y + _dot(br.astype(BF16), w_ref[j * GROUP_W:(j + 1) * GROUP_W, :])
    ms = jnp.mean(y * y, axis=-1, keepdims=True)
    out_ref[...] = x_ref[...] + y * lax.rsqrt(ms + NORM_EPS) * png_ref[...]


def _outproj(x2d, outs, h_main, w_out, png, row0=0, n_rows=None):
    n_rows = x2d.shape[0] if n_rows is None else n_rows
    nt = n_rows // ROW_TILE
    t0 = row0 // ROW_TILE
    row = lambda i: (t0 + i, 0)
    blk = lambda j: pl.BlockSpec((ROW_TILE, GROUP_W), lambda i: (t0 + i, j))
    return pl.pallas_call(
        _outproj_kernel,
        grid=(nt,),
        in_specs=[pl.BlockSpec((ROW_TILE, D_MODEL), row)]
        + [pl.BlockSpec((ROW_TILE, GROUP_W), row)] * 5
        + [blk(BLK_AG), blk(BLK_BG), blk(BLK_CG), blk(BLK_DO), blk(BLK_DG), blk(BLK_XG)]
        + [pl.BlockSpec((5 * GROUP_W, D_MODEL), lambda i: (0, 0)),
           pl.BlockSpec((1, D_MODEL), lambda i: (0, 0))],
        out_specs=pl.BlockSpec((ROW_TILE, D_MODEL), lambda i: (i, 0)),
        out_shape=jax.ShapeDtypeStruct((n_rows, D_MODEL), F32),
        compiler_params=_cparams(("parallel",)),
        name="outproj",
    )(x2d, *outs, h_main, h_main, h_main, h_main, h_main, h_main, w_out, png)


def _rope_tables(L, group, rot_dim):
    half = rot_dim // 2
    inv = 1.0 / (ROPE_THETA ** (jnp.arange(0, rot_dim, 2, dtype=F32) / rot_dim))
    ang = jnp.arange(L, dtype=F32)[:, None] * inv[None, :]
    cos, sin = jnp.cos(ang), jnp.sin(ang)
    lane = np.arange(GROUP_W) % group
    in_rot = lane < rot_dim
    idx = lane % half
    c = jnp.where(in_rot[None, :], cos[:, idx], 1.0)
    s = jnp.where(in_rot[None, :], sin[:, idx], 0.0)
    p = np.zeros((GROUP_W, GROUP_W), np.float32)
    for j in range(GROUP_W):
        if lane[j] < half:
            p[j + half, j] = -1.0
        elif lane[j] < rot_dim:
            p[j - half, j] = 1.0
    return c, s, jnp.asarray(p, BF16)


def _hyena_features(L):
    t = jnp.linspace(0.0, 1.0, L, dtype=F32)[:, None]
    bands = jnp.linspace(1e-4, HYENA_BANDS - 1, HYENA_BANDS, dtype=F32)
    ang = (2.0 * math.pi / L) * jnp.arange(L, dtype=F32)[:, None] * bands[None, :]
    feats = jnp.concatenate([t, jnp.cos(ang), -jnp.sin(ang)], axis=-1)
    return jnp.pad(feats, ((0, 0), (0, LANE - HYENA_EMB))), t


def _relayout_w_in(w):
    g = GROUP_W
    off_c, off_d = 8 * g, 12 * g
    off_gate = off_d + 5 * g
    off_x = off_gate + N_MLSTM_GATES
    main = jnp.concatenate([w[:, 0:off_c], w[:, off_c + 3 * g:off_c + 4 * g],
                            w[:, off_d:off_gate], w[:, off_x:off_x + 2 * g]], axis=1)
    hy = w[:, off_c:off_c + 3 * g]
    gate_i, gate_f = _split_gates(w[:, off_gate:off_x])
    return jnp.concatenate([main, hy, gate_i, gate_f], axis=1).astype(BF16)


def _split_gates(t):
    nh = N_HEADS
    pad = [(0, 0)] * (t.ndim - 1) + [(0, LANE - 2 * nh)]
    gi = jnp.concatenate([t[..., 0:nh], t[..., 2 * nh:3 * nh]], axis=-1)
    gf = jnp.concatenate([t[..., nh:2 * nh], t[..., 3 * nh:4 * nh]], axis=-1)
    return jnp.pad(gi, pad), jnp.pad(gf, pad)


def _trunk(x, mem, splits, pre_norm_g, post_norm_g, w_in, w_out, diff_lambda, diff_subln_g,
           hy_conv_w, hy_conv_b, hy_ffn_w1, hy_ffn_b1, hy_ffn_w2, hy_ffn_b2, hy_ffn_w3,
           hy_freq, hy_bias, ml_conv_w, ml_conv_b, ml_gate_b, mem_norm_g, w_mem_kv):
    B, L, _ = x.shape
    M = mem.shape[1]
    depth = w_in.shape[0]
    g = GROUP_W
    h1, _, k1p, _ = _hy_dims(L)
    wl = HY_N2 * g

    ca, sa, pa = _rope_tables(L, HEAD_DIM, HEAD_DIM // ROPE_FRACTION)
    cb, sb, pb = _rope_tables(L, DIFF_QK_DIM, DIFF_QK_DIM // ROPE_FRACTION)
    rope_tabs = (ca, sa, cb, sb, pa, pb)
    dil_bias = jnp.asarray(_dil_bias_table())
    feats, tcol = _hyena_features(L)
    absdelta = jnp.abs(jnp.linspace(math.log(HYENA_TARGET) / HYENA_SLOW_DECAY,
                                    math.log(HYENA_TARGET) / HYENA_FAST_DECAY, g, dtype=F32))[None]
    m1, f3, f3i, m4r, m4i = _hy_tables(L)
    head_of = np.arange(g) // HEAD_DIM
    same_head = (head_of[:, None] == head_of[None, :]).astype(np.float32)
    hmean = jnp.asarray(same_head / HEAD_DIM, BF16)
    mlstm_consts = _mlstm_consts()

    x2d = x.reshape(B * L, D_MODEL)
    mem2d = mem.reshape(B * M, D_MODEL)
    for li in range(depth):
        lam_init = 0.8 - 0.6 * math.exp(-0.3 * li)
        w_all = _relayout_w_in(w_in[li])
        h_main, cv, cx1, cx2, gate_i, gate_f = _inproj(x2d, pre_norm_g[li][None], w_all,
                                                       rope_tabs, B, L)

        oa = _dilattn(h_main, dil_bias, B, L)
        ob = _diffattn(h_main, diff_lambda[li], jnp.tile(diff_subln_g[li], N_HEADS)[None],
                       hmean, lam_init, B, L)

        w1 = jnp.pad(hy_ffn_w1[li], ((0, LANE - HYENA_EMB), (0, 0)))
        hfilt = _hy_filter(feats, tcol, w1, hy_ffn_b1[li][None], hy_ffn_w2[li],
                           hy_ffn_b2[li][None], hy_ffn_w3[li], hy_freq[li][None], absdelta, L)
        a_filt = _hy_stage1_call(hfilt, m1, L)
        g_spec = _hy_spectrum(a_filt, f3, L)
        cw = jnp.tile(hy_conv_w[li].reshape(3, 3, g).transpose(1, 0, 2).reshape(9, g), (1, HY_N2))
        cbw = jnp.tile(hy_conv_b[li].reshape(3, g), (1, HY_N2))
        z, x1c, x2c, a = _hy_front(cv, cx1, cx2, cw, cbw, m1, B, L)
        bias_w = jnp.tile(hy_bias[li], (1, HY_N2))
        bsp = _hy_mid(a, g_spec, 0, f3, f3i, B, L)
        z, a = _hy_back(bsp, z, x1c, bias_w[0:1], m4r, m4i, m1, B, L, last=False)
        bsp = _hy_mid(a, g_spec, 1, f3, f3i, B, L)
        oc = _hy_back(bsp, z, x2c, bias_w[1:2], m4r, m4i, m1, B, L, last=True)

        bias_i, bias_f = _split_gates(ml_gate_b[li][None])
        od = _mlstm(h_main, gate_i, gate_f, ml_conv_w[li], ml_conv_b[li][None], bias_i, bias_f,
                    mlstm_consts, B, L)

        mkv = _memkv(mem2d, mem_norm_g[li][None], w_mem_kv[li].astype(BF16))
        ox = _memattn(h_main, mkv, B, L, M)

        branch_outs = (oa, ob, oc, od, ox)
        w_o, png = w_out[li].astype(BF16), post_norm_g[li][None]
        if li + 1 < depth:
            x2d = _outproj(x2d, branch_outs, h_main, w_o, png)
    ys, b0 = [], 0
    for nb in splits:
        y = _outproj(x2d, branch_outs, h_main, w_o, png, row0=b0 * L, n_rows=nb * L)
        ys.append(y.reshape(nb, L, D_MODEL))
        b0 += nb
    return tuple(ys)


def kernel(x_prompt, x_sample, mem_prompt, mem_sample, pre_norm_g, post_norm_g, w_in, w_out,
           diff_lambda, diff_subln_g, hy_conv_w, hy_conv_b, hy_ffn_w1, hy_ffn_b1, hy_ffn_w2,
           hy_ffn_b2, hy_ffn_w3, hy_freq, hy_bias, ml_conv_w, ml_conv_b, ml_gate_b,
           mem_norm_g, w_mem_kv):
    x = jnp.concatenate([x_prompt, x_sample], axis=0)
    mem = jnp.concatenate([mem_prompt, mem_sample], axis=0)
    return _trunk(x, mem, (x_prompt.shape[0], x_sample.shape[0]), pre_norm_g, post_norm_g, w_in,
                  w_out, diff_lambda, diff_subln_g, hy_conv_w, hy_conv_b, hy_ffn_w1, hy_ffn_b1,
                  hy_ffn_w2, hy_ffn_b2, hy_ffn_w3, hy_freq, hy_bias, ml_conv_w, ml_conv_b,
                  ml_gate_b, mem_norm_g, w_mem_kv)
```

```python
import functools
import math

import numpy as np
import jax
import jax.numpy as jnp
from jax import lax
from jax.experimental import pallas as pl
from jax.experimental.pallas import tpu as pltpu

F32 = jnp.float32
BF16 = jnp.bfloat16

D_MODEL = 1024
HEAD_DIM = 64
GROUP_W = 256
N_HEADS = GROUP_W // HEAD_DIM
NORM_EPS = 1e-6
NEG_INF = -1e30
ROPE_THETA = 500000.0
ROPE_FRACTION = 4
DIL_PATTERNS = ((128, 1), (512, 4), (2048, 16))
DIFF_QK_DIM = HEAD_DIM // 2
DIFF_SUBLN_EPS = 1e-5
HYENA_ORDER = 2
HYENA_BANDS = 16
HYENA_EMB = 1 + 2 * HYENA_BANDS
HYENA_HIDDEN = 64
HYENA_FAST_DECAY = 0.3
HYENA_SLOW_DECAY = 1.5
HYENA_TARGET = 1e-2
N_MLSTM_GATES = 4 * N_HEADS

(BLK_AQ, BLK_AK, BLK_AV, BLK_AG, BLK_BQ, BLK_BK, BLK_BV, BLK_BG, BLK_CG,
 BLK_DQ, BLK_DK, BLK_DV, BLK_DO, BLK_DG, BLK_XQ, BLK_XG) = range(16)
N_MAIN_BLK = 16
MAIN_W = N_MAIN_BLK * GROUP_W
W_ALL = MAIN_W + 4 * GROUP_W

VMEM_LIMIT_BYTES = 56 * 1024 * 1024
LANE = 128
F32_SUBLANES = 8
BF16_SUBLANES = 16

ROW_TILE = 512
DIL_Q_TILE = 256
DIL_PAD = 1024
DIL_K_TILE = 256
DIL_VT_ROWS = HEAD_DIM + BF16_SUBLANES
DIFF_Q_TILE = 256
DIFF_K_CHUNK = 512
DIFF_VT_ROWS = HEAD_DIM + BF16_SUBLANES
LOG2E = 1.4426950408889634
MEM_Q_TILE = 512
MLSTM_T = 128
CONV_HALO = 16
HY_N2 = 128
HY_K1_GROUP = 12


def _cparams(sem, vmem=VMEM_LIMIT_BYTES):
    return pltpu.CompilerParams(dimension_semantics=sem, vmem_limit_bytes=vmem)


def _sigmoid(v):
    return 1.0 / (1.0 + jnp.exp(-v))


def _silu(v):
    return v * _sigmoid(v)


def _dot(a, b):
    return jnp.dot(a, b, preferred_element_type=F32)


def _dot_tn(a, b):
    return lax.dot_general(a, b, (((0,), (0,)), ((), ())), preferred_element_type=F32)


def _inproj_kernel(x_ref, g_ref, w_ref, ca_ref, sa_ref, cb_ref, sb_ref, pa_ref, pb_ref,
                   h_ref, cv_ref, cx1_ref, cx2_ref, gi_ref, gf_ref):
    x = x_ref[...]
    ms = jnp.mean(x * x, axis=-1, keepdims=True)
    xn = (x * lax.rsqrt(ms + NORM_EPS) * g_ref[...]).astype(BF16)

    def proj(j, width=GROUP_W):
        return _dot(xn, w_ref[:, j * GROUP_W:j * GROUP_W + width])

    def rope(acc, c_ref, s_ref, p_ref):
        partner = _dot(acc.astype(BF16), p_ref[...])
        return acc * c_ref[...] + partner * s_ref[...]

    def finish(j, acc):
        if j in (BLK_AQ, BLK_AK):
            acc = rope(acc, ca_ref, sa_ref, pa_ref)
        if j in (BLK_BQ, BLK_BK):
            acc = rope(acc, cb_ref, sb_ref, pb_ref)
        if j in (BLK_AQ, BLK_XQ):
            acc = acc * (LOG2E / math.sqrt(HEAD_DIM))
        if j == BLK_BQ:
            acc = acc * (LOG2E / math.sqrt(DIFF_QK_DIM))
        if j < N_MAIN_BLK:
            h_ref[:, j * GROUP_W:(j + 1) * GROUP_W] = acc.astype(BF16)
        elif j < N_MAIN_BLK + 3:
            (cv_ref, cx1_ref, cx2_ref)[j - N_MAIN_BLK][...] = acc.astype(BF16)
        else:
            gi_ref[...] = acc[:, 0:LANE]
            gf_ref[...] = acc[:, LANE:2 * LANE]

    for j in range(0, W_ALL // GROUP_W, 2):
        acc2 = proj(j, 2 * GROUP_W)
        finish(j, acc2[:, 0:GROUP_W])
        finish(j + 1, acc2[:, GROUP_W:2 * GROUP_W])


def _inproj(x2d, g, w_all, rope_tabs, B, L):
    ca, sa, cb, sb, pa, pb = rope_tabs
    nt = L // ROW_TILE
    n_tok = B * L
    row = lambda i, b: (b * nt + i, 0)
    tab = lambda i, b: (i, 0)
    const = lambda i, b: (0, 0)
    out_shapes = (
        jax.ShapeDtypeStruct((n_tok, MAIN_W), BF16),
        jax.ShapeDtypeStruct((n_tok, GROUP_W), BF16),
        jax.ShapeDtypeStruct((n_tok, GROUP_W), BF16),
        jax.ShapeDtypeStruct((n_tok, GROUP_W), BF16),
        jax.ShapeDtypeStruct((n_tok, LANE), F32),
        jax.ShapeDtypeStruct((n_tok, LANE), F32),
    )
    return pl.pallas_call(
        _inproj_kernel,
        grid=(nt, B),
        in_specs=[
            pl.BlockSpec((ROW_TILE, D_MODEL), row),
            pl.BlockSpec((1, D_MODEL), const),
            pl.BlockSpec((D_MODEL, W_ALL), const),
            pl.BlockSpec((ROW_TILE, GROUP_W), tab),
            pl.BlockSpec((ROW_TILE, GROUP_W), tab),
            pl.BlockSpec((ROW_TILE, GROUP_W), tab),
            pl.BlockSpec((ROW_TILE, GROUP_W), tab),
            pl.BlockSpec((GROUP_W, GROUP_W), const),
            pl.BlockSpec((GROUP_W, GROUP_W), const),
        ],
        out_specs=(
            pl.BlockSpec((ROW_TILE, MAIN_W), row),
            pl.BlockSpec((ROW_TILE, GROUP_W), row),
            pl.BlockSpec((ROW_TILE, GROUP_W), row),
            pl.BlockSpec((ROW_TILE, GROUP_W), row),
            pl.BlockSpec((ROW_TILE, LANE), row),
            pl.BlockSpec((ROW_TILE, LANE), row),
        ),
        out_shape=out_shapes,
        compiler_params=_cparams(("parallel", "parallel")),
        name="inproj",
    )(x2d, g, w_all, ca, sa, cb, sb, pa, pb)


def _memkv_kernel(m_ref, g_ref, w_ref, o_ref):
    x = m_ref[...]
    ms = jnp.mean(x * x, axis=-1, keepdims=True)
    xn = (x * lax.rsqrt(ms + NORM_EPS) * g_ref[...]).astype(BF16)
    o_ref[...] = _dot(xn, w_ref[...]).astype(BF16)


def _memkv(mem2d, g, w):
    rows = mem2d.shape[0]
    tile = math.gcd(rows, ROW_TILE)
    return pl.pallas_call(
        _memkv_kernel,
        grid=(rows // tile,),
        in_specs=[
            pl.BlockSpec((tile, D_MODEL), lambda i: (i, 0)),
            pl.BlockSpec((1, D_MODEL), lambda i: (0, 0)),
            pl.BlockSpec((D_MODEL, 2 * GROUP_W), lambda i: (0, 0)),
        ],
        out_specs=pl.BlockSpec((tile, 2 * GROUP_W), lambda i: (i, 0)),
        out_shape=jax.ShapeDtypeStruct((rows, 2 * GROUP_W), BF16),
        compiler_params=_cparams(("parallel",)),
        name="memkv",
    )(mem2d, g, w)


def _memattn_kernel(q_ref, mk_ref, mv_ref, o_ref):
    hd = HEAD_DIM
    tq = q_ref.shape[0]
    qt = q_ref[...].astype(F32).T
    feat_head = lax.broadcasted_iota(jnp.int32, (GROUP_W, 1), 0) // hd
    mk = mk_ref[...]
    mvt = mv_ref[...].astype(F32).T.astype(BF16)
    ones_rows = (lax.broadcasted_iota(jnp.int32, (BF16_SUBLANES, mk.shape[0]), 0) == 0)
    heads = []
    for pair in range(N_HEADS // 2):
        qt2 = jnp.concatenate([jnp.where(feat_head == 2 * pair + c, qt, 0.0) for c in range(2)],
                              axis=1).astype(BF16)
        s = _dot(mk, qt2)
        m = jnp.max(s, axis=0, keepdims=True)
        p = jnp.exp2((s - m).astype(BF16))
        for c in range(2):
            h = 2 * pair + c
            vt_aug = jnp.concatenate([mvt[h * hd:(h + 1) * hd], ones_rows.astype(BF16)], axis=0)
            o = _dot(vt_aug, p[:, c * tq:(c + 1) * tq])
            heads.append(o[0:hd] * (1.0 / o[hd:hd + 1]))
    o_ref[...] = jnp.concatenate(heads, axis=0).T.astype(BF16)


def _memattn(h_main, mkv, B, L, M):
    nq = L // MEM_Q_TILE
    return pl.pallas_call(
        _memattn_kernel,
        grid=(B, nq),
        in_specs=[
            pl.BlockSpec((MEM_Q_TILE, GROUP_W), lambda b, i: (b * nq + i, BLK_XQ)),
            pl.BlockSpec((M, GROUP_W), lambda b, i: (b, 0)),
            pl.BlockSpec((M, GROUP_W), lambda b, i: (b, 1)),
        ],
        out_specs=pl.BlockSpec((MEM_Q_TILE, GROUP_W), lambda b, i: (b * nq + i, 0)),
        out_shape=jax.ShapeDtypeStruct((B * L, GROUP_W), BF16),
        compiler_params=_cparams(("parallel", "parallel")),
        name="memattn",
    )(h_main, mkv, mkv)


def _dil_bias_table():
    w = DIL_Q_TILE + 2 * DIL_PAD
    d = np.arange(DIL_Q_TILE)[None, :] - np.arange(w)[:, None] + DIL_PAD
    count = np.zeros(d.shape, np.float64)
    for win, dil in DIL_PATTERNS:
        reach = (win // (2 * dil)) * dil
        count += (d % dil == 0) & (np.abs(d) <= reach)
    return np.where(count > 0, np.log2(np.maximum(count, 1.0)), NEG_INF).astype(np.float32)


def _dilattn_kernel(q_ref, qn_ref, k_ref, v_ref, bias_ref, o_ref, kpad, vt_ref, kmask,
                    sa_ref, sb_ref, m_ref, *, L):
    i = pl.program_id(1)
    nq = pl.num_programs(1)
    tq, hd, kt = DIL_Q_TILE, HEAD_DIM, DIL_K_TILE
    w = tq + 2 * DIL_PAD
    n_pad = DIL_PAD // kt

    @pl.when(i == 0)
    def _():
        zeros = jnp.zeros((DIL_PAD, GROUP_W), BF16)
        kpad[0:DIL_PAD, :] = zeros
        kpad[DIL_PAD + L:DIL_PAD + L + DIL_PAD, :] = zeros
        kpad[DIL_PAD:DIL_PAD + L, :] = k_ref[...]
        off = jnp.full((DIL_PAD, LANE), NEG_INF, F32)
        kmask[0:DIL_PAD, :] = off
        kmask[DIL_PAD + L:DIL_PAD + L + DIL_PAD, :] = off
        kmask[DIL_PAD:DIL_PAD + L, :] = jnp.zeros((L, LANE), F32)
        tail = (lax.broadcasted_iota(jnp.int32, (DIL_VT_ROWS - hd, kt), 0) == 0).astype(BF16)
        for c in range(L // kt + 2 * n_pad):
            inside = n_pad <= c < n_pad + L // kt
            if inside:
                lo = (c - n_pad) * kt
                vt = v_ref[lo:lo + kt, :].astype(F32).T.astype(BF16)
            for h in range(N_HEADS):
                vt_ref[c, h, 0:hd, :] = (vt[h * hd:(h + 1) * hd] if inside
                                         else jnp.zeros((hd, kt), BF16))
                vt_ref[c, h, hd:DIL_VT_ROWS, :] = tail

    feat_head = lax.broadcasted_iota(jnp.int32, (GROUP_W, 1), 0) // hd
    s_refs = (sa_ref, sb_ref)
    tiles = [(t * kt, (t + 1) * kt) for t in range(w // kt)]

    def stage(src_ref, tile, pair_next, nxt, pair_cur, cur):
        q0n = pl.multiple_of(tile * tq, tq)
        qt = src_ref[...].astype(F32).T
        qt2 = jnp.concatenate([jnp.where(feat_head == 2 * pair_next + c, qt, 0.0)
                               for c in range(2)], axis=1).astype(BF16)
        mx = None
        if pair_cur is not None:
            m = m_ref[cur, 0:1, :]
            acc = [jnp.zeros((DIL_VT_ROWS, tq), F32) for _ in range(2)]
        for t, (lo, hi) in enumerate(tiles):
            rows = pl.ds(q0n + lo, kt)
            km = kmask[rows, :]
            bias = bias_ref[lo:hi, :] + jnp.concatenate([km, km], axis=1)
            s = _dot(kpad[rows, :], qt2) + jnp.concatenate([bias, bias], axis=1)
            s_refs[nxt][lo:hi, :] = s
            cm = jnp.max(s, axis=0, keepdims=True)
            mx = cm if mx is None else jnp.maximum(mx, cm)
            if pair_cur is not None:
                p = jnp.exp2((s_refs[cur][lo:hi, :] - m).astype(BF16))
                for c in range(2):
                    acc[c] = acc[c] + _dot(vt_ref[i * (tq // kt) + t, 2 * pair_cur + c],
                                           p[:, c * tq:(c + 1) * tq])
        m_ref[nxt] = jnp.broadcast_to(mx, m_ref.shape[1:])
        if pair_cur is not None:
            return [o[0:hd] * (1.0 / o[hd:hd + 1]) for o in acc]

    @pl.when(i == 0)
    def _():
        stage(q_ref, i, 0, 0, None, None)

    heads = stage(q_ref, i, 1, 1, 0, 0)
    heads += stage(qn_ref, jnp.minimum(i + 1, nq - 1), 0, 0, 1, 1)
    o_ref[...] = jnp.concatenate(heads, axis=0).T.astype(BF16)


def _dilattn(h_main, bias, B, L):
    nq = L // DIL_Q_TILE
    w = DIL_Q_TILE + 2 * DIL_PAD
    return pl.pallas_call(
        functools.partial(_dilattn_kernel, L=L),
        grid=(B, nq),
        in_specs=[
            pl.BlockSpec((DIL_Q_TILE, GROUP_W), lambda b, i: (b * nq + i, BLK_AQ)),
            pl.BlockSpec((DIL_Q_TILE, GROUP_W),
                         lambda b, i: (b * nq + jnp.minimum(i + 1, nq - 1), BLK_AQ)),
            pl.BlockSpec((L, GROUP_W), lambda b, i: (b, BLK_AK)),
            pl.BlockSpec((L, GROUP_W), lambda b, i: (b, BLK_AV)),
            pl.BlockSpec((w, DIL_Q_TILE), lambda b, i: (0, 0)),
        ],
        out_specs=pl.BlockSpec((DIL_Q_TILE, GROUP_W), lambda b, i: (b * nq + i, 0)),
        out_shape=jax.ShapeDtypeStruct((B * L, GROUP_W), BF16),
        scratch_shapes=[pltpu.VMEM((L + 2 * DIL_PAD, GROUP_W), BF16),
                        pltpu.VMEM(((L + 2 * DIL_PAD) // DIL_K_TILE, N_HEADS, DIL_VT_ROWS,
                                    DIL_K_TILE), BF16),
                        pltpu.VMEM((L + 2 * DIL_PAD, LANE), F32),
                        pltpu.VMEM((w, 2 * DIL_Q_TILE), F32),
                        pltpu.VMEM((w, 2 * DIL_Q_TILE), F32),
                        pltpu.VMEM((2, F32_SUBLANES, 2 * DIL_Q_TILE), F32)],
        compiler_params=_cparams(("parallel", "arbitrary")),
        name="dilattn",
    )(h_main, h_main, h_main, h_main, bias)


def _split3(x):
    hi = x.astype(BF16)
    r1 = x - hi.astype(F32)
    mid = r1.astype(BF16)
    lo = (r1 - mid.astype(F32)).astype(BF16)
    return hi, mid, lo


def _split_dot(x, mat):
    return sum(_dot(t, mat) for t in _split3(x))


def _dot_f32(a, b):
    ah, am, al = _split3(a)
    bh, bm, bl = _split3(b)
    return (_dot(ah, bh) + (_dot(ah, bm) + _dot(am, bh))
            + (_dot(ah, bl) + _dot(am, bm) + _dot(al, bh)))


def _diffattn_kernel(q_ref, qn_ref, k_ref, v_ref, lam_ref, g_ref, hmean_ref, o_ref,
                     vt_ref, sa_ref, sb_ref, m_ref, oh_ref, *, lam_init, L):
    hd = HEAD_DIM
    first_tile = pl.program_id(1) == 0

    @pl.when(first_tile)
    def _():
        tail = (lax.broadcasted_iota(jnp.int32, (DIFF_VT_ROWS - hd, DIFF_K_CHUNK), 0) == 0)
        for c in range(L // DIFF_K_CHUNK):
            lo, hi = c * DIFF_K_CHUNK, (c + 1) * DIFF_K_CHUNK
            vt = v_ref[lo:hi, :].astype(F32).T.astype(BF16)
            for h in range(N_HEADS):
                vt_ref[h, 0:hd, lo:hi] = vt[h * hd:(h + 1) * hd]
                vt_ref[h, hd:DIFF_VT_ROWS, lo:hi] = tail.astype(BF16)

    lp = lam_ref[...]
    lam = (jnp.exp(jnp.sum(lp[0:1] * lp[1:2], axis=-1, keepdims=True))
           - jnp.exp(jnp.sum(lp[2:3] * lp[3:4], axis=-1, keepdims=True)) + lam_init)
    qt = q_ref[...].astype(F32).T
    qt_next = qn_ref[...].astype(F32).T
    feat_group = lax.broadcasted_iota(jnp.int32, (GROUP_W, 1), 0) // DIFF_QK_DIM
    tq = qt.shape[1]
    chunks = [(c * DIFF_K_CHUNK, (c + 1) * DIFF_K_CHUNK) for c in range(L // DIFF_K_CHUNK)]

    def masked_qt(h):
        wraps = h == N_HEADS
        src = jnp.where(wraps, qt_next, qt)
        hh = jnp.where(wraps, 0, h)
        return jnp.concatenate([jnp.where(feat_group == 2 * hh + c, src, 0.0) for c in range(2)],
                               axis=1).astype(BF16)

    def stage(h_next, nxt, h_cur, cur):
        qt2 = masked_qt(h_next)
        mx = None
        if h_cur is not None:
            m = m_ref[cur, 0:1, :]
            o = jnp.zeros((DIFF_VT_ROWS, 2 * tq), F32)
        for lo, hi in chunks:
            s = _dot(k_ref[lo:hi, :], qt2)
            s_refs[nxt][lo:hi, :] = s
            cm = jnp.max(s, axis=0, keepdims=True)
            mx = cm if mx is None else jnp.maximum(mx, cm)
            if h_cur is not None:
                p = jnp.exp2((s_refs[cur][lo:hi, :] - m).astype(BF16))
                o = o + _dot(vt_ref[h_cur, :, lo:hi], p)
        m_ref[nxt] = jnp.broadcast_to(mx, m_ref.shape[1:])
        if h_cur is not None:
            on = o[0:hd] * (1.0 / o[hd:hd + 1])
            oh_ref[h_cur] = on[:, 0:tq] - on[:, tq:2 * tq] * lam

    s_refs = (sa_ref, sb_ref)

    @pl.when(first_tile)
    def _():
        stage(0, 0, None, None)

    def body(j, carry):
        stage(2 * j + 1, 1, 2 * j, 0)
        stage(2 * j + 2, 0, 2 * j + 1, 1)
        return carry

    lax.fori_loop(0, N_HEADS // 2, body, 0)
    acc = jnp.concatenate([oh_ref[h] for h in range(N_HEADS)], axis=0).T
    ms = _split_dot(acc * acc, hmean_ref[...])
    y = acc * lax.rsqrt(ms + DIFF_SUBLN_EPS) * g_ref[...] * (1.0 - lam_init)
    o_ref[...] = y.astype(BF16)


def _diffattn(h_main, lam_p, subln_g, hmean, lam_init, B, L):
    nq = L // DIFF_Q_TILE
    return pl.pallas_call(
        functools.partial(_diffattn_kernel, lam_init=lam_init, L=L),
        grid=(B, nq),
        in_specs=[
            pl.BlockSpec((DIFF_Q_TILE, GROUP_W), lambda b, i: (b * nq + i, BLK_BQ)),
            pl.BlockSpec((DIFF_Q_TILE, GROUP_W),
                         lambda b, i: (b * nq + jnp.minimum(i + 1, nq - 1), BLK_BQ)),
            pl.BlockSpec((L, GROUP_W), lambda b, i: (b, BLK_BK)),
            pl.BlockSpec((L, GROUP_W), lambda b, i: (b, BLK_BV)),
            pl.BlockSpec((4, DIFF_QK_DIM), lambda b, i: (0, 0)),
            pl.BlockSpec((1, GROUP_W), lambda b, i: (0, 0)),
            pl.BlockSpec((GROUP_W, GROUP_W), lambda b, i: (0, 0)),
        ],
        out_specs=pl.BlockSpec((DIFF_Q_TILE, GROUP_W), lambda b, i: (b * nq + i, 0)),
        out_shape=jax.ShapeDtypeStruct((B * L, GROUP_W), BF16),
        scratch_shapes=[pltpu.VMEM((N_HEADS, DIFF_VT_ROWS, L), BF16),
                        pltpu.VMEM((L, 2 * DIFF_Q_TILE), F32),
                        pltpu.VMEM((L, 2 * DIFF_Q_TILE), F32),
                        pltpu.VMEM((2, F32_SUBLANES, 2 * DIFF_Q_TILE), F32),
                        pltpu.VMEM((N_HEADS, HEAD_DIM, DIFF_Q_TILE), F32)],
        compiler_params=_cparams(("parallel", "arbitrary")),
        name="diffattn",
    )(h_main, h_main, h_main, h_main, lam_p, subln_g, hmean)


def _log_sigmoid(v):
    return jnp.minimum(v, 0.0) - jnp.log(1.0 + jnp.exp(-jnp.abs(v)))


def _mlstm_kernel(q_ref, k_ref, v_ref, gi_ref, gf_ref, cw_ref, cb_ref, bi_ref, bf_ref,
                  ltri_ref, utri_ref, ecol_ref, elane_ref, kmask_ref, vmask_ref, ones_ref,
                  bd_ref, hsum_ref, o_ref,
                  qpad, kpad, qs, ks, hfw, hbw, cst, nst, mst, *, L):
    T = MLSTM_T
    nc = L // T
    halo = CONV_HALO
    nh = N_HEADS

    zpad = jnp.zeros((halo, GROUP_W), BF16)
    for pad, src in ((qpad, q_ref), (kpad, k_ref)):
        pad[0:halo, :] = zpad
        pad[halo + L:halo + L + halo, :] = zpad
        pad[halo:halo + L, :] = src[...]

    def conv_body(c, carry):
        r0 = pl.multiple_of(c * T, T)
        for idx, (pad, dst, scale) in enumerate(((qpad, qs, 1.0),
                                                 (kpad, ks, 1.0 / math.sqrt(HEAD_DIM)))):
            xw = pad[pl.ds(r0, T + 2 * halo), :].astype(F32)
            xm = pltpu.roll(xw, 1, axis=0)[halo:halo + T]
            xp = pltpu.roll(xw, T + 2 * halo - 1, axis=0)[halo:halo + T]
            xc = xw[halo:halo + T]
            lo, hi = idx * GROUP_W, (idx + 1) * GROUP_W
            y = (xm * cw_ref[0:1, lo:hi] + xc * cw_ref[1:2, lo:hi] + xp * cw_ref[2:3, lo:hi]
                 + cb_ref[0:1, lo:hi])
            dst[pl.ds(r0, T), :] = (_silu(y) * scale).astype(BF16)
        return carry

    lax.fori_loop(0, nc, conv_body, 0)

    cst[...] = jnp.zeros(cst.shape, F32)
    nst[...] = jnp.zeros(nst.shape, F32)
    mst[...] = jnp.zeros(mst.shape, F32)

    is_fw = lax.broadcasted_iota(jnp.int32, (1, LANE), 1) < nh
    rowi = lax.broadcasted_iota(jnp.int32, (T, LANE), 0)
    row4 = lax.broadcasted_iota(jnp.int32, (T, nh * T), 0)
    col4 = lax.broadcasted_iota(jnp.int32, (T, nh * T), 1) % T
    causal = (col4 <= row4, col4 >= row4)

    def body(c, carry):
        rows = (pl.multiple_of(c * T, T), pl.multiple_of((nc - 1 - c) * T, T))
        gate_i = jnp.where(is_fw, gi_ref[pl.ds(rows[0], T), :], gi_ref[pl.ds(rows[1], T), :])
        gate_f = jnp.where(is_fw, gf_ref[pl.ds(rows[0], T), :], gf_ref[pl.ds(rows[1], T), :])
        gate_i = gate_i + bi_ref[...]
        parts = _split3(_log_sigmoid(gate_f + bf_ref[...]))
        pre = sum(_dot(ltri_ref[...], t) for t in parts)
        suf = sum(_dot(utri_ref[...], t) for t in parts)
        cum = jnp.where(is_fw, pre, suf)
        b_end = jnp.where(is_fw, pre[T - 1:T, :], suf[0:1, :])
        key_w = gate_i - cum

        pmax, smax = key_w, key_w
        sh = 1
        while sh < T:
            pmax = jnp.maximum(pmax, jnp.where(rowi >= sh, pltpu.roll(pmax, sh, axis=0), NEG_INF))
            smax = jnp.maximum(smax, jnp.where(rowi < T - sh, pltpu.roll(smax, T - sh, axis=0),
                                               NEG_INF))
            sh *= 2
        m_prev = mst[0:1, :]
        inter = cum + m_prev
        m_t = jnp.maximum(inter, cum + jnp.where(is_fw, pmax, smax))
        u = cum - m_t
        a = b_end + key_w
        m_new = jnp.maximum(b_end + m_prev, jnp.max(a, axis=0, keepdims=True))
        mst[0:1, :] = m_new
        stack = jnp.concatenate(
            [jnp.exp(inter - m_t), jnp.exp(-m_t), jnp.exp(a - m_new),
             jnp.broadcast_to(jnp.exp(b_end + m_prev - m_new), (F32_SUBLANES, LANE))], axis=0)
        st_hi, st_mid, _ = _split3(stack)
        key_w_t = key_w.T

        for d in range(2):
            r0 = rows[d]
            qc = qs[pl.ds(r0, T), :]
            kc = ks[pl.ds(r0, T), :]
            vc = v_ref[pl.ds(r0, T), :]
            ex = _dot(st_hi, elane_ref[d]) + _dot(st_mid, elane_ref[d])
            w_inter, floor, w_key = ex[0:T], ex[T:2 * T], ex[2 * T:3 * T]
            sp_row = ex[3 * T:3 * T + 1]

            r_row = jnp.concatenate([key_w_t[d * nh + h:d * nh + h + 1, :] for h in range(nh)],
                                    axis=1)
            dlog = _split_dot(u, ecol_ref[d]) + r_row
            decay = jnp.exp(jnp.where(causal[d], dlog, NEG_INF))

            kt = kc.astype(F32).T.astype(BF16)
            qk = _dot(qc, jnp.concatenate([kt] * nh, axis=1) * kmask_ref[...]) * decay
            qk_hi = qk.astype(BF16)
            qk_lo = (qk - qk_hi.astype(F32)).astype(BF16)
            vblk = jnp.concatenate([vc] * nh, axis=0) * vmask_ref[...]
            num = _dot(qk_hi, vblk)
            den = _dot(qk_hi, ones_ref[...]) + _dot(qk_lo, ones_ref[...])

            c_prev = cst[d]
            n_prev = nst[d:d + 1, :]
            num = num + w_inter * _dot(qc, c_prev.astype(BF16))
            den = den + w_inter * _dot((qc.astype(F32) * n_prev).astype(BF16), hsum_ref[...])
            h_out = num / jnp.maximum(jnp.abs(den), floor)
            if d == 0:
                hfw[pl.ds(r0, T), :] = h_out
            else:
                hbw[pl.ds(r0, T), :] = h_out

            kw = kc.astype(F32) * w_key
            cst[d] = c_prev * sp_row + _dot_tn(kw.astype(BF16), vc) * bd_ref[...]
            nst[d:d + 1, :] = n_prev * sp_row + jnp.sum(kw, axis=0, keepdims=True)
        return carry

    lax.fori_loop(0, nc, body, 0)
    o_ref[...] = (hfw[...] + hbw[...]).astype(BF16)


def _mlstm_consts():
    T, nh, g = MLSTM_T, N_HEADS, GROUP_W
    tri = np.tril(np.ones((T, T), np.float32))
    src = np.arange(LANE)[:, None]
    ecol = np.stack([(src == d * nh + np.arange(nh * T)[None, :] // T) for d in range(2)])
    elane = np.stack([(src == d * nh + np.arange(g)[None, :] // HEAD_DIM) for d in range(2)])
    head_of = np.arange(g) // HEAD_DIM
    blk_of = np.arange(nh * T) // T
    kmask = head_of[:, None] == blk_of[None, :]
    vmask = blk_of[:, None] == head_of[None, :]
    same_head = head_of[:, None] == head_of[None, :]
    b16 = lambda a: jnp.asarray(a.astype(np.float32), BF16)
    return (b16(tri), b16(tri.T), b16(ecol), b16(elane), b16(kmask), b16(vmask), b16(vmask),
            jnp.asarray(same_head.astype(np.float32)), b16(same_head))


def _mlstm(h_main, gate_i, gate_f, conv_w, conv_b, bias_i, bias_f, consts, B, L):
    T = MLSTM_T
    const = lambda b: (0, 0)
    const3 = lambda b: (0, 0, 0)
    return pl.pallas_call(
        functools.partial(_mlstm_kernel, L=L),
        grid=(B,),
        in_specs=[
            pl.BlockSpec((L, GROUP_W), lambda b: (b, BLK_DQ)),
            pl.BlockSpec((L, GROUP_W), lambda b: (b, BLK_DK)),
            pl.BlockSpec((L, GROUP_W), lambda b: (b, BLK_DV)),
            pl.BlockSpec((L, LANE), lambda b: (b, 0)),
            pl.BlockSpec((L, LANE), lambda b: (b, 0)),
            pl.BlockSpec((3, 2 * GROUP_W), const),
            pl.BlockSpec((1, 2 * GROUP_W), const),
            pl.BlockSpec((1, LANE), const),
            pl.BlockSpec((1, LANE), const),
            pl.BlockSpec((T, T), const),
            pl.BlockSpec((T, T), const),
            pl.BlockSpec((2, LANE, N_HEADS * T), const3),
            pl.BlockSpec((2, LANE, GROUP_W), const3),
            pl.BlockSpec((GROUP_W, N_HEADS * T), const),
            pl.BlockSpec((N_HEADS * T, GROUP_W), const),
            pl.BlockSpec((N_HEADS * T, GROUP_W), const),
            pl.BlockSpec((GROUP_W, GROUP_W), const),
            pl.BlockSpec((GROUP_W, GROUP_W), const),
        ],
        out_specs=pl.BlockSpec((L, GROUP_W), lambda b: (b, 0)),
        out_shape=jax.ShapeDtypeStruct((B * L, GROUP_W), BF16),
        scratch_shapes=[
            pltpu.VMEM((L + 2 * CONV_HALO, GROUP_W), BF16),
            pltpu.VMEM((L + 2 * CONV_HALO, GROUP_W), BF16),
            pltpu.VMEM((L, GROUP_W), BF16),
            pltpu.VMEM((L, GROUP_W), BF16),
            pltpu.VMEM((L, GROUP_W), F32),
            pltpu.VMEM((L, GROUP_W), F32),
            pltpu.VMEM((2, GROUP_W, GROUP_W), F32),
            pltpu.VMEM((F32_SUBLANES, GROUP_W), F32),
            pltpu.VMEM((F32_SUBLANES, LANE), F32),
        ],
        compiler_params=_cparams(("parallel",)),
        name="mlstm",
    )(h_main, h_main, h_main, gate_i, gate_f, conv_w, conv_b, bias_i, bias_f, *consts)


def _hy_dims(L):
    h1 = L // HY_N2
    k1h = h1 + 1
    k1p = -(-k1h // BF16_SUBLANES) * BF16_SUBLANES
    kg = HY_K1_GROUP if k1p % HY_K1_GROUP == 0 else BF16_SUBLANES
    return h1, k1h, k1p, kg


def _hy_tables(L):
    h1, k1h, k1p, _ = _hy_dims(L)
    n1_len = 2 * h1
    n = 2 * L
    k1 = np.arange(k1p)[:, None].astype(np.float64)
    live = (np.arange(k1p) < k1h)[:, None]
    n1 = np.arange(h1)[None, :].astype(np.float64)
    ang = 2.0 * np.pi * k1 * n1 / n1_len
    m1 = np.concatenate([np.where(live, np.cos(ang), 0.0), np.where(live, -np.sin(ang), 0.0)], 0)

    n2 = np.arange(HY_N2)[None, None, :].astype(np.float64)
    k2 = np.arange(HY_N2)[None, :, None].astype(np.float64)
    kk = np.arange(k1p)[:, None, None] + n1_len * k2
    th = 2.0 * np.pi * kk * n2 / n
    fr, fi = np.cos(th), -np.sin(th)
    f3 = np.concatenate([np.concatenate([fr, -fi], 2), np.concatenate([fi, fr], 2)], 1)
    er, ei = np.transpose(np.cos(th), (0, 2, 1)), np.transpose(np.sin(th), (0, 2, 1))
    f3i = np.concatenate([np.concatenate([er, -ei], 2), np.concatenate([ei, er], 2)], 1)
    live3 = (np.arange(k1p) < k1h)[:, None, None]
    f3 = np.where(live3, f3, 0.0)
    f3i = np.where(live3, f3i, 0.0)

    nn1 = np.arange(h1)[:, None].astype(np.float64)
    kc = np.arange(k1p)[None, :].astype(np.float64)
    ph = 2.0 * np.pi * nn1 * kc / n1_len
    edge = (np.arange(k1p) == 0) | (np.arange(k1p) == h1)
    livec = (np.arange(k1p) < k1h)[None, :]
    m4r = np.where(livec, np.where(edge[None, :], np.cos(ph), 2.0 * np.cos(ph)), 0.0) / n
    m4i = np.where(livec & ~edge[None, :], -2.0 * np.sin(ph), 0.0) / n
    return (jnp.asarray(m1, BF16), jnp.asarray(f3, BF16), jnp.asarray(f3i, BF16),
            jnp.asarray(m4r, BF16), jnp.asarray(m4i, BF16))


def _to_wide(x, h1):
    return x.astype(F32).reshape(h1, HY_N2 * GROUP_W)


def _hy_stage1(z_bf16, m1_ref, a_ref, k1p):
    a = _dot(m1_ref[...], z_bf16)
    a_ref[0, 0] = a[0:k1p].reshape(k1p, HY_N2, GROUP_W).astype(BF16)
    a_ref[0, 1] = a[k1p:2 * k1p].reshape(k1p, HY_N2, GROUP_W).astype(BF16)


def _hy_conv3_wide(x, w_ref, b_ref, j, h1):
    c = GROUP_W
    wl = HY_N2 * c
    rowi = lax.broadcasted_iota(jnp.int32, (h1, c), 0)
    tail = x[:, wl - c:wl]
    head = x[:, 0:c]
    prev_tail = jnp.where(rowi == 0, 0.0, pltpu.roll(tail, 1, axis=0))
    next_head = jnp.where(rowi == h1 - 1, 0.0, pltpu.roll(head, h1 - 1, axis=0))
    xm = jnp.concatenate([prev_tail, x[:, 0:wl - c]], axis=1)
    xp = jnp.concatenate([x[:, c:wl], next_head], axis=1)
    return (xm * w_ref[3 * j:3 * j + 1, :] + x * w_ref[3 * j + 1:3 * j + 2, :]
            + xp * w_ref[3 * j + 2:3 * j + 3, :] + b_ref[j:j + 1, :])


def _hy_front_kernel(v_ref, x1_ref, x2_ref, w_ref, b_ref, m1_ref,
                     z_ref, x1c_ref, x2c_ref, a_ref, *, h1, k1p):
    z = _hy_conv3_wide(_to_wide(v_ref[...], h1), w_ref, b_ref, 0, h1).astype(BF16)
    z_ref[0] = z
    x1c_ref[0] = _hy_conv3_wide(_to_wide(x1_ref[...], h1), w_ref, b_ref, 1, h1).astype(BF16)
    x2c_ref[0] = _hy_conv3_wide(_to_wide(x2_ref[...], h1), w_ref, b_ref, 2, h1).astype(BF16)
    _hy_stage1(z, m1_ref, a_ref, k1p)


def _hy_front(cv, cx1, cx2, w_wide, b_wide, m1, B, L):
    h1, _, k1p, _ = _hy_dims(L)
    wl = HY_N2 * GROUP_W
    nat = pl.BlockSpec((L, GROUP_W), lambda b: (b, 0))
    seq = pl.BlockSpec((1, h1, wl), lambda b: (b, 0, 0))
    wide = jax.ShapeDtypeStruct((B, h1, wl), BF16)
    slab = (1, 2, k1p, HY_N2, GROUP_W)
    return pl.pallas_call(
        functools.partial(_hy_front_kernel, h1=h1, k1p=k1p),
        grid=(B,),
        in_specs=[nat, nat, nat,
                  pl.BlockSpec((9, wl), lambda b: (0, 0)),
                  pl.BlockSpec((3, wl), lambda b: (0, 0)),
                  pl.BlockSpec((2 * k1p, h1), lambda b: (0, 0))],
        out_specs=(seq, seq, seq, pl.BlockSpec(slab, lambda b: (b, 0, 0, 0, 0))),
        out_shape=(wide, wide, wide,
                   jax.ShapeDtypeStruct((B, 2, k1p, HY_N2, GROUP_W), BF16)),
        compiler_params=_cparams(("parallel",)),
        name="hy_front",
    )(cv, cx1, cx2, w_wide, b_wide, m1)


def _hy_stage1_kernel(z_ref, m1_ref, a_ref, *, h1, k1p):
    _hy_stage1(_to_wide(z_ref[0], h1).astype(BF16), m1_ref, a_ref, k1p)


def _hy_stage1_call(z_nat, m1, L):
    h1, _, k1p, _ = _hy_dims(L)
    nb = z_nat.shape[0]
    return pl.pallas_call(
        functools.partial(_hy_stage1_kernel, h1=h1, k1p=k1p),
        grid=(nb,),
        in_specs=[pl.BlockSpec((1, L, GROUP_W), lambda b: (b, 0, 0)),
                  pl.BlockSpec((2 * k1p, h1), lambda b: (0, 0))],
        out_specs=pl.BlockSpec((1, 2, k1p, HY_N2, GROUP_W), lambda b: (b, 0, 0, 0, 0)),
        out_shape=jax.ShapeDtypeStruct((nb, 2, k1p, HY_N2, GROUP_W), BF16),
        compiler_params=_cparams(("parallel",)),
        name="hy_stage1",
    )(z_nat, m1)


def _hy_spectrum_kernel(af_ref, ab_ref, f3_ref, g_ref, *, kg, k1h):
    grp = pl.program_id(0)

    @pl.when(grp * kg < k1h)
    def _():
        for kk in range(kg):
            xs = []
            for a_ref in (af_ref, ab_ref):
                a2 = jnp.concatenate([a_ref[0, 0, kk], a_ref[0, 1, kk]], axis=0)
                xs.append(_dot(f3_ref[kk], a2))
            g_ref[0, 0, kk] = xs[0][0:HY_N2] + xs[1][0:HY_N2]
            g_ref[0, 1, kk] = xs[0][HY_N2:2 * HY_N2] - xs[1][HY_N2:2 * HY_N2]

    @pl.when(grp * kg >= k1h)
    def _():
        g_ref[...] = jnp.zeros(g_ref.shape, F32)


def _hy_spectrum(a_filt, f3, L):
    _, k1h, k1p, kg = _hy_dims(L)
    a5 = a_filt
    blk = (1, 2, kg, HY_N2, GROUP_W)
    return pl.pallas_call(
        functools.partial(_hy_spectrum_kernel, kg=kg, k1h=k1h),
        grid=(k1p // kg, HYENA_ORDER),
        in_specs=[pl.BlockSpec(blk, lambda g, o: (2 * o, 0, g, 0, 0)),
                  pl.BlockSpec(blk, lambda g, o: (2 * o + 1, 0, g, 0, 0)),
                  pl.BlockSpec((kg, 2 * HY_N2, 2 * HY_N2), lambda g, o: (g, 0, 0))],
        out_specs=pl.BlockSpec(blk, lambda g, o: (o, 0, g, 0, 0)),
        out_shape=jax.ShapeDtypeStruct((HYENA_ORDER, 2, k1p, HY_N2, GROUP_W), F32),
        compiler_params=_cparams(("parallel", "parallel")),
        name="hy_spectrum",
    )(a5, a5, f3)


def _hy_mid_kernel(a_ref, g_ref, f3_ref, f3i_ref, b_ref, *, kg, k1h):
    grp = pl.program_id(0)

    @pl.when(grp * kg < k1h)
    def _():
        for kk in range(kg):
            a2 = jnp.concatenate([a_ref[0, 0, kk], a_ref[0, 1, kk]], axis=0)
            x = _dot(f3_ref[kk], a2)
            xr, xi = x[0:HY_N2], x[HY_N2:2 * HY_N2]
            gr, gi = g_ref[0, 0, kk], g_ref[0, 1, kk]
            y2 = jnp.concatenate([xr * gr - xi * gi, xr * gi + xi * gr], axis=0).astype(BF16)
            bm = _dot(f3i_ref[kk], y2)
            b_ref[0, 0, kk] = bm[0:HY_N2].astype(BF16)
            b_ref[0, 1, kk] = bm[HY_N2:2 * HY_N2].astype(BF16)

    @pl.when(grp * kg >= k1h)
    def _():
        b_ref[...] = jnp.zeros(b_ref.shape, BF16)


def _hy_mid(a, g_spec, order, f3, f3i, B, L):
    _, k1h, k1p, kg = _hy_dims(L)
    blk = (1, 2, kg, HY_N2, GROUP_W)
    tab = pl.BlockSpec((kg, 2 * HY_N2, 2 * HY_N2), lambda g, b: (g, 0, 0))
    return pl.pallas_call(
        functools.partial(_hy_mid_kernel, kg=kg, k1h=k1h),
        grid=(k1p // kg, B),
        in_specs=[pl.BlockSpec(blk, lambda g, b: (b, 0, g, 0, 0)),
                  pl.BlockSpec(blk, lambda g, b: (order, 0, g, 0, 0)),
                  tab, tab],
        out_specs=pl.BlockSpec(blk, lambda g, b: (b, 0, g, 0, 0)),
        out_shape=jax.ShapeDtypeStruct((B, 2, k1p, HY_N2, GROUP_W), BF16),
        compiler_params=_cparams(("parallel", "parallel")),
        name="hy_mid",
    )(a, g_spec, f3, f3i)


def _hy_back_kernel(b_ref, z_ref, x_ref, bias_ref, m4r_ref, m4i_ref, *rest, k1p, last):
    wl = HY_N2 * GROUP_W
    br = b_ref[0, 0].astype(F32).reshape(k1p, wl).astype(BF16)
    bi = b_ref[0, 1].astype(F32).reshape(k1p, wl).astype(BF16)
    y = _dot(m4r_ref[...], br) + _dot(m4i_ref[...], bi)
    z_new = x_ref[0].astype(F32) * (y + z_ref[0].astype(F32) * bias_ref[...])
    zb = z_new.astype(BF16)
    if last:
        (o_ref,) = rest
        o_ref[...] = z_new.reshape(o_ref.shape).astype(BF16)
    else:
        m1_ref, o_ref, a_ref = rest
        o_ref[0] = zb
        _hy_stage1(zb, m1_ref, a_ref, k1p)


def _hy_back(b, z, xg, bias_wide, m4r, m4i, m1, B, L, last):
    h1, _, k1p, _ = _hy_dims(L)
    wl = HY_N2 * GROUP_W
    seq = pl.BlockSpec((1, h1, wl), lambda i: (i, 0, 0))
    slab = pl.BlockSpec((1, 2, k1p, HY_N2, GROUP_W), lambda i: (i, 0, 0, 0, 0))
    const = lambda i: (0, 0)
    in_specs = [slab, seq, seq, pl.BlockSpec((1, wl), const),
                pl.BlockSpec((h1, k1p), const), pl.BlockSpec((h1, k1p), const)]
    args = [b, z, xg, bias_wide, m4r, m4i]
    if last:
        out_specs = pl.BlockSpec((L, GROUP_W), lambda i: (i, 0))
        out_shape = jax.ShapeDtypeStruct((B * L, GROUP_W), BF16)
    else:
        in_specs.append(pl.BlockSpec((2 * k1p, h1), const))
        args.append(m1)
        out_specs = (seq, slab)
        out_shape = (jax.ShapeDtypeStruct((B, h1, wl), BF16),
                     jax.ShapeDtypeStruct((B, 2, k1p, HY_N2, GROUP_W), BF16))
    return pl.pallas_call(
        functools.partial(_hy_back_kernel, k1p=k1p, last=last),
        grid=(B,),
        in_specs=in_specs,
        out_specs=out_specs,
        out_shape=out_shape,
        compiler_params=_cparams(("parallel",)),
        name="hy_back_last" if last else "hy_back",
    )(*args)


def _hy_filter_kernel(f_ref, t_ref, w1_ref, b1_ref, w2_ref, b2_ref, w3_ref, fr_ref, ad_ref,
                      o_ref, *, tile):
    i = pl.program_id(0)
    freq = fr_ref[...]
    z = jnp.sin(freq * (_dot_f32(f_ref[...], w1_ref[...]) + b1_ref[...]))
    z = jnp.sin(freq * (_dot_f32(z, w2_ref[...]) + b2_ref[...]))
    decay = jnp.exp(-t_ref[...] * ad_ref[...])
    rowi = i * tile + lax.broadcasted_iota(jnp.int32, (tile, GROUP_W), 0)
    for j in range(2 * HYENA_ORDER):
        hj = _dot_f32(z, w3_ref[:, j * GROUP_W:(j + 1) * GROUP_W]) * decay
        if j % 2 == 1:
            hj = jnp.where(rowi == 0, 0.0, hj)
        o_ref[j] = hj.astype(BF16)


def _hy_filter(feats, tcol, w1, b1, w2, b2, w3, freq, absdelta, L):
    tile = math.gcd(L, ROW_TILE)
    const = lambda i: (0, 0)
    return pl.pallas_call(
        functools.partial(_hy_filter_kernel, tile=tile),
        grid=(L // tile,),
        in_specs=[pl.BlockSpec((tile, LANE), lambda i: (i, 0)),
                  pl.BlockSpec((tile, 1), lambda i: (i, 0)),
                  pl.BlockSpec((LANE, HYENA_HIDDEN), const),
                  pl.BlockSpec((1, HYENA_HIDDEN), const),
                  pl.BlockSpec((HYENA_HIDDEN, HYENA_HIDDEN), const),
                  pl.BlockSpec((1, HYENA_HIDDEN), const),
                  pl.BlockSpec((HYENA_HIDDEN, 2 * HYENA_ORDER * GROUP_W), const),
                  pl.BlockSpec((1, HYENA_HIDDEN), const),
                  pl.BlockSpec((1, GROUP_W), const)],
        out_specs=pl.BlockSpec((2 * HYENA_ORDER, tile, GROUP_W), lambda i: (0, i, 0)),
        out_shape=jax.ShapeDtypeStruct((2 * HYENA_ORDER, L, GROUP_W), BF16),
        compiler_params=_cparams(("parallel",)),
        name="hy_filter",
    )(feats, tcol, w1, b1, w2, b2, w3, freq, absdelta)


def _outproj_kernel(x_ref, oa_ref, ob_ref, oc_ref, od_ref, ox_ref,
                    ga_ref, gb_ref, gc_ref, gdo_ref, gdg_ref, gx_ref, w_ref, png_ref, out_ref):
    f = lambda r: r[...].astype(F32)
    branches = (
        f(oa_ref) * _silu(f(ga_ref)),
        f(ob_ref) * _silu(f(gb_ref)),
        f(oc_ref) * _silu(f(gc_ref)),
        f(od_ref) * _sigmoid(f(gdo_ref)) * _silu(f(gdg_ref)),
        f(ox_ref) * _silu(f(gx_ref)),
    )
    y = jnp.zeros(x_ref.shape, F32)
    for j, br in enumerate(branches):
        y = y + _dot(br.astype(BF16), w_ref[j * GROUP_W:(j + 1) * GROUP_W, :])
    ms = jnp.mean(y * y, axis=-1, keepdims=True)
    out_ref[...] = x_ref[...] + y * lax.rsqrt(ms + NORM_EPS) * png_ref[...]


def _outproj(x2d, outs, h_main, w_out, png, row0=0, n_rows=None):
    n_rows = x2d.shape[0] if n_rows is None else n_rows
    nt = n_rows // ROW_TILE
    t0 = row0 // ROW_TILE
    row = lambda i: (t0 + i, 0)
    blk = lambda j: pl.BlockSpec((ROW_TILE, GROUP_W), lambda i: (t0 + i, j))
    return pl.pallas_call(
        _outproj_kernel,
        grid=(nt,),
        in_specs=[pl.BlockSpec((ROW_TILE, D_MODEL), row)]
        + [pl.BlockSpec((ROW_TILE, GROUP_W), row)] * 5
        + [blk(BLK_AG), blk(BLK_BG), blk(BLK_CG), blk(BLK_DO), blk(BLK_DG), blk(BLK_XG)]
        + [pl.BlockSpec((5 * GROUP_W, D_MODEL), lambda i: (0, 0)),
           pl.BlockSpec((1, D_MODEL), lambda i: (0, 0))],
        out_specs=pl.BlockSpec((ROW_TILE, D_MODEL), lambda i: (i, 0)),
        out_shape=jax.ShapeDtypeStruct((n_rows, D_MODEL), F32),
        compiler_params=_cparams(("parallel",)),
        name="outproj",
    )(x2d, *outs, h_main, h_main, h_main, h_main, h_main, h_main, w_out, png)


def _rope_tables(L, group, rot_dim):
    half = rot_dim // 2
    inv = 1.0 / (ROPE_THETA ** (jnp.arange(0, rot_dim, 2, dtype=F32) / rot_dim))
    ang = jnp.arange(L, dtype=F32)[:, None] * inv[None, :]
    cos, sin = jnp.cos(ang), jnp.sin(ang)
    lane = np.arange(GROUP_W) % group
    in_rot = lane < rot_dim
    idx = lane % half
    c = jnp.where(in_rot[None, :], cos[:, idx], 1.0)
    s = jnp.where(in_rot[None, :], sin[:, idx], 0.0)
    p = np.zeros((GROUP_W, GROUP_W), np.float32)
    for j in range(GROUP_W):
        if lane[j] < half:
            p[j + half, j] = -1.0
        elif lane[j] < rot_dim:
            p[j - half, j] = 1.0
    return c, s, jnp.asarray(p, BF16)


def _hyena_features(L):
    t = jnp.linspace(0.0, 1.0, L, dtype=F32)[:, None]
    bands = jnp.linspace(1e-4, HYENA_BANDS - 1, HYENA_BANDS, dtype=F32)
    ang = (2.0 * math.pi / L) * jnp.arange(L, dtype=F32)[:, None] * bands[None, :]
    feats = jnp.concatenate([t, jnp.cos(ang), -jnp.sin(ang)], axis=-1)
    return jnp.pad(feats, ((0, 0), (0, LANE - HYENA_EMB))), t


def _relayout_w_in(w):
    g = GROUP_W
    off_c, off_d = 8 * g, 12 * g
    off_gate = off_d + 5 * g
    off_x = off_gate + N_MLSTM_GATES
    main = jnp.concatenate([w[:, 0:off_c], w[:, off_c + 3 * g:off_c + 4 * g],
                            w[:, off_d:off_gate], w[:, off_x:off_x + 2 * g]], axis=1)
    hy = w[:, off_c:off_c + 3 * g]
    gate_i, gate_f = _split_gates(w[:, off_gate:off_x])
    return jnp.concatenate([main, hy, gate_i, gate_f], axis=1).astype(BF16)


def _split_gates(t):
    nh = N_HEADS
    pad = [(0, 0)] * (t.ndim - 1) + [(0, LANE - 2 * nh)]
    gi = jnp.concatenate([t[..., 0:nh], t[..., 2 * nh:3 * nh]], axis=-1)
    gf = jnp.concatenate([t[..., nh:2 * nh], t[..., 3 * nh:4 * nh]], axis=-1)
    return jnp.pad(gi, pad), jnp.pad(gf, pad)


def _trunk(x, mem, splits, pre_norm_g, post_norm_g, w_in, w_out, diff_lambda, diff_subln_g,
           hy_conv_w, hy_conv_b, hy_ffn_w1, hy_ffn_b1, hy_ffn_w2, hy_ffn_b2, hy_ffn_w3,
           hy_freq, hy_bias, ml_conv_w, ml_conv_b, ml_gate_b, mem_norm_g, w_mem_kv):
    B, L, _ = x.shape
    M = mem.shape[1]
    depth = w_in.shape[0]
    g = GROUP_W
    h1, _, k1p, _ = _hy_dims(L)
    wl = HY_N2 * g

    ca, sa, pa = _rope_tables(L, HEAD_DIM, HEAD_DIM // ROPE_FRACTION)
    cb, sb, pb = _rope_tables(L, DIFF_QK_DIM, DIFF_QK_DIM // ROPE_FRACTION)
    rope_tabs = (ca, sa, cb, sb, pa, pb)
    dil_bias = jnp.asarray(_dil_bias_table())
    feats, tcol = _hyena_features(L)
    absdelta = jnp.abs(jnp.linspace(math.log(HYENA_TARGET) / HYENA_SLOW_DECAY,
                                    math.log(HYENA_TARGET) / HYENA_FAST_DECAY, g, dtype=F32))[None]
    m1, f3, f3i, m4r, m4i = _hy_tables(L)
    head_of = np.arange(g) // HEAD_DIM
    same_head = (head_of[:, None] == head_of[None, :]).astype(np.float32)
    hmean = jnp.asarray(same_head / HEAD_DIM, BF16)
    mlstm_consts = _mlstm_consts()

    x2d = x.reshape(B * L, D_MODEL)
    mem2d = mem.reshape(B * M, D_MODEL)
    for li in range(depth):
        lam_init = 0.8 - 0.6 * math.exp(-0.3 * li)
        w_all = _relayout_w_in(w_in[li])
        h_main, cv, cx1, cx2, gate_i, gate_f = _inproj(x2d, pre_norm_g[li][None], w_all,
                                                       rope_tabs, B, L)

        oa = _dilattn(h_main, dil_bias, B, L)
        ob = _diffattn(h_main, diff_lambda[li], jnp.tile(diff_subln_g[li], N_HEADS)[None],
                       hmean, lam_init, B, L)

        w1 = jnp.pad(hy_ffn_w1[li], ((0, LANE - HYENA_EMB), (0, 0)))
        hfilt = _hy_filter(feats, tcol, w1, hy_ffn_b1[li][None], hy_ffn_w2[li],
                           hy_ffn_b2[li][None], hy_ffn_w3[li], hy_freq[li][None], absdelta, L)
        a_filt = _hy_stage1_call(hfilt, m1, L)
        g_spec = _hy_spectrum(a_filt, f3, L)
        cw = jnp.tile(hy_conv_w[li].reshape(3, 3, g).transpose(1, 0, 2).reshape(9, g), (1, HY_N2))
        cbw = jnp.tile(hy_conv_b[li].reshape(3, g), (1, HY_N2))
        z, x1c, x2c, a = _hy_front(cv, cx1, cx2, cw, cbw, m1, B, L)
        bias_w = jnp.tile(hy_bias[li], (1, HY_N2))
        bsp = _hy_mid(a, g_spec, 0, f3, f3i, B, L)
        z, a = _hy_back(bsp, z, x1c, bias_w[0:1], m4r, m4i, m1, B, L, last=False)
        bsp = _hy_mid(a, g_spec, 1, f3, f3i, B, L)
        oc = _hy_back(bsp, z, x2c, bias_w[1:2], m4r, m4i, m1, B, L, last=True)

        bias_i, bias_f = _split_gates(ml_gate_b[li][None])
        od = _mlstm(h_main, gate_i, gate_f, ml_conv_w[li], ml_conv_b[li][None], bias_i, bias_f,
                    mlstm_consts, B, L)

        mkv = _memkv(mem2d, mem_norm_g[li][None], w_mem_kv[li].astype(BF16))
        ox = _memattn(h_main, mkv, B, L, M)

        branch_outs = (oa, ob, oc, od, ox)
        w_o, png = w_out[li].astype(BF16), post_norm_g[li][None]
        if li + 1 < depth:
            x2d = _outproj(x2d, branch_outs, h_main, w_o, png)
    ys, b0 = [], 0
    for nb in splits:
        y = _outproj(x2d, branch_outs, h_main, w_o, png, row0=b0 * L, n_rows=nb * L)
        ys.append(y.reshape(nb, L, D_MODEL))
        b0 += nb
    return tuple(ys)


def kernel(x_prompt, x_sample, mem_prompt, mem_sample, pre_norm_g, post_norm_g, w_in, w_out,
           diff_lambda, diff_subln_g, hy_conv_w, hy_conv_b, hy_ffn_w1, hy_ffn_b1, hy_ffn_w2,
           hy_ffn_b2, hy_ffn_w3, hy_freq, hy_bias, ml_conv_w, ml_conv_b, ml_gate_b,
           mem_norm_g, w_mem_kv):
    x = jnp.concatenate([x_prompt, x_sample], axis=0)
    mem = jnp.concatenate([mem_prompt, mem_sample], axis=0)
    return _trunk(x, mem, (x_prompt.shape[0], x_sample.shape[0]), pre_norm_g, post_norm_g, w_in,
                  w_out, diff_lambda, diff_subln_g, hy_conv_w, hy_conv_b, hy_ffn_w1, hy_ffn_b1,
                  hy_ffn_w2, hy_ffn_b2, hy_ffn_w3, hy_freq, hy_bias, ml_conv_w, ml_conv_b,
                  ml_gate_b, mem_norm_g, w_mem_kv)
```

```python
import functools
import math

import numpy as np
import jax
import jax.numpy as jnp
from jax import lax
from jax.experimental import pallas as pl
from jax.experimental.pallas import tpu as pltpu

F32 = jnp.float32
BF16 = jnp.bfloat16

D_MODEL = 1024
HEAD_DIM = 64
GROUP_W = 256
N_HEADS = GROUP_W // HEAD_DIM
NORM_EPS = 1e-6
NEG_INF = -1e30
ROPE_THETA = 500000.0
ROPE_FRACTION = 4
DIL_PATTERNS = ((128, 1), (512, 4), (2048, 16))
DIFF_QK_DIM = HEAD_DIM // 2
DIFF_SUBLN_EPS = 1e-5
HYENA_ORDER = 2
HYENA_BANDS = 16
HYENA_EMB = 1 + 2 * HYENA_BANDS
HYENA_HIDDEN = 64
HYENA_FAST_DECAY = 0.3
HYENA_SLOW_DECAY = 1.5
HYENA_TARGET = 1e-2
N_MLSTM_GATES = 4 * N_HEADS

(BLK_AQ, BLK_AK, BLK_AV, BLK_AG, BLK_BQ, BLK_BK, BLK_BV, BLK_BG, BLK_CG,
 BLK_DQ, BLK_DK, BLK_DV, BLK_DO, BLK_DG, BLK_XQ, BLK_XG) = range(16)
N_MAIN_BLK = 16
MAIN_W = N_MAIN_BLK * GROUP_W
W_ALL = MAIN_W + 4 * GROUP_W

VMEM_LIMIT_BYTES = 56 * 1024 * 1024
LANE = 128
F32_SUBLANES = 8
BF16_SUBLANES = 16

ROW_TILE = 512
DIL_Q_TILE = 256
DIL_PAD = 1024
DIL_K_TILE = 256
DIL_VT_ROWS = HEAD_DIM + BF16_SUBLANES
DIFF_Q_TILE = 256
DIFF_K_CHUNK = 512
DIFF_VT_ROWS = HEAD_DIM + BF16_SUBLANES
LOG2E = 1.4426950408889634
MEM_Q_TILE = 512
MLSTM_T = 128
CONV_HALO = 16
HY_N2 = 128
HY_K1_GROUP = 12


def _cparams(sem, vmem=VMEM_LIMIT_BYTES):
    return pltpu.CompilerParams(dimension_semantics=sem, vmem_limit_bytes=vmem)


def _sigmoid(v):
    return 1.0 / (1.0 + jnp.exp(-v))


def _silu(v):
    return v * _sigmoid(v)


def _dot(a, b):
    return jnp.dot(a, b, preferred_element_type=F32)


def _dot_tn(a, b):
    return lax.dot_general(a, b, (((0,), (0,)), ((), ())), preferred_element_type=F32)


def _inproj_kernel(x_ref, g_ref, w_ref, ca_ref, sa_ref, cb_ref, sb_ref, pa_ref, pb_ref,
                   h_ref, cv_ref, cx1_ref, cx2_ref, gi_ref, gf_ref):
    x = x_ref[...]
    ms = jnp.mean(x * x, axis=-1, keepdims=True)
    xn = (x * lax.rsqrt(ms + NORM_EPS) * g_ref[...]).astype(BF16)

    def proj(j, width=GROUP_W):
        return _dot(xn, w_ref[:, j * GROUP_W:j * GROUP_W + width])

    def rope(acc, c_ref, s_ref, p_ref):
        partner = _dot(acc.astype(BF16), p_ref[...])
        return acc * c_ref[...] + partner * s_ref[...]

    def finish(j, acc):
        if j in (BLK_AQ, BLK_AK):
            acc = rope(acc, ca_ref, sa_ref, pa_ref)
        if j in (BLK_BQ, BLK_BK):
            acc = rope(acc, cb_ref, sb_ref, pb_ref)
        if j in (BLK_AQ, BLK_XQ):
            acc = acc * (LOG2E / math.sqrt(HEAD_DIM))
        if j == BLK_BQ:
            acc = acc * (LOG2E / math.sqrt(DIFF_QK_DIM))
        if j < N_MAIN_BLK:
            h_ref[:, j * GROUP_W:(j + 1) * GROUP_W] = acc.astype(BF16)
        elif j < N_MAIN_BLK + 3:
            (cv_ref, cx1_ref, cx2_ref)[j - N_MAIN_BLK][...] = acc.astype(BF16)
        else:
            gi_ref[...] = acc[:, 0:LANE]
            gf_ref[...] = acc[:, LANE:2 * LANE]

    for j in range(0, W_ALL // GROUP_W, 2):
        acc2 = proj(j, 2 * GROUP_W)
        finish(j, acc2[:, 0:GROUP_W])
        finish(j + 1, acc2[:, GROUP_W:2 * GROUP_W])


def _inproj(x2d, g, w_all, rope_tabs, B, L):
    ca, sa, cb, sb, pa, pb = rope_tabs
    nt = L // ROW_TILE
    n_tok = B * L
    row = lambda i, b: (b * nt + i, 0)
    tab = lambda i, b: (i, 0)
    const = lambda i, b: (0, 0)
    out_shapes = (
        jax.ShapeDtypeStruct((n_tok, MAIN_W), BF16),
        jax.ShapeDtypeStruct((n_tok, GROUP_W), BF16),
        jax.ShapeDtypeStruct((n_tok, GROUP_W), BF16),
        jax.ShapeDtypeStruct((n_tok, GROUP_W), BF16),
        jax.ShapeDtypeStruct((n_tok, LANE), F32),
        jax.ShapeDtypeStruct((n_tok, LANE), F32),
    )
    return pl.pallas_call(
        _inproj_kernel,
        grid=(nt, B),
        in_specs=[
            pl.BlockSpec((ROW_TILE, D_MODEL), row),
            pl.BlockSpec((1, D_MODEL), const),
            pl.BlockSpec((D_MODEL, W_ALL), const),
            pl.BlockSpec((ROW_TILE, GROUP_W), tab),
            pl.BlockSpec((ROW_TILE, GROUP_W), tab),
            pl.BlockSpec((ROW_TILE, GROUP_W), tab),
            pl.BlockSpec((ROW_TILE, GROUP_W), tab),
            pl.BlockSpec((GROUP_W, GROUP_W), const),
            pl.BlockSpec((GROUP_W, GROUP_W), const),
        ],
        out_specs=(
            pl.BlockSpec((ROW_TILE, MAIN_W), row),
            pl.BlockSpec((ROW_TILE, GROUP_W), row),
            pl.BlockSpec((ROW_TILE, GROUP_W), row),
            pl.BlockSpec((ROW_TILE, GROUP_W), row),
            pl.BlockSpec((ROW_TILE, LANE), row),
            pl.BlockSpec((ROW_TILE, LANE), row),
        ),
        out_shape=out_shapes,
        compiler_params=_cparams(("parallel", "parallel")),
        name="inproj",
    )(x2d, g, w_all, ca, sa, cb, sb, pa, pb)


def _memkv_kernel(m_ref, g_ref, w_ref, o_ref):
    x = m_ref[...]
    ms = jnp.mean(x * x, axis=-1, keepdims=True)
    xn = (x * lax.rsqrt(ms + NORM_EPS) * g_ref[...]).astype(BF16)
    o_ref[...] = _dot(xn, w_ref[...]).astype(BF16)


def _memkv(mem2d, g, w):
    rows = mem2d.shape[0]
    tile = math.gcd(rows, ROW_TILE)
    return pl.pallas_call(
        _memkv_kernel,
        grid=(rows // tile,),
        in_specs=[
            pl.BlockSpec((tile, D_MODEL), lambda i: (i, 0)),
            pl.BlockSpec((1, D_MODEL), lambda i: (0, 0)),
            pl.BlockSpec((D_MODEL, 2 * GROUP_W), lambda i: (0, 0)),
        ],
        out_specs=pl.BlockSpec((tile, 2 * GROUP_W), lambda i: (i, 0)),
        out_shape=jax.ShapeDtypeStruct((rows, 2 * GROUP_W), BF16),
        compiler_params=_cparams(("parallel",)),
        name="memkv",
    )(mem2d, g, w)


def _memattn_kernel(q_ref, mk_ref, mv_ref, o_ref):
    q = q_ref[...]
    mk = mk_ref[...]
    mv = mv_ref[...]
    lane_head = lax.broadcasted_iota(jnp.int32, (1, GROUP_W), 1) // HEAD_DIM
    acc = jnp.zeros(q.shape, F32)
    for h in range(N_HEADS):
        hm = (lane_head == h).astype(BF16)
        s = lax.dot_general(q * hm, mk, (((1,), (1,)), ((), ())), preferred_element_type=F32)
        m = jnp.max(s, axis=-1, keepdims=True)
        p = jnp.exp2(s - m)
        l = jnp.sum(p, axis=-1, keepdims=True)
        acc = acc + _dot(p.astype(BF16), mv * hm) * (1.0 / l)
    o_ref[...] = acc.astype(BF16)


def _memattn(h_main, mkv, B, L, M):
    nq = L // MEM_Q_TILE
    return pl.pallas_call(
        _memattn_kernel,
        grid=(B, nq),
        in_specs=[
            pl.BlockSpec((MEM_Q_TILE, GROUP_W), lambda b, i: (b * nq + i, BLK_XQ)),
            pl.BlockSpec((M, GROUP_W), lambda b, i: (b, 0)),
            pl.BlockSpec((M, GROUP_W), lambda b, i: (b, 1)),
        ],
        out_specs=pl.BlockSpec((MEM_Q_TILE, GROUP_W), lambda b, i: (b * nq + i, 0)),
        out_shape=jax.ShapeDtypeStruct((B * L, GROUP_W), BF16),
        compiler_params=_cparams(("parallel", "parallel")),
        name="memattn",
    )(h_main, mkv, mkv)


def _dil_bias_table():
    w = DIL_Q_TILE + 2 * DIL_PAD
    d = np.arange(DIL_Q_TILE)[None, :] - np.arange(w)[:, None] + DIL_PAD
    count = np.zeros(d.shape, np.float64)
    for win, dil in DIL_PATTERNS:
        reach = (win // (2 * dil)) * dil
        count += (d % dil == 0) & (np.abs(d) <= reach)
    return np.where(count > 0, np.log2(np.maximum(count, 1.0)), NEG_INF).astype(np.float32)


def _dilattn_kernel(q_ref, qn_ref, k_ref, v_ref, bias_ref, o_ref, kpad, vt_ref, kmask,
                    sa_ref, sb_ref, m_ref, *, L):
    i = pl.program_id(1)
    nq = pl.num_programs(1)
    tq, hd, kt = DIL_Q_TILE, HEAD_DIM, DIL_K_TILE
    w = tq + 2 * DIL_PAD
    n_pad = DIL_PAD // kt

    @pl.when(i == 0)
    def _():
        zeros = jnp.zeros((DIL_PAD, GROUP_W), BF16)
        kpad[0:DIL_PAD, :] = zeros
        kpad[DIL_PAD + L:DIL_PAD + L + DIL_PAD, :] = zeros
        kpad[DIL_PAD:DIL_PAD + L, :] = k_ref[...]
        off = jnp.full((DIL_PAD, LANE), NEG_INF, F32)
        kmask[0:DIL_PAD, :] = off
        kmask[DIL_PAD + L:DIL_PAD + L + DIL_PAD, :] = off
        kmask[DIL_PAD:DIL_PAD + L, :] = jnp.zeros((L, LANE), F32)
        tail = (lax.broadcasted_iota(jnp.int32, (DIL_VT_ROWS - hd, kt), 0) == 0).astype(BF16)
        for c in range(L // kt + 2 * n_pad):
            inside = n_pad <= c < n_pad + L // kt
            if inside:
                lo = (c - n_pad) * kt
                vt = v_ref[lo:lo + kt, :].astype(F32).T.astype(BF16)
            for h in range(N_HEADS):
                vt_ref[c, h, 0:hd, :] = (vt[h * hd:(h + 1) * hd] if inside
                                         else jnp.zeros((hd, kt), BF16))
                vt_ref[c, h, hd:DIL_VT_ROWS, :] = tail

    feat_head = lax.broadcasted_iota(jnp.int32, (GROUP_W, 1), 0) // hd
    s_refs = (sa_ref, sb_ref)
    tiles = [(t * kt, (t + 1) * kt) for t in range(w // kt)]

    def stage(src_ref, tile, pair_next, nxt, pair_cur, cur):
        q0n = pl.multiple_of(tile * tq, tq)
        qt = src_ref[...].astype(F32).T
        qt2 = jnp.concatenate([jnp.where(feat_head == 2 * pair_next + c, qt, 0.0)
                               for c in range(2)], axis=1).astype(BF16)
        mx = None
        if pair_cur is not None:
            m = m_ref[cur, 0:1, :]
            acc = [jnp.zeros((DIL_VT_ROWS, tq), F32) for _ in range(2)]
        for t, (lo, hi) in enumerate(tiles):
            rows = pl.ds(q0n + lo, kt)
            km = kmask[rows, :]
            bias = bias_ref[lo:hi, :] + jnp.concatenate([km, km], axis=1)
            s = _dot(kpad[rows, :], qt2) + jnp.concatenate([bias, bias], axis=1)
            s_refs[nxt][lo:hi, :] = s
            cm = jnp.max(s, axis=0, keepdims=True)
            mx = cm if mx is None else jnp.maximum(mx, cm)
            if pair_cur is not None:
                p = jnp.exp2((s_refs[cur][lo:hi, :] - m).astype(BF16))
                for c in range(2):
                    acc[c] = acc[c] + _dot(vt_ref[i * (tq // kt) + t, 2 * pair_cur + c],
                                           p[:, c * tq:(c + 1) * tq])
        m_ref[nxt] = jnp.broadcast_to(mx, m_ref.shape[1:])
        if pair_cur is not None:
            return [o[0:hd] * (1.0 / o[hd:hd + 1]) for o in acc]

    @pl.when(i == 0)
    def _():
        stage(q_ref, i, 0, 0, None, None)

    heads = stage(q_ref, i, 1, 1, 0, 0)
    heads += stage(qn_ref, jnp.minimum(i + 1, nq - 1), 0, 0, 1, 1)
    o_ref[...] = jnp.concatenate(heads, axis=0).T.astype(BF16)


def _dilattn(h_main, bias, B, L):
    nq = L // DIL_Q_TILE
    w = DIL_Q_TILE + 2 * DIL_PAD
    return pl.pallas_call(
        functools.partial(_dilattn_kernel, L=L),
        grid=(B, nq),
        in_specs=[
            pl.BlockSpec((DIL_Q_TILE, GROUP_W), lambda b, i: (b * nq + i, BLK_AQ)),
            pl.BlockSpec((DIL_Q_TILE, GROUP_W),
                         lambda b, i: (b * nq + jnp.minimum(i + 1, nq - 1), BLK_AQ)),
            pl.BlockSpec((L, GROUP_W), lambda b, i: (b, BLK_AK)),
            pl.BlockSpec((L, GROUP_W), lambda b, i: (b, BLK_AV)),
            pl.BlockSpec((w, DIL_Q_TILE), lambda b, i: (0, 0)),
        ],
        out_specs=pl.BlockSpec((DIL_Q_TILE, GROUP_W), lambda b, i: (b * nq + i, 0)),
        out_shape=jax.ShapeDtypeStruct((B * L, GROUP_W), BF16),
        scratch_shapes=[pltpu.VMEM((L + 2 * DIL_PAD, GROUP_W), BF16),
                        pltpu.VMEM(((L + 2 * DIL_PAD) // DIL_K_TILE, N_HEADS, DIL_VT_ROWS,
                                    DIL_K_TILE), BF16),
                        pltpu.VMEM((L + 2 * DIL_PAD, LANE), F32),
                        pltpu.VMEM((w, 2 * DIL_Q_TILE), F32),
                        pltpu.VMEM((w, 2 * DIL_Q_TILE), F32),
                        pltpu.VMEM((2, F32_SUBLANES, 2 * DIL_Q_TILE), F32)],
        compiler_params=_cparams(("parallel", "arbitrary")),
        name="dilattn",
    )(h_main, h_main, h_main, h_main, bias)


def _split3(x):
    hi = x.astype(BF16)
    r1 = x - hi.astype(F32)
    mid = r1.astype(BF16)
    lo = (r1 - mid.astype(F32)).astype(BF16)
    return hi, mid, lo


def _split_dot(x, mat):
    return sum(_dot(t, mat) for t in _split3(x))


def _dot_f32(a, b):
    ah, am, al = _split3(a)
    bh, bm, bl = _split3(b)
    return (_dot(ah, bh) + (_dot(ah, bm) + _dot(am, bh))
            + (_dot(ah, bl) + _dot(am, bm) + _dot(al, bh)))


def _diffattn_kernel(q_ref, qn_ref, k_ref, v_ref, lam_ref, g_ref, hmean_ref, o_ref,
                     vt_ref, sa_ref, sb_ref, m_ref, oh_ref, *, lam_init, L):
    hd = HEAD_DIM
    first_tile = pl.program_id(1) == 0

    @pl.when(first_tile)
    def _():
        tail = (lax.broadcasted_iota(jnp.int32, (DIFF_VT_ROWS - hd, DIFF_K_CHUNK), 0) == 0)
        for c in range(L // DIFF_K_CHUNK):
            lo, hi = c * DIFF_K_CHUNK, (c + 1) * DIFF_K_CHUNK
            vt = v_ref[lo:hi, :].astype(F32).T.astype(BF16)
            for h in range(N_HEADS):
                vt_ref[h, 0:hd, lo:hi] = vt[h * hd:(h + 1) * hd]
                vt_ref[h, hd:DIFF_VT_ROWS, lo:hi] = tail.astype(BF16)

    lp = lam_ref[...]
    lam = (jnp.exp(jnp.sum(lp[0:1] * lp[1:2], axis=-1, keepdims=True))
           - jnp.exp(jnp.sum(lp[2:3] * lp[3:4], axis=-1, keepdims=True)) + lam_init)
    qt = q_ref[...].astype(F32).T
    qt_next = qn_ref[...].astype(F32).T
    feat_group = lax.broadcasted_iota(jnp.int32, (GROUP_W, 1), 0) // DIFF_QK_DIM
    tq = qt.shape[1]
    chunks = [(c * DIFF_K_CHUNK, (c + 1) * DIFF_K_CHUNK) for c in range(L // DIFF_K_CHUNK)]

    def masked_qt(h):
        wraps = h == N_HEADS
        src = jnp.where(wraps, qt_next, qt)
        hh = jnp.where(wraps, 0, h)
        return jnp.concatenate([jnp.where(feat_group == 2 * hh + c, src, 0.0) for c in range(2)],
                               axis=1).astype(BF16)

    def stage(h_next, nxt, h_cur, cur):
        qt2 = masked_qt(h_next)
        mx = None
        if h_cur is not None:
            m = m_ref[cur, 0:1, :]
            o = jnp.zeros((DIFF_VT_ROWS, 2 * tq), F32)
        for lo, hi in chunks:
            s = _dot(k_ref[lo:hi, :], qt2)
            s_refs[nxt][lo:hi, :] = s
            cm = jnp.max(s, axis=0, keepdims=True)
            mx = cm if mx is None else jnp.maximum(mx, cm)
            if h_cur is not None:
                p = jnp.exp2((s_refs[cur][lo:hi, :] - m).astype(BF16))
                o = o + _dot(vt_ref[h_cur, :, lo:hi], p)
        m_ref[nxt] = jnp.broadcast_to(mx, m_ref.shape[1:])
        if h_cur is not None:
            on = o[0:hd] * (1.0 / o[hd:hd + 1])
            oh_ref[h_cur] = on[:, 0:tq] - on[:, tq:2 * tq] * lam

    s_refs = (sa_ref, sb_ref)

    @pl.when(first_tile)
    def _():
        stage(0, 0, None, None)

    def body(j, carry):
        stage(2 * j + 1, 1, 2 * j, 0)
        stage(2 * j + 2, 0, 2 * j + 1, 1)
        return carry

    lax.fori_loop(0, N_HEADS // 2, body, 0)
    acc = jnp.concatenate([oh_ref[h] for h in range(N_HEADS)], axis=0).T
    ms = _split_dot(acc * acc, hmean_ref[...])
    y = acc * lax.rsqrt(ms + DIFF_SUBLN_EPS) * g_ref[...] * (1.0 - lam_init)
    o_ref[...] = y.astype(BF16)


def _diffattn(h_main, lam_p, subln_g, hmean, lam_init, B, L):
    nq = L // DIFF_Q_TILE
    return pl.pallas_call(
        functools.partial(_diffattn_kernel, lam_init=lam_init, L=L),
        grid=(B, nq),
        in_specs=[
            pl.BlockSpec((DIFF_Q_TILE, GROUP_W), lambda b, i: (b * nq + i, BLK_BQ)),
            pl.BlockSpec((DIFF_Q_TILE, GROUP_W),
                         lambda b, i: (b * nq + jnp.minimum(i + 1, nq - 1), BLK_BQ)),
            pl.BlockSpec((L, GROUP_W), lambda b, i: (b, BLK_BK)),
            pl.BlockSpec((L, GROUP_W), lambda b, i: (b, BLK_BV)),
            pl.BlockSpec((4, DIFF_QK_DIM), lambda b, i: (0, 0)),
            pl.BlockSpec((1, GROUP_W), lambda b, i: (0, 0)),
            pl.BlockSpec((GROUP_W, GROUP_W), lambda b, i: (0, 0)),
        ],
        out_specs=pl.BlockSpec((DIFF_Q_TILE, GROUP_W), lambda b, i: (b * nq + i, 0)),
        out_shape=jax.ShapeDtypeStruct((B * L, GROUP_W), BF16),
        scratch_shapes=[pltpu.VMEM((N_HEADS, DIFF_VT_ROWS, L), BF16),
                        pltpu.VMEM((L, 2 * DIFF_Q_TILE), F32),
                        pltpu.VMEM((L, 2 * DIFF_Q_TILE), F32),
                        pltpu.VMEM((2, F32_SUBLANES, 2 * DIFF_Q_TILE), F32),
                        pltpu.VMEM((N_HEADS, HEAD_DIM, DIFF_Q_TILE), F32)],
        compiler_params=_cparams(("parallel", "arbitrary")),
        name="diffattn",
    )(h_main, h_main, h_main, h_main, lam_p, subln_g, hmean)


def _log_sigmoid(v):
    return jnp.minimum(v, 0.0) - jnp.log(1.0 + jnp.exp(-jnp.abs(v)))


def _mlstm_kernel(q_ref, k_ref, v_ref, gi_ref, gf_ref, cw_ref, cb_ref, bi_ref, bf_ref,
                  ltri_ref, utri_ref, ecol_ref, elane_ref, kmask_ref, vmask_ref, ones_ref,
                  bd_ref, hsum_ref, o_ref,
                  qpad, kpad, qs, ks, hfw, hbw, cst, nst, mst, *, L):
    T = MLSTM_T
    nc = L // T
    halo = CONV_HALO
    nh = N_HEADS

    zpad = jnp.zeros((halo, GROUP_W), BF16)
    for pad, src in ((qpad, q_ref), (kpad, k_ref)):
        pad[0:halo, :] = zpad
        pad[halo + L:halo + L + halo, :] = zpad
        pad[halo:halo + L, :] = src[...]

    def conv_body(c, carry):
        r0 = pl.multiple_of(c * T, T)
        for idx, (pad, dst, scale) in enumerate(((qpad, qs, 1.0),
                                                 (kpad, ks, 1.0 / math.sqrt(HEAD_DIM)))):
            xw = pad[pl.ds(r0, T + 2 * halo), :].astype(F32)
            xm = pltpu.roll(xw, 1, axis=0)[halo:halo + T]
            xp = pltpu.roll(xw, T + 2 * halo - 1, axis=0)[halo:halo + T]
            xc = xw[halo:halo + T]
            lo, hi = idx * GROUP_W, (idx + 1) * GROUP_W
            y = (xm * cw_ref[0:1, lo:hi] + xc * cw_ref[1:2, lo:hi] + xp * cw_ref[2:3, lo:hi]
                 + cb_ref[0:1, lo:hi])
            dst[pl.ds(r0, T), :] = (_silu(y) * scale).astype(BF16)
        return carry

    lax.fori_loop(0, nc, conv_body, 0)

    cst[...] = jnp.zeros(cst.shape, F32)
    nst[...] = jnp.zeros(nst.shape, F32)
    mst[...] = jnp.zeros(mst.shape, F32)

    is_fw = lax.broadcasted_iota(jnp.int32, (1, LANE), 1) < nh
    rowi = lax.broadcasted_iota(jnp.int32, (T, LANE), 0)
    row4 = lax.broadcasted_iota(jnp.int32, (T, nh * T), 0)
    col4 = lax.broadcasted_iota(jnp.int32, (T, nh * T), 1) % T
    causal = (col4 <= row4, col4 >= row4)

    def body(c, carry):
        rows = (pl.multiple_of(c * T, T), pl.multiple_of((nc - 1 - c) * T, T))
        gate_i = jnp.where(is_fw, gi_ref[pl.ds(rows[0], T), :], gi_ref[pl.ds(rows[1], T), :])
        gate_f = jnp.where(is_fw, gf_ref[pl.ds(rows[0], T), :], gf_ref[pl.ds(rows[1], T), :])
        gate_i = gate_i + bi_ref[...]
        parts = _split3(_log_sigmoid(gate_f + bf_ref[...]))
        pre = sum(_dot(ltri_ref[...], t) for t in parts)
        suf = sum(_dot(utri_ref[...], t) for t in parts)
        cum = jnp.where(is_fw, pre, suf)
        b_end = jnp.where(is_fw, pre[T - 1:T, :], suf[0:1, :])
        key_w = gate_i - cum

        pmax, smax = key_w, key_w
        sh = 1
        while sh < T:
            pmax = jnp.maximum(pmax, jnp.where(rowi >= sh, pltpu.roll(pmax, sh, axis=0), NEG_INF))
            smax = jnp.maximum(smax, jnp.where(rowi < T - sh, pltpu.roll(smax, T - sh, axis=0),
                                               NEG_INF))
            sh *= 2
        m_prev = mst[0:1, :]
        inter = cum + m_prev
        m_t = jnp.maximum(inter, cum + jnp.where(is_fw, pmax, smax))
        u = cum - m_t
        a = b_end + key_w
        m_new = jnp.maximum(b_end + m_prev, jnp.max(a, axis=0, keepdims=True))
        mst[0:1, :] = m_new
        stack = jnp.concatenate(
            [jnp.exp(inter - m_t), jnp.exp(-m_t), jnp.exp(a - m_new),
             jnp.broadcast_to(jnp.exp(b_end + m_prev - m_new), (F32_SUBLANES, LANE))], axis=0)
        st_hi, st_mid, _ = _split3(stack)
        key_w_t = key_w.T

        for d in range(2):
            r0 = rows[d]
            qc = qs[pl.ds(r0, T), :]
            kc = ks[pl.ds(r0, T), :]
            vc = v_ref[pl.ds(r0, T), :]
            ex = _dot(st_hi, elane_ref[d]) + _dot(st_mid, elane_ref[d])
            w_inter, floor, w_key = ex[0:T], ex[T:2 * T], ex[2 * T:3 * T]
            sp_row = ex[3 * T:3 * T + 1]

            r_row = jnp.concatenate([key_w_t[d * nh + h:d * nh + h + 1, :] for h in range(nh)],
                                    axis=1)
            dlog = _split_dot(u, ecol_ref[d]) + r_row
            decay = jnp.exp(jnp.where(causal[d], dlog, NEG_INF))

            kt = kc.astype(F32).T.astype(BF16)
            qk = _dot(qc, jnp.concatenate([kt] * nh, axis=1) * kmask_ref[...]) * decay
            qk_hi = qk.astype(BF16)
            qk_lo = (qk - qk_hi.astype(F32)).astype(BF16)
            vblk = jnp.concatenate([vc] * nh, axis=0) * vmask_ref[...]
            num = _dot(qk_hi, vblk)
            den = _dot(qk_hi, ones_ref[...]) + _dot(qk_lo, ones_ref[...])

            c_prev = cst[d]
            n_prev = nst[d:d + 1, :]
            num = num + w_inter * _dot(qc, c_prev.astype(BF16))
            den = den + w_inter * _dot((qc.astype(F32) * n_prev).astype(BF16), hsum_ref[...])
            h_out = num / jnp.maximum(jnp.abs(den), floor)
            if d == 0:
                hfw[pl.ds(r0, T), :] = h_out
            else:
                hbw[pl.ds(r0, T), :] = h_out

            kw = kc.astype(F32) * w_key
            cst[d] = c_prev * sp_row + _dot_tn(kw.astype(BF16), vc) * bd_ref[...]
            nst[d:d + 1, :] = n_prev * sp_row + jnp.sum(kw, axis=0, keepdims=True)
        return carry

    lax.fori_loop(0, nc, body, 0)
    o_ref[...] = (hfw[...] + hbw[...]).astype(BF16)


def _mlstm_consts():
    T, nh, g = MLSTM_T, N_HEADS, GROUP_W
    tri = np.tril(np.ones((T, T), np.float32))
    src = np.arange(LANE)[:, None]
    ecol = np.stack([(src == d * nh + np.arange(nh * T)[None, :] // T) for d in range(2)])
    elane = np.stack([(src == d * nh + np.arange(g)[None, :] // HEAD_DIM) for d in range(2)])
    head_of = np.arange(g) // HEAD_DIM
    blk_of = np.arange(nh * T) // T
    kmask = head_of[:, None] == blk_of[None, :]
    vmask = blk_of[:, None] == head_of[None, :]
    same_head = head_of[:, None] == head_of[None, :]
    b16 = lambda a: jnp.asarray(a.astype(np.float32), BF16)
    return (b16(tri), b16(tri.T), b16(ecol), b16(elane), b16(kmask), b16(vmask), b16(vmask),
            jnp.asarray(same_head.astype(np.float32)), b16(same_head))


def _mlstm(h_main, gate_i, gate_f, conv_w, conv_b, bias_i, bias_f, consts, B, L):
    T = MLSTM_T
    const = lambda b: (0, 0)
    const3 = lambda b: (0, 0, 0)
    return pl.pallas_call(
        functools.partial(_mlstm_kernel, L=L),
        grid=(B,),
        in_specs=[
            pl.BlockSpec((L, GROUP_W), lambda b: (b, BLK_DQ)),
            pl.BlockSpec((L, GROUP_W), lambda b: (b, BLK_DK)),
            pl.BlockSpec((L, GROUP_W), lambda b: (b, BLK_DV)),
            pl.BlockSpec((L, LANE), lambda b: (b, 0)),
            pl.BlockSpec((L, LANE), lambda b: (b, 0)),
            pl.BlockSpec((3, 2 * GROUP_W), const),
            pl.BlockSpec((1, 2 * GROUP_W), const),
            pl.BlockSpec((1, LANE), const),
            pl.BlockSpec((1, LANE), const),
            pl.BlockSpec((T, T), const),
            pl.BlockSpec((T, T), const),
            pl.BlockSpec((2, LANE, N_HEADS * T), const3),
            pl.BlockSpec((2, LANE, GROUP_W), const3),
            pl.BlockSpec((GROUP_W, N_HEADS * T), const),
            pl.BlockSpec((N_HEADS * T, GROUP_W), const),
            pl.BlockSpec((N_HEADS * T, GROUP_W), const),
            pl.BlockSpec((GROUP_W, GROUP_W), const),
            pl.BlockSpec((GROUP_W, GROUP_W), const),
        ],
        out_specs=pl.BlockSpec((L, GROUP_W), lambda b: (b, 0)),
        out_shape=jax.ShapeDtypeStruct((B * L, GROUP_W), BF16),
        scratch_shapes=[
            pltpu.VMEM((L + 2 * CONV_HALO, GROUP_W), BF16),
            pltpu.VMEM((L + 2 * CONV_HALO, GROUP_W), BF16),
            pltpu.VMEM((L, GROUP_W), BF16),
            pltpu.VMEM((L, GROUP_W), BF16),
            pltpu.VMEM((L, GROUP_W), F32),
            pltpu.VMEM((L, GROUP_W), F32),
            pltpu.VMEM((2, GROUP_W, GROUP_W), F32),
            pltpu.VMEM((F32_SUBLANES, GROUP_W), F32),
            pltpu.VMEM((F32_SUBLANES, LANE), F32),
        ],
        compiler_params=_cparams(("parallel",)),
        name="mlstm",
    )(h_main, h_main, h_main, gate_i, gate_f, conv_w, conv_b, bias_i, bias_f, *consts)


def _hy_dims(L):
    h1 = L // HY_N2
    k1h = h1 + 1
    k1p = -(-k1h // BF16_SUBLANES) * BF16_SUBLANES
    kg = HY_K1_GROUP if k1p % HY_K1_GROUP == 0 else BF16_SUBLANES
    return h1, k1h, k1p, kg


def _hy_tables(L):
    h1, k1h, k1p, _ = _hy_dims(L)
    n1_len = 2 * h1
    n = 2 * L
    k1 = np.arange(k1p)[:, None].astype(np.float64)
    live = (np.arange(k1p) < k1h)[:, None]
    n1 = np.arange(h1)[None, :].astype(np.float64)
    ang = 2.0 * np.pi * k1 * n1 / n1_len
    m1 = np.concatenate([np.where(live, np.cos(ang), 0.0), np.where(live, -np.sin(ang), 0.0)], 0)

    n2 = np.arange(HY_N2)[None, None, :].astype(np.float64)
    k2 = np.arange(HY_N2)[None, :, None].astype(np.float64)
    kk = np.arange(k1p)[:, None, None] + n1_len * k2
    th = 2.0 * np.pi * kk * n2 / n
    fr, fi = np.cos(th), -np.sin(th)
    f3 = np.concatenate([np.concatenate([fr, -fi], 2), np.concatenate([fi, fr], 2)], 1)
    er, ei = np.transpose(np.cos(th), (0, 2, 1)), np.transpose(np.sin(th), (0, 2, 1))
    f3i = np.concatenate([np.concatenate([er, -ei], 2), np.concatenate([ei, er], 2)], 1)
    live3 = (np.arange(k1p) < k1h)[:, None, None]
    f3 = np.where(live3, f3, 0.0)
    f3i = np.where(live3, f3i, 0.0)

    nn1 = np.arange(h1)[:, None].astype(np.float64)
    kc = np.arange(k1p)[None, :].astype(np.float64)
    ph = 2.0 * np.pi * nn1 * kc / n1_len
    edge = (np.arange(k1p) == 0) | (np.arange(k1p) == h1)
    livec = (np.arange(k1p) < k1h)[None, :]
    m4r = np.where(livec, np.where(edge[None, :], np.cos(ph), 2.0 * np.cos(ph)), 0.0) / n
    m4i = np.where(livec & ~edge[None, :], -2.0 * np.sin(ph), 0.0) / n
    return (jnp.asarray(m1, BF16), jnp.asarray(f3, BF16), jnp.asarray(f3i, BF16),
            jnp.asarray(m4r, BF16), jnp.asarray(m4i, BF16))


def _to_wide(x, h1):
    return x.astype(F32).reshape(h1, HY_N2 * GROUP_W)


def _hy_stage1(z_bf16, m1_ref, a_ref, k1p):
    a = _dot(m1_ref[...], z_bf16)
    a_ref[0, 0] = a[0:k1p].reshape(k1p, HY_N2, GROUP_W).astype(BF16)
    a_ref[0, 1] = a[k1p:2 * k1p].reshape(k1p, HY_N2, GROUP_W).astype(BF16)


def _hy_conv3_wide(x, w_ref, b_ref, j, h1):
    c = GROUP_W
    wl = HY_N2 * c
    rowi = lax.broadcasted_iota(jnp.int32, (h1, c), 0)
    tail = x[:, wl - c:wl]
    head = x[:, 0:c]
    prev_tail = jnp.where(rowi == 0, 0.0, pltpu.roll(tail, 1, axis=0))
    next_head = jnp.where(rowi == h1 - 1, 0.0, pltpu.roll(head, h1 - 1, axis=0))
    xm = jnp.concatenate([prev_tail, x[:, 0:wl - c]], axis=1)
    xp = jnp.concatenate([x[:, c:wl], next_head], axis=1)
    return (xm * w_ref[3 * j:3 * j + 1, :] + x * w_ref[3 * j + 1:3 * j + 2, :]
            + xp * w_ref[3 * j + 2:3 * j + 3, :] + b_ref[j:j + 1, :])


def _hy_front_kernel(v_ref, x1_ref, x2_ref, w_ref, b_ref, m1_ref,
                     z_ref, x1c_ref, x2c_ref, a_ref, *, h1, k1p):
    z = _hy_conv3_wide(_to_wide(v_ref[...], h1), w_ref, b_ref, 0, h1).astype(BF16)
    z_ref[0] = z
    x1c_ref[0] = _hy_conv3_wide(_to_wide(x1_ref[...], h1), w_ref, b_ref, 1, h1).astype(BF16)
    x2c_ref[0] = _hy_conv3_wide(_to_wide(x2_ref[...], h1), w_ref, b_ref, 2, h1).astype(BF16)
    _hy_stage1(z, m1_ref, a_ref, k1p)


def _hy_front(cv, cx1, cx2, w_wide, b_wide, m1, B, L):
    h1, _, k1p, _ = _hy_dims(L)
    wl = HY_N2 * GROUP_W
    nat = pl.BlockSpec((L, GROUP_W), lambda b: (b, 0))
    seq = pl.BlockSpec((1, h1, wl), lambda b: (b, 0, 0))
    wide = jax.ShapeDtypeStruct((B, h1, wl), BF16)
    slab = (1, 2, k1p, HY_N2, GROUP_W)
    return pl.pallas_call(
        functools.partial(_hy_front_kernel, h1=h1, k1p=k1p),
        grid=(B,),
        in_specs=[nat, nat, nat,
                  pl.BlockSpec((9, wl), lambda b: (0, 0)),
                  pl.BlockSpec((3, wl), lambda b: (0, 0)),
                  pl.BlockSpec((2 * k1p, h1), lambda b: (0, 0))],
        out_specs=(seq, seq, seq, pl.BlockSpec(slab, lambda b: (b, 0, 0, 0, 0))),
        out_shape=(wide, wide, wide,
                   jax.ShapeDtypeStruct((B, 2, k1p, HY_N2, GROUP_W), BF16)),
        compiler_params=_cparams(("parallel",)),
        name="hy_front",
    )(cv, cx1, cx2, w_wide, b_wide, m1)


def _hy_stage1_kernel(z_ref, m1_ref, a_ref, *, h1, k1p):
    _hy_stage1(_to_wide(z_ref[0], h1).astype(BF16), m1_ref, a_ref, k1p)


def _hy_stage1_call(z_nat, m1, L):
    h1, _, k1p, _ = _hy_dims(L)
    nb = z_nat.shape[0]
    return pl.pallas_call(
        functools.partial(_hy_stage1_kernel, h1=h1, k1p=k1p),
        grid=(nb,),
        in_specs=[pl.BlockSpec((1, L, GROUP_W), lambda b: (b, 0, 0)),
                  pl.BlockSpec((2 * k1p, h1), lambda b: (0, 0))],
        out_specs=pl.BlockSpec((1, 2, k1p, HY_N2, GROUP_W), lambda b: (b, 0, 0, 0, 0)),
        out_shape=jax.ShapeDtypeStruct((nb, 2, k1p, HY_N2, GROUP_W), BF16),
        compiler_params=_cparams(("parallel",)),
        name="hy_stage1",
    )(z_nat, m1)


def _hy_spectrum_kernel(af_ref, ab_ref, f3_ref, g_ref, *, kg, k1h):
    grp = pl.program_id(0)

    @pl.when(grp * kg < k1h)
    def _():
        for kk in range(kg):
            xs = []
            for a_ref in (af_ref, ab_ref):
                a2 = jnp.concatenate([a_ref[0, 0, kk], a_ref[0, 1, kk]], axis=0)
                xs.append(_dot(f3_ref[kk], a2))
            g_ref[0, 0, kk] = xs[0][0:HY_N2] + xs[1][0:HY_N2]
            g_ref[0, 1, kk] = xs[0][HY_N2:2 * HY_N2] - xs[1][HY_N2:2 * HY_N2]

    @pl.when(grp * kg >= k1h)
    def _():
        g_ref[...] = jnp.zeros(g_ref.shape, F32)


def _hy_spectrum(a_filt, f3, L):
    _, k1h, k1p, kg = _hy_dims(L)
    a5 = a_filt
    blk = (1, 2, kg, HY_N2, GROUP_W)
    return pl.pallas_call(
        functools.partial(_hy_spectrum_kernel, kg=kg, k1h=k1h),
        grid=(k1p // kg, HYENA_ORDER),
        in_specs=[pl.BlockSpec(blk, lambda g, o: (2 * o, 0, g, 0, 0)),
                  pl.BlockSpec(blk, lambda g, o: (2 * o + 1, 0, g, 0, 0)),
                  pl.BlockSpec((kg, 2 * HY_N2, 2 * HY_N2), lambda g, o: (g, 0, 0))],
        out_specs=pl.BlockSpec(blk, lambda g, o: (o, 0, g, 0, 0)),
        out_shape=jax.ShapeDtypeStruct((HYENA_ORDER, 2, k1p, HY_N2, GROUP_W), F32),
        compiler_params=_cparams(("parallel", "parallel")),
        name="hy_spectrum",
    )(a5, a5, f3)


def _hy_mid_kernel(a_ref, g_ref, f3_ref, f3i_ref, b_ref, *, kg, k1h):
    grp = pl.program_id(0)

    @pl.when(grp * kg < k1h)
    def _():
        for kk in range(kg):
            a2 = jnp.concatenate([a_ref[0, 0, kk], a_ref[0, 1, kk]], axis=0)
            x = _dot(f3_ref[kk], a2)
            xr, xi = x[0:HY_N2], x[HY_N2:2 * HY_N2]
            gr, gi = g_ref[0, 0, kk], g_ref[0, 1, kk]
            y2 = jnp.concatenate([xr * gr - xi * gi, xr * gi + xi * gr], axis=0).astype(BF16)
            bm = _dot(f3i_ref[kk], y2)
            b_ref[0, 0, kk] = bm[0:HY_N2].astype(BF16)
            b_ref[0, 1, kk] = bm[HY_N2:2 * HY_N2].astype(BF16)

    @pl.when(grp * kg >= k1h)
    def _():
        b_ref[...] = jnp.zeros(b_ref.shape, BF16)


def _hy_mid(a, g_spec, order, f3, f3i, B, L):
    _, k1h, k1p, kg = _hy_dims(L)
    blk = (1, 2, kg, HY_N2, GROUP_W)
    tab = pl.BlockSpec((kg, 2 * HY_N2, 2 * HY_N2), lambda g, b: (g, 0, 0))
    return pl.pallas_call(
        functools.partial(_hy_mid_kernel, kg=kg, k1h=k1h),
        grid=(k1p // kg, B),
        in_specs=[pl.BlockSpec(blk, lambda g, b: (b, 0, g, 0, 0)),
                  pl.BlockSpec(blk, lambda g, b: (order, 0, g, 0, 0)),
                  tab, tab],
        out_specs=pl.BlockSpec(blk, lambda g, b: (b, 0, g, 0, 0)),
        out_shape=jax.ShapeDtypeStruct((B, 2, k1p, HY_N2, GROUP_W), BF16),
        compiler_params=_cparams(("parallel", "parallel")),
        name="hy_mid",
    )(a, g_spec, f3, f3i)


def _hy_back_kernel(b_ref, z_ref, x_ref, bias_ref, m4r_ref, m4i_ref, *rest, k1p, last):
    wl = HY_N2 * GROUP_W
    br = b_ref[0, 0].astype(F32).reshape(k1p, wl).astype(BF16)
    bi = b_ref[0, 1].astype(F32).reshape(k1p, wl).astype(BF16)
    y = _dot(m4r_ref[...], br) + _dot(m4i_ref[...], bi)
    z_new = x_ref[0].astype(F32) * (y + z_ref[0].astype(F32) * bias_ref[...])
    zb = z_new.astype(BF16)
    if last:
        (o_ref,) = rest
        o_ref[...] = z_new.reshape(o_ref.shape).astype(BF16)
    else:
        m1_ref, o_ref, a_ref = rest
        o_ref[0] = zb
        _hy_stage1(zb, m1_ref, a_ref, k1p)


def _hy_back(b, z, xg, bias_wide, m4r, m4i, m1, B, L, last):
    h1, _, k1p, _ = _hy_dims(L)
    wl = HY_N2 * GROUP_W
    seq = pl.BlockSpec((1, h1, wl), lambda i: (i, 0, 0))
    slab = pl.BlockSpec((1, 2, k1p, HY_N2, GROUP_W), lambda i: (i, 0, 0, 0, 0))
    const = lambda i: (0, 0)
    in_specs = [slab, seq, seq, pl.BlockSpec((1, wl), const),
                pl.BlockSpec((h1, k1p), const), pl.BlockSpec((h1, k1p), const)]
    args = [b, z, xg, bias_wide, m4r, m4i]
    if last:
        out_specs = pl.BlockSpec((L, GROUP_W), lambda i: (i, 0))
        out_shape = jax.ShapeDtypeStruct((B * L, GROUP_W), BF16)
    else:
        in_specs.append(pl.BlockSpec((2 * k1p, h1), const))
        args.append(m1)
        out_specs = (seq, slab)
        out_shape = (jax.ShapeDtypeStruct((B, h1, wl), BF16),
                     jax.ShapeDtypeStruct((B, 2, k1p, HY_N2, GROUP_W), BF16))
    return pl.pallas_call(
        functools.partial(_hy_back_kernel, k1p=k1p, last=last),
        grid=(B,),
        in_specs=in_specs,
        out_specs=out_specs,
        out_shape=out_shape,
        compiler_params=_cparams(("parallel",)),
        name="hy_back_last" if last else "hy_back",
    )(*args)


def _hy_filter_kernel(f_ref, t_ref, w1_ref, b1_ref, w2_ref, b2_ref, w3_ref, fr_ref, ad_ref,
                      o_ref, *, tile):
    i = pl.program_id(0)
    freq = fr_ref[...]
    z = jnp.sin(freq * (_dot_f32(f_ref[...], w1_ref[...]) + b1_ref[...]))
    z = jnp.sin(freq * (_dot_f32(z, w2_ref[...]) + b2_ref[...]))
    decay = jnp.exp(-t_ref[...] * ad_ref[...])
    rowi = i * tile + lax.broadcasted_iota(jnp.int32, (tile, GROUP_W), 0)
    for j in range(2 * HYENA_ORDER):
        hj = _dot_f32(z, w3_ref[:, j * GROUP_W:(j + 1) * GROUP_W]) * decay
        if j % 2 == 1:
            hj = jnp.where(rowi == 0, 0.0, hj)
        o_ref[j] = hj.astype(BF16)


def _hy_filter(feats, tcol, w1, b1, w2, b2, w3, freq, absdelta, L):
    tile = math.gcd(L, ROW_TILE)
    const = lambda i: (0, 0)
    return pl.pallas_call(
        functools.partial(_hy_filter_kernel, tile=tile),
        grid=(L // tile,),
        in_specs=[pl.BlockSpec((tile, LANE), lambda i: (i, 0)),
                  pl.BlockSpec((tile, 1), lambda i: (i, 0)),
                  pl.BlockSpec((LANE, HYENA_HIDDEN), const),
                  pl.BlockSpec((1, HYENA_HIDDEN), const),
                  pl.BlockSpec((HYENA_HIDDEN, HYENA_HIDDEN), const),
                  pl.BlockSpec((1, HYENA_HIDDEN), const),
                  pl.BlockSpec((HYENA_HIDDEN, 2 * HYENA_ORDER * GROUP_W), const),
                  pl.BlockSpec((1, HYENA_HIDDEN), const),
                  pl.BlockSpec((1, GROUP_W), const)],
        out_specs=pl.BlockSpec((2 * HYENA_ORDER, tile, GROUP_W), lambda i: (0, i, 0)),
        out_shape=jax.ShapeDtypeStruct((2 * HYENA_ORDER, L, GROUP_W), BF16),
        compiler_params=_cparams(("parallel",)),
        name="hy_filter",
    )(feats, tcol, w1, b1, w2, b2, w3, freq, absdelta)


def _outproj_kernel(x_ref, oa_ref, ob_ref, oc_ref, od_ref, ox_ref,
                    ga_ref, gb_ref, gc_ref, gdo_ref, gdg_ref, gx_ref, w_ref, png_ref, out_ref):
    f = lambda r: r[...].astype(F32)
    branches = (
        f(oa_ref) * _silu(f(ga_ref)),
        f(ob_ref) * _silu(f(gb_ref)),
        f(oc_ref) * _silu(f(gc_ref)),
        f(od_ref) * _sigmoid(f(gdo_ref)) * _silu(f(gdg_ref)),
        f(ox_ref) * _silu(f(gx_ref)),
    )
    y = jnp.zeros(x_ref.shape, F32)
    for j, br in enumerate(branches):
        y = y + _dot(br.astype(BF16), w_ref[j * GROUP_W:(j + 1) * GROUP_W, :])
    ms = jnp.mean(y * y, axis=-1, keepdims=True)
    out_ref[...] = x_ref[...] + y * lax.rsqrt(ms + NORM_EPS) * png_ref[...]


def _outproj(x2d, outs, h_main, w_out, png, row0=0, n_rows=None):
    n_rows = x2d.shape[0] if n_rows is None else n_rows
    nt = n_rows // ROW_TILE
    t0 = row0 // ROW_TILE
    row = lambda i: (t0 + i, 0)
    blk = lambda j: pl.BlockSpec((ROW_TILE, GROUP_W), lambda i: (t0 + i, j))
    return pl.pallas_call(
        _outproj_kernel,
        grid=(nt,),
        in_specs=[pl.BlockSpec((ROW_TILE, D_MODEL), row)]
        + [pl.BlockSpec((ROW_TILE, GROUP_W), row)] * 5
        + [blk(BLK_AG), blk(BLK_BG), blk(BLK_CG), blk(BLK_DO), blk(BLK_DG), blk(BLK_XG)]
        + [pl.BlockSpec((5 * GROUP_W, D_MODEL), lambda i: (0, 0)),
           pl.BlockSpec((1, D_MODEL), lambda i: (0, 0))],
        out_specs=pl.BlockSpec((ROW_TILE, D_MODEL), lambda i: (i, 0)),
        out_shape=jax.ShapeDtypeStruct((n_rows, D_MODEL), F32),
        compiler_params=_cparams(("parallel",)),
        name="outproj",
    )(x2d, *outs, h_main, h_main, h_main, h_main, h_main, h_main, w_out, png)


def _rope_tables(L, group, rot_dim):
    half = rot_dim // 2
    inv = 1.0 / (ROPE_THETA ** (jnp.arange(0, rot_dim, 2, dtype=F32) / rot_dim))
    ang = jnp.arange(L, dtype=F32)[:, None] * inv[None, :]
    cos, sin = jnp.cos(ang), jnp.sin(ang)
    lane = np.arange(GROUP_W) % group
    in_rot = lane < rot_dim
    idx = lane % half
    c = jnp.where(in_rot[None, :], cos[:, idx], 1.0)
    s = jnp.where(in_rot[None, :], sin[:, idx], 0.0)
    p = np.zeros((GROUP_W, GROUP_W), np.float32)
    for j in range(GROUP_W):
        if lane[j] < half:
            p[j + half, j] = -1.0
        elif lane[j] < rot_dim:
            p[j - half, j] = 1.0
    return c, s, jnp.asarray(p, BF16)


def _hyena_features(L):
    t = jnp.linspace(0.0, 1.0, L, dtype=F32)[:, None]
    bands = jnp.linspace(1e-4, HYENA_BANDS - 1, HYENA_BANDS, dtype=F32)
    ang = (2.0 * math.pi / L) * jnp.arange(L, dtype=F32)[:, None] * bands[None, :]
    feats = jnp.concatenate([t, jnp.cos(ang), -jnp.sin(ang)], axis=-1)
    return jnp.pad(feats, ((0, 0), (0, LANE - HYENA_EMB))), t


def _relayout_w_in(w):
    g = GROUP_W
    off_c, off_d = 8 * g, 12 * g
    off_gate = off_d + 5 * g
    off_x = off_gate + N_MLSTM_GATES
    main = jnp.concatenate([w[:, 0:off_c], w[:, off_c + 3 * g:off_c + 4 * g],
                            w[:, off_d:off_gate], w[:, off_x:off_x + 2 * g]], axis=1)
    hy = w[:, off_c:off_c + 3 * g]
    gate_i, gate_f = _split_gates(w[:, off_gate:off_x])
    return jnp.concatenate([main, hy, gate_i, gate_f], axis=1).astype(BF16)


def _split_gates(t):
    nh = N_HEADS
    pad = [(0, 0)] * (t.ndim - 1) + [(0, LANE - 2 * nh)]
    gi = jnp.concatenate([t[..., 0:nh], t[..., 2 * nh:3 * nh]], axis=-1)
    gf = jnp.concatenate([t[..., nh:2 * nh], t[..., 3 * nh:4 * nh]], axis=-1)
    return jnp.pad(gi, pad), jnp.pad(gf, pad)


def _trunk(x, mem, splits, pre_norm_g, post_norm_g, w_in, w_out, diff_lambda, diff_subln_g,
           hy_conv_w, hy_conv_b, hy_ffn_w1, hy_ffn_b1, hy_ffn_w2, hy_ffn_b2, hy_ffn_w3,
           hy_freq, hy_bias, ml_conv_w, ml_conv_b, ml_gate_b, mem_norm_g, w_mem_kv):
    B, L, _ = x.shape
    M = mem.shape[1]
    depth = w_in.shape[0]
    g = GROUP_W
    h1, _, k1p, _ = _hy_dims(L)
    wl = HY_N2 * g

    ca, sa, pa = _rope_tables(L, HEAD_DIM, HEAD_DIM // ROPE_FRACTION)
    cb, sb, pb = _rope_tables(L, DIFF_QK_DIM, DIFF_QK_DIM // ROPE_FRACTION)
    rope_tabs = (ca, sa, cb, sb, pa, pb)
    dil_bias = jnp.asarray(_dil_bias_table())
    feats, tcol = _hyena_features(L)
    absdelta = jnp.abs(jnp.linspace(math.log(HYENA_TARGET) / HYENA_SLOW_DECAY,
                                    math.log(HYENA_TARGET) / HYENA_FAST_DECAY, g, dtype=F32))[None]
    m1, f3, f3i, m4r, m4i = _hy_tables(L)
    head_of = np.arange(g) // HEAD_DIM
    same_head = (head_of[:, None] == head_of[None, :]).astype(np.float32)
    hmean = jnp.asarray(same_head / HEAD_DIM, BF16)
    mlstm_consts = _mlstm_consts()

    x2d = x.reshape(B * L, D_MODEL)
    mem2d = mem.reshape(B * M, D_MODEL)
    for li in range(depth):
        lam_init = 0.8 - 0.6 * math.exp(-0.3 * li)
        w_all = _relayout_w_in(w_in[li])
        h_main, cv, cx1, cx2, gate_i, gate_f = _inproj(x2d, pre_norm_g[li][None], w_all,
                                                       rope_tabs, B, L)

        oa = _dilattn(h_main, dil_bias, B, L)
        ob = _diffattn(h_main, diff_lambda[li], jnp.tile(diff_subln_g[li], N_HEADS)[None],
                       hmean, lam_init, B, L)

        w1 = jnp.pad(hy_ffn_w1[li], ((0, LANE - HYENA_EMB), (0, 0)))
        hfilt = _hy_filter(feats, tcol, w1, hy_ffn_b1[li][None], hy_ffn_w2[li],
                           hy_ffn_b2[li][None], hy_ffn_w3[li], hy_freq[li][None], absdelta, L)
        a_filt = _hy_stage1_call(hfilt, m1, L)
        g_spec = _hy_spectrum(a_filt, f3, L)
        cw = jnp.tile(hy_conv_w[li].reshape(3, 3, g).transpose(1, 0, 2).reshape(9, g), (1, HY_N2))
        cbw = jnp.tile(hy_conv_b[li].reshape(3, g), (1, HY_N2))
        z, x1c, x2c, a = _hy_front(cv, cx1, cx2, cw, cbw, m1, B, L)
        bias_w = jnp.tile(hy_bias[li], (1, HY_N2))
        bsp = _hy_mid(a, g_spec, 0, f3, f3i, B, L)
        z, a = _hy_back(bsp, z, x1c, bias_w[0:1], m4r, m4i, m1, B, L, last=False)
        bsp = _hy_mid(a, g_spec, 1, f3, f3i, B, L)
        oc = _hy_back(bsp, z, x2c, bias_w[1:2], m4r, m4i, m1, B, L, last=True)

        bias_i, bias_f = _split_gates(ml_gate_b[li][None])
        od = _mlstm(h_main, gate_i, gate_f, ml_conv_w[li], ml_conv_b[li][None], bias_i, bias_f,
                    mlstm_consts, B, L)

        mkv = _memkv(mem2d, mem_norm_g[li][None], w_mem_kv[li].astype(BF16))
        ox = _memattn(h_main, mkv, B, L, M)

        branch_outs = (oa, ob, oc, od, ox)
        w_o, png = w_out[li].astype(BF16), post_norm_g[li][None]
        if li + 1 < depth:
            x2d = _outproj(x2d, branch_outs, h_main, w_o, png)
    ys, b0 = [], 0
    for nb in splits:
        y = _outproj(x2d, branch_outs, h_main, w_o, png, row0=b0 * L, n_rows=nb * L)
        ys.append(y.reshape(nb, L, D_MODEL))
        b0 += nb
    return tuple(ys)


def kernel(x_prompt, x_sample, mem_prompt, mem_sample, pre_norm_g, post_norm_g, w_in, w_out,
           diff_lambda, diff_subln_g, hy_conv_w, hy_conv_b, hy_ffn_w1, hy_ffn_b1, hy_ffn_w2,
           hy_ffn_b2, hy_ffn_w3, hy_freq, hy_bias, ml_conv_w, ml_conv_b, ml_gate_b,
           mem_norm_g, w_mem_kv):
    x = jnp.concatenate([x_prompt, x_sample], axis=0)
    mem = jnp.concatenate([mem_prompt, mem_sample], axis=0)
    return _trunk(x, mem, (x_prompt.shape[0], x_sample.shape[0]), pre_norm_g, post_norm_g, w_in,
                  w_out, diff_lambda, diff_subln_g, hy_conv_w, hy_conv_b, hy_ffn_w1, hy_ffn_b1,
                  hy_ffn_w2, hy_ffn_b2, hy_ffn_w3, hy_freq, hy_bias, ml_conv_w, ml_conv_b,
                  ml_gate_b, mem_norm_g, w_mem_kv)
```

```python
import functools
import math

import numpy as np
import jax
import jax.numpy as jnp
from jax import lax
from jax.experimental import pallas as pl
from jax.experimental.pallas import tpu as pltpu

F32 = jnp.float32
BF16 = jnp.bfloat16

D_MODEL = 1024
HEAD_DIM = 64
GROUP_W = 256
N_HEADS = GROUP_W // HEAD_DIM
NORM_EPS = 1e-6
NEG_INF = -1e30
ROPE_THETA = 500000.0
ROPE_FRACTION = 4
DIL_PATTERNS = ((128, 1), (512, 4), (2048, 16))
DIFF_QK_DIM = HEAD_DIM // 2
DIFF_SUBLN_EPS = 1e-5
HYENA_ORDER = 2
HYENA_BANDS = 16
HYENA_EMB = 1 + 2 * HYENA_BANDS
HYENA_HIDDEN = 64
HYENA_FAST_DECAY = 0.3
HYENA_SLOW_DECAY = 1.5
HYENA_TARGET = 1e-2
N_MLSTM_GATES = 4 * N_HEADS

(BLK_AG, BLK_BG, BLK_CG, BLK_DO, BLK_DG, BLK_XG, BLK_AQ, BLK_AK, BLK_AV, BLK_BQ, BLK_BK, BLK_BV,
 BLK_DQ, BLK_DK, BLK_DV, BLK_XQ) = range(16)
N_GATE_BLK = 6
N_MAIN_BLK = 16
MAIN_W = N_MAIN_BLK * GROUP_W
W_ALL = MAIN_W + 4 * GROUP_W

VMEM_LIMIT_BYTES = 56 * 1024 * 1024
LANE = 128
F32_SUBLANES = 8
BF16_SUBLANES = 16

ROW_TILE = 512
DIL_Q_TILE = 256
DIL_PAD = 1024
DIL_K_TILE = 256
DIL_VT_ROWS = HEAD_DIM + BF16_SUBLANES
DIFF_Q_TILE = 256
DIFF_K_CHUNK = 512
DIFF_VT_ROWS = HEAD_DIM + BF16_SUBLANES
LOG2E = 1.4426950408889634
MEM_Q_TILE = 512
MLSTM_T = 128
CONV_HALO = 16
HY_N2 = 128
HY_K1_GROUP = 12


def _cparams(sem, vmem=VMEM_LIMIT_BYTES):
    return pltpu.CompilerParams(dimension_semantics=sem, vmem_limit_bytes=vmem)


def _sigmoid(v):
    return 1.0 / (1.0 + jnp.exp(-v))


def _silu(v):
    return v * _sigmoid(v)


def _dot(a, b):
    return jnp.dot(a, b, preferred_element_type=F32)


def _dot_tn(a, b):
    return lax.dot_general(a, b, (((0,), (0,)), ((), ())), preferred_element_type=F32)


def _inproj_kernel(x_ref, g_ref, w_ref, ca_ref, sa_ref, cb_ref, sb_ref, pa_ref, pb_ref,
                   h_ref, cv_ref, cx1_ref, cx2_ref, gi_ref, gf_ref):
    x = x_ref[...]
    ms = jnp.mean(x * x, axis=-1, keepdims=True)
    xn = (x * lax.rsqrt(ms + NORM_EPS) * g_ref[...]).astype(BF16)

    def proj(j, width=GROUP_W):
        return _dot(xn, w_ref[:, j * GROUP_W:j * GROUP_W + width])

    def rope(acc, c_ref, s_ref, p_ref):
        partner = _dot(acc.astype(BF16), p_ref[...])
        return acc * c_ref[...] + partner * s_ref[...]

    def finish(j, acc):
        if j in (BLK_AQ, BLK_AK):
            acc = rope(acc, ca_ref, sa_ref, pa_ref)
        if j in (BLK_BQ, BLK_BK):
            acc = rope(acc, cb_ref, sb_ref, pb_ref)
        if j in (BLK_AQ, BLK_XQ):
            acc = acc * (LOG2E / math.sqrt(HEAD_DIM))
        if j == BLK_BQ:
            acc = acc * (LOG2E / math.sqrt(DIFF_QK_DIM))
        if j < N_MAIN_BLK:
            h_ref[:, j * GROUP_W:(j + 1) * GROUP_W] = acc.astype(BF16)
        elif j < N_MAIN_BLK + 3:
            (cv_ref, cx1_ref, cx2_ref)[j - N_MAIN_BLK][...] = acc.astype(BF16)
        else:
            gi_ref[...] = acc[:, 0:LANE]
            gf_ref[...] = acc[:, LANE:2 * LANE]

    for j in range(0, W_ALL // GROUP_W, 2):
        acc2 = proj(j, 2 * GROUP_W)
        finish(j, acc2[:, 0:GROUP_W])
        finish(j + 1, acc2[:, GROUP_W:2 * GROUP_W])


def _inproj(x2d, g, w_all, rope_tabs, B, L):
    ca, sa, cb, sb, pa, pb = rope_tabs
    nt = L // ROW_TILE
    n_tok = B * L
    row = lambda i, b: (b * nt + i, 0)
    tab = lambda i, b: (i, 0)
    const = lambda i, b: (0, 0)
    out_shapes = (
        jax.ShapeDtypeStruct((n_tok, MAIN_W), BF16),
        jax.ShapeDtypeStruct((n_tok, GROUP_W), BF16),
        jax.ShapeDtypeStruct((n_tok, GROUP_W), BF16),
        jax.ShapeDtypeStruct((n_tok, GROUP_W), BF16),
        jax.ShapeDtypeStruct((n_tok, LANE), F32),
        jax.ShapeDtypeStruct((n_tok, LANE), F32),
    )
    return pl.pallas_call(
        _inproj_kernel,
        grid=(nt, B),
        in_specs=[
            pl.BlockSpec((ROW_TILE, D_MODEL), row),
            pl.BlockSpec((1, D_MODEL), const),
            pl.BlockSpec((D_MODEL, W_ALL), const),
            pl.BlockSpec((ROW_TILE, GROUP_W), tab),
            pl.BlockSpec((ROW_TILE, GROUP_W), tab),
            pl.BlockSpec((ROW_TILE, GROUP_W), tab),
            pl.BlockSpec((ROW_TILE, GROUP_W), tab),
            pl.BlockSpec((GROUP_W, GROUP_W), const),
            pl.BlockSpec((GROUP_W, GROUP_W), const),
        ],
        out_specs=(
            pl.BlockSpec((ROW_TILE, MAIN_W), row),
            pl.BlockSpec((ROW_TILE, GROUP_W), row),
            pl.BlockSpec((ROW_TILE, GROUP_W), row),
            pl.BlockSpec((ROW_TILE, GROUP_W), row),
            pl.BlockSpec((ROW_TILE, LANE), row),
            pl.BlockSpec((ROW_TILE, LANE), row),
        ),
        out_shape=out_shapes,
        compiler_params=_cparams(("parallel", "parallel")),
        name="inproj",
    )(x2d, g, w_all, ca, sa, cb, sb, pa, pb)


def _memkv_kernel(m_ref, g_ref, w_ref, o_ref):
    x = m_ref[...]
    ms = jnp.mean(x * x, axis=-1, keepdims=True)
    xn = (x * lax.rsqrt(ms + NORM_EPS) * g_ref[...]).astype(BF16)
    o_ref[...] = _dot(xn, w_ref[...]).astype(BF16)


def _memkv(mem2d, g, w):
    rows = mem2d.shape[0]
    tile = math.gcd(rows, ROW_TILE)
    return pl.pallas_call(
        _memkv_kernel,
        grid=(rows // tile,),
        in_specs=[
            pl.BlockSpec((tile, D_MODEL), lambda i: (i, 0)),
            pl.BlockSpec((1, D_MODEL), lambda i: (0, 0)),
            pl.BlockSpec((D_MODEL, 2 * GROUP_W), lambda i: (0, 0)),
        ],
        out_specs=pl.BlockSpec((tile, 2 * GROUP_W), lambda i: (i, 0)),
        out_shape=jax.ShapeDtypeStruct((rows, 2 * GROUP_W), BF16),
        compiler_params=_cparams(("parallel",)),
        name="memkv",
    )(mem2d, g, w)


def _memattn_kernel(q_ref, mk_ref, mv_ref, o_ref):
    q = q_ref[...]
    mk = mk_ref[...]
    mv = mv_ref[...]
    lane_head = lax.broadcasted_iota(jnp.int32, (1, GROUP_W), 1) // HEAD_DIM
    acc = jnp.zeros(q.shape, F32)
    for h in range(N_HEADS):
        hm = (lane_head == h).astype(BF16)
        s = lax.dot_general(q * hm, mk, (((1,), (1,)), ((), ())), preferred_element_type=F32)
        m = jnp.max(s, axis=-1, keepdims=True)
        p = jnp.exp2(s - m)
        l = jnp.sum(p, axis=-1, keepdims=True)
        acc = acc + _dot(p.astype(BF16), mv * hm) * (1.0 / l)
    o_ref[...] = acc.astype(BF16)


def _memattn(h_main, mkv, B, L, M):
    nq = L // MEM_Q_TILE
    return pl.pallas_call(
        _memattn_kernel,
        grid=(B, nq),
        in_specs=[
            pl.BlockSpec((MEM_Q_TILE, GROUP_W), lambda b, i: (b * nq + i, BLK_XQ)),
            pl.BlockSpec((M, GROUP_W), lambda b, i: (b, 0)),
            pl.BlockSpec((M, GROUP_W), lambda b, i: (b, 1)),
        ],
        out_specs=pl.BlockSpec((MEM_Q_TILE, GROUP_W), lambda b, i: (b * nq + i, 0)),
        out_shape=jax.ShapeDtypeStruct((B * L, GROUP_W), BF16),
        compiler_params=_cparams(("parallel", "parallel")),
        name="memattn",
    )(h_main, mkv, mkv)


def _dil_bias_table():
    w = DIL_Q_TILE + 2 * DIL_PAD
    d = np.arange(DIL_Q_TILE)[None, :] - np.arange(w)[:, None] + DIL_PAD
    count = np.zeros(d.shape, np.float64)
    for win, dil in DIL_PATTERNS:
        reach = (win // (2 * dil)) * dil
        count += (d % dil == 0) & (np.abs(d) <= reach)
    return np.where(count > 0, np.log2(np.maximum(count, 1.0)), NEG_INF).astype(np.float32)


def _dilattn_kernel(q_ref, qn_ref, k_ref, v_ref, bias_ref, o_ref, kpad, vt_ref, kmask,
                    sa_ref, sb_ref, m_ref, *, L):
    i = pl.program_id(1)
    nq = pl.num_programs(1)
    tq, hd, kt = DIL_Q_TILE, HEAD_DIM, DIL_K_TILE
    w = tq + 2 * DIL_PAD
    n_pad = DIL_PAD // kt

    @pl.when(i == 0)
    def _():
        zeros = jnp.zeros((DIL_PAD, GROUP_W), BF16)
        kpad[0:DIL_PAD, :] = zeros
        kpad[DIL_PAD + L:DIL_PAD + L + DIL_PAD, :] = zeros
        kpad[DIL_PAD:DIL_PAD + L, :] = k_ref[...]
        off = jnp.full((DIL_PAD, LANE), NEG_INF, F32)
        kmask[0:DIL_PAD, :] = off
        kmask[DIL_PAD + L:DIL_PAD + L + DIL_PAD, :] = off
        kmask[DIL_PAD:DIL_PAD + L, :] = jnp.zeros((L, LANE), F32)
        tail = (lax.broadcasted_iota(jnp.int32, (DIL_VT_ROWS - hd, kt), 0) == 0).astype(BF16)
        for c in range(L // kt + 2 * n_pad):
            inside = n_pad <= c < n_pad + L // kt
            if inside:
                lo = (c - n_pad) * kt
                vt = v_ref[lo:lo + kt, :].astype(F32).T.astype(BF16)
            for h in range(N_HEADS):
                vt_ref[c, h, 0:hd, :] = (vt[h * hd:(h + 1) * hd] if inside
                                         else jnp.zeros((hd, kt), BF16))
                vt_ref[c, h, hd:DIL_VT_ROWS, :] = tail

    feat_head = lax.broadcasted_iota(jnp.int32, (GROUP_W, 1), 0) // hd
    s_refs = (sa_ref, sb_ref)
    tiles = [(t * kt, (t + 1) * kt) for t in range(w // kt)]

    def stage(src_ref, tile, pair_next, nxt, pair_cur, cur):
        q0n = pl.multiple_of(tile * tq, tq)
        qt = src_ref[...].astype(F32).T
        qt2 = jnp.concatenate([jnp.where(feat_head == 2 * pair_next + c, qt, 0.0)
                               for c in range(2)], axis=1).astype(BF16)
        mx = None
        if pair_cur is not None:
            m = m_ref[cur, 0:1, :]
            acc = [jnp.zeros((DIL_VT_ROWS, tq), F32) for _ in range(2)]
        for t, (lo, hi) in enumerate(tiles):
            rows = pl.ds(q0n + lo, kt)
            km = kmask[rows, :]
            bias = bias_ref[lo:hi, :] + jnp.concatenate([km, km], axis=1)
            s = _dot(kpad[rows, :], qt2) + jnp.concatenate([bias, bias], axis=1)
            s_refs[nxt][lo:hi, :] = s
            cm = jnp.max(s, axis=0, keepdims=True)
            mx = cm if mx is None else jnp.maximum(mx, cm)
            if pair_cur is not None:
                p = jnp.exp2((s_refs[cur][lo:hi, :] - m).astype(BF16))
                for c in range(2):
                    acc[c] = acc[c] + _dot(vt_ref[i * (tq // kt) + t, 2 * pair_cur + c],
                                           p[:, c * tq:(c + 1) * tq])
        m_ref[nxt] = jnp.broadcast_to(mx, m_ref.shape[1:])
        if pair_cur is not None:
            return [o[0:hd] * (1.0 / o[hd:hd + 1]) for o in acc]

    @pl.when(i == 0)
    def _():
        stage(q_ref, i, 0, 0, None, None)

    heads = stage(q_ref, i, 1, 1, 0, 0)
    heads += stage(qn_ref, jnp.minimum(i + 1, nq - 1), 0, 0, 1, 1)
    o_ref[...] = jnp.concatenate(heads, axis=0).T.astype(BF16)


def _dilattn(h_main, bias, B, L):
    nq = L // DIL_Q_TILE
    w = DIL_Q_TILE + 2 * DIL_PAD
    return pl.pallas_call(
        functools.partial(_dilattn_kernel, L=L),
        grid=(B, nq),
        in_specs=[
            pl.BlockSpec((DIL_Q_TILE, GROUP_W), lambda b, i: (b * nq + i, BLK_AQ)),
            pl.BlockSpec((DIL_Q_TILE, GROUP_W),
                         lambda b, i: (b * nq + jnp.minimum(i + 1, nq - 1), BLK_AQ)),
            pl.BlockSpec((L, GROUP_W), lambda b, i: (b, BLK_AK)),
            pl.BlockSpec((L, GROUP_W), lambda b, i: (b, BLK_AV)),
            pl.BlockSpec((w, DIL_Q_TILE), lambda b, i: (0, 0)),
        ],
        out_specs=pl.BlockSpec((DIL_Q_TILE, GROUP_W), lambda b, i: (b * nq + i, 0)),
        out_shape=jax.ShapeDtypeStruct((B * L, GROUP_W), BF16),
        scratch_shapes=[pltpu.VMEM((L + 2 * DIL_PAD, GROUP_W), BF16),
                        pltpu.VMEM(((L + 2 * DIL_PAD) // DIL_K_TILE, N_HEADS, DIL_VT_ROWS,
                                    DIL_K_TILE), BF16),
                        pltpu.VMEM((L + 2 * DIL_PAD, LANE), F32),
                        pltpu.VMEM((w, 2 * DIL_Q_TILE), F32),
                        pltpu.VMEM((w, 2 * DIL_Q_TILE), F32),
                        pltpu.VMEM((2, F32_SUBLANES, 2 * DIL_Q_TILE), F32)],
        compiler_params=_cparams(("parallel", "arbitrary")),
        name="dilattn",
    )(h_main, h_main, h_main, h_main, bias)


def _split3(x):
    hi = x.astype(BF16)
    r1 = x - hi.astype(F32)
    mid = r1.astype(BF16)
    lo = (r1 - mid.astype(F32)).astype(BF16)
    return hi, mid, lo


def _split_dot(x, mat):
    return sum(_dot(t, mat) for t in _split3(x))


def _dot_f32(a, b):
    ah, am, al = _split3(a)
    bh, bm, bl = _split3(b)
    return (_dot(ah, bh) + (_dot(ah, bm) + _dot(am, bh))
            + (_dot(ah, bl) + _dot(am, bm) + _dot(al, bh)))


def _diffattn_kernel(q_ref, qn_ref, k_ref, v_ref, lam_ref, g_ref, hmean_ref, o_ref,
                     vt_ref, sa_ref, sb_ref, m_ref, oh_ref, *, lam_init, L):
    hd = HEAD_DIM
    first_tile = pl.program_id(1) == 0

    @pl.when(first_tile)
    def _():
        tail = (lax.broadcasted_iota(jnp.int32, (DIFF_VT_ROWS - hd, DIFF_K_CHUNK), 0) == 0)
        for c in range(L // DIFF_K_CHUNK):
            lo, hi = c * DIFF_K_CHUNK, (c + 1) * DIFF_K_CHUNK
            vt = v_ref[lo:hi, :].astype(F32).T.astype(BF16)
            for h in range(N_HEADS):
                vt_ref[h, 0:hd, lo:hi] = vt[h * hd:(h + 1) * hd]
                vt_ref[h, hd:DIFF_VT_ROWS, lo:hi] = tail.astype(BF16)

    lp = lam_ref[...]
    lam = (jnp.exp(jnp.sum(lp[0:1] * lp[1:2], axis=-1, keepdims=True))
           - jnp.exp(jnp.sum(lp[2:3] * lp[3:4], axis=-1, keepdims=True)) + lam_init)
    qt = q_ref[...].astype(F32).T
    qt_next = qn_ref[...].astype(F32).T
    feat_group = lax.broadcasted_iota(jnp.int32, (GROUP_W, 1), 0) // DIFF_QK_DIM
    tq = qt.shape[1]
    chunks = [(c * DIFF_K_CHUNK, (c + 1) * DIFF_K_CHUNK) for c in range(L // DIFF_K_CHUNK)]

    def masked_qt(h):
        wraps = h == N_HEADS
        src = jnp.where(wraps, qt_next, qt)
        hh = jnp.where(wraps, 0, h)
        return jnp.concatenate([jnp.where(feat_group == 2 * hh + c, src, 0.0) for c in range(2)],
                               axis=1).astype(BF16)

    def stage(h_next, nxt, h_cur, cur):
        qt2 = masked_qt(h_next)
        mx = None
        if h_cur is not None:
            m = m_ref[cur, 0:1, :]
            o = jnp.zeros((DIFF_VT_ROWS, 2 * tq), F32)
        for lo, hi in chunks:
            s = _dot(k_ref[lo:hi, :], qt2)
            s_refs[nxt][lo:hi, :] = s
            cm = jnp.max(s, axis=0, keepdims=True)
            mx = cm if mx is None else jnp.maximum(mx, cm)
            if h_cur is not None:
                p = jnp.exp2((s_refs[cur][lo:hi, :] - m).astype(BF16))
                o = o + _dot(vt_ref[h_cur, :, lo:hi], p)
        m_ref[nxt] = jnp.broadcast_to(mx, m_ref.shape[1:])
        if h_cur is not None:
            on = o[0:hd] * (1.0 / o[hd:hd + 1])
            oh_ref[h_cur] = on[:, 0:tq] - on[:, tq:2 * tq] * lam

    s_refs = (sa_ref, sb_ref)

    @pl.when(first_tile)
    def _():
        stage(0, 0, None, None)

    def body(j, carry):
        stage(2 * j + 1, 1, 2 * j, 0)
        stage(2 * j + 2, 0, 2 * j + 1, 1)
        return carry

    lax.fori_loop(0, N_HEADS // 2, body, 0)
    acc = jnp.concatenate([oh_ref[h] for h in range(N_HEADS)], axis=0).T
    ms = _split_dot(acc * acc, hmean_ref[...])
    y = acc * lax.rsqrt(ms + DIFF_SUBLN_EPS) * g_ref[...] * (1.0 - lam_init)
    o_ref[...] = y.astype(BF16)


def _diffattn(h_main, lam_p, subln_g, hmean, lam_init, B, L):
    nq = L // DIFF_Q_TILE
    return pl.pallas_call(
        functools.partial(_diffattn_kernel, lam_init=lam_init, L=L),
        grid=(B, nq),
        in_specs=[
            pl.BlockSpec((DIFF_Q_TILE, GROUP_W), lambda b, i: (b * nq + i, BLK_BQ)),
            pl.BlockSpec((DIFF_Q_TILE, GROUP_W),
                         lambda b, i: (b * nq + jnp.minimum(i + 1, nq - 1), BLK_BQ)),
            pl.BlockSpec((L, GROUP_W), lambda b, i: (b, BLK_BK)),
            pl.BlockSpec((L, GROUP_W), lambda b, i: (b, BLK_BV)),
            pl.BlockSpec((4, DIFF_QK_DIM), lambda b, i: (0, 0)),
            pl.BlockSpec((1, GROUP_W), lambda b, i: (0, 0)),
            pl.BlockSpec((GROUP_W, GROUP_W), lambda b, i: (0, 0)),
        ],
        out_specs=pl.BlockSpec((DIFF_Q_TILE, GROUP_W), lambda b, i: (b * nq + i, 0)),
        out_shape=jax.ShapeDtypeStruct((B * L, GROUP_W), BF16),
        scratch_shapes=[pltpu.VMEM((N_HEADS, DIFF_VT_ROWS, L), BF16),
                        pltpu.VMEM((L, 2 * DIFF_Q_TILE), F32),
                        pltpu.VMEM((L, 2 * DIFF_Q_TILE), F32),
                        pltpu.VMEM((2, F32_SUBLANES, 2 * DIFF_Q_TILE), F32),
                        pltpu.VMEM((N_HEADS, HEAD_DIM, DIFF_Q_TILE), F32)],
        compiler_params=_cparams(("parallel", "arbitrary")),
        name="diffattn",
    )(h_main, h_main, h_main, h_main, lam_p, subln_g, hmean)


def _log_sigmoid(v):
    return jnp.minimum(v, 0.0) - jnp.log(1.0 + jnp.exp(-jnp.abs(v)))


def _mlstm_kernel(q_ref, k_ref, v_ref, gi_ref, gf_ref, cw_ref, cb_ref, bi_ref, bf_ref,
                  ltri_ref, utri_ref, ecol_ref, elane_ref, kmask_ref, vmask_ref, ones_ref,
                  bd_ref, hsum_ref, o_ref,
                  qpad, kpad, qs, ks, hfw, hbw, cst, nst, mst, *, L):
    T = MLSTM_T
    nc = L // T
    halo = CONV_HALO
    nh = N_HEADS

    zpad = jnp.zeros((halo, GROUP_W), BF16)
    for pad, src in ((qpad, q_ref), (kpad, k_ref)):
        pad[0:halo, :] = zpad
        pad[halo + L:halo + L + halo, :] = zpad
        pad[halo:halo + L, :] = src[...]

    def conv_body(c, carry):
        r0 = pl.multiple_of(c * T, T)
        for idx, (pad, dst, scale) in enumerate(((qpad, qs, 1.0),
                                                 (kpad, ks, 1.0 / math.sqrt(HEAD_DIM)))):
            xw = pad[pl.ds(r0, T + 2 * halo), :].astype(F32)
            xm = pltpu.roll(xw, 1, axis=0)[halo:halo + T]
            xp = pltpu.roll(xw, T + 2 * halo - 1, axis=0)[halo:halo + T]
            xc = xw[halo:halo + T]
            lo, hi = idx * GROUP_W, (idx + 1) * GROUP_W
            y = (xm * cw_ref[0:1, lo:hi] + xc * cw_ref[1:2, lo:hi] + xp * cw_ref[2:3, lo:hi]
                 + cb_ref[0:1, lo:hi])
            dst[pl.ds(r0, T), :] = (_silu(y) * scale).astype(BF16)
        return carry

    lax.fori_loop(0, nc, conv_body, 0)

    cst[...] = jnp.zeros(cst.shape, F32)
    nst[...] = jnp.zeros(nst.shape, F32)
    mst[...] = jnp.zeros(mst.shape, F32)

    is_fw = lax.broadcasted_iota(jnp.int32, (1, LANE), 1) < nh
    rowi = lax.broadcasted_iota(jnp.int32, (T, LANE), 0)
    row4 = lax.broadcasted_iota(jnp.int32, (T, nh * T), 0)
    col4 = lax.broadcasted_iota(jnp.int32, (T, nh * T), 1) % T
    causal = (col4 <= row4, col4 >= row4)

    def body(c, carry):
        rows = (pl.multiple_of(c * T, T), pl.multiple_of((nc - 1 - c) * T, T))
        gate_i = jnp.where(is_fw, gi_ref[pl.ds(rows[0], T), :], gi_ref[pl.ds(rows[1], T), :])
        gate_f = jnp.where(is_fw, gf_ref[pl.ds(rows[0], T), :], gf_ref[pl.ds(rows[1], T), :])
        gate_i = gate_i + bi_ref[...]
        parts = _split3(_log_sigmoid(gate_f + bf_ref[...]))
        pre = sum(_dot(ltri_ref[...], t) for t in parts)
        suf = sum(_dot(utri_ref[...], t) for t in parts)
        cum = jnp.where(is_fw, pre, suf)
        b_end = jnp.where(is_fw, pre[T - 1:T, :], suf[0:1, :])
        key_w = gate_i - cum

        pmax, smax = key_w, key_w
        sh = 1
        while sh < T:
            pmax = jnp.maximum(pmax, jnp.where(rowi >= sh, pltpu.roll(pmax, sh, axis=0), NEG_INF))
            smax = jnp.maximum(smax, jnp.where(rowi < T - sh, pltpu.roll(smax, T - sh, axis=0),
                                               NEG_INF))
            sh *= 2
        m_prev = mst[0:1, :]
        inter = cum + m_prev
        m_t = jnp.maximum(inter, cum + jnp.where(is_fw, pmax, smax))
        u = cum - m_t
        a = b_end + key_w
        m_new = jnp.maximum(b_end + m_prev, jnp.max(a, axis=0, keepdims=True))
        mst[0:1, :] = m_new
        stack = jnp.concatenate(
            [jnp.exp(inter - m_t), jnp.exp(-m_t), jnp.exp(a - m_new),
             jnp.broadcast_to(jnp.exp(b_end + m_prev - m_new), (F32_SUBLANES, LANE))], axis=0)
        st_hi, st_mid, _ = _split3(stack)
        key_w_t = key_w.T

        for d in range(2):
            r0 = rows[d]
            qc = qs[pl.ds(r0, T), :]
            kc = ks[pl.ds(r0, T), :]
            vc = v_ref[pl.ds(r0, T), :]
            ex = _dot(st_hi, elane_ref[d]) + _dot(st_mid, elane_ref[d])
            w_inter, floor, w_key = ex[0:T], ex[T:2 * T], ex[2 * T:3 * T]
            sp_row = ex[3 * T:3 * T + 1]

            r_row = jnp.concatenate([key_w_t[d * nh + h:d * nh + h + 1, :] for h in range(nh)],
                                    axis=1)
            dlog = _split_dot(u, ecol_ref[d]) + r_row
            decay = jnp.exp(jnp.where(causal[d], dlog, NEG_INF))

            kt = kc.astype(F32).T.astype(BF16)
            qk = _dot(qc, jnp.concatenate([kt] * nh, axis=1) * kmask_ref[...]) * decay
            qk_hi = qk.astype(BF16)
            qk_lo = (qk - qk_hi.astype(F32)).astype(BF16)
            vblk = jnp.concatenate([vc] * nh, axis=0) * vmask_ref[...]
            num = _dot(qk_hi, vblk)
            den = _dot(qk_hi, ones_ref[...]) + _dot(qk_lo, ones_ref[...])

            c_prev = cst[d]
            n_prev = nst[d:d + 1, :]
            num = num + w_inter * _dot(qc, c_prev.astype(BF16))
            den = den + w_inter * _dot((qc.astype(F32) * n_prev).astype(BF16), hsum_ref[...])
            h_out = num / jnp.maximum(jnp.abs(den), floor)
            if d == 0:
                hfw[pl.ds(r0, T), :] = h_out
            else:
                hbw[pl.ds(r0, T), :] = h_out

            kw = kc.astype(F32) * w_key
            cst[d] = c_prev * sp_row + _dot_tn(kw.astype(BF16), vc) * bd_ref[...]
            nst[d:d + 1, :] = n_prev * sp_row + jnp.sum(kw, axis=0, keepdims=True)
        return carry

    lax.fori_loop(0, nc, body, 0)
    o_ref[...] = (hfw[...] + hbw[...]).astype(BF16)


def _mlstm_consts():
    T, nh, g = MLSTM_T, N_HEADS, GROUP_W
    tri = np.tril(np.ones((T, T), np.float32))
    src = np.arange(LANE)[:, None]
    ecol = np.stack([(src == d * nh + np.arange(nh * T)[None, :] // T) for d in range(2)])
    elane = np.stack([(src == d * nh + np.arange(g)[None, :] // HEAD_DIM) for d in range(2)])
    head_of = np.arange(g) // HEAD_DIM
    blk_of = np.arange(nh * T) // T
    kmask = head_of[:, None] == blk_of[None, :]
    vmask = blk_of[:, None] == head_of[None, :]
    same_head = head_of[:, None] == head_of[None, :]
    b16 = lambda a: jnp.asarray(a.astype(np.float32), BF16)
    return (b16(tri), b16(tri.T), b16(ecol), b16(elane), b16(kmask), b16(vmask), b16(vmask),
            jnp.asarray(same_head.astype(np.float32)), b16(same_head))


def _mlstm(h_main, gate_i, gate_f, conv_w, conv_b, bias_i, bias_f, consts, B, L):
    T = MLSTM_T
    const = lambda b: (0, 0)
    const3 = lambda b: (0, 0, 0)
    return pl.pallas_call(
        functools.partial(_mlstm_kernel, L=L),
        grid=(B,),
        in_specs=[
            pl.BlockSpec((L, GROUP_W), lambda b: (b, BLK_DQ)),
            pl.BlockSpec((L, GROUP_W), lambda b: (b, BLK_DK)),
            pl.BlockSpec((L, GROUP_W), lambda b: (b, BLK_DV)),
            pl.BlockSpec((L, LANE), lambda b: (b, 0)),
            pl.BlockSpec((L, LANE), lambda b: (b, 0)),
            pl.BlockSpec((3, 2 * GROUP_W), const),
            pl.BlockSpec((1, 2 * GROUP_W), const),
            pl.BlockSpec((1, LANE), const),
            pl.BlockSpec((1, LANE), const),
            pl.BlockSpec((T, T), const),
            pl.BlockSpec((T, T), const),
            pl.BlockSpec((2, LANE, N_HEADS * T), const3),
            pl.BlockSpec((2, LANE, GROUP_W), const3),
            pl.BlockSpec((GROUP_W, N_HEADS * T), const),
            pl.BlockSpec((N_HEADS * T, GROUP_W), const),
            pl.BlockSpec((N_HEADS * T, GROUP_W), const),
            pl.BlockSpec((GROUP_W, GROUP_W), const),
            pl.BlockSpec((GROUP_W, GROUP_W), const),
        ],
        out_specs=pl.BlockSpec((L, GROUP_W), lambda b: (b, 0)),
        out_shape=jax.ShapeDtypeStruct((B * L, GROUP_W), BF16),
        scratch_shapes=[
            pltpu.VMEM((L + 2 * CONV_HALO, GROUP_W), BF16),
            pltpu.VMEM((L + 2 * CONV_HALO, GROUP_W), BF16),
            pltpu.VMEM((L, GROUP_W), BF16),
            pltpu.VMEM((L, GROUP_W), BF16),
            pltpu.VMEM((L, GROUP_W), F32),
            pltpu.VMEM((L, GROUP_W), F32),
            pltpu.VMEM((2, GROUP_W, GROUP_W), F32),
            pltpu.VMEM((F32_SUBLANES, GROUP_W), F32),
            pltpu.VMEM((F32_SUBLANES, LANE), F32),
        ],
        compiler_params=_cparams(("parallel",)),
        name="mlstm",
    )(h_main, h_main, h_main, gate_i, gate_f, conv_w, conv_b, bias_i, bias_f, *consts)


def _hy_dims(L):
    h1 = L // HY_N2
    k1h = h1 + 1
    k1p = -(-k1h // BF16_SUBLANES) * BF16_SUBLANES
    kg = HY_K1_GROUP if k1p % HY_K1_GROUP == 0 else BF16_SUBLANES
    return h1, k1h, k1p, kg


def _hy_tables(L):
    h1, k1h, k1p, _ = _hy_dims(L)
    n1_len = 2 * h1
    n = 2 * L
    k1 = np.arange(k1p)[:, None].astype(np.float64)
    live = (np.arange(k1p) < k1h)[:, None]
    n1 = np.arange(h1)[None, :].astype(np.float64)
    ang = 2.0 * np.pi * k1 * n1 / n1_len
    m1 = np.concatenate([np.where(live, np.cos(ang), 0.0), np.where(live, -np.sin(ang), 0.0)], 0)

    n2 = np.arange(HY_N2)[None, None, :].astype(np.float64)
    k2 = np.arange(HY_N2)[None, :, None].astype(np.float64)
    kk = np.arange(k1p)[:, None, None] + n1_len * k2
    th = 2.0 * np.pi * kk * n2 / n
    fr, fi = np.cos(th), -np.sin(th)
    f3 = np.concatenate([np.concatenate([fr, -fi], 2), np.concatenate([fi, fr], 2)], 1)
    er, ei = np.transpose(np.cos(th), (0, 2, 1)), np.transpose(np.sin(th), (0, 2, 1))
    f3i = np.concatenate([np.concatenate([er, -ei], 2), np.concatenate([ei, er], 2)], 1)
    live3 = (np.arange(k1p) < k1h)[:, None, None]
    f3 = np.where(live3, f3, 0.0)
    f3i = np.where(live3, f3i, 0.0)

    nn1 = np.arange(h1)[:, None].astype(np.float64)
    kc = np.arange(k1p)[None, :].astype(np.float64)
    ph = 2.0 * np.pi * nn1 * kc / n1_len
    edge = (np.arange(k1p) == 0) | (np.arange(k1p) == h1)
    livec = (np.arange(k1p) < k1h)[None, :]
    m4r = np.where(livec, np.where(edge[None, :], np.cos(ph), 2.0 * np.cos(ph)), 0.0) / n
    m4i = np.where(livec & ~edge[None, :], -2.0 * np.sin(ph), 0.0) / n
    return (jnp.asarray(m1, BF16), jnp.asarray(f3, BF16), jnp.asarray(f3i, BF16),
            jnp.asarray(m4r, BF16), jnp.asarray(m4i, BF16))


def _to_wide(x, h1):
    return x.astype(F32).reshape(h1, HY_N2 * GROUP_W)


def _hy_stage1(z_bf16, m1_ref, a_ref, k1p):
    a = _dot(m1_ref[...], z_bf16)
    a_ref[0, 0] = a[0:k1p].reshape(k1p, HY_N2, GROUP_W).astype(BF16)
    a_ref[0, 1] = a[k1p:2 * k1p].reshape(k1p, HY_N2, GROUP_W).astype(BF16)


def _hy_conv3_wide(x, w_ref, b_ref, j, h1):
    c = GROUP_W
    wl = HY_N2 * c
    rowi = lax.broadcasted_iota(jnp.int32, (h1, c), 0)
    tail = x[:, wl - c:wl]
    head = x[:, 0:c]
    prev_tail = jnp.where(rowi == 0, 0.0, pltpu.roll(tail, 1, axis=0))
    next_head = jnp.where(rowi == h1 - 1, 0.0, pltpu.roll(head, h1 - 1, axis=0))
    xm = jnp.concatenate([prev_tail, x[:, 0:wl - c]], axis=1)
    xp = jnp.concatenate([x[:, c:wl], next_head], axis=1)
    return (xm * w_ref[3 * j:3 * j + 1, :] + x * w_ref[3 * j + 1:3 * j + 2, :]
            + xp * w_ref[3 * j + 2:3 * j + 3, :] + b_ref[j:j + 1, :])


def _hy_front_kernel(v_ref, x1_ref, x2_ref, w_ref, b_ref, m1_ref,
                     z_ref, x1c_ref, x2c_ref, a_ref, *, h1, k1p):
    z = _hy_conv3_wide(_to_wide(v_ref[...], h1), w_ref, b_ref, 0, h1).astype(BF16)
    z_ref[0] = z
    x1c_ref[0] = _hy_conv3_wide(_to_wide(x1_ref[...], h1), w_ref, b_ref, 1, h1).astype(BF16)
    x2c_ref[0] = _hy_conv3_wide(_to_wide(x2_ref[...], h1), w_ref, b_ref, 2, h1).astype(BF16)
    _hy_stage1(z, m1_ref, a_ref, k1p)


def _hy_front(cv, cx1, cx2, w_wide, b_wide, m1, B, L):
    h1, _, k1p, _ = _hy_dims(L)
    wl = HY_N2 * GROUP_W
    nat = pl.BlockSpec((L, GROUP_W), lambda b: (b, 0))
    seq = pl.BlockSpec((1, h1, wl), lambda b: (b, 0, 0))
    wide = jax.ShapeDtypeStruct((B, h1, wl), BF16)
    slab = (1, 2, k1p, HY_N2, GROUP_W)
    return pl.pallas_call(
        functools.partial(_hy_front_kernel, h1=h1, k1p=k1p),
        grid=(B,),
        in_specs=[nat, nat, nat,
                  pl.BlockSpec((9, wl), lambda b: (0, 0)),
                  pl.BlockSpec((3, wl), lambda b: (0, 0)),
                  pl.BlockSpec((2 * k1p, h1), lambda b: (0, 0))],
        out_specs=(seq, seq, seq, pl.BlockSpec(slab, lambda b: (b, 0, 0, 0, 0))),
        out_shape=(wide, wide, wide,
                   jax.ShapeDtypeStruct((B, 2, k1p, HY_N2, GROUP_W), BF16)),
        compiler_params=_cparams(("parallel",)),
        name="hy_front",
    )(cv, cx1, cx2, w_wide, b_wide, m1)


def _hy_stage1_kernel(z_ref, m1_ref, a_ref, *, h1, k1p):
    _hy_stage1(_to_wide(z_ref[0], h1).astype(BF16), m1_ref, a_ref, k1p)


def _hy_stage1_call(z_nat, m1, L):
    h1, _, k1p, _ = _hy_dims(L)
    nb = z_nat.shape[0]
    return pl.pallas_call(
        functools.partial(_hy_stage1_kernel, h1=h1, k1p=k1p),
        grid=(nb,),
        in_specs=[pl.BlockSpec((1, L, GROUP_W), lambda b: (b, 0, 0)),
                  pl.BlockSpec((2 * k1p, h1), lambda b: (0, 0))],
        out_specs=pl.BlockSpec((1, 2, k1p, HY_N2, GROUP_W), lambda b: (b, 0, 0, 0, 0)),
        out_shape=jax.ShapeDtypeStruct((nb, 2, k1p, HY_N2, GROUP_W), BF16),
        compiler_params=_cparams(("parallel",)),
        name="hy_stage1",
    )(z_nat, m1)


def _hy_spectrum_kernel(af_ref, ab_ref, f3_ref, g_ref, *, kg, k1h):
    grp = pl.program_id(0)

    @pl.when(grp * kg < k1h)
    def _():
        for kk in range(kg):
            xs = []
            for a_ref in (af_ref, ab_ref):
                a2 = jnp.concatenate([a_ref[0, 0, kk], a_ref[0, 1, kk]], axis=0)
                xs.append(_dot(f3_ref[kk], a2))
            g_ref[0, 0, kk] = xs[0][0:HY_N2] + xs[1][0:HY_N2]
            g_ref[0, 1, kk] = xs[0][HY_N2:2 * HY_N2] - xs[1][HY_N2:2 * HY_N2]

    @pl.when(grp * kg >= k1h)
    def _():
        g_ref[...] = jnp.zeros(g_ref.shape, F32)


def _hy_spectrum(a_filt, f3, L):
    _, k1h, k1p, kg = _hy_dims(L)
    a5 = a_filt
    blk = (1, 2, kg, HY_N2, GROUP_W)
    return pl.pallas_call(
        functools.partial(_hy_spectrum_kernel, kg=kg, k1h=k1h),
        grid=(k1p // kg, HYENA_ORDER),
        in_specs=[pl.BlockSpec(blk, lambda g, o: (2 * o, 0, g, 0, 0)),
                  pl.BlockSpec(blk, lambda g, o: (2 * o + 1, 0, g, 0, 0)),
                  pl.BlockSpec((kg, 2 * HY_N2, 2 * HY_N2), lambda g, o: (g, 0, 0))],
        out_specs=pl.BlockSpec(blk, lambda g, o: (o, 0, g, 0, 0)),
        out_shape=jax.ShapeDtypeStruct((HYENA_ORDER, 2, k1p, HY_N2, GROUP_W), F32),
        compiler_params=_cparams(("parallel", "parallel")),
        name="hy_spectrum",
    )(a5, a5, f3)


def _hy_mid_kernel(a_ref, g_ref, f3_ref, f3i_ref, b_ref, *, kg, k1h):
    grp = pl.program_id(0)

    @pl.when(grp * kg < k1h)
    def _():
        for kk in range(kg):
            a2 = jnp.concatenate([a_ref[0, 0, kk], a_ref[0, 1, kk]], axis=0)
            x = _dot(f3_ref[kk], a2)
            xr, xi = x[0:HY_N2], x[HY_N2:2 * HY_N2]
            gr, gi = g_ref[0, 0, kk], g_ref[0, 1, kk]
            y2 = jnp.concatenate([xr * gr - xi * gi, xr * gi + xi * gr], axis=0).astype(BF16)
            bm = _dot(f3i_ref[kk], y2)
            b_ref[0, 0, kk] = bm[0:HY_N2].astype(BF16)
            b_ref[0, 1, kk] = bm[HY_N2:2 * HY_N2].astype(BF16)

    @pl.when(grp * kg >= k1h)
    def _():
        b_ref[...] = jnp.zeros(b_ref.shape, BF16)


def _hy_mid(a, g_spec, order, f3, f3i, B, L):
    _, k1h, k1p, kg = _hy_dims(L)
    blk = (1, 2, kg, HY_N2, GROUP_W)
    tab = pl.BlockSpec((kg, 2 * HY_N2, 2 * HY_N2), lambda g, b: (g, 0, 0))
    return pl.pallas_call(
        functools.partial(_hy_mid_kernel, kg=kg, k1h=k1h),
        grid=(k1p // kg, B),
        in_specs=[pl.BlockSpec(blk, lambda g, b: (b, 0, g, 0, 0)),
                  pl.BlockSpec(blk, lambda g, b: (order, 0, g, 0, 0)),
                  tab, tab],
        out_specs=pl.BlockSpec(blk, lambda g, b: (b, 0, g, 0, 0)),
        out_shape=jax.ShapeDtypeStruct((B, 2, k1p, HY_N2, GROUP_W), BF16),
        compiler_params=_cparams(("parallel", "parallel")),
        name="hy_mid",
    )(a, g_spec, f3, f3i)


def _hy_back_kernel(b_ref, z_ref, x_ref, bias_ref, m4r_ref, m4i_ref, *rest, k1p, last):
    wl = HY_N2 * GROUP_W
    br = b_ref[0, 0].astype(F32).reshape(k1p, wl).astype(BF16)
    bi = b_ref[0, 1].astype(F32).reshape(k1p, wl).astype(BF16)
    y = _dot(m4r_ref[...], br) + _dot(m4i_ref[...], bi)
    z_new = x_ref[0].astype(F32) * (y + z_ref[0].astype(F32) * bias_ref[...])
    zb = z_new.astype(BF16)
    if last:
        (o_ref,) = rest
        o_ref[...] = z_new.reshape(o_ref.shape).astype(BF16)
    else:
        m1_ref, o_ref, a_ref = rest
        o_ref[0] = zb
        _hy_stage1(zb, m1_ref, a_ref, k1p)


def _hy_back(b, z, xg, bias_wide, m4r, m4i, m1, B, L, last):
    h1, _, k1p, _ = _hy_dims(L)
    wl = HY_N2 * GROUP_W
    seq = pl.BlockSpec((1, h1, wl), lambda i: (i, 0, 0))
    slab = pl.BlockSpec((1, 2, k1p, HY_N2, GROUP_W), lambda i: (i, 0, 0, 0, 0))
    const = lambda i: (0, 0)
    in_specs = [slab, seq, seq, pl.BlockSpec((1, wl), const),
                pl.BlockSpec((h1, k1p), const), pl.BlockSpec((h1, k1p), const)]
    args = [b, z, xg, bias_wide, m4r, m4i]
    if last:
        out_specs = pl.BlockSpec((L, GROUP_W), lambda i: (i, 0))
        out_shape = jax.ShapeDtypeStruct((B * L, GROUP_W), BF16)
    else:
        in_specs.append(pl.BlockSpec((2 * k1p, h1), const))
        args.append(m1)
        out_specs = (seq, slab)
        out_shape = (jax.ShapeDtypeStruct((B, h1, wl), BF16),
                     jax.ShapeDtypeStruct((B, 2, k1p, HY_N2, GROUP_W), BF16))
    return pl.pallas_call(
        functools.partial(_hy_back_kernel, k1p=k1p, last=last),
        grid=(B,),
        in_specs=in_specs,
        out_specs=out_specs,
        out_shape=out_shape,
        compiler_params=_cparams(("parallel",)),
        name="hy_back_last" if last else "hy_back",
    )(*args)


def _hy_filter_kernel(f_ref, t_ref, w1_ref, b1_ref, w2_ref, b2_ref, w3_ref, fr_ref, ad_ref,
                      o_ref, *, tile):
    i = pl.program_id(0)
    freq = fr_ref[...]
    z = jnp.sin(freq * (_dot_f32(f_ref[...], w1_ref[...]) + b1_ref[...]))
    z = jnp.sin(freq * (_dot_f32(z, w2_ref[...]) + b2_ref[...]))
    decay = jnp.exp(-t_ref[...] * ad_ref[...])
    rowi = i * tile + lax.broadcasted_iota(jnp.int32, (tile, GROUP_W), 0)
    for j in range(2 * HYENA_ORDER):
        hj = _dot_f32(z, w3_ref[:, j * GROUP_W:(j + 1) * GROUP_W]) * decay
        if j % 2 == 1:
            hj = jnp.where(rowi == 0, 0.0, hj)
        o_ref[j] = hj.astype(BF16)


def _hy_filter(feats, tcol, w1, b1, w2, b2, w3, freq, absdelta, L):
    tile = math.gcd(L, ROW_TILE)
    const = lambda i: (0, 0)
    return pl.pallas_call(
        functools.partial(_hy_filter_kernel, tile=tile),
        grid=(L // tile,),
        in_specs=[pl.BlockSpec((tile, LANE), lambda i: (i, 0)),
                  pl.BlockSpec((tile, 1), lambda i: (i, 0)),
                  pl.BlockSpec((LANE, HYENA_HIDDEN), const),
                  pl.BlockSpec((1, HYENA_HIDDEN), const),
                  pl.BlockSpec((HYENA_HIDDEN, HYENA_HIDDEN), const),
                  pl.BlockSpec((1, HYENA_HIDDEN), const),
                  pl.BlockSpec((HYENA_HIDDEN, 2 * HYENA_ORDER * GROUP_W), const),
                  pl.BlockSpec((1, HYENA_HIDDEN), const),
                  pl.BlockSpec((1, GROUP_W), const)],
        out_specs=pl.BlockSpec((2 * HYENA_ORDER, tile, GROUP_W), lambda i: (0, i, 0)),
        out_shape=jax.ShapeDtypeStruct((2 * HYENA_ORDER, L, GROUP_W), BF16),
        compiler_params=_cparams(("parallel",)),
        name="hy_filter",
    )(feats, tcol, w1, b1, w2, b2, w3, freq, absdelta)


def _outproj_kernel(x_ref, oa_ref, ob_ref, oc_ref, od_ref, ox_ref, gate_ref, w_ref, png_ref,
                    out_ref):
    f = lambda r: r[...].astype(F32)
    gate = lambda j: gate_ref[:, j * GROUP_W:(j + 1) * GROUP_W].astype(F32)
    branches = (
        f(oa_ref) * _silu(gate(BLK_AG)),
        f(ob_ref) * _silu(gate(BLK_BG)),
        f(oc_ref) * _silu(gate(BLK_CG)),
        f(od_ref) * _sigmoid(gate(BLK_DO)) * _silu(gate(BLK_DG)),
        f(ox_ref) * _silu(gate(BLK_XG)),
    )
    y = jnp.zeros(x_ref.shape, F32)
    for j, br in enumerate(branches):
        y = y + _dot(br.astype(BF16), w_ref[j * GROUP_W:(j + 1) * GROUP_W, :])
    ms = jnp.mean(y * y, axis=-1, keepdims=True)
    out_ref[...] = x_ref[...] + y * lax.rsqrt(ms + NORM_EPS) * png_ref[...]


def _outproj(x2d, outs, h_main, w_out, png, row0=0, n_rows=None):
    n_rows = x2d.shape[0] if n_rows is None else n_rows
    nt = n_rows // ROW_TILE
    t0 = row0 // ROW_TILE
    row = lambda i: (t0 + i, 0)
    return pl.pallas_call(
        _outproj_kernel,
        grid=(nt,),
        in_specs=[pl.BlockSpec((ROW_TILE, D_MODEL), row)]
        + [pl.BlockSpec((ROW_TILE, GROUP_W), row)] * 5
        + [pl.BlockSpec((ROW_TILE, N_GATE_BLK * GROUP_W), row)]
        + [pl.BlockSpec((5 * GROUP_W, D_MODEL), lambda i: (0, 0)),
           pl.BlockSpec((1, D_MODEL), lambda i: (0, 0))],
        out_specs=pl.BlockSpec((ROW_TILE, D_MODEL), lambda i: (i, 0)),
        out_shape=jax.ShapeDtypeStruct((n_rows, D_MODEL), F32),
        compiler_params=_cparams(("parallel",)),
        name="outproj",
    )(x2d, *outs, h_main, w_out, png)


def _rope_tables(L, group, rot_dim):
    half = rot_dim // 2
    inv = 1.0 / (ROPE_THETA ** (jnp.arange(0, rot_dim, 2, dtype=F32) / rot_dim))
    ang = jnp.arange(L, dtype=F32)[:, None] * inv[None, :]
    cos, sin = jnp.cos(ang), jnp.sin(ang)
    lane = np.arange(GROUP_W) % group
    in_rot = lane < rot_dim
    idx = lane % half
    c = jnp.where(in_rot[None, :], cos[:, idx], 1.0)
    s = jnp.where(in_rot[None, :], sin[:, idx], 0.0)
    p = np.zeros((GROUP_W, GROUP_W), np.float32)
    for j in range(GROUP_W):
        if lane[j] < half:
            p[j + half, j] = -1.0
        elif lane[j] < rot_dim:
            p[j - half, j] = 1.0
    return c, s, jnp.asarray(p, BF16)


def _hyena_features(L):
    t = jnp.linspace(0.0, 1.0, L, dtype=F32)[:, None]
    bands = jnp.linspace(1e-4, HYENA_BANDS - 1, HYENA_BANDS, dtype=F32)
    ang = (2.0 * math.pi / L) * jnp.arange(L, dtype=F32)[:, None] * bands[None, :]
    feats = jnp.concatenate([t, jnp.cos(ang), -jnp.sin(ang)], axis=-1)
    return jnp.pad(feats, ((0, 0), (0, LANE - HYENA_EMB))), t


def _relayout_w_in(w):
    g = GROUP_W
    off_c, off_d = 8 * g, 12 * g
    off_gate = off_d + 5 * g
    off_x = off_gate + N_MLSTM_GATES
    src = {BLK_AQ: 0, BLK_AK: g, BLK_AV: 2 * g, BLK_AG: 3 * g,
           BLK_BQ: 4 * g, BLK_BK: 5 * g, BLK_BV: 6 * g, BLK_BG: 7 * g, BLK_CG: off_c + 3 * g,
           BLK_DQ: off_d, BLK_DK: off_d + g, BLK_DV: off_d + 2 * g, BLK_DO: off_d + 3 * g,
           BLK_DG: off_d + 4 * g, BLK_XQ: off_x, BLK_XG: off_x + g}
    main = jnp.concatenate([w[:, src[j]:src[j] + g] for j in range(N_MAIN_BLK)], axis=1)
    hy = w[:, off_c:off_c + 3 * g]
    gate_i, gate_f = _split_gates(w[:, off_gate:off_x])
    return jnp.concatenate([main, hy, gate_i, gate_f], axis=1).astype(BF16)


def _split_gates(t):
    nh = N_HEADS
    pad = [(0, 0)] * (t.ndim - 1) + [(0, LANE - 2 * nh)]
    gi = jnp.concatenate([t[..., 0:nh], t[..., 2 * nh:3 * nh]], axis=-1)
    gf = jnp.concatenate([t[..., nh:2 * nh], t[..., 3 * nh:4 * nh]], axis=-1)
    return jnp.pad(gi, pad), jnp.pad(gf, pad)


def _trunk(x, mem, splits, pre_norm_g, post_norm_g, w_in, w_out, diff_lambda, diff_subln_g,
           hy_conv_w, hy_conv_b, hy_ffn_w1, hy_ffn_b1, hy_ffn_w2, hy_ffn_b2, hy_ffn_w3,
           hy_freq, hy_bias, ml_conv_w, ml_conv_b, ml_gate_b, mem_norm_g, w_mem_kv):
    B, L, _ = x.shape
    M = mem.shape[1]
    depth = w_in.shape[0]
    g = GROUP_W
    h1, _, k1p, _ = _hy_dims(L)
    wl = HY_N2 * g

    ca, sa, pa = _rope_tables(L, HEAD_DIM, HEAD_DIM // ROPE_FRACTION)
    cb, sb, pb = _rope_tables(L, DIFF_QK_DIM, DIFF_QK_DIM // ROPE_FRACTION)
    rope_tabs = (ca, sa, cb, sb, pa, pb)
    dil_bias = jnp.asarray(_dil_bias_table())
    feats, tcol = _hyena_features(L)
    absdelta = jnp.abs(jnp.linspace(math.log(HYENA_TARGET) / HYENA_SLOW_DECAY,
                                    math.log(HYENA_TARGET) / HYENA_FAST_DECAY, g, dtype=F32))[None]
    m1, f3, f3i, m4r, m4i = _hy_tables(L)
    head_of = np.arange(g) // HEAD_DIM
    same_head = (head_of[:, None] == head_of[None, :]).astype(np.float32)
    hmean = jnp.asarray(same_head / HEAD_DIM, BF16)
    mlstm_consts = _mlstm_consts()

    x2d = x.reshape(B * L, D_MODEL)
    mem2d = mem.reshape(B * M, D_MODEL)
    for li in range(depth):
        lam_init = 0.8 - 0.6 * math.exp(-0.3 * li)
        w_all = _relayout_w_in(w_in[li])
        h_main, cv, cx1, cx2, gate_i, gate_f = _inproj(x2d, pre_norm_g[li][None], w_all,
                                                       rope_tabs, B, L)

        oa = _dilattn(h_main, dil_bias, B, L)
        ob = _diffattn(h_main, diff_lambda[li], jnp.tile(diff_subln_g[li], N_HEADS)[None],
                       hmean, lam_init, B, L)

        w1 = jnp.pad(hy_ffn_w1[li], ((0, LANE - HYENA_EMB), (0, 0)))
        hfilt = _hy_filter(feats, tcol, w1, hy_ffn_b1[li][None], hy_ffn_w2[li],
                           hy_ffn_b2[li][None], hy_ffn_w3[li], hy_freq[li][None], absdelta, L)
        a_filt = _hy_stage1_call(hfilt, m1, L)
        g_spec = _hy_spectrum(a_filt, f3, L)
        cw = jnp.tile(hy_conv_w[li].reshape(3, 3, g).transpose(1, 0, 2).reshape(9, g), (1, HY_N2))
        cbw = jnp.tile(hy_conv_b[li].reshape(3, g), (1, HY_N2))
        z, x1c, x2c, a = _hy_front(cv, cx1, cx2, cw, cbw, m1, B, L)
        bias_w = jnp.tile(hy_bias[li], (1, HY_N2))
        bsp = _hy_mid(a, g_spec, 0, f3, f3i, B, L)
        z, a = _hy_back(bsp, z, x1c, bias_w[0:1], m4r, m4i, m1, B, L, last=False)
        bsp = _hy_mid(a, g_spec, 1, f3, f3i, B, L)
        oc = _hy_back(bsp, z, x2c, bias_w[1:2], m4r, m4i, m1, B, L, last=True)

        bias_i, bias_f = _split_gates(ml_gate_b[li][None])
        od = _mlstm(h_main, gate_i, gate_f, ml_conv_w[li], ml_conv_b[li][None], bias_i, bias_f,
                    mlstm_consts, B, L)

        mkv = _memkv(mem2d, mem_norm_g[li][None], w_mem_kv[li].astype(BF16))
        ox = _memattn(h_main, mkv, B, L, M)

        branch_outs = (oa, ob, oc, od, ox)
        w_o, png = w_out[li].astype(BF16), post_norm_g[li][None]
        if li + 1 < depth:
            x2d = _outproj(x2d, branch_outs, h_main, w_o, png)
    ys, b0 = [], 0
    for nb in splits:
        y = _outproj(x2d, branch_outs, h_main, w_o, png, row0=b0 * L, n_rows=nb * L)
        ys.append(y.reshape(nb, L, D_MODEL))
        b0 += nb
    return tuple(ys)


def kernel(x_prompt, x_sample, mem_prompt, mem_sample, pre_norm_g, post_norm_g, w_in, w_out,
           diff_lambda, diff_subln_g, hy_conv_w, hy_conv_b, hy_ffn_w1, hy_ffn_b1, hy_ffn_w2,
           hy_ffn_b2, hy_ffn_w3, hy_freq, hy_bias, ml_conv_w, ml_conv_b, ml_gate_b,
           mem_norm_g, w_mem_kv):
    x = jnp.concatenate([x_prompt, x_sample], axis=0)
    mem = jnp.concatenate([mem_prompt, mem_sample], axis=0)
    return _trunk(x, mem, (x_prompt.shape[0], x_sample.shape[0]), pre_norm_g, post_norm_g, w_in,
                  w_out, diff_lambda, diff_subln_g, hy_conv_w, hy_conv_b, hy_ffn_w1, hy_ffn_b1,
                  hy_ffn_w2, hy_ffn_b2, hy_ffn_w3, hy_freq, hy_bias, ml_conv_w, ml_conv_b,
                  ml_gate_b, mem_norm_g, w_mem_kv)
```
